```python
import math
import jax, jax.numpy as jnp
from jax import lax
import numpy as np

D_MODEL = 1024
BATCH = 32
SEQ = 256
DEPTH = 2
DEC_BATCH = 4
DEC_SEQ = 2048
PAST_LEN = 512

GRID_W = 64
NA_W = D_MODEL // 4
NA_HEADS = 4
NA_DH = NA_W // NA_HEADS
NA_WIN_R = 8
NA_WIN_C = 16
DF_W = D_MODEL // 4
DF_HEADS = 4
DF_DV = DF_W // DF_HEADS
DF_DQK = DF_DV // 2
ML_W = D_MODEL // 2
ML_HEADS = 4
ML_DH = ML_W // ML_HEADS
ML_CHUNK = 128
MIX_W = NA_W + DF_W + ML_W
IN_COLS = 3 * NA_W + 2 * DF_HEADS * 2 * DF_DQK + DF_W + 4 * ML_W + 4 * ML_HEADS
N_EXPERTS = 32
TOP_K = 4
D_EXPERT = D_MODEL // 4
D_SHARED = D_EXPERT
ROUTED_SCALE = 2.5
ROPE_BASE = 10000.0
Q_BLOCK = 128
EPS = 1e-6

kernel_name = "hybrid_na_mlstm_diffattn_moe_dit_step"

F32 = jnp.float32


def rmsnorm(x, g):
    xf = x.astype(F32)
    y = xf * lax.rsqrt(jnp.mean(xf * xf, axis=-1, keepdims=True) + EPS)
    return (y * g.astype(F32)).astype(x.dtype)


def heads(a, n):
    B, L, _ = a.shape
    return a.reshape(B, L, n, -1).transpose(0, 2, 1, 3)


def merge(a):
    B, H, L, d = a.shape
    return a.transpose(0, 2, 1, 3).reshape(B, L, H * d)


def diff_heads(a):
    B, L, _ = a.shape
    return a.reshape(B, L, DF_HEADS, 2, DF_DQK).transpose(0, 2, 3, 1, 4)


def in_splits():
    sizes = [NA_W] * 3 + [DF_HEADS * 2 * DF_DQK] * 2 + [DF_W] + [ML_W] * 4
    return [int(s) for s in np.cumsum(sizes)]


def rope_2d(x):
    L, d = x.shape[-2], x.shape[-1]
    t = jnp.arange(L)
    row, col = t // GRID_W, t % GRID_W
    da = d // 2
    nf = da // 2
    inv = 1.0 / (ROPE_BASE ** (jnp.arange(nf, dtype=F32) / nf))
    xf = x.astype(F32)

    def rot(xa, pos):
        ang = pos.astype(F32)[:, None] * inv
        cs, sn = jnp.cos(ang), jnp.sin(ang)
        x1, x2 = xa[..., :nf], xa[..., nf:]
        return jnp.concatenate([x1 * cs - x2 * sn, x1 * sn + x2 * cs], axis=-1)

    return jnp.concatenate([rot(xf[..., :da], row), rot(xf[..., da:], col)], axis=-1).astype(x.dtype)


def block_attn(q, k, v):
    B, H, Lq, d = q.shape
    nb = Lq // Q_BLOCK
    qb = jnp.moveaxis(q.reshape(B, H, nb, Q_BLOCK, d), 2, 0)
    scale = d ** -0.5

    def f(qi):
        s = jnp.einsum('bhqd,bhkd->bhqk', qi, k).astype(F32) * scale
        p = jax.nn.softmax(s, axis=-1).astype(v.dtype)
        return jnp.einsum('bhqk,bhkd->bhqd', p, v)

    o = lax.map(f, qb)
    return jnp.moveaxis(o, 0, 2).reshape(B, H, Lq, v.shape[-1])


def diff_attn(q, k, v, lam):
    B, H, _, Lq, d = q.shape
    nb = Lq // Q_BLOCK
    qb = jnp.moveaxis(q.reshape(B, H, 2, nb, Q_BLOCK, d), 3, 0)
    scale = d ** -0.5

    def f(qi):
        s = jnp.einsum('bhaqd,bhakd->bhaqk', qi, k).astype(F32) * scale
        p = jax.nn.softmax(s, axis=-1)
        w = (p[:, :, 0] - lam * p[:, :, 1]).astype(v.dtype)
        return jnp.einsum('bhqk,bhkd->bhqd', w, v)

    o = lax.map(f, qb)
    return jnp.moveaxis(o, 0, 2).reshape(B, H, Lq, v.shape[-1])


def na_latent(q, k, v, kc, vc, rpb):
    B, H, L, d = q.shape
    rows = L // GRID_W
    wr = min(NA_WIN_R, rows)
    scale = d ** -0.5
    k5 = k.reshape(B, H, rows, GRID_W, d)
    v5 = v.reshape(B, H, rows, GRID_W, d)
    qr = jnp.moveaxis(q.reshape(B, H, rows, GRID_W, d), 2, 0)
    cols = jnp.arange(GRID_W)
    col_idx = jnp.clip(cols - NA_WIN_C // 2, 0, GRID_W - NA_WIN_C)[:, None] + jnp.arange(NA_WIN_C)
    col_off = col_idx - cols[:, None] + NA_WIN_C - 1
    nloc = wr * NA_WIN_C

    def row_fn(args):
        r, qi = args
        rs = jnp.clip(r - wr // 2, 0, rows - wr)
        kw = lax.dynamic_slice_in_dim(k5, rs, wr, axis=2)[:, :, :, col_idx]
        vw = lax.dynamic_slice_in_dim(v5, rs, wr, axis=2)[:, :, :, col_idx]
        row_off = rs + jnp.arange(wr) - r + NA_WIN_R - 1
        bias = rpb[:, row_off][:, :, col_off]
        s_loc = (jnp.einsum('bhqd,bhrqjd->bhqrj', qi, kw).astype(F32) * scale
                 + jnp.transpose(bias, (0, 2, 1, 3)).astype(F32))
        s_ctx = jnp.einsum('bhqd,bhcd->bhqc', qi, kc).astype(F32) * scale
        p = jax.nn.softmax(jnp.concatenate([s_loc.reshape(B, H, GRID_W, nloc), s_ctx], axis=-1), axis=-1)
        p = p.astype(v.dtype)
        return (jnp.einsum('bhqrj,bhrqjd->bhqd', p[..., :nloc].reshape(B, H, GRID_W, wr, NA_WIN_C), vw)
                + jnp.einsum('bhqc,bhcd->bhqd', p[..., nloc:], vc))

    o = lax.map(row_fn, (jnp.arange(rows), qr))
    return jnp.moveaxis(o, 0, 2).reshape(B, H, L, d)


def mlstm_chunked(q, k, v, ig, lf, C0, n0, m0):
    B, H, L, dk = q.shape
    dv = v.shape[-1]
    nc = L // ML_CHUNK

    def chunks(a):
        a = a.astype(F32)
        return jnp.moveaxis(a.reshape(B, H, nc, ML_CHUNK, *a.shape[3:]), 2, 0)

    tril = jnp.tril(jnp.ones((ML_CHUNK, ML_CHUNK), dtype=bool))

    def step(carry, inp):
        C, n, m = carry
        qc, kc, vc, ic, fc = inp
        b = jnp.cumsum(fc, axis=-1)
        logD = jnp.where(tril, b[..., :, None] - b[..., None, :] + ic[..., None, :], -jnp.inf)
        m_t = jnp.maximum(b + m[..., None], jnp.max(logD, axis=-1))
        Dw = jnp.exp(logD - m_t[..., None])
        w0 = jnp.exp(b + m[..., None] - m_t)
        S = jnp.einsum('bhtd,bhsd->bhts', qc, kc) * Dw
        num = w0[..., None] * jnp.einsum('bhtd,bhde->bhte', qc, C) + jnp.einsum('bhts,bhse->bhte', S, vc)
        den = w0 * jnp.einsum('bhtd,bhd->bht', qc, n) + jnp.sum(S, axis=-1)
        h = num / jnp.maximum(jnp.abs(den), jnp.exp(-m_t))[..., None]
        m_new = m_t[..., -1]
        ws = jnp.exp(b[..., -1:] - b + ic - m_new[..., None])
        decay = jnp.exp(b[..., -1] + m - m_new)
        C = decay[..., None, None] * C + jnp.einsum('bhsd,bhse->bhde', ws[..., None] * kc, vc)
        n = decay[..., None] * n + jnp.einsum('bhs,bhsd->bhd', ws, kc)
        return (C, n, m_new), h

    carry0 = (C0.astype(F32), n0.astype(F32), m0.astype(F32))
    (C, n, m), hs = lax.scan(step, carry0, (chunks(q), chunks(k), chunks(v), chunks(ig), chunks(lf)))
    h = jnp.moveaxis(hs, 0, 2).reshape(B, H, L, dv)
    return h.astype(v.dtype), (C, n, m)


def mlstm_bidir(q, k, v, gates, C0, n0, m0):
    ig_f, fg_f, ig_b, fg_b = gates[0], gates[1], gates[2], gates[3]
    flip = lambda a: jnp.flip(a, axis=2)
    hf, (Cf, nf, mf) = mlstm_chunked(q, k, v, ig_f, jax.nn.log_sigmoid(fg_f), C0[:, 0], n0[:, 0], m0[:, 0])
    hb, (Cb, nb, mb) = mlstm_chunked(flip(q), flip(k), flip(v), flip(ig_b), flip(jax.nn.log_sigmoid(fg_b)),
                                     C0[:, 1], n0[:, 1], m0[:, 1])
    h = hf + flip(hb)
    return h, jnp.stack([Cf, Cb], axis=1), jnp.stack([nf, nb], axis=1), jnp.stack([mf, mb], axis=1)


def mixers(h, lp, lam_init, ctx):
    B, L, _ = h.shape
    p = h @ lp['w_in']
    naq, nak, nav, dfq, dfk, dfv, mq, mk, mv, mo, mg = jnp.split(p, in_splits(), axis=-1)
    naq = rmsnorm(heads(naq, NA_HEADS), lp['na_qn'])
    nak = rmsnorm(heads(nak, NA_HEADS), lp['na_kn'])
    nav = heads(nav, NA_HEADS)
    dfq = rmsnorm(diff_heads(dfq), lp['df_qn'])
    dfk = rmsnorm(diff_heads(dfk), lp['df_kn'])
    dfv = heads(dfv, DF_HEADS)
    mq = heads(mq, ML_HEADS)
    mk = heads(mk, ML_HEADS) * (ML_DH ** -0.5)
    mv = heads(mv, ML_HEADS)
    mo = heads(mo, ML_HEADS)
    gates = (mg.reshape(B, L, 4, ML_HEADS).transpose(2, 0, 3, 1).astype(F32)
             + lp['ml_gate_b'].astype(F32)[:, None, :, None])
    lam_v = lp['df_lam'].astype(F32)
    lam = jnp.exp(jnp.sum(lam_v[0] * lam_v[1])) - jnp.exp(jnp.sum(lam_v[2] * lam_v[3])) + lam_init

    if ctx is None:
        na_o = block_attn(naq, nak, nav)
        df_o = diff_attn(dfq, dfk, dfv, lam)
        C0 = jnp.zeros((B, 2, ML_HEADS, ML_DH, ML_DH), F32)
        n0 = jnp.zeros((B, 2, ML_HEADS, ML_DH), F32)
        m0 = jnp.zeros((B, 2, ML_HEADS), F32)
    else:
        ck_na, cv_na, ck_df, cv_df, C0, n0, m0 = ctx
        na_o = na_latent(naq, nak, nav, ck_na.astype(naq.dtype), cv_na.astype(nav.dtype), lp['na_rpb'])
        k_all = jnp.concatenate([rope_2d(dfk), ck_df.astype(dfk.dtype)], axis=3)
        v_all = jnp.concatenate([dfv, cv_df.astype(dfv.dtype)], axis=2)
        df_o = diff_attn(rope_2d(dfq), k_all, v_all, lam)
    ml_h, Cn, nn_, mn = mlstm_bidir(mq, mk, mv, gates, C0, n0, m0)

    df_o = rmsnorm(df_o, lp['df_subln']) * (1.0 - lam_init)
    ml_o = rmsnorm(ml_h, lp['ml_outn']) * jax.nn.sigmoid(mo)
    out = jnp.concatenate([merge(na_o), merge(df_o), merge(ml_o)], axis=-1) @ lp['w_out']
    new_ctx = (nak, nav, dfk, dfv, Cn, nn_, mn) if ctx is None else None
    return out, new_ctx


def moe(h, lp):
    shp = h.shape
    t = h.reshape(-1, shp[-1])
    s = jax.nn.sigmoid((t @ lp['w_router']).astype(F32))
    _, idx = lax.top_k(s + lp['b_router'].astype(F32), TOP_K)
    s_sel = jnp.take_along_axis(s, idx, axis=-1)
    wts = s_sel / jnp.sum(s_sel, axis=-1, keepdims=True) * ROUTED_SCALE
    gates = jnp.sum(jax.nn.one_hot(idx, N_EXPERTS, dtype=F32) * wts[..., None], axis=1).astype(t.dtype)
    hid = jax.nn.silu(jnp.einsum('td,edf->tef', t, lp['w_exp1'])) * jnp.einsum('td,edf->tef', t, lp['w_exp3'])
    routed = jnp.einsum('tef,efd->td', hid * gates[:, :, None], lp['w_exp2'])
    shared = (jax.nn.silu(t @ lp['w_sh1']) * (t @ lp['w_sh3'])) @ lp['w_sh2']
    return (routed + shared).reshape(shp)


def trunk_layer(x, cond, ctx, lp, lam_init):
    mod = (jax.nn.silu(cond) @ lp['w_ada'] + lp['b_ada'])[:, None, :]
    sh1, sc1, g1, sh2, sc2, g2 = jnp.split(mod, 6, axis=-1)
    h = rmsnorm(x, lp['g_norm1']) * (1 + sc1) + sh1
    mix, new_ctx = mixers(h, lp, lam_init, ctx)
    x = x + g1 * mix
    h = rmsnorm(x, lp['g_norm2']) * (1 + sc2) + sh2
    x = x + g2 * moe(h, lp)
    return x, new_ctx


def setup_inputs(seed: int = 0) -> dict:
    key = jax.random.key(seed)
    ks = iter(jax.random.split(key, 48))
    nrm = lambda shape, scale: jax.random.normal(next(ks), shape, F32) * scale
    D = D_MODEL
    ig = nrm((DEPTH, 2, ML_HEADS), 0.1)
    fg = jnp.linspace(3.0, 6.0, ML_HEADS)[None, None, :] + nrm((DEPTH, 2, ML_HEADS), 0.1)
    ml_gate_b = jnp.stack([ig[:, 0], fg[:, 0], ig[:, 1], fg[:, 1]], axis=1)
    return {
        'x_prompt': nrm((BATCH, SEQ, D), 1.0),
        'x_sample': nrm((DEC_BATCH, DEC_SEQ, D), 1.0),
        'cache_na_k': nrm((DEC_BATCH, DEPTH, NA_HEADS, PAST_LEN, NA_DH), 1.0),
        'cache_na_v': nrm((DEC_BATCH, DEPTH, NA_HEADS, PAST_LEN, NA_DH), 1.0),
        'cache_df_k': nrm((DEC_BATCH, DEPTH, DF_HEADS, 2, PAST_LEN, DF_DQK), 1.0),
        'cache_df_v': nrm((DEC_BATCH, DEPTH, DF_HEADS, PAST_LEN, DF_DV), 1.0),
        'state_ml_C': nrm((DEC_BATCH, DEPTH, 2, ML_HEADS, ML_DH, ML_DH), 0.3),
        'state_ml_n': nrm((DEC_BATCH, DEPTH, 2, ML_HEADS, ML_DH), 0.3),
        'state_ml_m': nrm((DEC_BATCH, DEPTH, 2, ML_HEADS), 0.5),
        'c': nrm((DEC_BATCH, D), 1.0),
        'c_ctx': nrm((D,), 1.0),
        'w_ada': nrm((DEPTH, D, 6 * D), 0.5 * D ** -0.5),
        'b_ada': nrm((DEPTH, 6 * D), 0.02),
        'g_norm1': 1.0 + nrm((DEPTH, D), 0.05),
        'g_norm2': 1.0 + nrm((DEPTH, D), 0.05),
        'w_in': nrm((DEPTH, D, IN_COLS), D ** -0.5),
        'w_out': nrm((DEPTH, MIX_W, D), MIX_W ** -0.5),
        'na_qn': 1.0 + nrm((DEPTH, NA_DH), 0.05),
        'na_kn': 1.0 + nrm((DEPTH, NA_DH), 0.05),
        'na_rpb': nrm((DEPTH, NA_HEADS, 2 * NA_WIN_R - 1, 2 * NA_WIN_C - 1), 0.02),
        'df_qn': 1.0 + nrm((DEPTH, DF_DQK), 0.05),
        'df_kn': 1.0 + nrm((DEPTH, DF_DQK), 0.05),
        'df_lam': nrm((DEPTH, 4, DF_DQK), 0.1),
        'df_subln': 1.0 + nrm((DEPTH, DF_DV), 0.05),
        'ml_gate_b': ml_gate_b,
        'ml_outn': 1.0 + nrm((DEPTH, ML_DH), 0.05),
        'w_router': nrm((DEPTH, D, N_EXPERTS), D ** -0.5),
        'b_router': nrm((DEPTH, N_EXPERTS), 0.01),
        'w_exp1': nrm((DEPTH, N_EXPERTS, D, D_EXPERT), D ** -0.5),
        'w_exp3': nrm((DEPTH, N_EXPERTS, D, D_EXPERT), D ** -0.5),
        'w_exp2': nrm((DEPTH, N_EXPERTS, D_EXPERT, D), D_EXPERT ** -0.5),
        'w_sh1': nrm((DEPTH, D, D_SHARED), D ** -0.5),
        'w_sh3': nrm((DEPTH, D, D_SHARED), D ** -0.5),
        'w_sh2': nrm((DEPTH, D_SHARED, D), D_SHARED ** -0.5),
    }


def reference(x_prompt, x_sample, cache_na_k, cache_na_v, cache_df_k, cache_df_v, state_ml_C, state_ml_n,
              state_ml_m, c, c_ctx, w_ada, b_ada, g_norm1, g_norm2, w_in, w_out, na_qn, na_kn, na_rpb,
              df_qn, df_kn, df_lam, df_subln, ml_gate_b, ml_outn, w_router, b_router, w_exp1, w_exp3, w_exp2,
              w_sh1, w_sh3, w_sh2):
    y_p = x_prompt
    y_s = x_sample
    outs = [[] for _ in range(7)]
    for l in range(DEPTH):
        lp = {
            'w_ada': w_ada[l], 'b_ada': b_ada[l], 'g_norm1': g_norm1[l], 'g_norm2': g_norm2[l],
            'w_in': w_in[l], 'w_out': w_out[l], 'na_qn': na_qn[l], 'na_kn': na_kn[l], 'na_rpb': na_rpb[l],
            'df_qn': df_qn[l], 'df_kn': df_kn[l], 'df_lam': df_lam[l], 'df_subln': df_subln[l],
            'ml_gate_b': ml_gate_b[l], 'ml_outn': ml_outn[l], 'w_router': w_router[l], 'b_router': b_router[l],
            'w_exp1': w_exp1[l], 'w_exp3': w_exp3[l], 'w_exp2': w_exp2[l],
            'w_sh1': w_sh1[l], 'w_sh3': w_sh3[l], 'w_sh2': w_sh2[l],
        }
        lam_init = 0.8 - 0.6 * math.exp(-0.3 * l)
        y_p, ctx_l = trunk_layer(y_p, c_ctx[None, :], None, lp, lam_init)
        for i in range(7):
            outs[i].append(ctx_l[i])
        cache_l = (cache_na_k[:, l], cache_na_v[:, l], cache_df_k[:, l], cache_df_v[:, l],
                   state_ml_C[:, l], state_ml_n[:, l], state_ml_m[:, l])
        y_s, _ = trunk_layer(y_s, c, cache_l, lp, lam_init)
    new_na_k = jnp.stack(outs[0], axis=1)
    new_na_v = jnp.stack(outs[1], axis=1)
    new_df_k = jnp.stack(outs[2], axis=1)
    new_df_v = jnp.stack(outs[3], axis=1)
    new_ml_C = jnp.stack(outs[4], axis=1)
    new_ml_n = jnp.stack(outs[5], axis=1)
    new_ml_m = jnp.stack(outs[6], axis=1)
    return (y_p, y_s, new_na_k, new_na_v, new_df_k, new_df_v, new_ml_C, new_ml_n, new_ml_m)
```

```python
import functools
import math

import numpy as np
import jax
import jax.numpy as jnp
from jax import lax
from jax.experimental import pallas as pl
from jax.experimental.pallas import tpu as pltpu

F32 = jnp.float32
BF16 = jnp.bfloat16

D_MODEL = 1024
DEPTH = 2
GRID_W = 64
NA_HEADS = 4
NA_DH = 64
NA_WIN_R = 8
NA_WIN_C = 16
DF_HEADS = 4
DF_DV = 64
DF_DQK = 32
ML_HEADS = 4
ML_DH = 128
ML_CHUNK = 128
N_EXPERTS = 32
TOP_K = 4
D_EXPERT = 256
ROUTED_SCALE = 2.5
ROPE_BASE = 10000.0
EPS = 1e-6

LANES = 128
IN_COLS = 3600
IN_COLS_PAD = 3712
GATE_COL0 = 3584
MASK_VALUE = -1e30
VMEM_LIMIT = 56 * 1024 * 1024

TM_PROJ = 512
TM_MOE = 1024
Q_BLOCK_DF = 256


def _cparams(n_axes):
    return pltpu.CompilerParams(dimension_semantics=("arbitrary",) * n_axes,
                                vmem_limit_bytes=VMEM_LIMIT)


def _silu(x):
    return x * jax.nn.sigmoid(x)


def _dot(a, b):
    return jnp.dot(a, b, preferred_element_type=F32)


def _dot_nt(a, b):
    return lax.dot_general(a, b, (((1,), (1,)), ((), ())), preferred_element_type=F32)


def _dot_tn(a, b):
    return lax.dot_general(a, b, (((0,), (0,)), ((), ())), preferred_element_type=F32)


def _lane_iota(shape):
    return lax.broadcasted_iota(jnp.int32, shape, 1)


def _mod_kernel(cond_ref, w_ref, b_ref, o_ref):
    s = _silu(cond_ref[...]).astype(BF16)
    o_ref[0] = _dot(s, w_ref[0].astype(BF16)) + b_ref[0]


def _modulation(cond, w_ada, b_ada):
    tn = 1536
    n = 6 * D_MODEL
    return pl.pallas_call(
        _mod_kernel,
        grid=(DEPTH, n // tn),
        in_specs=[pl.BlockSpec((8, D_MODEL), lambda l, j: (0, 0)),
                  pl.BlockSpec((1, D_MODEL, tn), lambda l, j: (l, 0, j)),
                  pl.BlockSpec((1, 1, tn), lambda l, j: (l, 0, j))],
        out_specs=pl.BlockSpec((1, 8, tn), lambda l, j: (l, 0, j)),
        out_shape=jax.ShapeDtypeStruct((DEPTH, 8, n), F32),
        compiler_params=_cparams(2),
        name="adaln_mod",
    )(cond, w_ada, b_ada.reshape(DEPTH, 1, n))


def _group_rms(v, gmat_ref, group):
    sq = v * v
    hi = sq.astype(BF16)
    lo = (sq - hi.astype(F32)).astype(BF16)
    ss = _dot(hi, gmat_ref[...]) + _dot(lo, gmat_ref[...])
    return v * lax.rsqrt(ss * (1.0 / group) + EPS)


def _rope(v, cos, sin_signed):
    first = (_lane_iota(v.shape) % 16) < 8
    partner = jnp.where(first, pltpu.roll(v, v.shape[1] - 8, 1), pltpu.roll(v, 8, 1))
    return v * cos + partner * sin_signed


def _inproj_kernel(*refs, rope):
    if rope:
        (x_ref, mod_ref, g1_ref, w_ref, qn_ref, kn_ref, dqn_ref, dkn_ref, gb_ref, g64_ref, g32_ref,
         cos_ref, sin_ref, naq_ref, nak_ref, nav_ref, dfq_ref, dfk_ref, dfv_ref,
         mq_ref, mk_ref, mv_ref, mo_ref, mg_ref) = refs
    else:
        (x_ref, mod_ref, g1_ref, w_ref, qn_ref, kn_ref, dqn_ref, dkn_ref, gb_ref, g64_ref, g32_ref,
         naq_ref, nak_ref, nav_ref, dfq_ref, dfk_ref, dfv_ref,
         mq_ref, mk_ref, mv_ref, mo_ref, mg_ref) = refs
    x = x_ref[...]
    y = x * lax.rsqrt(jnp.mean(x * x, axis=-1, keepdims=True) + EPS) * g1_ref[...]
    h = (y * (1.0 + mod_ref[0, 1:2, :]) + mod_ref[0, 0:1, :]).astype(BF16)

    def proj(a, b):
        return _dot(h, w_ref[:, a:b])

    naq_ref[...] = _group_rms(proj(0, 256), g64_ref, NA_DH) * qn_ref[...]
    nak_ref[...] = _group_rms(proj(256, 512), g64_ref, NA_DH) * kn_ref[...]
    nav_ref[...] = proj(512, 768)
    dfq = _group_rms(proj(768, 1024), g32_ref, DF_DQK) * dqn_ref[...]
    dfk = _group_rms(proj(1024, 1280), g32_ref, DF_DQK) * dkn_ref[...]
    if rope:
        dfq = _rope(dfq, cos_ref[...], sin_ref[...])
        dfk = _rope(dfk, cos_ref[...], sin_ref[...])
    dfq_ref[...] = dfq
    dfk_ref[...] = dfk
    dfv_ref[...] = proj(1280, 1536)
    mq_ref[...] = proj(1536, 2048)
    mk_ref[...] = proj(2048, 2560) * (ML_DH ** -0.5)
    mv_ref[...] = proj(2560, 3072)
    mo_ref[...] = proj(3072, 3584)
    g = proj(GATE_COL0, IN_COLS_PAD) + gb_ref[...]
    lane = _lane_iota(g.shape)
    is_forget = ((lane // ML_HEADS) % 2 == 1) & (lane < 4 * ML_HEADS)
    log_sig = jnp.minimum(g, 0.0) - jnp.log1p(jnp.exp(-jnp.abs(g)))
    mg_ref[...] = jnp.where(is_forget, log_sig, g)


def _inproj(x, mod_p, tokens_per_mod, g1, w_in_bf, qn, kn, dqn, dkn, gate_b, g64, g32, rope_tabs):
    T = x.shape[0]
    tm = TM_PROJ
    rope = rope_tabs is not None
    row = lambda i: (i, 0)
    const = lambda i: (0, 0)
    in_specs = [pl.BlockSpec((tm, D_MODEL), row),
                pl.BlockSpec((1, 6, D_MODEL), lambda i: ((i * tm) // tokens_per_mod, 0, 0)),
                pl.BlockSpec((1, D_MODEL), const),
                pl.BlockSpec((D_MODEL, IN_COLS_PAD), const),
                pl.BlockSpec((1, 256), const), pl.BlockSpec((1, 256), const),
                pl.BlockSpec((1, 256), const), pl.BlockSpec((1, 256), const),
                pl.BlockSpec((1, LANES), const),
                pl.BlockSpec((256, 256), const), pl.BlockSpec((256, 256), const)]
    args = [x, mod_p, g1, w_in_bf, qn, kn, dqn, dkn, gate_b, g64, g32]
    if rope:
        seq = rope_tabs[0].shape[0]
        tab = lambda i: (i % (seq // tm), 0)
        in_specs += [pl.BlockSpec((tm, 256), tab), pl.BlockSpec((tm, 256), tab)]
        args += list(rope_tabs)
    widths = [256] * 6 + [512] * 4 + [LANES]
    return pl.pallas_call(
        functools.partial(_inproj_kernel, rope=rope),
        grid=(T // tm,),
        in_specs=in_specs,
        out_specs=[pl.BlockSpec((tm, w), row) for w in widths],
        out_shape=[jax.ShapeDtypeStruct((T, w), F32) for w in widths],
        compiler_params=_cparams(1),
        name="inproj_rope" if rope else "inproj",
    )(*args)


def _df_lambda(lam_ref, lam_init):
    v = lam_ref[...]
    a = jnp.sum(v[0:1] * v[1:2], axis=1, keepdims=True)
    b = jnp.sum(v[2:3] * v[3:4], axis=1, keepdims=True)
    return jnp.exp(a) - jnp.exp(b) + lam_init


def _softmax_parts(scores):
    m = functools.reduce(jnp.maximum, [jnp.max(s, axis=-1, keepdims=True) for s in scores])
    es = [jnp.exp(s - m) for s in scores]
    tot = functools.reduce(lambda a, b: a + b, [jnp.sum(e, axis=-1, keepdims=True) for e in es])
    return es, 1.0 / tot


def _subln(o, subln_row, lam_init):
    lane = _lane_iota(o.shape)
    sq = o * o
    s0 = jnp.sum(jnp.where(lane < DF_DV, sq, 0.0), axis=-1, keepdims=True)
    s1 = jnp.sum(jnp.where(lane >= DF_DV, sq, 0.0), axis=-1, keepdims=True)
    ms = jnp.where(lane < DF_DV, s0, s1) * (1.0 / DF_DV)
    return o * lax.rsqrt(ms + EPS) * subln_row * (1.0 - lam_init)


def _diff_attn_tile(q, k_segs, v_segs, lam, t):
    sl = slice(LANES * t, LANES * (t + 1))
    qp = q[:, sl] * (DF_DQK ** -0.5)
    lane = _lane_iota(qp.shape)
    kps = [k[:, sl].astype(BF16) for k in k_segs]
    vps = [v[:, sl].astype(BF16) for v in v_segs]
    outs = []
    for j in range(2):
        ws = None
        for a in range(2):
            qm = jnp.where(lane // DF_DQK == 2 * j + a, qp, 0.0).astype(BF16)
            es, r = _softmax_parts([_dot_nt(qm, kp) for kp in kps])
            coef = r if a == 0 else -(lam * r)
            ws = [e * coef for e in es] if a == 0 else [w + e * coef for w, e in zip(ws, es)]
        o = functools.reduce(lambda x, y: x + y, [_dot(w.astype(BF16), vp) for w, vp in zip(ws, vps)])
        outs.append(o)
    return jnp.where(lane < DF_DV, outs[0], outs[1])


def _ctx_attn_kernel(naq_ref, nak_ref, nav_ref, dfq_ref, dfk_ref, dfv_ref, lam_ref, subln_ref,
                     nao_ref, dfo_ref, *, lam_init):
    lam = _df_lambda(lam_ref, lam_init)
    for t in range(2):
        sl = slice(LANES * t, LANES * (t + 1))
        qp = naq_ref[:, sl] * (NA_DH ** -0.5)
        kp = nak_ref[:, sl].astype(BF16)
        vp = nav_ref[:, sl].astype(BF16)
        lane = _lane_iota(qp.shape)
        outs = []
        for j in range(2):
            qm = jnp.where(lane // NA_DH == j, qp, 0.0).astype(BF16)
            (e,), r = _softmax_parts([_dot_nt(qm, kp)])
            outs.append(_dot(e.astype(BF16), vp) * r)
        nao_ref[:, sl] = jnp.where(lane < NA_DH, outs[0], outs[1])
        o = _diff_attn_tile(dfq_ref[...], [dfk_ref[...]], [dfv_ref[...]], lam, t)
        dfo_ref[:, sl] = _subln(o, subln_ref[:, sl], lam_init)


def _ctx_attention(naq, nak, nav, dfq, dfk, dfv, df_lam, subln, seq, lam_init):
    T = naq.shape[0]
    blk = pl.BlockSpec((seq, 256), lambda b: (b, 0))
    return pl.pallas_call(
        functools.partial(_ctx_attn_kernel, lam_init=lam_init),
        grid=(T // seq,),
        in_specs=[blk] * 6 + [pl.BlockSpec((4, DF_DQK), lambda b: (0, 0)),
                              pl.BlockSpec((1, 256), lambda b: (0, 0))],
        out_specs=[blk, blk],
        out_shape=[jax.ShapeDtypeStruct((T, 256), F32)] * 2,
        compiler_params=_cparams(1),
        name="ctx_attention",
    )(naq, nak, nav, dfq, dfk, dfv, df_lam, subln)


def _na_bias_kernel(rpb_ref, o_ref):
    hd = pl.program_id(0)
    ro0 = pl.program_id(1)
    shape = (GRID_W, NA_WIN_R * GRID_W)
    qi = lax.broadcasted_iota(jnp.int32, shape, 0)
    ci = lax.broadcasted_iota(jnp.int32, shape, 1)
    kc = ci % GRID_W
    rr = ci // GRID_W
    cs = jnp.clip(qi - NA_WIN_C // 2, 0, GRID_W - NA_WIN_C)
    inwin = (kc >= cs) & (kc < cs + NA_WIN_C)
    off = kc - qi + NA_WIN_C - 1
    n_c = 2 * NA_WIN_C - 1
    n_r = 2 * NA_WIN_R - 1
    val = jnp.full(shape, MASK_VALUE, F32)
    for r in range(NA_WIN_R):
        base = hd * (n_r * n_c) + (ro0 + r) * n_c
        for d in range(n_c):
            val = jnp.where((rr == r) & (off == d) & inwin, rpb_ref[base + d], val)
    o_ref[0, 0] = val


def _na_bias(rpb):
    n = NA_WIN_R * GRID_W
    return pl.pallas_call(
        _na_bias_kernel,
        grid=(NA_HEADS, NA_WIN_R),
        in_specs=[pl.BlockSpec(memory_space=pltpu.SMEM)],
        out_specs=pl.BlockSpec((1, 1, GRID_W, n), lambda h, r: (h, r, 0, 0)),
        out_shape=jax.ShapeDtypeStruct((NA_HEADS, NA_WIN_R, GRID_W, n), F32),
        compiler_params=_cparams(2),
        name="na_bias",
    )(rpb.reshape(-1))


def _lat_na_kernel(q_ref, k_ref, v_ref, kc_ref, vc_ref, bias_ref, o_ref, *, rows):
    kc = kc_ref[0].astype(BF16)
    vc = vc_ref[0].astype(BF16)
    nwin = NA_WIN_R * GRID_W

    def body(r, carry):
        rs = jnp.clip(r - NA_WIN_R // 2, 0, rows - NA_WIN_R)
        ro0 = rs - r + NA_WIN_R - 1
        q0 = pl.multiple_of(r * GRID_W, GRID_W)
        k0 = pl.multiple_of(rs * GRID_W, GRID_W)
        q = q_ref[pl.ds(q0, GRID_W), :] * (NA_DH ** -0.5)
        kw = k_ref[pl.ds(k0, nwin), :].astype(BF16)
        vw = v_ref[pl.ds(k0, nwin), :].astype(BF16)
        for t in range(2):
            sl = slice(LANES * t, LANES * (t + 1))
            qp = q[:, sl]
            lane = _lane_iota(qp.shape)
            outs = []
            for j in range(2):
                qm = jnp.where(lane // NA_DH == j, qp, 0.0).astype(BF16)
                s_loc = _dot_nt(qm, kw[:, sl]) + bias_ref[2 * t + j, ro0]
                s_ctx = _dot_nt(qm, kc[:, sl])
                (e_loc, e_ctx), rcp = _softmax_parts([s_loc, s_ctx])
                o = _dot(e_loc.astype(BF16), vw[:, sl]) + _dot(e_ctx.astype(BF16), vc[:, sl])
                outs.append(o * rcp)
            o_ref[pl.ds(q0, GRID_W), sl] = jnp.where(lane < NA_DH, outs[0], outs[1])
        return carry

    lax.fori_loop(0, rows, body, 0)


def _lat_na(q, k, v, kc, vc, bias, seq):
    T = q.shape[0]
    past = kc.shape[1]
    nwin = NA_WIN_R * GRID_W
    blk = pl.BlockSpec((seq, 256), lambda b: (b, 0))
    cblk = pl.BlockSpec((1, past, 256), lambda b: (b, 0, 0))
    return pl.pallas_call(
        functools.partial(_lat_na_kernel, rows=seq // GRID_W),
        grid=(T // seq,),
        in_specs=[blk, blk, blk, cblk, cblk,
                  pl.BlockSpec((NA_HEADS, NA_WIN_R, GRID_W, nwin), lambda b: (0, 0, 0, 0))],
        out_specs=blk,
        out_shape=jax.ShapeDtypeStruct((T, 256), F32),
        compiler_params=_cparams(1),
        name="lat_na",
    )(q, k, v, kc, vc, bias)


def _lat_df_kernel(q_ref, k_ref, v_ref, kc_ref, vc_ref, lam_ref, subln_ref, o_ref, *, lam_init):
    lam = _df_lambda(lam_ref, lam_init)
    for t in range(2):
        sl = slice(LANES * t, LANES * (t + 1))
        o = _diff_attn_tile(q_ref[...], [k_ref[...], kc_ref[0]], [v_ref[...], vc_ref[0]], lam, t)
        o_ref[:, sl] = _subln(o, subln_ref[:, sl], lam_init)


def _lat_df(q, k, v, kc, vc, df_lam, subln, seq, lam_init):
    T = q.shape[0]
    past = kc.shape[1]
    nq = seq // Q_BLOCK_DF
    qblk = pl.BlockSpec((Q_BLOCK_DF, 256), lambda b, i: (b * nq + i, 0))
    kblk = pl.BlockSpec((seq, 256), lambda b, i: (b, 0))
    cblk = pl.BlockSpec((1, past, 256), lambda b, i: (b, 0, 0))
    return pl.pallas_call(
        functools.partial(_lat_df_kernel, lam_init=lam_init),
        grid=(T // seq, nq),
        in_specs=[qblk, kblk, kblk, cblk, cblk,
                  pl.BlockSpec((4, DF_DQK), lambda b, i: (0, 0)),
                  pl.BlockSpec((1, 256), lambda b, i: (0, 0))],
        out_specs=qblk,
        out_shape=jax.ShapeDtypeStruct((T, 256), F32),
        compiler_params=_cparams(2),
        name="lat_df",
    )(q, k, v, kc, vc, df_lam, subln)


def _mlstm_kernel(*refs, nc, has_init, emit_state):
    refs = list(refs)
    q_ref, k_ref, v_ref, g_ref = refs[:4]
    pos = 4
    if has_init:
        c0_ref, m0_ref = refs[pos:pos + 2]
        pos += 2
    h_ref = refs[pos]
    pos += 1
    if emit_state:
        cn_ref, mn_ref = refs[pos:pos + 2]
        pos += 2
    caug_sc, m_sc = refs[pos:pos + 2]

    if has_init:
        caug_sc[...] = c0_ref[0]
        m_sc[...] = m0_ref[0]
    else:
        caug_sc[...] = jnp.zeros(caug_sc.shape, F32)
        m_sc[...] = jnp.zeros(m_sc.shape, F32)
    h_ref[...] = jnp.zeros(h_ref.shape, F32)

    ti = lax.broadcasted_iota(jnp.int32, (ML_CHUNK, ML_CHUNK), 0)
    si = lax.broadcasted_iota(jnp.int32, (ML_CHUNK, ML_CHUNK), 1)
    masks = (si <= ti, si >= ti)
    ones_col = jnp.where(_lane_iota((ML_CHUNK, LANES)) == 0, 1.0, 0.0).astype(BF16)

    def body(c, carry):
        for d in range(2):
            ci = c if d == 0 else nc - 1 - c
            start = pl.multiple_of(ci * ML_CHUNK, ML_CHUNK)
            rows = pl.ds(start, ML_CHUNK)
            mask = masks[d]
            tri = jnp.where(mask, 1.0, 0.0).astype(BF16)
            gblk = g_ref[rows, :]
            hi = gblk.astype(BF16)
            lo = (gblk - hi.astype(F32)).astype(BF16)
            bc = _dot(tri, hi) + _dot(tri, lo)
            g_t = gblk.T
            bc_t = bc.T
            last = ML_CHUNK - 1 if d == 0 else 0
            for hd in range(ML_HEADS):
                ch_i = 2 * ML_HEADS * d + hd
                ch_f = ch_i + ML_HEADS
                idx = ML_HEADS * d + hd
                hs = slice(ML_DH * hd, ML_DH * (hd + 1))
                ic_col = gblk[:, ch_i:ch_i + 1]
                ic_row = g_t[ch_i:ch_i + 1, :]
                b_col = bc[:, ch_f:ch_f + 1]
                b_row = bc_t[ch_f:ch_f + 1, :]
                m_prev = m_sc[idx][0:1, 0:1]
                log_d = jnp.where(mask, b_col - b_row + ic_row, -jnp.inf)
                m_t = jnp.maximum(b_col + m_prev, jnp.max(log_d, axis=1, keepdims=True))
                dw = jnp.exp(log_d - m_t)
                w0 = jnp.exp(b_col + m_prev - m_t)
                qh = q_ref[rows, hs].astype(BF16)
                kf = k_ref[rows, hs]
                vaug = jnp.concatenate([v_ref[rows, hs].astype(BF16), ones_col], axis=1)
                s = _dot_nt(qh, kf.astype(BF16)) * dw
                caug = caug_sc[idx]
                r = w0 * _dot(qh, caug.astype(BF16)) + _dot(s.astype(BF16), vaug)
                den = jnp.maximum(jnp.abs(r[:, ML_DH:ML_DH + 1]), jnp.exp(-m_t))
                h_ref[rows, hs] += r[:, :ML_DH] / den
                m_new = m_t[last:last + 1, :]
                b_last = b_col[last:last + 1, :]
                ws = jnp.exp(b_last - b_col + ic_col - m_new)
                decay = jnp.exp(b_last + m_prev - m_new)
                caug_sc[idx] = decay * caug + _dot_tn((ws * kf).astype(BF16), vaug)
                m_sc[idx] = jnp.broadcast_to(m_new, m_sc.shape[1:])
        return carry

    lax.fori_loop(0, nc, body, 0)
    if emit_state:
        cn_ref[0] = caug_sc[...]
        mn_ref[0] = m_sc[...]


def _mlstm(q, k, v, g, seq, init=None, emit_state=False):
    T = q.shape[0]
    n_chain = 2 * ML_HEADS
    blk = pl.BlockSpec((seq, 512), lambda b: (b, 0))
    gblk = pl.BlockSpec((seq, LANES), lambda b: (b, 0))
    cblk = pl.BlockSpec((1, n_chain, ML_DH, 2 * ML_DH), lambda b: (b, 0, 0, 0))
    mblk = pl.BlockSpec((1, n_chain, 8, LANES), lambda b: (b, 0, 0, 0))
    in_specs = [blk, blk, blk, gblk]
    args = [q, k, v, g]
    if init is not None:
        in_specs += [cblk, mblk]
        args += list(init)
    out_specs = [blk]
    out_shape = [jax.ShapeDtypeStruct((T, 512), F32)]
    nb = T // seq
    if emit_state:
        out_specs += [cblk, mblk]
        out_shape += [jax.ShapeDtypeStruct((nb, n_chain, ML_DH, 2 * ML_DH), F32),
                      jax.ShapeDtypeStruct((nb, n_chain, 8, LANES), F32)]
    return pl.pallas_call(
        functools.partial(_mlstm_kernel, nc=seq // ML_CHUNK, has_init=init is not None,
                          emit_state=emit_state),
        grid=(nb,),
        in_specs=in_specs,
        out_specs=out_specs,
        out_shape=out_shape,
        scratch_shapes=[pltpu.VMEM((n_chain, ML_DH, 2 * ML_DH), F32),
                        pltpu.VMEM((n_chain, 8, LANES), F32)],
        compiler_params=_cparams(1),
        name="mlstm_state" if emit_state else "mlstm",
    )(*args)


def _outproj_kernel(x_ref, nao_ref, dfo_ref, mlh_ref, mo_ref, mod_ref, g2_ref, outn_ref, wout_ref,
                    wr_ref, br_ref, x1_ref, h2_ref, gates_ref):
    mlh = mlh_ref[...]
    parts = []
    for hd in range(ML_HEADS):
        seg = mlh[:, ML_DH * hd:ML_DH * (hd + 1)]
        parts.append(seg * lax.rsqrt(jnp.mean(seg * seg, axis=-1, keepdims=True) + EPS))
    ml_o = jnp.concatenate(parts, axis=1) * outn_ref[...] * jax.nn.sigmoid(mo_ref[...])
    mix = (_dot(nao_ref[...].astype(BF16), wout_ref[0:256, :])
           + _dot(dfo_ref[...].astype(BF16), wout_ref[256:512, :])
           + _dot(ml_o.astype(BF16), wout_ref[512:1024, :]))
    x1 = x_ref[...] + mod_ref[0, 2:3, :] * mix
    x1_ref[...] = x1
    y = x1 * lax.rsqrt(jnp.mean(x1 * x1, axis=-1, keepdims=True) + EPS) * g2_ref[...]
    h2 = y * (1.0 + mod_ref[0, 4:5, :]) + mod_ref[0, 3:4, :]
    h2_ref[...] = h2.astype(BF16)

    logits = jnp.dot(h2, wr_ref[...], precision=lax.Precision.HIGHEST, preferred_element_type=F32)
    s = jax.nn.sigmoid(logits)
    work = s + br_ref[...]
    lane = _lane_iota(work.shape)
    s_sel = jnp.zeros(work.shape, F32)
    for _ in range(TOP_K):
        mx = jnp.max(work, axis=-1, keepdims=True)
        first = jnp.min(jnp.where(work == mx, lane, N_EXPERTS), axis=-1, keepdims=True)
        hit = lane == first
        s_sel = jnp.where(hit, s, s_sel)
        work = jnp.where(hit, -jnp.inf, work)
    gates_ref[...] = s_sel / jnp.sum(s_sel, axis=-1, keepdims=True) * ROUTED_SCALE


def _outproj(x, nao, dfo, mlh, mo, mod_p, tokens_per_mod, g2, outn, w_out_bf, w_router, b_router):
    T = x.shape[0]
    tm = TM_PROJ
    row = lambda i: (i, 0)
    const = lambda i: (0, 0)
    return pl.pallas_call(
        _outproj_kernel,
        grid=(T // tm,),
        in_specs=[pl.BlockSpec((tm, D_MODEL), row),
                  pl.BlockSpec((tm, 256), row), pl.BlockSpec((tm, 256), row),
                  pl.BlockSpec((tm, 512), row), pl.BlockSpec((tm, 512), row),
                  pl.BlockSpec((1, 6, D_MODEL), lambda i: ((i * tm) // tokens_per_mod, 0, 0)),
                  pl.BlockSpec((1, D_MODEL), const), pl.BlockSpec((1, 512), const),
                  pl.BlockSpec((D_MODEL, D_MODEL), const),
                  pl.BlockSpec((D_MODEL, N_EXPERTS), const), pl.BlockSpec((1, N_EXPERTS), const)],
        out_specs=[pl.BlockSpec((tm, D_MODEL), row), pl.BlockSpec((tm, D_MODEL), row),
                   pl.BlockSpec((tm, N_EXPERTS), row)],
        out_shape=[jax.ShapeDtypeStruct((T, D_MODEL), F32), jax.ShapeDtypeStruct((T, D_MODEL), BF16),
                   jax.ShapeDtypeStruct((T, N_EXPERTS), F32)],
        compiler_params=_cparams(1),
        name="outproj_router",
    )(x, nao, dfo, mlh, mo, mod_p, g2, outn, w_out_bf, w_router, b_router)


def _moe_kernel(h_ref, gates_ref, x1_ref, mod_ref, w1_ref, w3_ref, w2_ref, s1_ref, s3_ref, s2_ref, o_ref):
    e = pl.program_id(1)
    h = h_ref[...]

    @pl.when(e == 0)
    def _():
        a = _dot(h, s1_ref[...].astype(BF16))
        b = _dot(h, s3_ref[...].astype(BF16))
        o_ref[...] = _dot((_silu(a) * b).astype(BF16), s2_ref[...].astype(BF16))

    a = _dot(h, w1_ref[0].astype(BF16))
    b = _dot(h, w3_ref[0].astype(BF16))
    gates = gates_ref[...]
    g = jnp.sum(jnp.where(_lane_iota(gates.shape) == e, gates, 0.0), axis=-1, keepdims=True)
    hid = (_silu(a) * b * g).astype(BF16)
    o_ref[...] += _dot(hid, w2_ref[0].astype(BF16))

    @pl.when(e == N_EXPERTS - 1)
    def _():
        o_ref[...] = x1_ref[...] + mod_ref[0, 5:6, :] * o_ref[...]


def _moe(h2, gates, x1, mod_p, tokens_per_mod, w1, w3, w2, s1, s3, s2):
    T = h2.shape[0]
    tm = TM_MOE
    row = lambda i, e: (i, 0)
    const = lambda i, e: (0, 0)
    return pl.pallas_call(
        _moe_kernel,
        grid=(T // tm, N_EXPERTS),
        in_specs=[pl.BlockSpec((tm, D_MODEL), row), pl.BlockSpec((tm, N_EXPERTS), row),
                  pl.BlockSpec((tm, D_MODEL), row),
                  pl.BlockSpec((1, 6, D_MODEL), lambda i, e: ((i * tm) // tokens_per_mod, 0, 0)),
                  pl.BlockSpec((1, D_MODEL, D_EXPERT), lambda i, e: (e, 0, 0)),
                  pl.BlockSpec((1, D_MODEL, D_EXPERT), lambda i, e: (e, 0, 0)),
                  pl.BlockSpec((1, D_EXPERT, D_MODEL), lambda i, e: (e, 0, 0)),
                  pl.BlockSpec((D_MODEL, D_EXPERT), const), pl.BlockSpec((D_MODEL, D_EXPERT), const),
                  pl.BlockSpec((D_EXPERT, D_MODEL), const)],
        out_specs=pl.BlockSpec((tm, D_MODEL), row),
        out_shape=jax.ShapeDtypeStruct((T, D_MODEL), F32),
        compiler_params=_cparams(2),
        name="moe_experts",
    )(h2, gates, x1, mod_p, w1, w3, w2, s1, s3, s2)


def _group_matrix(group):
    i = np.arange(256)
    return jnp.asarray((i[:, None] // group) == (i[None, :] // group), BF16)


def _rope_tables(seq):
    t = np.arange(seq)
    lane = np.arange(256)
    j = lane % DF_DQK
    nf = DF_DQK // 4
    pos = jnp.where((j < DF_DQK // 2)[None, :], (t // GRID_W)[:, None], (t % GRID_W)[:, None]).astype(F32)
    inv = 1.0 / (ROPE_BASE ** (jnp.arange(nf, dtype=F32) / nf))
    ang = pos * inv[j % nf][None, :]
    sign = jnp.where((lane % (2 * nf)) < nf, -1.0, 1.0).astype(F32)
    return jnp.cos(ang), jnp.sin(ang) * sign[None, :]


def _heads(a, batch, seq, n):
    return a.reshape(batch, seq, n, -1).transpose(0, 2, 1, 3)


def _token_major(a):
    b, h, l, d = a.shape
    return a.transpose(0, 2, 1, 3).reshape(b, l, h * d)


def kernel(x_prompt, x_sample, cache_na_k, cache_na_v, cache_df_k, cache_df_v, state_ml_C, state_ml_n,
           state_ml_m, c, c_ctx, w_ada, b_ada, g_norm1, g_norm2, w_in, w_out, na_qn, na_kn, na_rpb,
           df_qn, df_kn, df_lam, df_subln, ml_gate_b, ml_outn, w_router, b_router, w_exp1, w_exp3, w_exp2,
           w_sh1, w_sh3, w_sh2):
    bc, lc, _ = x_prompt.shape
    bs, ls, _ = x_sample.shape
    past = cache_na_k.shape[3]

    cond = jnp.zeros((8, D_MODEL), F32).at[0].set(c_ctx).at[1:1 + bs].set(c)
    mod = _modulation(cond, w_ada, b_ada)
    g64 = _group_matrix(NA_DH)
    g32 = _group_matrix(DF_DQK)
    rope_tabs = _rope_tables(ls)

    y_p = x_prompt.reshape(bc * lc, D_MODEL)
    y_s = x_sample.reshape(bs * ls, D_MODEL)
    outs = [[] for _ in range(7)]
    for l in range(DEPTH):
        lam_init = 0.8 - 0.6 * math.exp(-0.3 * l)
        mod_l = mod[l].reshape(8, 6, D_MODEL)
        w_in_bf = jnp.pad(w_in[l].astype(BF16), ((0, 0), (0, IN_COLS_PAD - IN_COLS)))
        w_out_bf = w_out[l].astype(BF16)
        g1 = g_norm1[l][None, :]
        g2 = g_norm2[l][None, :]
        qn = jnp.tile(na_qn[l], NA_HEADS)[None, :]
        kn = jnp.tile(na_kn[l], NA_HEADS)[None, :]
        dqn = jnp.tile(df_qn[l], 2 * DF_HEADS)[None, :]
        dkn = jnp.tile(df_kn[l], 2 * DF_HEADS)[None, :]
        subln = jnp.tile(df_subln[l], DF_HEADS)[None, :]
        outn = jnp.tile(ml_outn[l], ML_HEADS)[None, :]
        gate_b = jnp.pad(ml_gate_b[l].reshape(-1), (0, LANES - 4 * ML_HEADS))[None, :]
        br = b_router[l][None, :]
        proj_w = (g1, w_in_bf, qn, kn, dqn, dkn, gate_b, g64, g32)
        moe_w = (w_exp1[l], w_exp3[l], w_exp2[l], w_sh1[l], w_sh3[l], w_sh2[l])

        mod_c = mod_l[0:1]
        tpm = bc * lc
        naq, nak, nav, dfq, dfk, dfv, mq, mk, mv, mo, mg = _inproj(y_p, mod_c, tpm, *proj_w, None)
        nao, dfo = _ctx_attention(naq, nak, nav, dfq, dfk, dfv, df_lam[l], subln, lc, lam_init)
        mlh, c_n, m_n = _mlstm(mq, mk, mv, mg, lc, emit_state=True)
        x1, h2, gates = _outproj(y_p, nao, dfo, mlh, mo, mod_c, tpm, g2, outn, w_out_bf, w_router[l], br)
        y_p = _moe(h2, gates, x1, mod_c, tpm, *moe_w)
        outs[0].append(_heads(nak, bc, lc, NA_HEADS))
        outs[1].append(_heads(nav, bc, lc, NA_HEADS))
        outs[2].append(dfk.reshape(bc, lc, DF_HEADS, 2, DF_DQK).transpose(0, 2, 3, 1, 4))
        outs[3].append(_heads(dfv, bc, lc, DF_HEADS))
        c_n = c_n.reshape(bc, 2, ML_HEADS, ML_DH, 2 * ML_DH)
        outs[4].append(c_n[..., :ML_DH])
        outs[5].append(c_n[..., ML_DH])
        outs[6].append(m_n[:, :, 0, 0].reshape(bc, 2, ML_HEADS))

        mod_s = mod_l[1:1 + bs]
        naq, nak, nav, dfq, dfk, dfv, mq, mk, mv, mo, mg = _inproj(y_s, mod_s, ls, *proj_w, rope_tabs)
        kc_na = _token_major(cache_na_k[:, l])
        vc_na = _token_major(cache_na_v[:, l])
        kc_df = cache_df_k[:, l].transpose(0, 3, 1, 2, 4).reshape(bs, past, 256)
        vc_df = _token_major(cache_df_v[:, l])
        nao = _lat_na(naq, nak, nav, kc_na, vc_na, _na_bias(na_rpb[l]), ls)
        dfo = _lat_df(dfq, dfk, dfv, kc_df, vc_df, df_lam[l], subln, ls, lam_init)
        c0 = jnp.concatenate([state_ml_C[:, l], state_ml_n[:, l][..., None],
                              jnp.zeros((bs, 2, ML_HEADS, ML_DH, ML_DH - 1), F32)], axis=-1)
        c0 = c0.reshape(bs, 2 * ML_HEADS, ML_DH, 2 * ML_DH)
        m0 = jnp.broadcast_to(state_ml_m[:, l].reshape(bs, 2 * ML_HEADS, 1, 1), (bs, 2 * ML_HEADS, 8, LANES))
        (mlh,) = _mlstm(mq, mk, mv, mg, ls, init=(c0, m0))
        x1, h2, gates = _outproj(y_s, nao, dfo, mlh, mo, mod_s, ls, g2, outn, w_out_bf, w_router[l], br)
        y_s = _moe(h2, gates, x1, mod_s, ls, *moe_w)

    stacked = [jnp.stack(o, axis=1) for o in outs]
    return (y_p.reshape(bc, lc, D_MODEL), y_s.reshape(bs, ls, D_MODEL), *stacked)
```

```python
import functools
import math

import numpy as np
import jax
import jax.numpy as jnp
from jax import lax
from jax.experimental import pallas as pl
from jax.experimental.pallas import tpu as pltpu

F32 = jnp.float32
BF16 = jnp.bfloat16

D_MODEL = 1024
DEPTH = 2
GRID_W = 64
NA_HEADS = 4
NA_DH = 64
NA_WIN_R = 8
NA_WIN_C = 16
DF_HEADS = 4
DF_DV = 64
DF_DQK = 32
ML_HEADS = 4
ML_DH = 128
ML_CHUNK = 128
N_EXPERTS = 32
TOP_K = 4
D_EXPERT = 256
ROUTED_SCALE = 2.5
ROPE_BASE = 10000.0
EPS = 1e-6

LANES = 128
IN_COLS = 3600
IN_COLS_PAD = 3712
GATE_COL0 = 3584
MASK_VALUE = -1e30
VMEM_LIMIT = 56 * 1024 * 1024

TM_PROJ = 512
TM_MOE = 1024
MOE_EXPERT_BLOCK = 2
Q_BLOCK_DF = 256


def _cparams(n_axes):
    return pltpu.CompilerParams(dimension_semantics=("arbitrary",) * n_axes,
                                vmem_limit_bytes=VMEM_LIMIT)


def _silu(x):
    return x * jax.nn.sigmoid(x)


def _dot(a, b):
    return jnp.dot(a, b, preferred_element_type=F32)


def _dot_nt(a, b):
    return lax.dot_general(a, b, (((1,), (1,)), ((), ())), preferred_element_type=F32)


def _dot_tn(a, b):
    return lax.dot_general(a, b, (((0,), (0,)), ((), ())), preferred_element_type=F32)


def _lane_iota(shape):
    return lax.broadcasted_iota(jnp.int32, shape, 1)


def _mod_kernel(cond_ref, w_ref, b_ref, o_ref):
    s = _silu(cond_ref[...]).astype(BF16)
    o_ref[0] = _dot(s, w_ref[0].astype(BF16)) + b_ref[0]


def _modulation(cond, w_ada, b_ada):
    tn = 1536
    n = 6 * D_MODEL
    return pl.pallas_call(
        _mod_kernel,
        grid=(DEPTH, n // tn),
        in_specs=[pl.BlockSpec((8, D_MODEL), lambda l, j: (0, 0)),
                  pl.BlockSpec((1, D_MODEL, tn), lambda l, j: (l, 0, j)),
                  pl.BlockSpec((1, 1, tn), lambda l, j: (l, 0, j))],
        out_specs=pl.BlockSpec((1, 8, tn), lambda l, j: (l, 0, j)),
        out_shape=jax.ShapeDtypeStruct((DEPTH, 8, n), F32),
        compiler_params=_cparams(2),
        name="adaln_mod",
    )(cond, w_ada, b_ada.reshape(DEPTH, 1, n))


def _group_rms(v, gmat_ref, group):
    sq = v * v
    hi = sq.astype(BF16)
    lo = (sq - hi.astype(F32)).astype(BF16)
    ss = _dot(hi, gmat_ref[...]) + _dot(lo, gmat_ref[...])
    return v * lax.rsqrt(ss * (1.0 / group) + EPS)


def _rope(v, cos, sin_signed):
    first = (_lane_iota(v.shape) % 16) < 8
    partner = jnp.where(first, pltpu.roll(v, v.shape[1] - 8, 1), pltpu.roll(v, 8, 1))
    return v * cos + partner * sin_signed


def _inproj_kernel(*refs, rope):
    if rope:
        (x_ref, mod_ref, g1_ref, w_ref, qn_ref, kn_ref, dqn_ref, dkn_ref, gb_ref, g64_ref, g32_ref,
         cos_ref, sin_ref, naq_ref, nak_ref, nav_ref, dfq_ref, dfk_ref, dfv_ref,
         mq_ref, mk_ref, mv_ref, mo_ref, mg_ref) = refs
    else:
        (x_ref, mod_ref, g1_ref, w_ref, qn_ref, kn_ref, dqn_ref, dkn_ref, gb_ref, g64_ref, g32_ref,
         naq_ref, nak_ref, nav_ref, dfq_ref, dfk_ref, dfv_ref,
         mq_ref, mk_ref, mv_ref, mo_ref, mg_ref) = refs
    x = x_ref[...]
    y = x * lax.rsqrt(jnp.mean(x * x, axis=-1, keepdims=True) + EPS) * g1_ref[...]
    h = (y * (1.0 + mod_ref[0, 1:2, :]) + mod_ref[0, 0:1, :]).astype(BF16)

    def proj(a, b):
        return _dot(h, w_ref[:, a:b])

    naq_ref[...] = _group_rms(proj(0, 256), g64_ref, NA_DH) * qn_ref[...]
    nak_ref[...] = _group_rms(proj(256, 512), g64_ref, NA_DH) * kn_ref[...]
    nav_ref[...] = proj(512, 768)
    dfq = _group_rms(proj(768, 1024), g32_ref, DF_DQK) * dqn_ref[...]
    dfk = _group_rms(proj(1024, 1280), g32_ref, DF_DQK) * dkn_ref[...]
    if rope:
        dfq = _rope(dfq, cos_ref[...], sin_ref[...])
        dfk = _rope(dfk, cos_ref[...], sin_ref[...])
    dfq_ref[...] = dfq
    dfk_ref[...] = dfk
    dfv_ref[...] = proj(1280, 1536)
    mq_ref[...] = proj(1536, 2048)
    mk_ref[...] = proj(2048, 2560) * (ML_DH ** -0.5)
    mv_ref[...] = proj(2560, 3072)
    mo_ref[...] = proj(3072, 3584)
    g = proj(GATE_COL0, IN_COLS_PAD) + gb_ref[...]
    lane = _lane_iota(g.shape)
    is_forget = ((lane // ML_HEADS) % 2 == 1) & (lane < 4 * ML_HEADS)
    log_sig = jnp.minimum(g, 0.0) - jnp.log1p(jnp.exp(-jnp.abs(g)))
    mg_ref[...] = jnp.where(is_forget, log_sig, g)


def _inproj(x, mod_p, tokens_per_mod, g1, w_in_bf, qn, kn, dqn, dkn, gate_b, g64, g32, rope_tabs):
    T = x.shape[0]
    tm = TM_PROJ
    rope = rope_tabs is not None
    row = lambda i: (i, 0)
    const = lambda i: (0, 0)
    in_specs = [pl.BlockSpec((tm, D_MODEL), row),
                pl.BlockSpec((1, 6, D_MODEL), lambda i: ((i * tm) // tokens_per_mod, 0, 0)),
                pl.BlockSpec((1, D_MODEL), const),
                pl.BlockSpec((D_MODEL, IN_COLS_PAD), const),
                pl.BlockSpec((1, 256), const), pl.BlockSpec((1, 256), const),
                pl.BlockSpec((1, 256), const), pl.BlockSpec((1, 256), const),
                pl.BlockSpec((1, LANES), const),
                pl.BlockSpec((256, 256), const), pl.BlockSpec((256, 256), const)]
    args = [x, mod_p, g1, w_in_bf, qn, kn, dqn, dkn, gate_b, g64, g32]
    if rope:
        seq = rope_tabs[0].shape[0]
        tab = lambda i: (i % (seq // tm), 0)
        in_specs += [pl.BlockSpec((tm, 256), tab), pl.BlockSpec((tm, 256), tab)]
        args += list(rope_tabs)
    widths = [256] * 6 + [512] * 4 + [LANES]
    return pl.pallas_call(
        functools.partial(_inproj_kernel, rope=rope),
        grid=(T // tm,),
        in_specs=in_specs,
        out_specs=[pl.BlockSpec((tm, w), row) for w in widths],
        out_shape=[jax.ShapeDtypeStruct((T, w), F32) for w in widths],
        compiler_params=_cparams(1),
        name="inproj_rope" if rope else "inproj",
    )(*args)


def _df_lambda(lam_ref, lam_init):
    v = lam_ref[...]
    a = jnp.sum(v[0:1] * v[1:2], axis=1, keepdims=True)
    b = jnp.sum(v[2:3] * v[3:4], axis=1, keepdims=True)
    return jnp.exp(a) - jnp.exp(b) + lam_init


def _softmax_parts(scores):
    m = functools.reduce(jnp.maximum, [jnp.max(s, axis=-1, keepdims=True) for s in scores])
    es = [jnp.exp(s - m) for s in scores]
    tot = functools.reduce(lambda a, b: a + b, [jnp.sum(e, axis=-1, keepdims=True) for e in es])
    return es, 1.0 / tot


def _subln(o, subln_row, lam_init):
    lane = _lane_iota(o.shape)
    sq = o * o
    s0 = jnp.sum(jnp.where(lane < DF_DV, sq, 0.0), axis=-1, keepdims=True)
    s1 = jnp.sum(jnp.where(lane >= DF_DV, sq, 0.0), axis=-1, keepdims=True)
    ms = jnp.where(lane < DF_DV, s0, s1) * (1.0 / DF_DV)
    return o * lax.rsqrt(ms + EPS) * subln_row * (1.0 - lam_init)


def _diff_attn_tile(q, k_segs, v_segs, lam, t):
    sl = slice(LANES * t, LANES * (t + 1))
    qp = q[:, sl] * (DF_DQK ** -0.5)
    lane = _lane_iota(qp.shape)
    kps = [k[:, sl].astype(BF16) for k in k_segs]
    vps = [v[:, sl].astype(BF16) for v in v_segs]
    outs = []
    for j in range(2):
        ws = None
        for a in range(2):
            qm = jnp.where(lane // DF_DQK == 2 * j + a, qp, 0.0).astype(BF16)
            es, r = _softmax_parts([_dot_nt(qm, kp) for kp in kps])
            coef = r if a == 0 else -(lam * r)
            ws = [e * coef for e in es] if a == 0 else [w + e * coef for w, e in zip(ws, es)]
        o = functools.reduce(lambda x, y: x + y, [_dot(w.astype(BF16), vp) for w, vp in zip(ws, vps)])
        outs.append(o)
    return jnp.where(lane < DF_DV, outs[0], outs[1])


def _ctx_attn_kernel(naq_ref, nak_ref, nav_ref, dfq_ref, dfk_ref, dfv_ref, lam_ref, subln_ref,
                     nao_ref, dfo_ref, *, lam_init):
    lam = _df_lambda(lam_ref, lam_init)
    for t in range(2):
        sl = slice(LANES * t, LANES * (t + 1))
        qp = naq_ref[:, sl] * (NA_DH ** -0.5)
        kp = nak_ref[:, sl].astype(BF16)
        vp = nav_ref[:, sl].astype(BF16)
        lane = _lane_iota(qp.shape)
        outs = []
        for j in range(2):
            qm = jnp.where(lane // NA_DH == j, qp, 0.0).astype(BF16)
            (e,), r = _softmax_parts([_dot_nt(qm, kp)])
            outs.append(_dot(e.astype(BF16), vp) * r)
        nao_ref[:, sl] = jnp.where(lane < NA_DH, outs[0], outs[1])
        o = _diff_attn_tile(dfq_ref[...], [dfk_ref[...]], [dfv_ref[...]], lam, t)
        dfo_ref[:, sl] = _subln(o, subln_ref[:, sl], lam_init)


def _ctx_attention(naq, nak, nav, dfq, dfk, dfv, df_lam, subln, seq, lam_init):
    T = naq.shape[0]
    blk = pl.BlockSpec((seq, 256), lambda b: (b, 0))
    return pl.pallas_call(
        functools.partial(_ctx_attn_kernel, lam_init=lam_init),
        grid=(T // seq,),
        in_specs=[blk] * 6 + [pl.BlockSpec((4, DF_DQK), lambda b: (0, 0)),
                              pl.BlockSpec((1, 256), lambda b: (0, 0))],
        out_specs=[blk, blk],
        out_shape=[jax.ShapeDtypeStruct((T, 256), F32)] * 2,
        compiler_params=_cparams(1),
        name="ctx_attention",
    )(naq, nak, nav, dfq, dfk, dfv, df_lam, subln)


def _na_bias_kernel(rpb_ref, o_ref):
    hd = pl.program_id(0)
    ro0 = pl.program_id(1)
    shape = (GRID_W, NA_WIN_R * GRID_W)
    qi = lax.broadcasted_iota(jnp.int32, shape, 0)
    ci = lax.broadcasted_iota(jnp.int32, shape, 1)
    kc = ci % GRID_W
    cs = jnp.clip(qi - NA_WIN_C // 2, 0, GRID_W - NA_WIN_C)
    inwin = (kc >= cs) & (kc < cs + NA_WIN_C)
    off = jnp.where(inwin, kc - qi + NA_WIN_C - 1, -1)
    rr_row = lax.broadcasted_iota(jnp.int32, (1, shape[1]), 1) // GRID_W
    n_c = 2 * NA_WIN_C - 1
    n_r = 2 * NA_WIN_R - 1
    val = jnp.full(shape, MASK_VALUE, F32)
    for d in range(n_c):
        vrow = jnp.zeros((1, shape[1]), F32)
        for r in range(NA_WIN_R):
            vrow = jnp.where(rr_row == r, rpb_ref[hd * (n_r * n_c) + (ro0 + r) * n_c + d], vrow)
        val = jnp.where(off == d, vrow, val)
    o_ref[0, 0] = val


def _na_bias(rpb):
    n = NA_WIN_R * GRID_W
    return pl.pallas_call(
        _na_bias_kernel,
        grid=(NA_HEADS, NA_WIN_R),
        in_specs=[pl.BlockSpec(memory_space=pltpu.SMEM)],
        out_specs=pl.BlockSpec((1, 1, GRID_W, n), lambda h, r: (h, r, 0, 0)),
        out_shape=jax.ShapeDtypeStruct((NA_HEADS, NA_WIN_R, GRID_W, n), F32),
        compiler_params=_cparams(2),
        name="na_bias",
    )(rpb.reshape(-1))


def _lat_na_kernel(q_ref, k_ref, v_ref, kc_ref, vc_ref, bias_ref, o_ref, *, rows):
    kc = kc_ref[0].astype(BF16)
    vc = vc_ref[0].astype(BF16)
    nwin = NA_WIN_R * GRID_W

    def body(r, carry):
        rs = jnp.clip(r - NA_WIN_R // 2, 0, rows - NA_WIN_R)
        ro0 = rs - r + NA_WIN_R - 1
        q0 = pl.multiple_of(r * GRID_W, GRID_W)
        k0 = pl.multiple_of(rs * GRID_W, GRID_W)
        q = q_ref[pl.ds(q0, GRID_W), :] * (NA_DH ** -0.5)
        kw = k_ref[pl.ds(k0, nwin), :].astype(BF16)
        vw = v_ref[pl.ds(k0, nwin), :].astype(BF16)
        for t in range(2):
            sl = slice(LANES * t, LANES * (t + 1))
            qp = q[:, sl]
            lane = _lane_iota(qp.shape)
            outs = []
            for j in range(2):
                qm = jnp.where(lane // NA_DH == j, qp, 0.0).astype(BF16)
                s_loc = _dot_nt(qm, kw[:, sl]) + bias_ref[2 * t + j, ro0]
                s_ctx = _dot_nt(qm, kc[:, sl])
                (e_loc, e_ctx), rcp = _softmax_parts([s_loc, s_ctx])
                o = _dot(e_loc.astype(BF16), vw[:, sl]) + _dot(e_ctx.astype(BF16), vc[:, sl])
                outs.append(o * rcp)
            o_ref[pl.ds(q0, GRID_W), sl] = jnp.where(lane < NA_DH, outs[0], outs[1])
        return carry

    lax.fori_loop(0, rows, body, 0)


def _lat_na(q, k, v, kc, vc, bias, seq):
    T = q.shape[0]
    past = kc.shape[1]
    nwin = NA_WIN_R * GRID_W
    blk = pl.BlockSpec((seq, 256), lambda b: (b, 0))
    cblk = pl.BlockSpec((1, past, 256), lambda b: (b, 0, 0))
    return pl.pallas_call(
        functools.partial(_lat_na_kernel, rows=seq // GRID_W),
        grid=(T // seq,),
        in_specs=[blk, blk, blk, cblk, cblk,
                  pl.BlockSpec((NA_HEADS, NA_WIN_R, GRID_W, nwin), lambda b: (0, 0, 0, 0))],
        out_specs=blk,
        out_shape=jax.ShapeDtypeStruct((T, 256), F32),
        compiler_params=_cparams(1),
        name="lat_na",
    )(q, k, v, kc, vc, bias)


def _lat_df_kernel(q_ref, k_ref, v_ref, kc_ref, vc_ref, lam_ref, subln_ref, o_ref, *, lam_init):
    lam = _df_lambda(lam_ref, lam_init)
    for t in range(2):
        sl = slice(LANES * t, LANES * (t + 1))
        o = _diff_attn_tile(q_ref[...], [k_ref[...], kc_ref[0]], [v_ref[...], vc_ref[0]], lam, t)
        o_ref[:, sl] = _subln(o, subln_ref[:, sl], lam_init)


def _lat_df(q, k, v, kc, vc, df_lam, subln, seq, lam_init):
    T = q.shape[0]
    past = kc.shape[1]
    nq = seq // Q_BLOCK_DF
    qblk = pl.BlockSpec((Q_BLOCK_DF, 256), lambda b, i: (b * nq + i, 0))
    kblk = pl.BlockSpec((seq, 256), lambda b, i: (b, 0))
    cblk = pl.BlockSpec((1, past, 256), lambda b, i: (b, 0, 0))
    return pl.pallas_call(
        functools.partial(_lat_df_kernel, lam_init=lam_init),
        grid=(T // seq, nq),
        in_specs=[qblk, kblk, kblk, cblk, cblk,
                  pl.BlockSpec((4, DF_DQK), lambda b, i: (0, 0)),
                  pl.BlockSpec((1, 256), lambda b, i: (0, 0))],
        out_specs=qblk,
        out_shape=jax.ShapeDtypeStruct((T, 256), F32),
        compiler_params=_cparams(2),
        name="lat_df",
    )(q, k, v, kc, vc, df_lam, subln)


def _mlstm_kernel(*refs, nc, has_init, emit_state):
    refs = list(refs)
    q_ref, k_ref, v_ref, g_ref = refs[:4]
    pos = 4
    if has_init:
        c0_ref, m0_ref = refs[pos:pos + 2]
        pos += 2
    h_ref = refs[pos]
    pos += 1
    if emit_state:
        cn_ref, mn_ref = refs[pos:pos + 2]
        pos += 2
    caug_sc, m_sc = refs[pos:pos + 2]

    if has_init:
        caug_sc[...] = c0_ref[0]
        m_sc[...] = m0_ref[0]
    else:
        caug_sc[...] = jnp.zeros(caug_sc.shape, F32)
        m_sc[...] = jnp.zeros(m_sc.shape, F32)
    h_ref[...] = jnp.zeros(h_ref.shape, F32)

    ti = lax.broadcasted_iota(jnp.int32, (ML_CHUNK, ML_CHUNK), 0)
    si = lax.broadcasted_iota(jnp.int32, (ML_CHUNK, ML_CHUNK), 1)
    masks = (si <= ti, si >= ti)
    ones_blk = jnp.ones((ML_CHUNK, LANES), BF16)
    full = (ML_CHUNK, ML_CHUNK)

    def body(c, carry):
        for d in range(2):
            ci = c if d == 0 else nc - 1 - c
            start = pl.multiple_of(ci * ML_CHUNK, ML_CHUNK)
            rows = pl.ds(start, ML_CHUNK)
            mask = masks[d]
            tri = jnp.where(mask, 1.0, 0.0).astype(BF16)
            gblk = g_ref[rows, :]
            hi = gblk.astype(BF16)
            lo = (gblk - hi.astype(F32)).astype(BF16)
            bc = _dot(tri, hi) + _dot(tri, lo)
            g_t = gblk.T
            bc_t = bc.T
            last = ML_CHUNK - 1 if d == 0 else 0
            for hd in range(ML_HEADS):
                ch_i = 2 * ML_HEADS * d + hd
                ch_f = ch_i + ML_HEADS
                idx = ML_HEADS * d + hd
                hs = slice(ML_DH * hd, ML_DH * (hd + 1))
                b_col = bc[:, ch_f:ch_f + 1]
                a_col = gblk[:, ch_i:ch_i + 1] - b_col
                a_msk = jnp.where(mask, g_t[ch_i:ch_i + 1, :] - bc_t[ch_f:ch_f + 1, :], -jnp.inf)
                m_col = jnp.max(a_msk, axis=1, keepdims=True)
                m_rep = jnp.broadcast_to(m_col, full)
                b_rep = jnp.broadcast_to(b_col, full)
                a_rep = jnp.broadcast_to(a_col, full)
                m_last = m_col[last:last + 1, :]
                b_last = b_col[last:last + 1, :]
                qh = q_ref[rows, hs].astype(BF16)
                kf = k_ref[rows, hs]
                vaug = jnp.concatenate([v_ref[rows, hs].astype(BF16), ones_blk], axis=1)
                s0 = _dot_nt(qh, kf.astype(BF16)) * jnp.exp(a_msk - m_rep)
                sv0 = _dot(s0.astype(BF16), vaug)
                u0 = _dot_tn((jnp.exp(a_rep - m_last) * kf).astype(BF16), vaug)
                m_prev = m_sc[idx][0:1, 0:1]
                g_rep = jnp.maximum(m_prev, m_rep)
                w0 = jnp.exp(m_prev - g_rep)
                cc = jnp.exp(m_rep - g_rep)
                caug = caug_sc[idx]
                qc = _dot(qh, caug.astype(BF16))
                num = w0 * qc[:, :ML_DH] + cc * sv0[:, :ML_DH]
                den = w0 * qc[:, ML_DH:] + cc * sv0[:, ML_DH:]
                h_ref[rows, hs] += num / jnp.maximum(jnp.abs(den), jnp.exp(-b_rep - g_rep))
                g_last = jnp.maximum(m_prev, m_last)
                caug_sc[idx] = jnp.exp(m_prev - g_last) * caug + jnp.exp(m_last - g_last) * u0
                m_sc[idx] = jnp.broadcast_to(b_last + g_last, m_sc.shape[1:])
        return carry

    lax.fori_loop(0, nc, body, 0)
    if emit_state:
        cn_ref[0] = caug_sc[...]
        mn_ref[0] = m_sc[...]


def _mlstm(q, k, v, g, seq, init=None, emit_state=False):
    T = q.shape[0]
    n_chain = 2 * ML_HEADS
    blk = pl.BlockSpec((seq, 512), lambda b: (b, 0))
    gblk = pl.BlockSpec((seq, LANES), lambda b: (b, 0))
    cblk = pl.BlockSpec((1, n_chain, ML_DH, 2 * ML_DH), lambda b: (b, 0, 0, 0))
    mblk = pl.BlockSpec((1, n_chain, 8, LANES), lambda b: (b, 0, 0, 0))
    in_specs = [blk, blk, blk, gblk]
    args = [q, k, v, g]
    if init is not None:
        in_specs += [cblk, mblk]
        args += list(init)
    out_specs = [blk]
    out_shape = [jax.ShapeDtypeStruct((T, 512), F32)]
    nb = T // seq
    if emit_state:
        out_specs += [cblk, mblk]
        out_shape += [jax.ShapeDtypeStruct((nb, n_chain, ML_DH, 2 * ML_DH), F32),
                      jax.ShapeDtypeStruct((nb, n_chain, 8, LANES), F32)]
    return pl.pallas_call(
        functools.partial(_mlstm_kernel, nc=seq // ML_CHUNK, has_init=init is not None,
                          emit_state=emit_state),
        grid=(nb,),
        in_specs=in_specs,
        out_specs=out_specs,
        out_shape=out_shape,
        scratch_shapes=[pltpu.VMEM((n_chain, ML_DH, 2 * ML_DH), F32),
                        pltpu.VMEM((n_chain, 8, LANES), F32)],
        compiler_params=_cparams(1),
        name="mlstm_state" if emit_state else "mlstm",
    )(*args)


def _outproj_kernel(x_ref, nao_ref, dfo_ref, mlh_ref, mo_ref, mod_ref, g2_ref, outn_ref, wout_ref,
                    wr_ref, br_ref, x1_ref, h2_ref, gates_ref):
    mlh = mlh_ref[...]
    parts = []
    for hd in range(ML_HEADS):
        seg = mlh[:, ML_DH * hd:ML_DH * (hd + 1)]
        parts.append(seg * lax.rsqrt(jnp.mean(seg * seg, axis=-1, keepdims=True) + EPS))
    ml_o = jnp.concatenate(parts, axis=1) * outn_ref[...] * jax.nn.sigmoid(mo_ref[...])
    mix = (_dot(nao_ref[...].astype(BF16), wout_ref[0:256, :])
           + _dot(dfo_ref[...].astype(BF16), wout_ref[256:512, :])
           + _dot(ml_o.astype(BF16), wout_ref[512:1024, :]))
    x1 = x_ref[...] + mod_ref[0, 2:3, :] * mix
    x1_ref[...] = x1
    y = x1 * lax.rsqrt(jnp.mean(x1 * x1, axis=-1, keepdims=True) + EPS) * g2_ref[...]
    h2 = y * (1.0 + mod_ref[0, 4:5, :]) + mod_ref[0, 3:4, :]
    h2_ref[...] = h2.astype(BF16)

    logits = jnp.dot(h2, wr_ref[...], precision=lax.Precision.HIGHEST, preferred_element_type=F32)
    s = jax.nn.sigmoid(logits)
    work = s + br_ref[...]
    lane = _lane_iota(work.shape)
    s_sel = jnp.zeros(work.shape, F32)
    for _ in range(TOP_K):
        mx = jnp.max(work, axis=-1, keepdims=True)
        first = jnp.min(jnp.where(work == mx, lane, N_EXPERTS), axis=-1, keepdims=True)
        hit = lane == first
        s_sel = jnp.where(hit, s, s_sel)
        work = jnp.where(hit, -jnp.inf, work)
    gates_ref[...] = s_sel / jnp.sum(s_sel, axis=-1, keepdims=True) * ROUTED_SCALE


def _outproj(x, nao, dfo, mlh, mo, mod_p, tokens_per_mod, g2, outn, w_out_bf, w_router, b_router):
    T = x.shape[0]
    tm = TM_PROJ
    row = lambda i: (i, 0)
    const = lambda i: (0, 0)
    return pl.pallas_call(
        _outproj_kernel,
        grid=(T // tm,),
        in_specs=[pl.BlockSpec((tm, D_MODEL), row),
                  pl.BlockSpec((tm, 256), row), pl.BlockSpec((tm, 256), row),
                  pl.BlockSpec((tm, 512), row), pl.BlockSpec((tm, 512), row),
                  pl.BlockSpec((1, 6, D_MODEL), lambda i: ((i * tm) // tokens_per_mod, 0, 0)),
                  pl.BlockSpec((1, D_MODEL), const), pl.BlockSpec((1, 512), const),
                  pl.BlockSpec((D_MODEL, D_MODEL), const),
                  pl.BlockSpec((D_MODEL, N_EXPERTS), const), pl.BlockSpec((1, N_EXPERTS), const)],
        out_specs=[pl.BlockSpec((tm, D_MODEL), row), pl.BlockSpec((tm, D_MODEL), row),
                   pl.BlockSpec((tm, N_EXPERTS), row)],
        out_shape=[jax.ShapeDtypeStruct((T, D_MODEL), F32), jax.ShapeDtypeStruct((T, D_MODEL), BF16),
                   jax.ShapeDtypeStruct((T, N_EXPERTS), F32)],
        compiler_params=_cparams(1),
        name="outproj_router",
    )(x, nao, dfo, mlh, mo, mod_p, g2, outn, w_out_bf, w_router, b_router)


def _moe_kernel(h_ref, gates_ref, x1_ref, mod_ref, w1_ref, w3_ref, w2_ref, s1_ref, s3_ref, s2_ref, o_ref):
    e = pl.program_id(1)
    h = h_ref[...]

    @pl.when(e == 0)
    def _():
        a = _dot(h, s1_ref[...].astype(BF16))
        b = _dot(h, s3_ref[...].astype(BF16))
        o_ref[...] = _dot((_silu(a) * b).astype(BF16), s2_ref[...].astype(BF16))

    gates = gates_ref[...]
    lane = _lane_iota(gates.shape)
    hids = []
    for j in range(MOE_EXPERT_BLOCK):
        a = _dot(h, w1_ref[j].astype(BF16))
        b = _dot(h, w3_ref[j].astype(BF16))
        g = jnp.sum(jnp.where(lane == e * MOE_EXPERT_BLOCK + j, gates, 0.0), axis=-1, keepdims=True)
        hids.append((_silu(a) * b * g).astype(BF16))
    w2 = w2_ref[...].reshape(MOE_EXPERT_BLOCK * D_EXPERT, D_MODEL).astype(BF16)
    o_ref[...] += _dot(jnp.concatenate(hids, axis=1), w2)

    @pl.when(e == N_EXPERTS // MOE_EXPERT_BLOCK - 1)
    def _():
        o_ref[...] = x1_ref[...] + mod_ref[0, 5:6, :] * o_ref[...]


def _moe(h2, gates, x1, mod_p, tokens_per_mod, w1, w3, w2, s1, s3, s2):
    T = h2.shape[0]
    tm = TM_MOE
    eb = MOE_EXPERT_BLOCK
    row = lambda i, e: (i, 0)
    const = lambda i, e: (0, 0)
    return pl.pallas_call(
        _moe_kernel,
        grid=(T // tm, N_EXPERTS // eb),
        in_specs=[pl.BlockSpec((tm, D_MODEL), row), pl.BlockSpec((tm, N_EXPERTS), row),
                  pl.BlockSpec((tm, D_MODEL), row),
                  pl.BlockSpec((1, 6, D_MODEL), lambda i, e: ((i * tm) // tokens_per_mod, 0, 0)),
                  pl.BlockSpec((eb, D_MODEL, D_EXPERT), lambda i, e: (e, 0, 0)),
                  pl.BlockSpec((eb, D_MODEL, D_EXPERT), lambda i, e: (e, 0, 0)),
                  pl.BlockSpec((eb, D_EXPERT, D_MODEL), lambda i, e: (e, 0, 0)),
                  pl.BlockSpec((D_MODEL, D_EXPERT), const), pl.BlockSpec((D_MODEL, D_EXPERT), const),
                  pl.BlockSpec((D_EXPERT, D_MODEL), const)],
        out_specs=pl.BlockSpec((tm, D_MODEL), row),
        out_shape=jax.ShapeDtypeStruct((T, D_MODEL), F32),
        compiler_params=_cparams(2),
        name="moe_experts",
    )(h2, gates, x1, mod_p, w1, w3, w2, s1, s3, s2)


def _group_matrix(group):
    i = np.arange(256)
    return jnp.asarray((i[:, None] // group) == (i[None, :] // group), BF16)


def _rope_tables(seq):
    t = np.arange(seq)
    lane = np.arange(256)
    j = lane % DF_DQK
    nf = DF_DQK // 4
    pos = jnp.where((j < DF_DQK // 2)[None, :], (t // GRID_W)[:, None], (t % GRID_W)[:, None]).astype(F32)
    inv = 1.0 / (ROPE_BASE ** (jnp.arange(nf, dtype=F32) / nf))
    ang = pos * inv[j % nf][None, :]
    sign = jnp.where((lane % (2 * nf)) < nf, -1.0, 1.0).astype(F32)
    return jnp.cos(ang), jnp.sin(ang) * sign[None, :]


def _heads(a, batch, seq, n):
    return a.reshape(batch, seq, n, -1).transpose(0, 2, 1, 3)


def _token_major(a):
    b, h, l, d = a.shape
    return a.transpose(0, 2, 1, 3).reshape(b, l, h * d)


def kernel(x_prompt, x_sample, cache_na_k, cache_na_v, cache_df_k, cache_df_v, state_ml_C, state_ml_n,
           state_ml_m, c, c_ctx, w_ada, b_ada, g_norm1, g_norm2, w_in, w_out, na_qn, na_kn, na_rpb,
           df_qn, df_kn, df_lam, df_subln, ml_gate_b, ml_outn, w_router, b_router, w_exp1, w_exp3, w_exp2,
           w_sh1, w_sh3, w_sh2):
    bc, lc, _ = x_prompt.shape
    bs, ls, _ = x_sample.shape
    past = cache_na_k.shape[3]

    cond = jnp.zeros((8, D_MODEL), F32).at[0].set(c_ctx).at[1:1 + bs].set(c)
    mod = _modulation(cond, w_ada, b_ada)
    g64 = _group_matrix(NA_DH)
    g32 = _group_matrix(DF_DQK)
    rope_tabs = _rope_tables(ls)

    y_p = x_prompt.reshape(bc * lc, D_MODEL)
    y_s = x_sample.reshape(bs * ls, D_MODEL)
    outs = [[] for _ in range(7)]
    for l in range(DEPTH):
        lam_init = 0.8 - 0.6 * math.exp(-0.3 * l)
        mod_l = mod[l].reshape(8, 6, D_MODEL)
        w_in_bf = jnp.pad(w_in[l].astype(BF16), ((0, 0), (0, IN_COLS_PAD - IN_COLS)))
        w_out_bf = w_out[l].astype(BF16)
        g1 = g_norm1[l][None, :]
        g2 = g_norm2[l][None, :]
        qn = jnp.tile(na_qn[l], NA_HEADS)[None, :]
        kn = jnp.tile(na_kn[l], NA_HEADS)[None, :]
        dqn = jnp.tile(df_qn[l], 2 * DF_HEADS)[None, :]
        dkn = jnp.tile(df_kn[l], 2 * DF_HEADS)[None, :]
        subln = jnp.tile(df_subln[l], DF_HEADS)[None, :]
        outn = jnp.tile(ml_outn[l], ML_HEADS)[None, :]
        gate_b = jnp.pad(ml_gate_b[l].reshape(-1), (0, LANES - 4 * ML_HEADS))[None, :]
        br = b_router[l][None, :]
        proj_w = (g1, w_in_bf, qn, kn, dqn, dkn, gate_b, g64, g32)
        moe_w = (w_exp1[l], w_exp3[l], w_exp2[l], w_sh1[l], w_sh3[l], w_sh2[l])

        mod_c = mod_l[0:1]
        tpm = bc * lc
        naq, nak, nav, dfq, dfk, dfv, mq, mk, mv, mo, mg = _inproj(y_p, mod_c, tpm, *proj_w, None)
        nao, dfo = _ctx_attention(naq, nak, nav, dfq, dfk, dfv, df_lam[l], subln, lc, lam_init)
        mlh, c_n, m_n = _mlstm(mq, mk, mv, mg, lc, emit_state=True)
        x1, h2, gates = _outproj(y_p, nao, dfo, mlh, mo, mod_c, tpm, g2, outn, w_out_bf, w_router[l], br)
        y_p = _moe(h2, gates, x1, mod_c, tpm, *moe_w)
        outs[0].append(_heads(nak, bc, lc, NA_HEADS))
        outs[1].append(_heads(nav, bc, lc, NA_HEADS))
        outs[2].append(dfk.reshape(bc, lc, DF_HEADS, 2, DF_DQK).transpose(0, 2, 3, 1, 4))
        outs[3].append(_heads(dfv, bc, lc, DF_HEADS))
        c_n = c_n.reshape(bc, 2, ML_HEADS, ML_DH, 2 * ML_DH)
        outs[4].append(c_n[..., :ML_DH])
        outs[5].append(c_n[..., ML_DH])
        outs[6].append(m_n[:, :, 0, 0].reshape(bc, 2, ML_HEADS))

        mod_s = mod_l[1:1 + bs]
        naq, nak, nav, dfq, dfk, dfv, mq, mk, mv, mo, mg = _inproj(y_s, mod_s, ls, *proj_w, rope_tabs)
        kc_na = _token_major(cache_na_k[:, l])
        vc_na = _token_major(cache_na_v[:, l])
        kc_df = cache_df_k[:, l].transpose(0, 3, 1, 2, 4).reshape(bs, past, 256)
        vc_df = _token_major(cache_df_v[:, l])
        nao = _lat_na(naq, nak, nav, kc_na, vc_na, _na_bias(na_rpb[l]), ls)
        dfo = _lat_df(dfq, dfk, dfv, kc_df, vc_df, df_lam[l], subln, ls, lam_init)
        n0 = jnp.broadcast_to(state_ml_n[:, l][..., None], (bs, 2, ML_HEADS, ML_DH, ML_DH))
        c0 = jnp.concatenate([state_ml_C[:, l], n0], axis=-1)
        c0 = c0.reshape(bs, 2 * ML_HEADS, ML_DH, 2 * ML_DH)
        m0 = jnp.broadcast_to(state_ml_m[:, l].reshape(bs, 2 * ML_HEADS, 1, 1), (bs, 2 * ML_HEADS, 8, LANES))
        (mlh,) = _mlstm(mq, mk, mv, mg, ls, init=(c0, m0))
        x1, h2, gates = _outproj(y_s, nao, dfo, mlh, mo, mod_s, ls, g2, outn, w_out_bf, w_router[l], br)
        y_s = _moe(h2, gates, x1, mod_s, ls, *moe_w)

    stacked = [jnp.stack(o, axis=1) for o in outs]
    return (y_p.reshape(bc, lc, D_MODEL), y_s.reshape(bs, ls, D_MODEL), *stacked)
```

```python
import functools
import math

import numpy as np
import jax
import jax.numpy as jnp
from jax import lax
from jax.experimental import pallas as pl
from jax.experimental.pallas import tpu as pltpu

F32 = jnp.float32
BF16 = jnp.bfloat16

D_MODEL = 1024
DEPTH = 2
GRID_W = 64
NA_HEADS = 4
NA_DH = 64
NA_WIN_R = 8
NA_WIN_C = 16
DF_HEADS = 4
DF_DV = 64
DF_DQK = 32
ML_HEADS = 4
ML_DH = 128
ML_CHUNK = 128
N_EXPERTS = 32
TOP_K = 4
D_EXPERT = 256
ROUTED_SCALE = 2.5
ROPE_BASE = 10000.0
EPS = 1e-6

LANES = 128
IN_COLS = 3600
IN_COLS_PAD = 3712
GATE_COL0 = 3584
MASK_VALUE = -1e30
VMEM_LIMIT = 56 * 1024 * 1024

TM_PROJ = 512
TM_MOE = 1024
MOE_EXPERT_BLOCK = 2
Q_BLOCK_DF = 256
CTX_ATTN_BATCHES = 2
LOOP_UNROLL = 2
LOG2E = math.log2(math.e)


def _cparams(n_axes):
    return pltpu.CompilerParams(dimension_semantics=("arbitrary",) * n_axes,
                                vmem_limit_bytes=VMEM_LIMIT)


def _silu(x):
    return x * jax.nn.sigmoid(x)


def _dot(a, b):
    return jnp.dot(a, b, preferred_element_type=F32)


def _dot_nt(a, b):
    return lax.dot_general(a, b, (((1,), (1,)), ((), ())), preferred_element_type=F32)


def _dot_tn(a, b):
    return lax.dot_general(a, b, (((0,), (0,)), ((), ())), preferred_element_type=F32)


def _lane_iota(shape):
    return lax.broadcasted_iota(jnp.int32, shape, 1)


def _mod_kernel(cond_ref, w_ref, b_ref, o_ref):
    s = _silu(cond_ref[...]).astype(BF16)
    o_ref[0] = _dot(s, w_ref[0].astype(BF16)) + b_ref[0]


def _modulation(cond, w_ada, b_ada):
    tn = 1536
    n = 6 * D_MODEL
    return pl.pallas_call(
        _mod_kernel,
        grid=(DEPTH, n // tn),
        in_specs=[pl.BlockSpec((8, D_MODEL), lambda l, j: (0, 0)),
                  pl.BlockSpec((1, D_MODEL, tn), lambda l, j: (l, 0, j)),
                  pl.BlockSpec((1, 1, tn), lambda l, j: (l, 0, j))],
        out_specs=pl.BlockSpec((1, 8, tn), lambda l, j: (l, 0, j)),
        out_shape=jax.ShapeDtypeStruct((DEPTH, 8, n), F32),
        compiler_params=_cparams(2),
        name="adaln_mod",
    )(cond, w_ada, b_ada.reshape(DEPTH, 1, n))


def _group_rms(v, gmat_ref, group):
    sq = v * v
    hi = sq.astype(BF16)
    lo = (sq - hi.astype(F32)).astype(BF16)
    ss = _dot(hi, gmat_ref[...]) + _dot(lo, gmat_ref[...])
    return v * lax.rsqrt(ss * (1.0 / group) + EPS)


def _rope(v, cos, sin_signed):
    first = (_lane_iota(v.shape) % 16) < 8
    partner = jnp.where(first, pltpu.roll(v, v.shape[1] - 8, 1), pltpu.roll(v, 8, 1))
    return v * cos + partner * sin_signed


def _inproj_kernel(*refs, rope, cache_seq):
    (x_ref, mod_ref, g1_ref, w_ref, qn_ref, kn_ref, dqn_ref, dkn_ref, gb_ref, g64_ref, g32_ref) = refs[:11]
    pos = 11
    if rope:
        cos_ref, sin_ref = refs[pos:pos + 2]
        pos += 2
    (naq_ref, nak_ref, nav_ref, dfq_ref, dfk_ref, dfv_ref,
     mq_ref, mk_ref, mv_ref, mo_ref, mg_ref) = refs[pos:pos + 11]
    pos += 11
    x = x_ref[...]
    y = x * lax.rsqrt(jnp.mean(x * x, axis=-1, keepdims=True) + EPS) * g1_ref[...]
    h = (y * (1.0 + mod_ref[0, 1:2, :]) + mod_ref[0, 0:1, :]).astype(BF16)

    def proj(a, b):
        return _dot(h, w_ref[:, a:b])

    naq_ref[...] = _group_rms(proj(0, 256), g64_ref, NA_DH) * qn_ref[...]
    nak = _group_rms(proj(256, 512), g64_ref, NA_DH) * kn_ref[...]
    nav = proj(512, 768)
    dfq = _group_rms(proj(768, 1024), g32_ref, DF_DQK) * dqn_ref[...]
    dfk = _group_rms(proj(1024, 1280), g32_ref, DF_DQK) * dkn_ref[...]
    dfv = proj(1280, 1536)
    if rope:
        dfq = _rope(dfq, cos_ref[...], sin_ref[...])
        dfk = _rope(dfk, cos_ref[...], sin_ref[...])
    nak_ref[...] = nak
    nav_ref[...] = nav
    dfq_ref[...] = dfq
    dfk_ref[...] = dfk
    dfv_ref[...] = dfv
    if cache_seq:
        nk_hm, nv_hm, dk_hm, dv_hm = refs[pos:pos + 4]
        for bi in range(x.shape[0] // cache_seq):
            rs = slice(bi * cache_seq, (bi + 1) * cache_seq)
            for hd in range(NA_HEADS):
                nk_hm[bi, hd] = nak[rs, NA_DH * hd:NA_DH * (hd + 1)]
                nv_hm[bi, hd] = nav[rs, NA_DH * hd:NA_DH * (hd + 1)]
                dv_hm[bi, hd] = dfv[rs, DF_DV * hd:DF_DV * (hd + 1)]
                for a in range(2):
                    c0 = (2 * hd + a) * DF_DQK
                    dk_hm[bi, hd, a] = dfk[rs, c0:c0 + DF_DQK]
    mq_ref[...] = proj(1536, 2048)
    mk_ref[...] = proj(2048, 2560) * (ML_DH ** -0.5)
    mv_ref[...] = proj(2560, 3072)
    mo_ref[...] = proj(3072, 3584)
    g = proj(GATE_COL0, IN_COLS_PAD) + gb_ref[...]
    lane = _lane_iota(g.shape)
    is_forget = ((lane // ML_HEADS) % 2 == 1) & (lane < 4 * ML_HEADS)
    log_sig = jnp.minimum(g, 0.0) - jnp.log1p(jnp.exp(-jnp.abs(g)))
    mg_ref[...] = jnp.where(is_forget, log_sig, g)


def _inproj(x, mod_p, tokens_per_mod, layer, g1, w_in_bf, qn, kn, dqn, dkn, gate_b, g64, g32, rope_tabs,
            cache_seq=0):
    T = x.shape[0]
    tm = TM_PROJ
    rope = rope_tabs is not None
    row = lambda i: (i, 0)
    const = lambda i: (0, 0)
    in_specs = [pl.BlockSpec((tm, D_MODEL), row),
                pl.BlockSpec((1, 6, D_MODEL), lambda i: ((i * tm) // tokens_per_mod, 0, 0)),
                pl.BlockSpec((1, D_MODEL), const),
                pl.BlockSpec((None, D_MODEL, IN_COLS_PAD), lambda i: (layer, 0, 0)),
                pl.BlockSpec((1, 256), const), pl.BlockSpec((1, 256), const),
                pl.BlockSpec((1, 256), const), pl.BlockSpec((1, 256), const),
                pl.BlockSpec((1, LANES), const),
                pl.BlockSpec((256, 256), const), pl.BlockSpec((256, 256), const)]
    args = [x, mod_p, g1, w_in_bf, qn, kn, dqn, dkn, gate_b, g64, g32]
    if rope:
        seq = rope_tabs[0].shape[0]
        tab = lambda i: (i % (seq // tm), 0)
        in_specs += [pl.BlockSpec((tm, 256), tab), pl.BlockSpec((tm, 256), tab)]
        args += list(rope_tabs)
    widths = [256] * 6 + [512] * 4 + [LANES]
    out_specs = [pl.BlockSpec((tm, w), row) for w in widths]
    out_shape = [jax.ShapeDtypeStruct((T, w), F32) for w in widths]
    if cache_seq:
        nb = tm // cache_seq
        hm = (nb, NA_HEADS, cache_seq, NA_DH)
        hm_df = (nb, DF_HEADS, 2, cache_seq, DF_DQK)
        for blk in (hm, hm, hm_df, hm):
            out_specs.append(pl.BlockSpec(blk, lambda i, n=len(blk): (i,) + (0,) * (n - 1)))
            out_shape.append(jax.ShapeDtypeStruct((T // cache_seq,) + blk[1:], F32))
    return pl.pallas_call(
        functools.partial(_inproj_kernel, rope=rope, cache_seq=cache_seq),
        grid=(T // tm,),
        in_specs=in_specs,
        out_specs=out_specs,
        out_shape=out_shape,
        compiler_params=_cparams(1),
        name="inproj_rope" if rope else "inproj",
    )(*args)


def _df_lambda(lam_ref, lam_init):
    v = lam_ref[...]
    a = jnp.sum(v[0:1] * v[1:2], axis=1, keepdims=True)
    b = jnp.sum(v[2:3] * v[3:4], axis=1, keepdims=True)
    return jnp.exp(a) - jnp.exp(b) + lam_init


def _softmax_parts(scores):
    m = functools.reduce(jnp.maximum, [jnp.max(s, axis=-1, keepdims=True) for s in scores])
    es = [jnp.exp2(s - m) for s in scores]
    tot = functools.reduce(lambda a, b: a + b, [jnp.sum(e, axis=-1, keepdims=True) for e in es])
    return es, 1.0 / tot


def _subln(o, subln_row, lam_init):
    lane = _lane_iota(o.shape)
    sq = o * o
    s0 = jnp.sum(jnp.where(lane < DF_DV, sq, 0.0), axis=-1, keepdims=True)
    s1 = jnp.sum(jnp.where(lane >= DF_DV, sq, 0.0), axis=-1, keepdims=True)
    ms = jnp.where(lane < DF_DV, s0, s1) * (1.0 / DF_DV)
    return o * lax.rsqrt(ms + EPS) * subln_row * (1.0 - lam_init)


def _diff_attn_tile(q, k_segs, v_segs, lam, t):
    sl = slice(LANES * t, LANES * (t + 1))
    qp = q[:, sl] * (DF_DQK ** -0.5 * LOG2E)
    lane = _lane_iota(qp.shape)
    kps = [k[:, sl].astype(BF16) for k in k_segs]
    vps = [v[:, sl].astype(BF16) for v in v_segs]
    outs = []
    for j in range(2):
        ws = None
        for a in range(2):
            qm = jnp.where(lane // DF_DQK == 2 * j + a, qp, 0.0).astype(BF16)
            es, r = _softmax_parts([_dot_nt(qm, kp) for kp in kps])
            coef = r if a == 0 else -(lam * r)
            ws = [e * coef for e in es] if a == 0 else [w + e * coef for w, e in zip(ws, es)]
        o = functools.reduce(lambda x, y: x + y, [_dot(w.astype(BF16), vp) for w, vp in zip(ws, vps)])
        outs.append(o)
    return jnp.where(lane < DF_DV, outs[0], outs[1])


def _ctx_attn_kernel(naq_ref, nak_ref, nav_ref, dfq_ref, dfk_ref, dfv_ref, lam_ref, subln_ref,
                     nao_ref, dfo_ref, *, lam_init, seq):
    lam = _df_lambda(lam_ref, lam_init)
    for bi in range(naq_ref.shape[0] // seq):
        rs = slice(bi * seq, (bi + 1) * seq)
        for t in range(2):
            sl = slice(LANES * t, LANES * (t + 1))
            qp = naq_ref[rs, sl] * (NA_DH ** -0.5 * LOG2E)
            kp = nak_ref[rs, sl].astype(BF16)
            vp = nav_ref[rs, sl].astype(BF16)
            lane = _lane_iota(qp.shape)
            outs = []
            for j in range(2):
                qm = jnp.where(lane // NA_DH == j, qp, 0.0).astype(BF16)
                (e,), r = _softmax_parts([_dot_nt(qm, kp)])
                outs.append(_dot(e.astype(BF16), vp) * r)
            nao_ref[rs, sl] = jnp.where(lane < NA_DH, outs[0], outs[1])
            o = _diff_attn_tile(dfq_ref[rs, :], [dfk_ref[rs, :]], [dfv_ref[rs, :]], lam, t)
            dfo_ref[rs, sl] = _subln(o, subln_ref[:, sl], lam_init)


def _ctx_attention(naq, nak, nav, dfq, dfk, dfv, df_lam, subln, seq, lam_init):
    T = naq.shape[0]
    rows = CTX_ATTN_BATCHES * seq
    blk = pl.BlockSpec((rows, 256), lambda b: (b, 0))
    return pl.pallas_call(
        functools.partial(_ctx_attn_kernel, lam_init=lam_init, seq=seq),
        grid=(T // rows,),
        in_specs=[blk] * 6 + [pl.BlockSpec((4, DF_DQK), lambda b: (0, 0)),
                              pl.BlockSpec((1, 256), lambda b: (0, 0))],
        out_specs=[blk, blk],
        out_shape=[jax.ShapeDtypeStruct((T, 256), F32)] * 2,
        compiler_params=_cparams(1),
        name="ctx_attention",
    )(naq, nak, nav, dfq, dfk, dfv, df_lam, subln)


def _na_bias_kernel(rpb_ref, o_ref):
    hd = pl.program_id(0)
    ro0 = pl.program_id(1)
    shape = (GRID_W, NA_WIN_R * GRID_W)
    qi = lax.broadcasted_iota(jnp.int32, shape, 0)
    ci = lax.broadcasted_iota(jnp.int32, shape, 1)
    kc = ci % GRID_W
    cs = jnp.clip(qi - NA_WIN_C // 2, 0, GRID_W - NA_WIN_C)
    inwin = (kc >= cs) & (kc < cs + NA_WIN_C)
    off = jnp.where(inwin, kc - qi + NA_WIN_C - 1, -1)
    rr_row = lax.broadcasted_iota(jnp.int32, (1, shape[1]), 1) // GRID_W
    n_c = 2 * NA_WIN_C - 1
    n_r = 2 * NA_WIN_R - 1
    val = jnp.full(shape, MASK_VALUE, F32)
    for d in range(n_c):
        vrow = jnp.zeros((1, shape[1]), F32)
        for r in range(NA_WIN_R):
            vrow = jnp.where(rr_row == r, rpb_ref[hd * (n_r * n_c) + (ro0 + r) * n_c + d], vrow)
        val = jnp.where(off == d, vrow, val)
    o_ref[0, 0] = val * LOG2E


def _na_bias(rpb):
    n = NA_WIN_R * GRID_W
    return pl.pallas_call(
        _na_bias_kernel,
        grid=(NA_HEADS, NA_WIN_R),
        in_specs=[pl.BlockSpec(memory_space=pltpu.SMEM)],
        out_specs=pl.BlockSpec((1, 1, GRID_W, n), lambda h, r: (h, r, 0, 0)),
        out_shape=jax.ShapeDtypeStruct((NA_HEADS, NA_WIN_R, GRID_W, n), F32),
        compiler_params=_cparams(2),
        name="na_bias",
    )(rpb.reshape(-1))


def _lat_na_kernel(q_ref, k_ref, v_ref, kc_ref, vc_ref, bias_ref, o_ref, *, rows):
    kc = kc_ref[0].astype(BF16)
    vc = vc_ref[0].astype(BF16)
    nwin = NA_WIN_R * GRID_W

    def body(r, carry):
        rs = jnp.clip(r - NA_WIN_R // 2, 0, rows - NA_WIN_R)
        ro0 = rs - r + NA_WIN_R - 1
        q0 = pl.multiple_of(r * GRID_W, GRID_W)
        k0 = pl.multiple_of(rs * GRID_W, GRID_W)
        q = q_ref[pl.ds(q0, GRID_W), :] * (NA_DH ** -0.5 * LOG2E)
        kw = k_ref[pl.ds(k0, nwin), :].astype(BF16)
        vw = v_ref[pl.ds(k0, nwin), :].astype(BF16)
        for t in range(2):
            sl = slice(LANES * t, LANES * (t + 1))
            qp = q[:, sl]
            lane = _lane_iota(qp.shape)
            outs = []
            for j in range(2):
                qm = jnp.where(lane // NA_DH == j, qp, 0.0).astype(BF16)
                s_loc = _dot_nt(qm, kw[:, sl]) + bias_ref[2 * t + j, ro0]
                s_ctx = _dot_nt(qm, kc[:, sl])
                (e_loc, e_ctx), rcp = _softmax_parts([s_loc, s_ctx])
                o = _dot(e_loc.astype(BF16), vw[:, sl]) + _dot(e_ctx.astype(BF16), vc[:, sl])
                outs.append(o * rcp)
            o_ref[pl.ds(q0, GRID_W), sl] = jnp.where(lane < NA_DH, outs[0], outs[1])
        return carry

    lax.fori_loop(0, rows, body, 0, unroll=LOOP_UNROLL)


def _lat_na(q, k, v, kc, vc, bias, seq):
    T = q.shape[0]
    past = kc.shape[1]
    nwin = NA_WIN_R * GRID_W
    blk = pl.BlockSpec((seq, 256), lambda b: (b, 0))
    cblk = pl.BlockSpec((1, past, 256), lambda b: (b, 0, 0))
    return pl.pallas_call(
        functools.partial(_lat_na_kernel, rows=seq // GRID_W),
        grid=(T // seq,),
        in_specs=[blk, blk, blk, cblk, cblk,
                  pl.BlockSpec((NA_HEADS, NA_WIN_R, GRID_W, nwin), lambda b: (0, 0, 0, 0))],
        out_specs=blk,
        out_shape=jax.ShapeDtypeStruct((T, 256), F32),
        compiler_params=_cparams(1),
        name="lat_na",
    )(q, k, v, kc, vc, bias)


def _lat_df_kernel(q_ref, k_ref, v_ref, kc_ref, vc_ref, lam_ref, subln_ref, o_ref, *, lam_init):
    lam = _df_lambda(lam_ref, lam_init)
    for t in range(2):
        sl = slice(LANES * t, LANES * (t + 1))
        o = _diff_attn_tile(q_ref[...], [k_ref[...], kc_ref[0]], [v_ref[...], vc_ref[0]], lam, t)
        o_ref[:, sl] = _subln(o, subln_ref[:, sl], lam_init)


def _lat_df(q, k, v, kc, vc, df_lam, subln, seq, lam_init):
    T = q.shape[0]
    past = kc.shape[1]
    nq = seq // Q_BLOCK_DF
    qblk = pl.BlockSpec((Q_BLOCK_DF, 256), lambda b, i: (b * nq + i, 0))
    kblk = pl.BlockSpec((seq, 256), lambda b, i: (b, 0))
    cblk = pl.BlockSpec((1, past, 256), lambda b, i: (b, 0, 0))
    return pl.pallas_call(
        functools.partial(_lat_df_kernel, lam_init=lam_init),
        grid=(T // seq, nq),
        in_specs=[qblk, kblk, kblk, cblk, cblk,
                  pl.BlockSpec((4, DF_DQK), lambda b, i: (0, 0)),
                  pl.BlockSpec((1, 256), lambda b, i: (0, 0))],
        out_specs=qblk,
        out_shape=jax.ShapeDtypeStruct((T, 256), F32),
        compiler_params=_cparams(2),
        name="lat_df",
    )(q, k, v, kc, vc, df_lam, subln)


def _mlstm_kernel(*refs, nc, has_init, emit_state):
    refs = list(refs)
    q_ref, k_ref, v_ref, g_ref = refs[:4]
    pos = 4
    if has_init:
        c0_ref, m0_ref = refs[pos:pos + 2]
        pos += 2
    h_ref = refs[pos]
    pos += 1
    if emit_state:
        cn_ref, nn_ref, mn_ref = refs[pos:pos + 3]
        pos += 3
    caug_sc, m_sc = refs[pos:pos + 2]

    if has_init:
        caug_sc[...] = c0_ref[0]
        m_sc[...] = m0_ref[0]
    else:
        caug_sc[...] = jnp.zeros(caug_sc.shape, F32)
        m_sc[...] = jnp.zeros(m_sc.shape, F32)
    h_ref[...] = jnp.zeros(h_ref.shape, F32)

    ti = lax.broadcasted_iota(jnp.int32, (ML_CHUNK, ML_CHUNK), 0)
    si = lax.broadcasted_iota(jnp.int32, (ML_CHUNK, ML_CHUNK), 1)
    masks = (si <= ti, si >= ti)
    ones_blk = jnp.ones((ML_CHUNK, LANES), BF16)
    full = (ML_CHUNK, ML_CHUNK)

    def body(c, carry):
        for d in range(2):
            ci = c if d == 0 else nc - 1 - c
            start = pl.multiple_of(ci * ML_CHUNK, ML_CHUNK)
            rows = pl.ds(start, ML_CHUNK)
            mask = masks[d]
            tri = jnp.where(mask, 1.0, 0.0).astype(BF16)
            gblk = g_ref[rows, :]
            hi = gblk.astype(BF16)
            lo = (gblk - hi.astype(F32)).astype(BF16)
            bc = _dot(tri, hi) + _dot(tri, lo)
            g_t = gblk.T
            bc_t = bc.T
            last = ML_CHUNK - 1 if d == 0 else 0
            for hd in range(ML_HEADS):
                ch_i = 2 * ML_HEADS * d + hd
                ch_f = ch_i + ML_HEADS
                idx = ML_HEADS * d + hd
                hs = slice(ML_DH * hd, ML_DH * (hd + 1))
                b_col = bc[:, ch_f:ch_f + 1]
                a_col = gblk[:, ch_i:ch_i + 1] - b_col
                a_msk = jnp.where(mask, g_t[ch_i:ch_i + 1, :] - bc_t[ch_f:ch_f + 1, :], -jnp.inf)
                m_col = jnp.max(a_msk, axis=1, keepdims=True)
                m_rep = jnp.broadcast_to(m_col, full)
                b_rep = jnp.broadcast_to(b_col, full)
                a_rep = jnp.broadcast_to(a_col, full)
                m_last = m_col[last:last + 1, :]
                b_last = b_col[last:last + 1, :]
                qh = q_ref[rows, hs].astype(BF16)
                kf = k_ref[rows, hs]
                vaug = jnp.concatenate([v_ref[rows, hs].astype(BF16), ones_blk], axis=1)
                s0 = _dot_nt(qh, kf.astype(BF16)) * jnp.exp(a_msk - m_rep)
                sv0 = _dot(s0.astype(BF16), vaug)
                u0 = _dot_tn((jnp.exp(a_rep - m_last) * kf).astype(BF16), vaug)
                m_prev = m_sc[idx][0:1, 0:1]
                g_rep = jnp.maximum(m_prev, m_rep)
                w0 = jnp.exp(m_prev - g_rep)
                cc = jnp.exp(m_rep - g_rep)
                caug = caug_sc[idx]
                qc = _dot(qh, caug.astype(BF16))
                num = w0 * qc[:, :ML_DH] + cc * sv0[:, :ML_DH]
                den = w0 * qc[:, ML_DH:] + cc * sv0[:, ML_DH:]
                h_ref[rows, hs] += num / jnp.maximum(jnp.abs(den), jnp.exp(-b_rep - g_rep))
                g_last = jnp.maximum(m_prev, m_last)
                caug_sc[idx] = jnp.exp(m_prev - g_last) * caug + jnp.exp(m_last - g_last) * u0
                m_sc[idx] = jnp.broadcast_to(b_last + g_last, m_sc.shape[1:])
        return carry

    lax.fori_loop(0, nc, body, 0, unroll=LOOP_UNROLL)
    if emit_state:
        cn_ref[0] = caug_sc[:, :, :ML_DH]
        for idx in range(2 * ML_HEADS):
            nn_ref[0, idx:idx + 1, :] = caug_sc[idx][:, ML_DH:].T[0:1, :]
        mn_ref[0] = m_sc[...]


def _mlstm(q, k, v, g, seq, init=None, emit_state=False):
    T = q.shape[0]
    n_chain = 2 * ML_HEADS
    blk = pl.BlockSpec((seq, 512), lambda b: (b, 0))
    gblk = pl.BlockSpec((seq, LANES), lambda b: (b, 0))
    cblk = pl.BlockSpec((1, n_chain, ML_DH, 2 * ML_DH), lambda b: (b, 0, 0, 0))
    mblk = pl.BlockSpec((1, n_chain, 8, LANES), lambda b: (b, 0, 0, 0))
    in_specs = [blk, blk, blk, gblk]
    args = [q, k, v, g]
    if init is not None:
        in_specs += [cblk, mblk]
        args += list(init)
    out_specs = [blk]
    out_shape = [jax.ShapeDtypeStruct((T, 512), F32)]
    nb = T // seq
    if emit_state:
        out_specs += [pl.BlockSpec((1, n_chain, ML_DH, ML_DH), lambda b: (b, 0, 0, 0)),
                      pl.BlockSpec((1, n_chain, ML_DH), lambda b: (b, 0, 0)), mblk]
        out_shape += [jax.ShapeDtypeStruct((nb, n_chain, ML_DH, ML_DH), F32),
                      jax.ShapeDtypeStruct((nb, n_chain, ML_DH), F32),
                      jax.ShapeDtypeStruct((nb, n_chain, 8, LANES), F32)]
    return pl.pallas_call(
        functools.partial(_mlstm_kernel, nc=seq // ML_CHUNK, has_init=init is not None,
                          emit_state=emit_state),
        grid=(nb,),
        in_specs=in_specs,
        out_specs=out_specs,
        out_shape=out_shape,
        scratch_shapes=[pltpu.VMEM((n_chain, ML_DH, 2 * ML_DH), F32),
                        pltpu.VMEM((n_chain, 8, LANES), F32)],
        compiler_params=_cparams(1),
        name="mlstm_state" if emit_state else "mlstm",
    )(*args)


def _outproj_kernel(x_ref, nao_ref, dfo_ref, mlh_ref, mo_ref, mod_ref, g2_ref, outn_ref, wout_ref,
                    wr_ref, br_ref, x1_ref, h2_ref, gates_ref):
    mlh = mlh_ref[...]
    parts = []
    for hd in range(ML_HEADS):
        seg = mlh[:, ML_DH * hd:ML_DH * (hd + 1)]
        parts.append(seg * lax.rsqrt(jnp.mean(seg * seg, axis=-1, keepdims=True) + EPS))
    ml_o = jnp.concatenate(parts, axis=1) * outn_ref[...] * jax.nn.sigmoid(mo_ref[...])
    mix = (_dot(nao_ref[...].astype(BF16), wout_ref[0:256, :])
           + _dot(dfo_ref[...].astype(BF16), wout_ref[256:512, :])
           + _dot(ml_o.astype(BF16), wout_ref[512:1024, :]))
    x1 = x_ref[...] + mod_ref[0, 2:3, :] * mix
    x1_ref[...] = x1
    y = x1 * lax.rsqrt(jnp.mean(x1 * x1, axis=-1, keepdims=True) + EPS) * g2_ref[...]
    h2 = y * (1.0 + mod_ref[0, 4:5, :]) + mod_ref[0, 3:4, :]
    h2_ref[...] = h2.astype(BF16)

    wr = wr_ref[...]
    h_hi = h2.astype(BF16)
    h_lo = (h2 - h_hi.astype(F32)).astype(BF16)
    w_hi = wr.astype(BF16)
    w_lo = (wr - w_hi.astype(F32)).astype(BF16)
    logits = _dot(h_hi, w_hi) + _dot(h_lo, w_hi) + _dot(h_hi, w_lo)
    s = jax.nn.sigmoid(logits)
    work = s + br_ref[...]
    lane = _lane_iota(work.shape)
    s_sel = jnp.zeros(work.shape, F32)
    for _ in range(TOP_K):
        mx = jnp.max(work, axis=-1, keepdims=True)
        first = jnp.min(jnp.where(work == mx, lane, N_EXPERTS), axis=-1, keepdims=True)
        hit = lane == first
        s_sel = jnp.where(hit, s, s_sel)
        work = jnp.where(hit, -jnp.inf, work)
    gates_ref[...] = s_sel / jnp.sum(s_sel, axis=-1, keepdims=True) * ROUTED_SCALE


def _outproj(x, nao, dfo, mlh, mo, mod_p, tokens_per_mod, layer, g2, outn, w_out_bf, w_router, b_router):
    T = x.shape[0]
    tm = TM_PROJ
    row = lambda i: (i, 0)
    const = lambda i: (0, 0)
    lay = lambda i: (layer, 0, 0)
    return pl.pallas_call(
        _outproj_kernel,
        grid=(T // tm,),
        in_specs=[pl.BlockSpec((tm, D_MODEL), row),
                  pl.BlockSpec((tm, 256), row), pl.BlockSpec((tm, 256), row),
                  pl.BlockSpec((tm, 512), row), pl.BlockSpec((tm, 512), row),
                  pl.BlockSpec((1, 6, D_MODEL), lambda i: ((i * tm) // tokens_per_mod, 0, 0)),
                  pl.BlockSpec((1, D_MODEL), const), pl.BlockSpec((1, 512), const),
                  pl.BlockSpec((None, D_MODEL, D_MODEL), lay),
                  pl.BlockSpec((None, D_MODEL, N_EXPERTS), lay), pl.BlockSpec((1, N_EXPERTS), const)],
        out_specs=[pl.BlockSpec((tm, D_MODEL), row), pl.BlockSpec((tm, D_MODEL), row),
                   pl.BlockSpec((tm, N_EXPERTS), row)],
        out_shape=[jax.ShapeDtypeStruct((T, D_MODEL), F32), jax.ShapeDtypeStruct((T, D_MODEL), BF16),
                   jax.ShapeDtypeStruct((T, N_EXPERTS), F32)],
        compiler_params=_cparams(1),
        name="outproj_router",
    )(x, nao, dfo, mlh, mo, mod_p, g2, outn, w_out_bf, w_router, b_router)


def _moe_kernel(h_ref, gates_ref, x1_ref, mod_ref, w1_ref, w3_ref, w2_ref, s1_ref, s3_ref, s2_ref, o_ref):
    e = pl.program_id(1)
    h = h_ref[...]

    @pl.when(e == 0)
    def _():
        a = _dot(h, s1_ref[...].astype(BF16))
        b = _dot(h, s3_ref[...].astype(BF16))
        o_ref[...] = _dot((_silu(a) * b).astype(BF16), s2_ref[...].astype(BF16))

    gates = gates_ref[...]
    lane = _lane_iota(gates.shape)
    hids = []
    for j in range(MOE_EXPERT_BLOCK):
        a = _dot(h, w1_ref[j].astype(BF16))
        b = _dot(h, w3_ref[j].astype(BF16))
        g = jnp.sum(jnp.where(lane == e * MOE_EXPERT_BLOCK + j, gates, 0.0), axis=-1, keepdims=True)
        hids.append((_silu(a) * b * g).astype(BF16))
    w2 = w2_ref[...].reshape(MOE_EXPERT_BLOCK * D_EXPERT, D_MODEL).astype(BF16)
    o_ref[...] += _dot(jnp.concatenate(hids, axis=1), w2)

    @pl.when(e == N_EXPERTS // MOE_EXPERT_BLOCK - 1)
    def _():
        o_ref[...] = x1_ref[...] + mod_ref[0, 5:6, :] * o_ref[...]


def _moe(h2, gates, x1, mod_p, tokens_per_mod, layer, w1, w3, w2, s1, s3, s2):
    T = h2.shape[0]
    tm = TM_MOE
    eb = MOE_EXPERT_BLOCK
    row = lambda i, e: (i, 0)
    wexp = lambda i, e: (layer, e, 0, 0)
    wsh = lambda i, e: (layer, 0, 0)
    return pl.pallas_call(
        _moe_kernel,
        grid=(T // tm, N_EXPERTS // eb),
        in_specs=[pl.BlockSpec((tm, D_MODEL), row), pl.BlockSpec((tm, N_EXPERTS), row),
                  pl.BlockSpec((tm, D_MODEL), row),
                  pl.BlockSpec((1, 6, D_MODEL), lambda i, e: ((i * tm) // tokens_per_mod, 0, 0)),
                  pl.BlockSpec((None, eb, D_MODEL, D_EXPERT), wexp),
                  pl.BlockSpec((None, eb, D_MODEL, D_EXPERT), wexp),
                  pl.BlockSpec((None, eb, D_EXPERT, D_MODEL), wexp),
                  pl.BlockSpec((None, D_MODEL, D_EXPERT), wsh), pl.BlockSpec((None, D_MODEL, D_EXPERT), wsh),
                  pl.BlockSpec((None, D_EXPERT, D_MODEL), wsh)],
        out_specs=pl.BlockSpec((tm, D_MODEL), row),
        out_shape=jax.ShapeDtypeStruct((T, D_MODEL), F32),
        compiler_params=_cparams(2),
        name="moe_experts",
    )(h2, gates, x1, mod_p, w1, w3, w2, s1, s3, s2)


def _group_matrix(group):
    i = np.arange(256)
    return jnp.asarray((i[:, None] // group) == (i[None, :] // group), BF16)


def _rope_tables(seq):
    t = np.arange(seq)
    lane = np.arange(256)
    j = lane % DF_DQK
    nf = DF_DQK // 4
    pos = np.where((j < DF_DQK // 2)[None, :], (t // GRID_W)[:, None], (t % GRID_W)[:, None]).astype(np.float32)
    inv = (1.0 / (ROPE_BASE ** (np.arange(nf, dtype=np.float32) / nf))).astype(np.float32)
    ang = pos * inv[j % nf][None, :]
    sign = np.where((lane % (2 * nf)) < nf, -1.0, 1.0).astype(np.float32)
    return jnp.asarray(np.cos(ang), F32), jnp.asarray(np.sin(ang) * sign[None, :], F32)


def _token_major(a):
    b, h, l, d = a.shape
    return a.transpose(0, 2, 1, 3).reshape(b, l, h * d)


def kernel(x_prompt, x_sample, cache_na_k, cache_na_v, cache_df_k, cache_df_v, state_ml_C, state_ml_n,
           state_ml_m, c, c_ctx, w_ada, b_ada, g_norm1, g_norm2, w_in, w_out, na_qn, na_kn, na_rpb,
           df_qn, df_kn, df_lam, df_subln, ml_gate_b, ml_outn, w_router, b_router, w_exp1, w_exp3, w_exp2,
           w_sh1, w_sh3, w_sh2):
    bc, lc, _ = x_prompt.shape
    bs, ls, _ = x_sample.shape
    past = cache_na_k.shape[3]

    cond = jnp.zeros((8, D_MODEL), F32).at[0].set(c_ctx).at[1:1 + bs].set(c)
    mod = _modulation(cond, w_ada, b_ada)
    g64 = _group_matrix(NA_DH)
    g32 = _group_matrix(DF_DQK)
    rope_tabs = _rope_tables(ls)
    w_in_bf = jnp.pad(w_in.astype(BF16), ((0, 0), (0, 0), (0, IN_COLS_PAD - IN_COLS)))
    w_out_bf = w_out.astype(BF16)
    moe_w = (w_exp1, w_exp3, w_exp2, w_sh1, w_sh3, w_sh2)

    y_p = x_prompt.reshape(bc * lc, D_MODEL)
    y_s = x_sample.reshape(bs * ls, D_MODEL)
    outs = [[] for _ in range(7)]
    for l in range(DEPTH):
        lam_init = 0.8 - 0.6 * math.exp(-0.3 * l)
        mod_l = mod[l].reshape(8, 6, D_MODEL)
        g1 = g_norm1[l][None, :]
        g2 = g_norm2[l][None, :]
        qn = jnp.tile(na_qn[l], NA_HEADS)[None, :]
        kn = jnp.tile(na_kn[l], NA_HEADS)[None, :]
        dqn = jnp.tile(df_qn[l], 2 * DF_HEADS)[None, :]
        dkn = jnp.tile(df_kn[l], 2 * DF_HEADS)[None, :]
        subln = jnp.tile(df_subln[l], DF_HEADS)[None, :]
        outn = jnp.tile(ml_outn[l], ML_HEADS)[None, :]
        gate_b = jnp.pad(ml_gate_b[l].reshape(-1), (0, LANES - 4 * ML_HEADS))[None, :]
        br = b_router[l][None, :]
        proj_w = (l, g1, w_in_bf, qn, kn, dqn, dkn, gate_b, g64, g32)

        mod_c = mod_l[0:1]
        tpm = bc * lc
        (naq, nak, nav, dfq, dfk, dfv, mq, mk, mv, mo, mg,
         nk_hm, nv_hm, dk_hm, dv_hm) = _inproj(y_p, mod_c, tpm, *proj_w, None, cache_seq=lc)
        nao, dfo = _ctx_attention(naq, nak, nav, dfq, dfk, dfv, df_lam[l], subln, lc, lam_init)
        mlh, c_n, n_n, m_n = _mlstm(mq, mk, mv, mg, lc, emit_state=True)
        x1, h2, gates = _outproj(y_p, nao, dfo, mlh, mo, mod_c, tpm, l, g2, outn, w_out_bf, w_router, br)
        y_p = _moe(h2, gates, x1, mod_c, tpm, l, *moe_w)
        outs[0].append(nk_hm)
        outs[1].append(nv_hm)
        outs[2].append(dk_hm)
        outs[3].append(dv_hm)
        outs[4].append(c_n.reshape(bc, 2, ML_HEADS, ML_DH, ML_DH))
        outs[5].append(n_n.reshape(bc, 2, ML_HEADS, ML_DH))
        outs[6].append(m_n[:, :, 0, 0].reshape(bc, 2, ML_HEADS))

        mod_s = mod_l[1:1 + bs]
        naq, nak, nav, dfq, dfk, dfv, mq, mk, mv, mo, mg = _inproj(y_s, mod_s, ls, *proj_w, rope_tabs)
        kc_na = _token_major(cache_na_k[:, l])
        vc_na = _token_major(cache_na_v[:, l])
        kc_df = cache_df_k[:, l].transpose(0, 3, 1, 2, 4).reshape(bs, past, 256)
        vc_df = _token_major(cache_df_v[:, l])
        nao = _lat_na(naq, nak, nav, kc_na, vc_na, _na_bias(na_rpb[l]), ls)
        dfo = _lat_df(dfq, dfk, dfv, kc_df, vc_df, df_lam[l], subln, ls, lam_init)
        n0 = jnp.broadcast_to(state_ml_n[:, l][..., None], (bs, 2, ML_HEADS, ML_DH, ML_DH))
        c0 = jnp.concatenate([state_ml_C[:, l], n0], axis=-1)
        c0 = c0.reshape(bs, 2 * ML_HEADS, ML_DH, 2 * ML_DH)
        m0 = jnp.broadcast_to(state_ml_m[:, l].reshape(bs, 2 * ML_HEADS, 1, 1), (bs, 2 * ML_HEADS, 8, LANES))
        (mlh,) = _mlstm(mq, mk, mv, mg, ls, init=(c0, m0))
        x1, h2, gates = _outproj(y_s, nao, dfo, mlh, mo, mod_s, ls, l, g2, outn, w_out_bf, w_router, br)
        y_s = _moe(h2, gates, x1, mod_s, ls, l, *moe_w)

    stacked = [jnp.stack(o, axis=1) for o in outs]
    return (y_p.reshape(bc, lc, D_MODEL), y_s.reshape(bs, ls, D_MODEL), *stacked)
```

```python
import functools
import math

import numpy as np
import jax
import jax.numpy as jnp
from jax import lax
from jax.experimental import pallas as pl
from jax.experimental.pallas import tpu as pltpu

F32 = jnp.float32
BF16 = jnp.bfloat16

D_MODEL = 1024
DEPTH = 2
GRID_W = 64
NA_HEADS = 4
NA_DH = 64
NA_WIN_R = 8
NA_WIN_C = 16
DF_HEADS = 4
DF_DV = 64
DF_DQK = 32
ML_HEADS = 4
ML_DH = 128
ML_CHUNK = 128
N_EXPERTS = 32
TOP_K = 4
D_EXPERT = 256
ROUTED_SCALE = 2.5
ROPE_BASE = 10000.0
EPS = 1e-6

LANES = 128
IN_COLS = 3600
IN_COLS_PAD = 3712
GATE_COL0 = 3584
MASK_VALUE = -1e30
VMEM_LIMIT = 56 * 1024 * 1024

TM_PROJ = 512
TM_MOE = 1024
MOE_EXPERT_BLOCK = 2
Q_BLOCK_DF = 256
CTX_ATTN_BATCHES = 2
LOOP_UNROLL = 2
NA_Q_ROWS = 4
NA_K_ROWS = NA_Q_ROWS + NA_WIN_R
TM_PROJ_CTX = 256
LOG2E = math.log2(math.e)


def _cparams(n_axes):
    return pltpu.CompilerParams(dimension_semantics=("arbitrary",) * n_axes,
                                vmem_limit_bytes=VMEM_LIMIT)


def _silu(x):
    return x * jax.nn.sigmoid(x)


def _dot(a, b):
    return jnp.dot(a, b, preferred_element_type=F32)


def _dot_nt(a, b):
    return lax.dot_general(a, b, (((1,), (1,)), ((), ())), preferred_element_type=F32)


def _dot_tn(a, b):
    return lax.dot_general(a, b, (((0,), (0,)), ((), ())), preferred_element_type=F32)


def _lane_iota(shape):
    return lax.broadcasted_iota(jnp.int32, shape, 1)


def _mod_kernel(cond_ref, w_ref, b_ref, o_ref):
    s = _silu(cond_ref[...]).astype(BF16)
    o_ref[0] = _dot(s, w_ref[0].astype(BF16)) + b_ref[0]


def _modulation(cond, w_ada, b_ada):
    tn = 1536
    n = 6 * D_MODEL
    return pl.pallas_call(
        _mod_kernel,
        grid=(DEPTH, n // tn),
        in_specs=[pl.BlockSpec((8, D_MODEL), lambda l, j: (0, 0)),
                  pl.BlockSpec((1, D_MODEL, tn), lambda l, j: (l, 0, j)),
                  pl.BlockSpec((1, 1, tn), lambda l, j: (l, 0, j))],
        out_specs=pl.BlockSpec((1, 8, tn), lambda l, j: (l, 0, j)),
        out_shape=jax.ShapeDtypeStruct((DEPTH, 8, n), F32),
        compiler_params=_cparams(2),
        name="adaln_mod",
    )(cond, w_ada, b_ada.reshape(DEPTH, 1, n))


def _group_rms(v, gmat_ref, group):
    sq = v * v
    hi = sq.astype(BF16)
    lo = (sq - hi.astype(F32)).astype(BF16)
    ss = _dot(hi, gmat_ref[...]) + _dot(lo, gmat_ref[...])
    return v * lax.rsqrt(ss * (1.0 / group) + EPS)


def _rope(v, cos, sin_signed):
    first = (_lane_iota(v.shape) % 16) < 8
    partner = jnp.where(first, pltpu.roll(v, v.shape[1] - 8, 1), pltpu.roll(v, 8, 1))
    return v * cos + partner * sin_signed


def _inproj_kernel(*refs, rope, cache_seq, cache_first):
    (x_ref, mod_ref, g1_ref, w_ref, qn_ref, kn_ref, dqn_ref, dkn_ref, gb_ref, g64_ref, g32_ref) = refs[:11]
    pos = 11
    if rope:
        cos_ref, sin_ref = refs[pos:pos + 2]
        pos += 2
    if cache_seq and not cache_first:
        pos += 4
    (naq_ref, nak_ref, nav_ref, dfq_ref, dfk_ref, dfv_ref,
     mq_ref, mk_ref, mv_ref, mo_ref, mg_ref) = refs[pos:pos + 11]
    pos += 11
    x = x_ref[...]
    y = x * lax.rsqrt(jnp.mean(x * x, axis=-1, keepdims=True) + EPS) * g1_ref[...]
    h = (y * (1.0 + mod_ref[0, 1:2, :]) + mod_ref[0, 0:1, :]).astype(BF16)

    def proj(a, b):
        return _dot(h, w_ref[:, a:b])

    naq_ref[...] = _group_rms(proj(0, 256), g64_ref, NA_DH) * qn_ref[...]
    nak = _group_rms(proj(256, 512), g64_ref, NA_DH) * kn_ref[...]
    nav = proj(512, 768)
    dfq = _group_rms(proj(768, 1024), g32_ref, DF_DQK) * dqn_ref[...]
    dfk = _group_rms(proj(1024, 1280), g32_ref, DF_DQK) * dkn_ref[...]
    dfv = proj(1280, 1536)
    if rope:
        dfq = _rope(dfq, cos_ref[...], sin_ref[...])
        dfk = _rope(dfk, cos_ref[...], sin_ref[...])
    nak_ref[...] = nak
    nav_ref[...] = nav
    dfq_ref[...] = dfq
    dfk_ref[...] = dfk
    dfv_ref[...] = dfv
    if cache_seq:
        nk_hm, nv_hm, dk_hm, dv_hm = refs[pos:pos + 4]

        def put(ref, bi, idx, val):
            if cache_first:
                for dl in range(DEPTH):
                    ref[(bi, dl) + idx] = val
            else:
                ref[(bi,) + idx] = val

        for bi in range(x.shape[0] // cache_seq):
            rs = slice(bi * cache_seq, (bi + 1) * cache_seq)
            for hd in range(NA_HEADS):
                put(nk_hm, bi, (hd,), nak[rs, NA_DH * hd:NA_DH * (hd + 1)])
                put(nv_hm, bi, (hd,), nav[rs, NA_DH * hd:NA_DH * (hd + 1)])
                put(dv_hm, bi, (hd,), dfv[rs, DF_DV * hd:DF_DV * (hd + 1)])
                for a in range(2):
                    c0 = (2 * hd + a) * DF_DQK
                    put(dk_hm, bi, (hd, a), dfk[rs, c0:c0 + DF_DQK])
    mq_ref[...] = proj(1536, 2048)
    mk_ref[...] = proj(2048, 2560) * (ML_DH ** -0.5)
    mv_ref[...] = proj(2560, 3072)
    mo_ref[...] = proj(3072, 3584)
    g = proj(GATE_COL0, IN_COLS_PAD) + gb_ref[...]
    lane = _lane_iota(g.shape)
    is_forget = ((lane // ML_HEADS) % 2 == 1) & (lane < 4 * ML_HEADS)
    log_sig = jnp.minimum(g, 0.0) - jnp.log1p(jnp.exp(-jnp.abs(g)))
    mg_ref[...] = jnp.where(is_forget, log_sig, g)


def _inproj(x, mod_p, tokens_per_mod, layer, g1, w_in_bf, qn, kn, dqn, dkn, gate_b, g64, g32, rope_tabs,
            cache_seq=0, cache_prev=None):
    T = x.shape[0]
    tm = TM_PROJ_CTX if cache_seq else TM_PROJ
    rope = rope_tabs is not None
    row = lambda i: (i, 0)
    const = lambda i: (0, 0)
    in_specs = [pl.BlockSpec((tm, D_MODEL), row),
                pl.BlockSpec((1, 6, D_MODEL), lambda i: ((i * tm) // tokens_per_mod, 0, 0)),
                pl.BlockSpec((1, D_MODEL), const),
                pl.BlockSpec((None, D_MODEL, IN_COLS_PAD), lambda i: (layer, 0, 0)),
                pl.BlockSpec((1, 256), const), pl.BlockSpec((1, 256), const),
                pl.BlockSpec((1, 256), const), pl.BlockSpec((1, 256), const),
                pl.BlockSpec((1, LANES), const),
                pl.BlockSpec((256, 256), const), pl.BlockSpec((256, 256), const)]
    args = [x, mod_p, g1, w_in_bf, qn, kn, dqn, dkn, gate_b, g64, g32]
    if rope:
        seq = rope_tabs[0].shape[0]
        tab = lambda i: (i % (seq // tm), 0)
        in_specs += [pl.BlockSpec((tm, 256), tab), pl.BlockSpec((tm, 256), tab)]
        args += list(rope_tabs)
    widths = [256] * 6 + [512] * 4 + [LANES]
    out_specs = [pl.BlockSpec((tm, w), row) for w in widths]
    out_shape = [jax.ShapeDtypeStruct((T, w), F32) for w in widths]
    aliases = {}
    if cache_seq:
        nb = tm // cache_seq
        hm = (NA_HEADS, cache_seq, NA_DH)
        hm_df = (DF_HEADS, 2, cache_seq, DF_DQK)
        for k, tail in enumerate((hm, hm, hm_df, hm)):
            zeros = (0,) * len(tail)
            if cache_prev is None:
                out_specs.append(pl.BlockSpec((nb, DEPTH) + tail, lambda i, z=zeros: (i, 0) + z))
            else:
                out_specs.append(pl.BlockSpec((nb, None) + tail, lambda i, z=zeros: (i, layer) + z))
                aliases[len(args)] = len(widths) + k
                in_specs.append(pl.BlockSpec(memory_space=pl.ANY))
                args.append(cache_prev[k])
            out_shape.append(jax.ShapeDtypeStruct((T // cache_seq, DEPTH) + tail, F32))
    return pl.pallas_call(
        functools.partial(_inproj_kernel, rope=rope, cache_seq=cache_seq, cache_first=cache_prev is None),
        grid=(T // tm,),
        in_specs=in_specs,
        out_specs=out_specs,
        out_shape=out_shape,
        input_output_aliases=aliases,
        compiler_params=_cparams(1),
        name="inproj_rope" if rope else "inproj",
    )(*args)


def _df_lambda(lam_ref, lam_init):
    v = lam_ref[...]
    a = jnp.sum(v[0:1] * v[1:2], axis=1, keepdims=True)
    b = jnp.sum(v[2:3] * v[3:4], axis=1, keepdims=True)
    return jnp.exp(a) - jnp.exp(b) + lam_init


def _softmax_parts(scores):
    m = functools.reduce(jnp.maximum, [jnp.max(s, axis=-1, keepdims=True) for s in scores])
    es = [jnp.exp2(s - m) for s in scores]
    tot = functools.reduce(lambda a, b: a + b, [jnp.sum(e, axis=-1, keepdims=True) for e in es])
    return es, 1.0 / tot


def _subln(o, subln_row, lam_init):
    lane = _lane_iota(o.shape)
    sq = o * o
    s0 = jnp.sum(jnp.where(lane < DF_DV, sq, 0.0), axis=-1, keepdims=True)
    s1 = jnp.sum(jnp.where(lane >= DF_DV, sq, 0.0), axis=-1, keepdims=True)
    ms = jnp.where(lane < DF_DV, s0, s1) * (1.0 / DF_DV)
    return o * lax.rsqrt(ms + EPS) * subln_row * (1.0 - lam_init)


def _diff_attn_tile(q, k_segs, v_segs, lam, t):
    sl = slice(LANES * t, LANES * (t + 1))
    qp = q[:, sl] * (DF_DQK ** -0.5 * LOG2E)
    lane = _lane_iota(qp.shape)
    kps = [k[:, sl].astype(BF16) for k in k_segs]
    vps = [v[:, sl].astype(BF16) for v in v_segs]
    outs = []
    for j in range(2):
        ws = None
        for a in range(2):
            qm = jnp.where(lane // DF_DQK == 2 * j + a, qp, 0.0).astype(BF16)
            es, r = _softmax_parts([_dot_nt(qm, kp) for kp in kps])
            coef = r if a == 0 else -(lam * r)
            ws = [e * coef for e in es] if a == 0 else [w + e * coef for w, e in zip(ws, es)]
        o = functools.reduce(lambda x, y: x + y, [_dot(w.astype(BF16), vp) for w, vp in zip(ws, vps)])
        outs.append(o)
    return jnp.where(lane < DF_DV, outs[0], outs[1])


def _ctx_attn_kernel(naq_ref, nak_ref, nav_ref, dfq_ref, dfk_ref, dfv_ref, lam_ref, subln_ref,
                     nao_ref, dfo_ref, *, lam_init, seq):
    lam = _df_lambda(lam_ref, lam_init)
    for bi in range(naq_ref.shape[0] // seq):
        rs = slice(bi * seq, (bi + 1) * seq)
        for t in range(2):
            sl = slice(LANES * t, LANES * (t + 1))
            qp = naq_ref[rs, sl] * (NA_DH ** -0.5 * LOG2E)
            kp = nak_ref[rs, sl].astype(BF16)
            vp = nav_ref[rs, sl].astype(BF16)
            lane = _lane_iota(qp.shape)
            outs = []
            for j in range(2):
                qm = jnp.where(lane // NA_DH == j, qp, 0.0).astype(BF16)
                (e,), r = _softmax_parts([_dot_nt(qm, kp)])
                outs.append(_dot(e.astype(BF16), vp) * r)
            nao_ref[rs, sl] = jnp.where(lane < NA_DH, outs[0], outs[1])
            o = _diff_attn_tile(dfq_ref[rs, :], [dfk_ref[rs, :]], [dfv_ref[rs, :]], lam, t)
            dfo_ref[rs, sl] = _subln(o, subln_ref[:, sl], lam_init)


def _ctx_attention(naq, nak, nav, dfq, dfk, dfv, df_lam, subln, seq, lam_init):
    T = naq.shape[0]
    rows = CTX_ATTN_BATCHES * seq
    blk = pl.BlockSpec((rows, 256), lambda b: (b, 0))
    return pl.pallas_call(
        functools.partial(_ctx_attn_kernel, lam_init=lam_init, seq=seq),
        grid=(T // rows,),
        in_specs=[blk] * 6 + [pl.BlockSpec((4, DF_DQK), lambda b: (0, 0)),
                              pl.BlockSpec((1, 256), lambda b: (0, 0))],
        out_specs=[blk, blk],
        out_shape=[jax.ShapeDtypeStruct((T, 256), F32)] * 2,
        compiler_params=_cparams(1),
        name="ctx_attention",
    )(naq, nak, nav, dfq, dfk, dfv, df_lam, subln)


def _na_bias_kernel(rpb_ref, o_ref):
    hd = pl.program_id(0)
    ro = pl.program_id(1)
    shape = (GRID_W, GRID_W)
    qi = lax.broadcasted_iota(jnp.int32, shape, 0)
    kc = lax.broadcasted_iota(jnp.int32, shape, 1)
    cs = jnp.clip(qi - NA_WIN_C // 2, 0, GRID_W - NA_WIN_C)
    inwin = (kc >= cs) & (kc < cs + NA_WIN_C)
    off = jnp.where(inwin, kc - qi + NA_WIN_C - 1, -1)
    n_c = 2 * NA_WIN_C - 1
    n_r = 2 * NA_WIN_R - 1
    val = jnp.full(shape, MASK_VALUE, F32)
    for d in range(n_c):
        val = jnp.where(off == d, rpb_ref[hd * (n_r * n_c) + ro * n_c + d], val)
    o_ref[0, 0] = val * LOG2E


def _na_bias_blocks(rpb):
    n_r = 2 * NA_WIN_R - 1
    return pl.pallas_call(
        _na_bias_kernel,
        grid=(NA_HEADS, n_r),
        in_specs=[pl.BlockSpec(memory_space=pltpu.SMEM)],
        out_specs=pl.BlockSpec((1, 1, GRID_W, GRID_W), lambda h, r: (h, r, 0, 0)),
        out_shape=jax.ShapeDtypeStruct((NA_HEADS, n_r, GRID_W, GRID_W), F32),
        compiler_params=_cparams(2),
        name="na_bias",
    )(rpb.reshape(-1))


def _na_block_geometry(rows):
    last_r0 = rows - NA_Q_ROWS
    return ((0, 0), (NA_Q_ROWS, 0), (last_r0, rows - NA_K_ROWS))


def _na_bias_tables(blocks, rows):
    masked = jnp.full(blocks[:, 0].shape, MASK_VALUE * LOG2E, F32)
    tabs = []
    for r0, k0 in _na_block_geometry(rows):
        q_rows = []
        for rq in range(NA_Q_ROWS):
            r = r0 + rq
            rs = min(max(r - NA_WIN_R // 2, 0), rows - NA_WIN_R)
            cols = []
            for kr in range(k0, k0 + NA_K_ROWS):
                cols.append(blocks[:, kr - r + NA_WIN_R - 1] if rs <= kr < rs + NA_WIN_R else masked)
            q_rows.append(jnp.concatenate(cols, axis=-1))
        tabs.append(jnp.concatenate(q_rows, axis=-2))
    return jnp.stack(tabs, axis=0)


def _lat_na_kernel(q_ref, k_ref, v_ref, kc_ref, vc_ref, bias_ref, o_ref, *, rows):
    i = pl.program_id(1)
    k0 = jnp.clip(i * NA_Q_ROWS - NA_WIN_R // 2, 0, rows - NA_K_ROWS)
    krows = pl.ds(pl.multiple_of(k0 * GRID_W, GRID_W), NA_K_ROWS * GRID_W)
    q = q_ref[...] * (NA_DH ** -0.5 * LOG2E)
    for t in range(2):
        sl = slice(LANES * t, LANES * (t + 1))
        qp = q[:, sl]
        kw = k_ref[krows, sl].astype(BF16)
        vw = v_ref[krows, sl].astype(BF16)
        kc = kc_ref[0, :, sl].astype(BF16)
        vc = vc_ref[0, :, sl].astype(BF16)
        lane = _lane_iota(qp.shape)
        outs = []
        for j in range(2):
            qm = jnp.where(lane // NA_DH == j, qp, 0.0).astype(BF16)
            s_loc = _dot_nt(qm, kw) + bias_ref[2 * t + j]
            s_ctx = _dot_nt(qm, kc)
            (e_loc, e_ctx), rcp = _softmax_parts([s_loc, s_ctx])
            outs.append((_dot(e_loc.astype(BF16), vw) + _dot(e_ctx.astype(BF16), vc)) * rcp)
        o_ref[:, sl] = jnp.where(lane < NA_DH, outs[0], outs[1])


def _lat_na(q, k, v, kc, vc, bias_tabs, seq):
    T = q.shape[0]
    past = kc.shape[1]
    rows = seq // GRID_W
    nq = rows // NA_Q_ROWS
    qrows = NA_Q_ROWS * GRID_W
    qblk = pl.BlockSpec((qrows, 256), lambda b, i: (b * nq + i, 0))
    kblk = pl.BlockSpec((seq, 256), lambda b, i: (b, 0))
    cblk = pl.BlockSpec((1, past, 256), lambda b, i: (b, 0, 0))
    variant = lambda b, i: (jnp.where(i == 0, 0, jnp.where(i == nq - 1, 2, 1)), 0, 0, 0)
    return pl.pallas_call(
        functools.partial(_lat_na_kernel, rows=rows),
        grid=(T // seq, nq),
        in_specs=[qblk, kblk, kblk, cblk, cblk,
                  pl.BlockSpec((None, NA_HEADS, qrows, NA_K_ROWS * GRID_W), variant)],
        out_specs=qblk,
        out_shape=jax.ShapeDtypeStruct((T, 256), F32),
        compiler_params=_cparams(2),
        name="lat_na",
    )(q, k, v, kc, vc, bias_tabs)


def _lat_df_kernel(q_ref, k_ref, v_ref, kc_ref, vc_ref, lam_ref, subln_ref, o_ref, *, lam_init):
    lam = _df_lambda(lam_ref, lam_init)
    for t in range(2):
        sl = slice(LANES * t, LANES * (t + 1))
        o = _diff_attn_tile(q_ref[...], [k_ref[...], kc_ref[0]], [v_ref[...], vc_ref[0]], lam, t)
        o_ref[:, sl] = _subln(o, subln_ref[:, sl], lam_init)


def _lat_df(q, k, v, kc, vc, df_lam, subln, seq, lam_init):
    T = q.shape[0]
    past = kc.shape[1]
    nq = seq // Q_BLOCK_DF
    qblk = pl.BlockSpec((Q_BLOCK_DF, 256), lambda b, i: (b * nq + i, 0))
    kblk = pl.BlockSpec((seq, 256), lambda b, i: (b, 0))
    cblk = pl.BlockSpec((1, past, 256), lambda b, i: (b, 0, 0))
    return pl.pallas_call(
        functools.partial(_lat_df_kernel, lam_init=lam_init),
        grid=(T // seq, nq),
        in_specs=[qblk, kblk, kblk, cblk, cblk,
                  pl.BlockSpec((4, DF_DQK), lambda b, i: (0, 0)),
                  pl.BlockSpec((1, 256), lambda b, i: (0, 0))],
        out_specs=qblk,
        out_shape=jax.ShapeDtypeStruct((T, 256), F32),
        compiler_params=_cparams(2),
        name="lat_df",
    )(q, k, v, kc, vc, df_lam, subln)


def _mlstm_kernel(*refs, nc, has_init, emit_state, state_first):
    refs = list(refs)
    q_ref, k_ref, v_ref, g_ref = refs[:4]
    pos = 4
    if has_init:
        c0_ref, m0_ref = refs[pos:pos + 2]
        pos += 2
    if emit_state and not state_first:
        pos += 2
    h_ref = refs[pos]
    pos += 1
    if emit_state:
        cn_ref, nn_ref, mn_ref = refs[pos:pos + 3]
        pos += 3
    caug_sc, m_sc = refs[pos:pos + 2]

    if has_init:
        caug_sc[...] = c0_ref[0]
        m_sc[...] = m0_ref[0]
    else:
        caug_sc[...] = jnp.zeros(caug_sc.shape, F32)
        m_sc[...] = jnp.zeros(m_sc.shape, F32)
    h_ref[...] = jnp.zeros(h_ref.shape, F32)

    ti = lax.broadcasted_iota(jnp.int32, (ML_CHUNK, ML_CHUNK), 0)
    si = lax.broadcasted_iota(jnp.int32, (ML_CHUNK, ML_CHUNK), 1)
    masks = (si <= ti, si >= ti)
    ones_blk = jnp.ones((ML_CHUNK, LANES), BF16)
    full = (ML_CHUNK, ML_CHUNK)

    def body(c, carry):
        for d in range(2):
            ci = c if d == 0 else nc - 1 - c
            start = pl.multiple_of(ci * ML_CHUNK, ML_CHUNK)
            rows = pl.ds(start, ML_CHUNK)
            mask = masks[d]
            tri = jnp.where(mask, 1.0, 0.0).astype(BF16)
            gblk = g_ref[rows, :]
            hi = gblk.astype(BF16)
            lo = (gblk - hi.astype(F32)).astype(BF16)
            bc = _dot(tri, hi) + _dot(tri, lo)
            g_t = gblk.T
            bc_t = bc.T
            last = ML_CHUNK - 1 if d == 0 else 0
            for hd in range(ML_HEADS):
                ch_i = 2 * ML_HEADS * d + hd
                ch_f = ch_i + ML_HEADS
                idx = ML_HEADS * d + hd
                hs = slice(ML_DH * hd, ML_DH * (hd + 1))
                b_col = bc[:, ch_f:ch_f + 1]
                a_col = gblk[:, ch_i:ch_i + 1] - b_col
                a_msk = jnp.where(mask, g_t[ch_i:ch_i + 1, :] - bc_t[ch_f:ch_f + 1, :], -jnp.inf)
                m_col = jnp.max(a_msk, axis=1, keepdims=True)
                m_rep = jnp.broadcast_to(m_col, full)
                b_rep = jnp.broadcast_to(b_col, full)
                a_rep = jnp.broadcast_to(a_col, full)
                m_last = m_col[last:last + 1, :]
                b_last = b_col[last:last + 1, :]
                qh = q_ref[rows, hs].astype(BF16)
                kf = k_ref[rows, hs]
                vaug = jnp.concatenate([v_ref[rows, hs].astype(BF16), ones_blk], axis=1)
                s0 = _dot_nt(qh, kf.astype(BF16)) * jnp.exp(a_msk - m_rep)
                sv0 = _dot(s0.astype(BF16), vaug)
                u0 = _dot_tn((jnp.exp(a_rep - m_last) * kf).astype(BF16), vaug)
                m_prev = m_sc[idx][0:1, 0:1]
                g_rep = jnp.maximum(m_prev, m_rep)
                w0 = jnp.exp(m_prev - g_rep)
                cc = jnp.exp(m_rep - g_rep)
                caug = caug_sc[idx]
                qc = _dot(qh, caug.astype(BF16))
                num = w0 * qc[:, :ML_DH] + cc * sv0[:, :ML_DH]
                den = w0 * qc[:, ML_DH:] + cc * sv0[:, ML_DH:]
                h_ref[rows, hs] += num / jnp.maximum(jnp.abs(den), jnp.exp(-b_rep - g_rep))
                g_last = jnp.maximum(m_prev, m_last)
                caug_sc[idx] = jnp.exp(m_prev - g_last) * caug + jnp.exp(m_last - g_last) * u0
                m_sc[idx] = jnp.broadcast_to(b_last + g_last, m_sc.shape[1:])
        return carry

    lax.fori_loop(0, nc, body, 0, unroll=LOOP_UNROLL)
    if emit_state:
        slots = [(0, dl) for dl in range(DEPTH)] if state_first else [(0,)]
        for slot in slots:
            cn_ref[slot] = caug_sc[:, :, :ML_DH]
        for idx in range(2 * ML_HEADS):
            n_row = caug_sc[idx][:, ML_DH:].T[0:1, :]
            for slot in slots:
                nn_ref[slot + (slice(idx, idx + 1), slice(None))] = n_row
        mn_ref[0] = m_sc[...]


def _mlstm(q, k, v, g, seq, init=None, emit_state=False, layer=0, state_prev=None):
    T = q.shape[0]
    n_chain = 2 * ML_HEADS
    blk = pl.BlockSpec((seq, 512), lambda b: (b, 0))
    gblk = pl.BlockSpec((seq, LANES), lambda b: (b, 0))
    cblk = pl.BlockSpec((1, n_chain, ML_DH, 2 * ML_DH), lambda b: (b, 0, 0, 0))
    mblk = pl.BlockSpec((1, n_chain, 8, LANES), lambda b: (b, 0, 0, 0))
    in_specs = [blk, blk, blk, gblk]
    args = [q, k, v, g]
    if init is not None:
        in_specs += [cblk, mblk]
        args += list(init)
    out_specs = [blk]
    out_shape = [jax.ShapeDtypeStruct((T, 512), F32)]
    nb = T // seq
    aliases = {}
    if emit_state:
        if state_prev is None:
            out_specs += [pl.BlockSpec((1, DEPTH, n_chain, ML_DH, ML_DH), lambda b: (b, 0, 0, 0, 0)),
                          pl.BlockSpec((1, DEPTH, n_chain, ML_DH), lambda b: (b, 0, 0, 0)), mblk]
        else:
            out_specs += [pl.BlockSpec((1, None, n_chain, ML_DH, ML_DH), lambda b: (b, layer, 0, 0, 0)),
                          pl.BlockSpec((1, None, n_chain, ML_DH), lambda b: (b, layer, 0, 0)), mblk]
            aliases = {len(args): 1, len(args) + 1: 2}
            in_specs += [pl.BlockSpec(memory_space=pl.ANY)] * 2
            args += list(state_prev)
        out_shape += [jax.ShapeDtypeStruct((nb, DEPTH, n_chain, ML_DH, ML_DH), F32),
                      jax.ShapeDtypeStruct((nb, DEPTH, n_chain, ML_DH), F32),
                      jax.ShapeDtypeStruct((nb, n_chain, 8, LANES), F32)]
    return pl.pallas_call(
        functools.partial(_mlstm_kernel, nc=seq // ML_CHUNK, has_init=init is not None,
                          emit_state=emit_state, state_first=state_prev is None),
        grid=(nb,),
        in_specs=in_specs,
        out_specs=out_specs,
        out_shape=out_shape,
        input_output_aliases=aliases,
        scratch_shapes=[pltpu.VMEM((n_chain, ML_DH, 2 * ML_DH), F32),
                        pltpu.VMEM((n_chain, 8, LANES), F32)],
        compiler_params=_cparams(1),
        name="mlstm_state" if emit_state else "mlstm",
    )(*args)


def _outproj_kernel(x_ref, nao_ref, dfo_ref, mlh_ref, mo_ref, mod_ref, g2_ref, outn_ref, wout_ref,
                    wr_ref, br_ref, x1_ref, h2_ref, gates_ref):
    mlh = mlh_ref[...]
    parts = []
    for hd in range(ML_HEADS):
        seg = mlh[:, ML_DH * hd:ML_DH * (hd + 1)]
        parts.append(seg * lax.rsqrt(jnp.mean(seg * seg, axis=-1, keepdims=True) + EPS))
    ml_o = jnp.concatenate(parts, axis=1) * outn_ref[...] * jax.nn.sigmoid(mo_ref[...])
    mix = (_dot(nao_ref[...].astype(BF16), wout_ref[0:256, :])
           + _dot(dfo_ref[...].astype(BF16), wout_ref[256:512, :])
           + _dot(ml_o.astype(BF16), wout_ref[512:1024, :]))
    x1 = x_ref[...] + mod_ref[0, 2:3, :] * mix
    x1_ref[...] = x1
    y = x1 * lax.rsqrt(jnp.mean(x1 * x1, axis=-1, keepdims=True) + EPS) * g2_ref[...]
    h2 = y * (1.0 + mod_ref[0, 4:5, :]) + mod_ref[0, 3:4, :]
    h2_ref[...] = h2.astype(BF16)

    wr = wr_ref[...]
    h_hi = h2.astype(BF16)
    h_lo = (h2 - h_hi.astype(F32)).astype(BF16)
    w_hi = wr.astype(BF16)
    w_lo = (wr - w_hi.astype(F32)).astype(BF16)
    logits = _dot(h_hi, w_hi) + _dot(h_lo, w_hi) + _dot(h_hi, w_lo)
    s = jax.nn.sigmoid(logits)
    work = s + br_ref[...]
    lane = _lane_iota(work.shape)
    s_sel = jnp.zeros(work.shape, F32)
    for _ in range(TOP_K):
        mx = jnp.max(work, axis=-1, keepdims=True)
        first = jnp.min(jnp.where(work == mx, lane, N_EXPERTS), axis=-1, keepdims=True)
        hit = lane == first
        s_sel = jnp.where(hit, s, s_sel)
        work = jnp.where(hit, -jnp.inf, work)
    gates_ref[...] = s_sel / jnp.sum(s_sel, axis=-1, keepdims=True) * ROUTED_SCALE


def _outproj(x, nao, dfo, mlh, mo, mod_p, tokens_per_mod, layer, g2, outn, w_out_bf, w_router, b_router):
    T = x.shape[0]
    tm = TM_PROJ
    row = lambda i: (i, 0)
    const = lambda i: (0, 0)
    lay = lambda i: (layer, 0, 0)
    return pl.pallas_call(
        _outproj_kernel,
        grid=(T // tm,),
        in_specs=[pl.BlockSpec((tm, D_MODEL), row),
                  pl.BlockSpec((tm, 256), row), pl.BlockSpec((tm, 256), row),
                  pl.BlockSpec((tm, 512), row), pl.BlockSpec((tm, 512), row),
                  pl.BlockSpec((1, 6, D_MODEL), lambda i: ((i * tm) // tokens_per_mod, 0, 0)),
                  pl.BlockSpec((1, D_MODEL), const), pl.BlockSpec((1, 512), const),
                  pl.BlockSpec((None, D_MODEL, D_MODEL), lay),
                  pl.BlockSpec((None, D_MODEL, N_EXPERTS), lay), pl.BlockSpec((1, N_EXPERTS), const)],
        out_specs=[pl.BlockSpec((tm, D_MODEL), row), pl.BlockSpec((tm, D_MODEL), row),
                   pl.BlockSpec((tm, N_EXPERTS), row)],
        out_shape=[jax.ShapeDtypeStruct((T, D_MODEL), F32), jax.ShapeDtypeStruct((T, D_MODEL), BF16),
                   jax.ShapeDtypeStruct((T, N_EXPERTS), F32)],
        compiler_params=_cparams(1),
        name="outproj_router",
    )(x, nao, dfo, mlh, mo, mod_p, g2, outn, w_out_bf, w_router, b_router)


def _moe_kernel(h_ref, gates_ref, x1_ref, mod_ref, w1_ref, w3_ref, w2_ref, s1_ref, s3_ref, s2_ref, o_ref):
    e = pl.program_id(1)
    h = h_ref[...]

    @pl.when(e == 0)
    def _():
        a = _dot(h, s1_ref[...].astype(BF16))
        b = _dot(h, s3_ref[...].astype(BF16))
        o_ref[...] = _dot((_silu(a) * b).astype(BF16), s2_ref[...].astype(BF16))

    gates = gates_ref[...]
    lane = _lane_iota(gates.shape)
    hids = []
    for j in range(MOE_EXPERT_BLOCK):
        a = _dot(h, w1_ref[j].astype(BF16))
        b = _dot(h, w3_ref[j].astype(BF16))
        g = jnp.sum(jnp.where(lane == e * MOE_EXPERT_BLOCK + j, gates, 0.0), axis=-1, keepdims=True)
        hids.append((_silu(a) * b * g).astype(BF16))
    w2 = w2_ref[...].reshape(MOE_EXPERT_BLOCK * D_EXPERT, D_MODEL).astype(BF16)
    o_ref[...] += _dot(jnp.concatenate(hids, axis=1), w2)

    @pl.when(e == N_EXPERTS // MOE_EXPERT_BLOCK - 1)
    def _():
        o_ref[...] = x1_ref[...] + mod_ref[0, 5:6, :] * o_ref[...]


def _moe(h2, gates, x1, mod_p, tokens_per_mod, layer, w1, w3, w2, s1, s3, s2):
    T = h2.shape[0]
    tm = TM_MOE
    eb = MOE_EXPERT_BLOCK
    row = lambda i, e: (i, 0)
    wexp = lambda i, e: (layer, e, 0, 0)
    wsh = lambda i, e: (layer, 0, 0)
    return pl.pallas_call(
        _moe_kernel,
        grid=(T // tm, N_EXPERTS // eb),
        in_specs=[pl.BlockSpec((tm, D_MODEL), row), pl.BlockSpec((tm, N_EXPERTS), row),
                  pl.BlockSpec((tm, D_MODEL), row),
                  pl.BlockSpec((1, 6, D_MODEL), lambda i, e: ((i * tm) // tokens_per_mod, 0, 0)),
                  pl.BlockSpec((None, eb, D_MODEL, D_EXPERT), wexp),
                  pl.BlockSpec((None, eb, D_MODEL, D_EXPERT), wexp),
                  pl.BlockSpec((None, eb, D_EXPERT, D_MODEL), wexp),
                  pl.BlockSpec((None, D_MODEL, D_EXPERT), wsh), pl.BlockSpec((None, D_MODEL, D_EXPERT), wsh),
                  pl.BlockSpec((None, D_EXPERT, D_MODEL), wsh)],
        out_specs=pl.BlockSpec((tm, D_MODEL), row),
        out_shape=jax.ShapeDtypeStruct((T, D_MODEL), F32),
        compiler_params=_cparams(2),
        name="moe_experts",
    )(h2, gates, x1, mod_p, w1, w3, w2, s1, s3, s2)


def _group_matrix(group):
    i = np.arange(256)
    return jnp.asarray((i[:, None] // group) == (i[None, :] // group), BF16)


def _rope_tables(seq):
    t = np.arange(seq)
    lane = np.arange(256)
    j = lane % DF_DQK
    nf = DF_DQK // 4
    pos = np.where((j < DF_DQK // 2)[None, :], (t // GRID_W)[:, None], (t % GRID_W)[:, None]).astype(np.float32)
    inv = (1.0 / (ROPE_BASE ** (np.arange(nf, dtype=np.float32) / nf))).astype(np.float32)
    ang = pos * inv[j % nf][None, :]
    sign = np.where((lane % (2 * nf)) < nf, -1.0, 1.0).astype(np.float32)
    return jnp.asarray(np.cos(ang), F32), jnp.asarray(np.sin(ang) * sign[None, :], F32)


def _token_major(a):
    b, h, l, d = a.shape
    return a.transpose(0, 2, 1, 3).reshape(b, l, h * d)


def kernel(x_prompt, x_sample, cache_na_k, cache_na_v, cache_df_k, cache_df_v, state_ml_C, state_ml_n,
           state_ml_m, c, c_ctx, w_ada, b_ada, g_norm1, g_norm2, w_in, w_out, na_qn, na_kn, na_rpb,
           df_qn, df_kn, df_lam, df_subln, ml_gate_b, ml_outn, w_router, b_router, w_exp1, w_exp3, w_exp2,
           w_sh1, w_sh3, w_sh2):
    bc, lc, _ = x_prompt.shape
    bs, ls, _ = x_sample.shape
    past = cache_na_k.shape[3]

    cond = jnp.zeros((8, D_MODEL), F32).at[0].set(c_ctx).at[1:1 + bs].set(c)
    mod = _modulation(cond, w_ada, b_ada)
    g64 = _group_matrix(NA_DH)
    g32 = _group_matrix(DF_DQK)
    rope_tabs = _rope_tables(ls)
    w_in_bf = jnp.pad(w_in.astype(BF16), ((0, 0), (0, 0), (0, IN_COLS_PAD - IN_COLS)))
    w_out_bf = w_out.astype(BF16)
    moe_w = (w_exp1, w_exp3, w_exp2, w_sh1, w_sh3, w_sh2)

    y_p = x_prompt.reshape(bc * lc, D_MODEL)
    y_s = x_sample.reshape(bs * ls, D_MODEL)
    caches = states = None
    m_states = []
    for l in range(DEPTH):
        lam_init = 0.8 - 0.6 * math.exp(-0.3 * l)
        mod_l = mod[l].reshape(8, 6, D_MODEL)
        g1 = g_norm1[l][None, :]
        g2 = g_norm2[l][None, :]
        qn = jnp.tile(na_qn[l], NA_HEADS)[None, :]
        kn = jnp.tile(na_kn[l], NA_HEADS)[None, :]
        dqn = jnp.tile(df_qn[l], 2 * DF_HEADS)[None, :]
        dkn = jnp.tile(df_kn[l], 2 * DF_HEADS)[None, :]
        subln = jnp.tile(df_subln[l], DF_HEADS)[None, :]
        outn = jnp.tile(ml_outn[l], ML_HEADS)[None, :]
        gate_b = jnp.pad(ml_gate_b[l].reshape(-1), (0, LANES - 4 * ML_HEADS))[None, :]
        br = b_router[l][None, :]
        proj_w = (l, g1, w_in_bf, qn, kn, dqn, dkn, gate_b, g64, g32)

        mod_c = mod_l[0:1]
        tpm = bc * lc
        res = _inproj(y_p, mod_c, tpm, *proj_w, None, cache_seq=lc, cache_prev=caches)
        naq, nak, nav, dfq, dfk, dfv, mq, mk, mv, mo, mg = res[:11]
        caches = tuple(res[11:])
        nao, dfo = _ctx_attention(naq, nak, nav, dfq, dfk, dfv, df_lam[l], subln, lc, lam_init)
        mlh, c_n, n_n, m_n = _mlstm(mq, mk, mv, mg, lc, emit_state=True, layer=l, state_prev=states)
        states = (c_n, n_n)
        x1, h2, gates = _outproj(y_p, nao, dfo, mlh, mo, mod_c, tpm, l, g2, outn, w_out_bf, w_router, br)
        y_p = _moe(h2, gates, x1, mod_c, tpm, l, *moe_w)
        m_states.append(m_n[:, :, 0, 0].reshape(bc, 2, ML_HEADS))

        mod_s = mod_l[1:1 + bs]
        naq, nak, nav, dfq, dfk, dfv, mq, mk, mv, mo, mg = _inproj(y_s, mod_s, ls, *proj_w, rope_tabs)
        kc_na = _token_major(cache_na_k[:, l])
        vc_na = _token_major(cache_na_v[:, l])
        kc_df = cache_df_k[:, l].transpose(0, 3, 1, 2, 4).reshape(bs, past, 256)
        vc_df = _token_major(cache_df_v[:, l])
        bias_tabs = _na_bias_tables(_na_bias_blocks(na_rpb[l]), ls // GRID_W)
        nao = _lat_na(naq, nak, nav, kc_na, vc_na, bias_tabs, ls)
        dfo = _lat_df(dfq, dfk, dfv, kc_df, vc_df, df_lam[l], subln, ls, lam_init)
        n0 = jnp.broadcast_to(state_ml_n[:, l][..., None], (bs, 2, ML_HEADS, ML_DH, ML_DH))
        c0 = jnp.concatenate([state_ml_C[:, l], n0], axis=-1)
        c0 = c0.reshape(bs, 2 * ML_HEADS, ML_DH, 2 * ML_DH)
        m0 = jnp.broadcast_to(state_ml_m[:, l].reshape(bs, 2 * ML_HEADS, 1, 1), (bs, 2 * ML_HEADS, 8, LANES))
        (mlh,) = _mlstm(mq, mk, mv, mg, ls, init=(c0, m0))
        x1, h2, gates = _outproj(y_s, nao, dfo, mlh, mo, mod_s, ls, l, g2, outn, w_out_bf, w_router, br)
        y_s = _moe(h2, gates, x1, mod_s, ls, l, *moe_w)

    new_ml_c = states[0].reshape(bc, DEPTH, 2, ML_HEADS, ML_DH, ML_DH)
    new_ml_n = states[1].reshape(bc, DEPTH, 2, ML_HEADS, ML_DH)
    return (y_p.reshape(bc, lc, D_MODEL), y_s.reshape(bs, ls, D_MODEL), *caches,
            new_ml_c, new_ml_n, jnp.stack(m_states, axis=1))
```

```python
import functools
import math

import numpy as np
import jax
import jax.numpy as jnp
from jax import lax
from jax.experimental import pallas as pl
from jax.experimental.pallas import tpu as pltpu

F32 = jnp.float32
BF16 = jnp.bfloat16

D_MODEL = 1024
DEPTH = 2
GRID_W = 64
NA_HEADS = 4
NA_DH = 64
NA_WIN_R = 8
NA_WIN_C = 16
DF_HEADS = 4
DF_DV = 64
DF_DQK = 32
ML_HEADS = 4
ML_DH = 128
ML_CHUNK = 128
N_EXPERTS = 32
TOP_K = 4
D_EXPERT = 256
ROUTED_SCALE = 2.5
ROPE_BASE = 10000.0
EPS = 1e-6

LANES = 128
IN_COLS = 3600
IN_COLS_PAD = 3712
GATE_COL0 = 3584
MASK_VALUE = -1e30
VMEM_LIMIT = 56 * 1024 * 1024

TM_PROJ = 512
TM_MOE = 1024
MOE_EXPERT_BLOCK = 2
Q_BLOCK_DF = 256
CTX_ATTN_BATCHES = 2
LOOP_UNROLL = 2
NA_Q_ROWS = 4
NA_K_ROWS = NA_Q_ROWS + NA_WIN_R
TM_PROJ_CTX = 256
LOG2E = math.log2(math.e)


def _cparams(n_axes):
    return pltpu.CompilerParams(dimension_semantics=("arbitrary",) * n_axes,
                                vmem_limit_bytes=VMEM_LIMIT)


def _silu(x):
    return x * jax.nn.sigmoid(x)


def _dot(a, b):
    return jnp.dot(a, b, preferred_element_type=F32)


def _dot_nt(a, b):
    return lax.dot_general(a, b, (((1,), (1,)), ((), ())), preferred_element_type=F32)


def _dot_tn(a, b):
    return lax.dot_general(a, b, (((0,), (0,)), ((), ())), preferred_element_type=F32)


def _lane_iota(shape):
    return lax.broadcasted_iota(jnp.int32, shape, 1)


def _mod_kernel(cond_ref, w_ref, b_ref, o_ref):
    s = _silu(cond_ref[...]).astype(BF16)
    o_ref[0] = _dot(s, w_ref[0].astype(BF16)) + b_ref[0]


def _modulation(cond, w_ada, b_ada):
    tn = 1536
    n = 6 * D_MODEL
    return pl.pallas_call(
        _mod_kernel,
        grid=(DEPTH, n // tn),
        in_specs=[pl.BlockSpec((8, D_MODEL), lambda l, j: (0, 0)),
                  pl.BlockSpec((1, D_MODEL, tn), lambda l, j: (l, 0, j)),
                  pl.BlockSpec((1, 1, tn), lambda l, j: (l, 0, j))],
        out_specs=pl.BlockSpec((1, 8, tn), lambda l, j: (l, 0, j)),
        out_shape=jax.ShapeDtypeStruct((DEPTH, 8, n), F32),
        compiler_params=_cparams(2),
        name="adaln_mod",
    )(cond, w_ada, b_ada.reshape(DEPTH, 1, n))


def _group_rms(v, gmat_ref, group):
    sq = v * v
    hi = sq.astype(BF16)
    lo = (sq - hi.astype(F32)).astype(BF16)
    ss = _dot(hi, gmat_ref[...]) + _dot(lo, gmat_ref[...])
    return v * lax.rsqrt(ss * (1.0 / group) + EPS)


def _rope(v, cos, sin_signed):
    first = (_lane_iota(v.shape) % 16) < 8
    partner = jnp.where(first, pltpu.roll(v, v.shape[1] - 8, 1), pltpu.roll(v, 8, 1))
    return v * cos + partner * sin_signed


def _inproj_kernel(*refs, rope, cache_seq, cache_first):
    (x_ref, mod_ref, g1_ref, w_ref, qn_ref, kn_ref, dqn_ref, dkn_ref, gb_ref, g64_ref, g32_ref) = refs[:11]
    pos = 11
    if rope:
        cos_ref, sin_ref = refs[pos:pos + 2]
        pos += 2
    if cache_seq and not cache_first:
        pos += 4
    (naq_ref, nak_ref, nav_ref, dfq_ref, dfk_ref, dfv_ref,
     mq_ref, mk_ref, mv_ref, mo_ref, mg_ref) = refs[pos:pos + 11]
    pos += 11
    x = x_ref[...]
    y = x * lax.rsqrt(jnp.mean(x * x, axis=-1, keepdims=True) + EPS) * g1_ref[...]
    h = (y * (1.0 + mod_ref[0, 1:2, :]) + mod_ref[0, 0:1, :]).astype(BF16)

    def proj(a, b):
        return _dot(h, w_ref[:, a:b])

    naq_ref[...] = _group_rms(proj(0, 256), g64_ref, NA_DH) * qn_ref[...]
    nak = _group_rms(proj(256, 512), g64_ref, NA_DH) * kn_ref[...]
    nav = proj(512, 768)
    dfq = _group_rms(proj(768, 1024), g32_ref, DF_DQK) * dqn_ref[...]
    dfk = _group_rms(proj(1024, 1280), g32_ref, DF_DQK) * dkn_ref[...]
    dfv = proj(1280, 1536)
    if rope:
        dfq = _rope(dfq, cos_ref[...], sin_ref[...])
        dfk = _rope(dfk, cos_ref[...], sin_ref[...])
    nak_ref[...] = nak
    nav_ref[...] = nav
    dfq_ref[...] = dfq
    dfk_ref[...] = dfk
    dfv_ref[...] = dfv
    if cache_seq:
        nk_hm, nv_hm, dk_hm, dv_hm = refs[pos:pos + 4]

        def put(ref, bi, idx, val):
            if cache_first:
                for dl in range(DEPTH):
                    ref[(bi, dl) + idx] = val
            else:
                ref[(bi,) + idx] = val

        for bi in range(x.shape[0] // cache_seq):
            rs = slice(bi * cache_seq, (bi + 1) * cache_seq)
            for hd in range(NA_HEADS):
                put(nk_hm, bi, (hd,), nak[rs, NA_DH * hd:NA_DH * (hd + 1)])
                put(nv_hm, bi, (hd,), nav[rs, NA_DH * hd:NA_DH * (hd + 1)])
                put(dv_hm, bi, (hd,), dfv[rs, DF_DV * hd:DF_DV * (hd + 1)])
                for a in range(2):
                    c0 = (2 * hd + a) * DF_DQK
                    put(dk_hm, bi, (hd, a), dfk[rs, c0:c0 + DF_DQK])
    mq_ref[...] = proj(1536, 2048)
    mk_ref[...] = proj(2048, 2560) * (ML_DH ** -0.5)
    mv_ref[...] = proj(2560, 3072)
    mo_ref[...] = proj(3072, 3584)
    g = proj(GATE_COL0, IN_COLS_PAD) + gb_ref[...]
    lane = _lane_iota(g.shape)
    is_forget = ((lane // ML_HEADS) % 2 == 1) & (lane < 4 * ML_HEADS)
    log_sig = jnp.minimum(g, 0.0) - jnp.log1p(jnp.exp(-jnp.abs(g)))
    mg_ref[...] = jnp.where(is_forget, log_sig, g)


def _inproj(x, mod_p, tokens_per_mod, layer, g1, w_in_bf, qn, kn, dqn, dkn, gate_b, g64, g32, rope_tabs,
            cache_seq=0, cache_prev=None):
    T = x.shape[0]
    tm = TM_PROJ_CTX if cache_seq else TM_PROJ
    rope = rope_tabs is not None
    row = lambda i: (i, 0)
    const = lambda i: (0, 0)
    in_specs = [pl.BlockSpec((tm, D_MODEL), row),
                pl.BlockSpec((1, 6, D_MODEL), lambda i: ((i * tm) // tokens_per_mod, 0, 0)),
                pl.BlockSpec((1, D_MODEL), const),
                pl.BlockSpec((None, D_MODEL, IN_COLS_PAD), lambda i: (layer, 0, 0)),
                pl.BlockSpec((1, 256), const), pl.BlockSpec((1, 256), const),
                pl.BlockSpec((1, 256), const), pl.BlockSpec((1, 256), const),
                pl.BlockSpec((1, LANES), const),
                pl.BlockSpec((256, 256), const), pl.BlockSpec((256, 256), const)]
    args = [x, mod_p, g1, w_in_bf, qn, kn, dqn, dkn, gate_b, g64, g32]
    if rope:
        seq = rope_tabs[0].shape[0]
        tab = lambda i: (i % (seq // tm), 0)
        in_specs += [pl.BlockSpec((tm, 256), tab), pl.BlockSpec((tm, 256), tab)]
        args += list(rope_tabs)
    widths = [256] * 6 + [512] * 4 + [LANES]
    out_specs = [pl.BlockSpec((tm, w), row) for w in widths]
    out_shape = [jax.ShapeDtypeStruct((T, w), F32) for w in widths]
    aliases = {}
    if cache_seq:
        nb = tm // cache_seq
        hm = (NA_HEADS, cache_seq, NA_DH)
        hm_df = (DF_HEADS, 2, cache_seq, DF_DQK)
        for k, tail in enumerate((hm, hm, hm_df, hm)):
            zeros = (0,) * len(tail)
            if cache_prev is None:
                out_specs.append(pl.BlockSpec((nb, DEPTH) + tail, lambda i, z=zeros: (i, 0) + z))
            else:
                out_specs.append(pl.BlockSpec((nb, None) + tail, lambda i, z=zeros: (i, layer) + z))
                aliases[len(args)] = len(widths) + k
                in_specs.append(pl.BlockSpec(memory_space=pl.ANY))
                args.append(cache_prev[k])
            out_shape.append(jax.ShapeDtypeStruct((T // cache_seq, DEPTH) + tail, F32))
    return pl.pallas_call(
        functools.partial(_inproj_kernel, rope=rope, cache_seq=cache_seq, cache_first=cache_prev is None),
        grid=(T // tm,),
        in_specs=in_specs,
        out_specs=out_specs,
        out_shape=out_shape,
        input_output_aliases=aliases,
        compiler_params=_cparams(1),
        name="inproj_rope" if rope else "inproj",
    )(*args)


def _df_lambda(lam_ref, lam_init):
    v = lam_ref[...]
    a = jnp.sum(v[0:1] * v[1:2], axis=1, keepdims=True)
    b = jnp.sum(v[2:3] * v[3:4], axis=1, keepdims=True)
    return jnp.exp(a) - jnp.exp(b) + lam_init


def _softmax_parts(scores):
    es = _softmax_exps(scores)
    tot = functools.reduce(lambda a, b: a + b, [jnp.sum(e, axis=-1, keepdims=True) for e in es])
    return [e.astype(BF16) for e in es], 1.0 / tot


def _softmax_exps(scores):
    m = functools.reduce(jnp.maximum, [jnp.max(s, axis=-1, keepdims=True) for s in scores])
    return [jnp.exp2(s - m) for s in scores]


def _attend_masked_heads(qp, k_segs, v_segs, bias=None):
    lane = _lane_iota(qp.shape)
    v_aug = [jnp.concatenate([v, jnp.ones(v.shape, BF16)], axis=1) for v in v_segs]
    outs = []
    for j in range(2):
        qm = jnp.where(lane // NA_DH == j, qp, 0.0).astype(BF16)
        scores = [_dot_nt(qm, k) for k in k_segs]
        if bias is not None:
            scores[0] = scores[0] + bias[j]
        es = _softmax_exps(scores)
        o = functools.reduce(lambda x, y: x + y, [_dot(e.astype(BF16), v) for e, v in zip(es, v_aug)])
        outs.append(o[:, :LANES] / o[:, LANES:])
    return jnp.where(lane < NA_DH, outs[0], outs[1])


def _subln(o, subln_row, lam_init):
    lane = _lane_iota(o.shape)
    sq = o * o
    s0 = jnp.sum(jnp.where(lane < DF_DV, sq, 0.0), axis=-1, keepdims=True)
    s1 = jnp.sum(jnp.where(lane >= DF_DV, sq, 0.0), axis=-1, keepdims=True)
    ms = jnp.where(lane < DF_DV, s0, s1) * (1.0 / DF_DV)
    return o * lax.rsqrt(ms + EPS) * subln_row * (1.0 - lam_init)


def _diff_attn_tile(q, k_segs, v_segs, lam, t):
    sl = slice(LANES * t, LANES * (t + 1))
    qp = q[:, sl] * (DF_DQK ** -0.5 * LOG2E)
    lane = _lane_iota(qp.shape)
    kps = [k[:, sl].astype(BF16) for k in k_segs]
    vps = [v[:, sl].astype(BF16) for v in v_segs]
    outs = []
    for j in range(2):
        maps = []
        for a in range(2):
            qm = jnp.where(lane // DF_DQK == 2 * j + a, qp, 0.0).astype(BF16)
            maps.append(_softmax_parts([_dot_nt(qm, kp) for kp in kps]))
        (e0, r0), (e1, r1) = maps
        c0 = r0.astype(BF16)
        c1 = (-(lam * r1)).astype(BF16)
        o = functools.reduce(lambda x, y: x + y,
                             [_dot(x0 * c0 + x1 * c1, vp) for x0, x1, vp in zip(e0, e1, vps)])
        outs.append(o)
    return jnp.where(lane < DF_DV, outs[0], outs[1])


def _ctx_attn_kernel(naq_ref, nak_ref, nav_ref, dfq_ref, dfk_ref, dfv_ref, lam_ref, subln_ref,
                     nao_ref, dfo_ref, *, lam_init, seq):
    lam = _df_lambda(lam_ref, lam_init)
    for bi in range(naq_ref.shape[0] // seq):
        rs = slice(bi * seq, (bi + 1) * seq)
        for t in range(2):
            sl = slice(LANES * t, LANES * (t + 1))
            qp = naq_ref[rs, sl] * (NA_DH ** -0.5 * LOG2E)
            kp = nak_ref[rs, sl].astype(BF16)
            vp = nav_ref[rs, sl].astype(BF16)
            nao_ref[rs, sl] = _attend_masked_heads(qp, [kp], [vp])
            o = _diff_attn_tile(dfq_ref[rs, :], [dfk_ref[rs, :]], [dfv_ref[rs, :]], lam, t)
            dfo_ref[rs, sl] = _subln(o, subln_ref[:, sl], lam_init)


def _ctx_attention(naq, nak, nav, dfq, dfk, dfv, df_lam, subln, seq, lam_init):
    T = naq.shape[0]
    rows = CTX_ATTN_BATCHES * seq
    blk = pl.BlockSpec((rows, 256), lambda b: (b, 0))
    return pl.pallas_call(
        functools.partial(_ctx_attn_kernel, lam_init=lam_init, seq=seq),
        grid=(T // rows,),
        in_specs=[blk] * 6 + [pl.BlockSpec((4, DF_DQK), lambda b: (0, 0)),
                              pl.BlockSpec((1, 256), lambda b: (0, 0))],
        out_specs=[blk, blk],
        out_shape=[jax.ShapeDtypeStruct((T, 256), F32)] * 2,
        compiler_params=_cparams(1),
        name="ctx_attention",
    )(naq, nak, nav, dfq, dfk, dfv, df_lam, subln)


def _na_bias_kernel(rpb_ref, o_ref):
    hd = pl.program_id(0)
    shape = (GRID_W, GRID_W)
    qi = lax.broadcasted_iota(jnp.int32, shape, 0)
    kc = lax.broadcasted_iota(jnp.int32, shape, 1)
    cs = jnp.clip(qi - NA_WIN_C // 2, 0, GRID_W - NA_WIN_C)
    inwin = (kc >= cs) & (kc < cs + NA_WIN_C)
    off = jnp.where(inwin, kc - qi + NA_WIN_C - 1, -1)
    n_c = 2 * NA_WIN_C - 1
    n_r = 2 * NA_WIN_R - 1
    for ro in range(n_r):
        val = jnp.full(shape, MASK_VALUE, F32)
        for d in range(n_c):
            val = jnp.where(off == d, rpb_ref[hd * (n_r * n_c) + ro * n_c + d], val)
        o_ref[0, ro] = val * LOG2E


def _na_bias_blocks(rpb):
    n_r = 2 * NA_WIN_R - 1
    n_lh = rpb.shape[0] * rpb.shape[1]
    return pl.pallas_call(
        _na_bias_kernel,
        grid=(n_lh,),
        in_specs=[pl.BlockSpec(memory_space=pltpu.SMEM)],
        out_specs=pl.BlockSpec((1, n_r, GRID_W, GRID_W), lambda h: (h, 0, 0, 0)),
        out_shape=jax.ShapeDtypeStruct((n_lh, n_r, GRID_W, GRID_W), F32),
        compiler_params=_cparams(1),
        name="na_bias",
    )(rpb.reshape(-1))


def _na_block_geometry(rows):
    last_r0 = rows - NA_Q_ROWS
    return ((0, 0), (NA_Q_ROWS, 0), (last_r0, rows - NA_K_ROWS))


def _na_bias_tables(blocks, rows):
    masked = jnp.full(blocks[:, 0].shape, MASK_VALUE * LOG2E, F32)
    tabs = []
    for r0, k0 in _na_block_geometry(rows):
        q_rows = []
        for rq in range(NA_Q_ROWS):
            r = r0 + rq
            rs = min(max(r - NA_WIN_R // 2, 0), rows - NA_WIN_R)
            cols = []
            for kr in range(k0, k0 + NA_K_ROWS):
                cols.append(blocks[:, kr - r + NA_WIN_R - 1] if rs <= kr < rs + NA_WIN_R else masked)
            q_rows.append(jnp.concatenate(cols, axis=-1))
        tabs.append(jnp.concatenate(q_rows, axis=-2))
    return jnp.stack(tabs, axis=0)


def _lat_na_kernel(q_ref, k_ref, v_ref, kc_ref, vc_ref, bias_ref, o_ref, *, rows):
    i = pl.program_id(1)
    k0 = jnp.clip(i * NA_Q_ROWS - NA_WIN_R // 2, 0, rows - NA_K_ROWS)
    krows = pl.ds(pl.multiple_of(k0 * GRID_W, GRID_W), NA_K_ROWS * GRID_W)
    q = q_ref[...] * (NA_DH ** -0.5 * LOG2E)
    for t in range(2):
        sl = slice(LANES * t, LANES * (t + 1))
        qp = q[:, sl]
        kw = k_ref[krows, sl].astype(BF16)
        vw = v_ref[krows, sl].astype(BF16)
        kc = kc_ref[0, :, sl].astype(BF16)
        vc = vc_ref[0, :, sl].astype(BF16)
        bias = [bias_ref[2 * t], bias_ref[2 * t + 1]]
        o_ref[:, sl] = _attend_masked_heads(qp, [kw, kc], [vw, vc], bias)


def _lat_na(q, k, v, kc, vc, bias_tabs, seq):
    T = q.shape[0]
    past = kc.shape[1]
    rows = seq // GRID_W
    nq = rows // NA_Q_ROWS
    qrows = NA_Q_ROWS * GRID_W
    qblk = pl.BlockSpec((qrows, 256), lambda b, i: (b * nq + i, 0))
    kblk = pl.BlockSpec((seq, 256), lambda b, i: (b, 0))
    cblk = pl.BlockSpec((1, past, 256), lambda b, i: (b, 0, 0))
    variant = lambda b, i: (jnp.where(i == 0, 0, jnp.where(i == nq - 1, 2, 1)), 0, 0, 0)
    return pl.pallas_call(
        functools.partial(_lat_na_kernel, rows=rows),
        grid=(T // seq, nq),
        in_specs=[qblk, kblk, kblk, cblk, cblk,
                  pl.BlockSpec((None, NA_HEADS, qrows, NA_K_ROWS * GRID_W), variant)],
        out_specs=qblk,
        out_shape=jax.ShapeDtypeStruct((T, 256), F32),
        compiler_params=_cparams(2),
        name="lat_na",
    )(q, k, v, kc, vc, bias_tabs)


def _lat_df_kernel(q_ref, k_ref, v_ref, kc_ref, vc_ref, lam_ref, subln_ref, o_ref, *, lam_init):
    lam = _df_lambda(lam_ref, lam_init)
    for t in range(2):
        sl = slice(LANES * t, LANES * (t + 1))
        o = _diff_attn_tile(q_ref[...], [k_ref[...], kc_ref[0]], [v_ref[...], vc_ref[0]], lam, t)
        o_ref[:, sl] = _subln(o, subln_ref[:, sl], lam_init)


def _lat_df(q, k, v, kc, vc, df_lam, subln, seq, lam_init):
    T = q.shape[0]
    past = kc.shape[1]
    nq = seq // Q_BLOCK_DF
    qblk = pl.BlockSpec((Q_BLOCK_DF, 256), lambda b, i: (b * nq + i, 0))
    kblk = pl.BlockSpec((seq, 256), lambda b, i: (b, 0))
    cblk = pl.BlockSpec((1, past, 256), lambda b, i: (b, 0, 0))
    return pl.pallas_call(
        functools.partial(_lat_df_kernel, lam_init=lam_init),
        grid=(T // seq, nq),
        in_specs=[qblk, kblk, kblk, cblk, cblk,
                  pl.BlockSpec((4, DF_DQK), lambda b, i: (0, 0)),
                  pl.BlockSpec((1, 256), lambda b, i: (0, 0))],
        out_specs=qblk,
        out_shape=jax.ShapeDtypeStruct((T, 256), F32),
        compiler_params=_cparams(2),
        name="lat_df",
    )(q, k, v, kc, vc, df_lam, subln)


def _mlstm_kernel(*refs, nc, has_init, emit_state, state_first):
    refs = list(refs)
    q_ref, k_ref, v_ref, g_ref = refs[:4]
    pos = 4
    if has_init:
        c0_ref, m0_ref = refs[pos:pos + 2]
        pos += 2
    if emit_state and not state_first:
        pos += 2
    h_ref = refs[pos]
    pos += 1
    if emit_state:
        cn_ref, nn_ref, mn_ref = refs[pos:pos + 3]
        pos += 3
    caug_sc, m_sc = refs[pos:pos + 2]

    if has_init:
        caug_sc[...] = c0_ref[0]
        m_sc[...] = m0_ref[0]
    else:
        caug_sc[...] = jnp.zeros(caug_sc.shape, F32)
        m_sc[...] = jnp.zeros(m_sc.shape, F32)
    h_ref[...] = jnp.zeros(h_ref.shape, F32)

    ti = lax.broadcasted_iota(jnp.int32, (ML_CHUNK, ML_CHUNK), 0)
    si = lax.broadcasted_iota(jnp.int32, (ML_CHUNK, ML_CHUNK), 1)
    masks = (si <= ti, si >= ti)
    ones_blk = jnp.ones((ML_CHUNK, LANES), BF16)
    full = (ML_CHUNK, ML_CHUNK)

    def body(c, carry):
        for d in range(2):
            ci = c if d == 0 else nc - 1 - c
            start = pl.multiple_of(ci * ML_CHUNK, ML_CHUNK)
            rows = pl.ds(start, ML_CHUNK)
            mask = masks[d]
            tri = jnp.where(mask, 1.0, 0.0).astype(BF16)
            gblk = g_ref[rows, :]
            hi = gblk.astype(BF16)
            lo = (gblk - hi.astype(F32)).astype(BF16)
            bc = _dot(tri, hi) + _dot(tri, lo)
            g_t = gblk.T
            bc_t = bc.T
            last = ML_CHUNK - 1 if d == 0 else 0
            for hd in range(ML_HEADS):
                ch_i = 2 * ML_HEADS * d + hd
                ch_f = ch_i + ML_HEADS
                idx = ML_HEADS * d + hd
                hs = slice(ML_DH * hd, ML_DH * (hd + 1))
                b_col = bc[:, ch_f:ch_f + 1]
                a_col = gblk[:, ch_i:ch_i + 1] - b_col
                a_msk = jnp.where(mask, g_t[ch_i:ch_i + 1, :] - bc_t[ch_f:ch_f + 1, :], -jnp.inf)
                m_col = jnp.max(a_msk, axis=1, keepdims=True)
                m_rep = jnp.broadcast_to(m_col, full)
                b_rep = jnp.broadcast_to(b_col, full)
                a_rep = jnp.broadcast_to(a_col, full)
                m_last = m_col[last:last + 1, :]
                b_last = b_col[last:last + 1, :]
                qh = q_ref[rows, hs].astype(BF16)
                kf = k_ref[rows, hs]
                vaug = jnp.concatenate([v_ref[rows, hs].astype(BF16), ones_blk], axis=1)
                s0 = _dot_nt(qh, kf.astype(BF16)) * jnp.exp(a_msk - m_rep)
                sv0 = _dot(s0.astype(BF16), vaug)
                u0 = _dot_tn((jnp.exp(a_rep - m_last) * kf).astype(BF16), vaug)
                m_prev = m_sc[idx][0:1, 0:1]
                g_rep = jnp.maximum(m_prev, m_rep)
                w0 = jnp.exp(m_prev - g_rep)
                cc = jnp.exp(m_rep - g_rep)
                caug = caug_sc[idx]
                qc = _dot(qh, caug.astype(BF16))
                num = w0 * qc[:, :ML_DH] + cc * sv0[:, :ML_DH]
                den = w0 * qc[:, ML_DH:] + cc * sv0[:, ML_DH:]
                h_ref[rows, hs] += num / jnp.maximum(jnp.abs(den), jnp.exp(-b_rep - g_rep))
                g_last = jnp.maximum(m_prev, m_last)
                caug_sc[idx] = jnp.exp(m_prev - g_last) * caug + jnp.exp(m_last - g_last) * u0
                m_sc[idx] = jnp.broadcast_to(b_last + g_last, m_sc.shape[1:])
        return carry

    lax.fori_loop(0, nc, body, 0, unroll=LOOP_UNROLL)
    if emit_state:
        slots = [(0, dl) for dl in range(DEPTH)] if state_first else [(0,)]
        for slot in slots:
            cn_ref[slot] = caug_sc[:, :, :ML_DH]
        for idx in range(2 * ML_HEADS):
            n_row = caug_sc[idx][:, ML_DH:].T[0:1, :]
            for slot in slots:
                nn_ref[slot + (slice(idx, idx + 1), slice(None))] = n_row
        mn_ref[0] = m_sc[...]


def _mlstm(q, k, v, g, seq, init=None, emit_state=False, layer=0, state_prev=None):
    T = q.shape[0]
    n_chain = 2 * ML_HEADS
    blk = pl.BlockSpec((seq, 512), lambda b: (b, 0))
    gblk = pl.BlockSpec((seq, LANES), lambda b: (b, 0))
    cblk = pl.BlockSpec((1, n_chain, ML_DH, 2 * ML_DH), lambda b: (b, 0, 0, 0))
    mblk = pl.BlockSpec((1, n_chain, 8, LANES), lambda b: (b, 0, 0, 0))
    in_specs = [blk, blk, blk, gblk]
    args = [q, k, v, g]
    if init is not None:
        in_specs += [cblk, mblk]
        args += list(init)
    out_specs = [blk]
    out_shape = [jax.ShapeDtypeStruct((T, 512), F32)]
    nb = T // seq
    aliases = {}
    if emit_state:
        if state_prev is None:
            out_specs += [pl.BlockSpec((1, DEPTH, n_chain, ML_DH, ML_DH), lambda b: (b, 0, 0, 0, 0)),
                          pl.BlockSpec((1, DEPTH, n_chain, ML_DH), lambda b: (b, 0, 0, 0)), mblk]
        else:
            out_specs += [pl.BlockSpec((1, None, n_chain, ML_DH, ML_DH), lambda b: (b, layer, 0, 0, 0)),
                          pl.BlockSpec((1, None, n_chain, ML_DH), lambda b: (b, layer, 0, 0)), mblk]
            aliases = {len(args): 1, len(args) + 1: 2}
            in_specs += [pl.BlockSpec(memory_space=pl.ANY)] * 2
            args += list(state_prev)
        out_shape += [jax.ShapeDtypeStruct((nb, DEPTH, n_chain, ML_DH, ML_DH), F32),
                      jax.ShapeDtypeStruct((nb, DEPTH, n_chain, ML_DH), F32),
                      jax.ShapeDtypeStruct((nb, n_chain, 8, LANES), F32)]
    return pl.pallas_call(
        functools.partial(_mlstm_kernel, nc=seq // ML_CHUNK, has_init=init is not None,
                          emit_state=emit_state, state_first=state_prev is None),
        grid=(nb,),
        in_specs=in_specs,
        out_specs=out_specs,
        out_shape=out_shape,
        input_output_aliases=aliases,
        scratch_shapes=[pltpu.VMEM((n_chain, ML_DH, 2 * ML_DH), F32),
                        pltpu.VMEM((n_chain, 8, LANES), F32)],
        compiler_params=_cparams(1),
        name="mlstm_state" if emit_state else "mlstm",
    )(*args)


def _outproj_kernel(x_ref, nao_ref, dfo_ref, mlh_ref, mo_ref, mod_ref, g2_ref, outn_ref, wout_ref,
                    wr_ref, br_ref, x1_ref, h2_ref, gates_ref):
    mlh = mlh_ref[...]
    parts = []
    for hd in range(ML_HEADS):
        seg = mlh[:, ML_DH * hd:ML_DH * (hd + 1)]
        parts.append(seg * lax.rsqrt(jnp.mean(seg * seg, axis=-1, keepdims=True) + EPS))
    ml_o = jnp.concatenate(parts, axis=1) * outn_ref[...] * jax.nn.sigmoid(mo_ref[...])
    mix = (_dot(nao_ref[...].astype(BF16), wout_ref[0:256, :])
           + _dot(dfo_ref[...].astype(BF16), wout_ref[256:512, :])
           + _dot(ml_o.astype(BF16), wout_ref[512:1024, :]))
    x1 = x_ref[...] + mod_ref[0, 2:3, :] * mix
    x1_ref[...] = x1
    y = x1 * lax.rsqrt(jnp.mean(x1 * x1, axis=-1, keepdims=True) + EPS) * g2_ref[...]
    h2 = y * (1.0 + mod_ref[0, 4:5, :]) + mod_ref[0, 3:4, :]
    h2_ref[...] = h2.astype(BF16)

    wr = wr_ref[...]
    h_hi = h2.astype(BF16)
    h_lo = (h2 - h_hi.astype(F32)).astype(BF16)
    w_hi = wr.astype(BF16)
    w_lo = (wr - w_hi.astype(F32)).astype(BF16)
    logits = _dot(h_hi, w_hi) + _dot(h_lo, w_hi) + _dot(h_hi, w_lo)
    s = jax.nn.sigmoid(logits)
    work = s + br_ref[...]
    lane = _lane_iota(work.shape)
    s_sel = jnp.zeros(work.shape, F32)
    for _ in range(TOP_K):
        mx = jnp.max(work, axis=-1, keepdims=True)
        first = jnp.min(jnp.where(work == mx, lane, N_EXPERTS), axis=-1, keepdims=True)
        hit = lane == first
        s_sel = jnp.where(hit, s, s_sel)
        work = jnp.where(hit, -jnp.inf, work)
    gates_ref[...] = s_sel / jnp.sum(s_sel, axis=-1, keepdims=True) * ROUTED_SCALE


def _outproj(x, nao, dfo, mlh, mo, mod_p, tokens_per_mod, layer, g2, outn, w_out_bf, w_router, b_router):
    T = x.shape[0]
    tm = TM_PROJ
    row = lambda i: (i, 0)
    const = lambda i: (0, 0)
    lay = lambda i: (layer, 0, 0)
    return pl.pallas_call(
        _outproj_kernel,
        grid=(T // tm,),
        in_specs=[pl.BlockSpec((tm, D_MODEL), row),
                  pl.BlockSpec((tm, 256), row), pl.BlockSpec((tm, 256), row),
                  pl.BlockSpec((tm, 512), row), pl.BlockSpec((tm, 512), row),
                  pl.BlockSpec((1, 6, D_MODEL), lambda i: ((i * tm) // tokens_per_mod, 0, 0)),
                  pl.BlockSpec((1, D_MODEL), const), pl.BlockSpec((1, 512), const),
                  pl.BlockSpec((None, D_MODEL, D_MODEL), lay),
                  pl.BlockSpec((None, D_MODEL, N_EXPERTS), lay), pl.BlockSpec((1, N_EXPERTS), const)],
        out_specs=[pl.BlockSpec((tm, D_MODEL), row), pl.BlockSpec((tm, D_MODEL), row),
                   pl.BlockSpec((tm, N_EXPERTS), row)],
        out_shape=[jax.ShapeDtypeStruct((T, D_MODEL), F32), jax.ShapeDtypeStruct((T, D_MODEL), BF16),
                   jax.ShapeDtypeStruct((T, N_EXPERTS), F32)],
        compiler_params=_cparams(1),
        name="outproj_router",
    )(x, nao, dfo, mlh, mo, mod_p, g2, outn, w_out_bf, w_router, b_router)


def _moe_kernel(h_ref, gates_ref, x1_ref, mod_ref, w1_ref, w3_ref, w2_ref, s1_ref, s3_ref, s2_ref, o_ref):
    e = pl.program_id(1)
    h = h_ref[...]

    @pl.when(e == 0)
    def _():
        a = _dot(h, s1_ref[...].astype(BF16))
        b = _dot(h, s3_ref[...].astype(BF16))
        o_ref[...] = _dot((_silu(a) * b).astype(BF16), s2_ref[...].astype(BF16))

    gates = gates_ref[...]
    lane = _lane_iota(gates.shape)
    hids = []
    for j in range(MOE_EXPERT_BLOCK):
        a = _dot(h, w1_ref[j].astype(BF16))
        b = _dot(h, w3_ref[j].astype(BF16))
        g = jnp.sum(jnp.where(lane == e * MOE_EXPERT_BLOCK + j, gates, 0.0), axis=-1, keepdims=True)
        hids.append((_silu(a) * b * g).astype(BF16))
    w2 = w2_ref[...].reshape(MOE_EXPERT_BLOCK * D_EXPERT, D_MODEL).astype(BF16)
    o_ref[...] += _dot(jnp.concatenate(hids, axis=1), w2)

    @pl.when(e == N_EXPERTS // MOE_EXPERT_BLOCK - 1)
    def _():
        o_ref[...] = x1_ref[...] + mod_ref[0, 5:6, :] * o_ref[...]


def _moe(h2, gates, x1, mod_p, tokens_per_mod, layer, w1, w3, w2, s1, s3, s2):
    T = h2.shape[0]
    tm = TM_MOE
    eb = MOE_EXPERT_BLOCK
    row = lambda i, e: (i, 0)
    wexp = lambda i, e: (layer, e, 0, 0)
    wsh = lambda i, e: (layer, 0, 0)
    return pl.pallas_call(
        _moe_kernel,
        grid=(T // tm, N_EXPERTS // eb),
        in_specs=[pl.BlockSpec((tm, D_MODEL), row), pl.BlockSpec((tm, N_EXPERTS), row),
                  pl.BlockSpec((tm, D_MODEL), row),
                  pl.BlockSpec((1, 6, D_MODEL), lambda i, e: ((i * tm) // tokens_per_mod, 0, 0)),
                  pl.BlockSpec((None, eb, D_MODEL, D_EXPERT), wexp),
                  pl.BlockSpec((None, eb, D_MODEL, D_EXPERT), wexp),
                  pl.BlockSpec((None, eb, D_EXPERT, D_MODEL), wexp),
                  pl.BlockSpec((None, D_MODEL, D_EXPERT), wsh), pl.BlockSpec((None, D_MODEL, D_EXPERT), wsh),
                  pl.BlockSpec((None, D_EXPERT, D_MODEL), wsh)],
        out_specs=pl.BlockSpec((tm, D_MODEL), row),
        out_shape=jax.ShapeDtypeStruct((T, D_MODEL), F32),
        compiler_params=_cparams(2),
        name="moe_experts",
    )(h2, gates, x1, mod_p, w1, w3, w2, s1, s3, s2)


def _group_matrix(group):
    i = np.arange(256)
    return jnp.asarray((i[:, None] // group) == (i[None, :] // group), BF16)


def _rope_tables(seq):
    t = np.arange(seq)
    lane = np.arange(256)
    j = lane % DF_DQK
    nf = DF_DQK // 4
    pos = np.where((j < DF_DQK // 2)[None, :], (t // GRID_W)[:, None], (t % GRID_W)[:, None]).astype(np.float32)
    inv = (1.0 / (ROPE_BASE ** (np.arange(nf, dtype=np.float32) / nf))).astype(np.float32)
    ang = pos * inv[j % nf][None, :]
    sign = np.where((lane % (2 * nf)) < nf, -1.0, 1.0).astype(np.float32)
    return jnp.asarray(np.cos(ang), F32), jnp.asarray(np.sin(ang) * sign[None, :], F32)


def _token_major(a):
    b, h, l, d = a.shape
    return a.transpose(0, 2, 1, 3).reshape(b, l, h * d)


def kernel(x_prompt, x_sample, cache_na_k, cache_na_v, cache_df_k, cache_df_v, state_ml_C, state_ml_n,
           state_ml_m, c, c_ctx, w_ada, b_ada, g_norm1, g_norm2, w_in, w_out, na_qn, na_kn, na_rpb,
           df_qn, df_kn, df_lam, df_subln, ml_gate_b, ml_outn, w_router, b_router, w_exp1, w_exp3, w_exp2,
           w_sh1, w_sh3, w_sh2):
    bc, lc, _ = x_prompt.shape
    bs, ls, _ = x_sample.shape
    past = cache_na_k.shape[3]

    cond = jnp.zeros((8, D_MODEL), F32).at[0].set(c_ctx).at[1:1 + bs].set(c)
    mod = _modulation(cond, w_ada, b_ada)
    g64 = _group_matrix(NA_DH)
    g32 = _group_matrix(DF_DQK)
    rope_tabs = _rope_tables(ls)
    w_in_bf = jnp.pad(w_in.astype(BF16), ((0, 0), (0, 0), (0, IN_COLS_PAD - IN_COLS)))
    w_out_bf = w_out.astype(BF16)
    na_blocks = _na_bias_blocks(na_rpb)
    moe_w = (w_exp1, w_exp3, w_exp2, w_sh1, w_sh3, w_sh2)

    y_p = x_prompt.reshape(bc * lc, D_MODEL)
    y_s = x_sample.reshape(bs * ls, D_MODEL)
    caches = states = None
    m_states = []
    for l in range(DEPTH):
        lam_init = 0.8 - 0.6 * math.exp(-0.3 * l)
        mod_l = mod[l].reshape(8, 6, D_MODEL)
        g1 = g_norm1[l][None, :]
        g2 = g_norm2[l][None, :]
        qn = jnp.tile(na_qn[l], NA_HEADS)[None, :]
        kn = jnp.tile(na_kn[l], NA_HEADS)[None, :]
        dqn = jnp.tile(df_qn[l], 2 * DF_HEADS)[None, :]
        dkn = jnp.tile(df_kn[l], 2 * DF_HEADS)[None, :]
        subln = jnp.tile(df_subln[l], DF_HEADS)[None, :]
        outn = jnp.tile(ml_outn[l], ML_HEADS)[None, :]
        gate_b = jnp.pad(ml_gate_b[l].reshape(-1), (0, LANES - 4 * ML_HEADS))[None, :]
        br = b_router[l][None, :]
        proj_w = (l, g1, w_in_bf, qn, kn, dqn, dkn, gate_b, g64, g32)

        mod_c = mod_l[0:1]
        tpm = bc * lc
        res = _inproj(y_p, mod_c, tpm, *proj_w, None, cache_seq=lc, cache_prev=caches)
        naq, nak, nav, dfq, dfk, dfv, mq, mk, mv, mo, mg = res[:11]
        caches = tuple(res[11:])
        nao, dfo = _ctx_attention(naq, nak, nav, dfq, dfk, dfv, df_lam[l], subln, lc, lam_init)
        mlh, c_n, n_n, m_n = _mlstm(mq, mk, mv, mg, lc, emit_state=True, layer=l, state_prev=states)
        states = (c_n, n_n)
        x1, h2, gates = _outproj(y_p, nao, dfo, mlh, mo, mod_c, tpm, l, g2, outn, w_out_bf, w_router, br)
        y_p = _moe(h2, gates, x1, mod_c, tpm, l, *moe_w)
        m_states.append(m_n[:, :, 0, 0].reshape(bc, 2, ML_HEADS))

        mod_s = mod_l[1:1 + bs]
        naq, nak, nav, dfq, dfk, dfv, mq, mk, mv, mo, mg = _inproj(y_s, mod_s, ls, *proj_w, rope_tabs)
        kc_na = _token_major(cache_na_k[:, l])
        vc_na = _token_major(cache_na_v[:, l])
        kc_df = cache_df_k[:, l].transpose(0, 3, 1, 2, 4).reshape(bs, past, 256)
        vc_df = _token_major(cache_df_v[:, l])
        bias_tabs = _na_bias_tables(na_blocks[l * NA_HEADS:(l + 1) * NA_HEADS], ls // GRID_W)
        nao = _lat_na(naq, nak, nav, kc_na, vc_na, bias_tabs, ls)
        dfo = _lat_df(dfq, dfk, dfv, kc_df, vc_df, df_lam[l], subln, ls, lam_init)
        n0 = jnp.broadcast_to(state_ml_n[:, l][..., None], (bs, 2, ML_HEADS, ML_DH, ML_DH))
        c0 = jnp.concatenate([state_ml_C[:, l], n0], axis=-1)
        c0 = c0.reshape(bs, 2 * ML_HEADS, ML_DH, 2 * ML_DH)
        m0 = jnp.broadcast_to(state_ml_m[:, l].reshape(bs, 2 * ML_HEADS, 1, 1), (bs, 2 * ML_HEADS, 8, LANES))
        (mlh,) = _mlstm(mq, mk, mv, mg, ls, init=(c0, m0))
        x1, h2, gates = _outproj(y_s, nao, dfo, mlh, mo, mod_s, ls, l, g2, outn, w_out_bf, w_router, br)
        y_s = _moe(h2, gates, x1, mod_s, ls, l, *moe_w)

    new_ml_c = states[0].reshape(bc, DEPTH, 2, ML_HEADS, ML_DH, ML_DH)
    new_ml_n = states[1].reshape(bc, DEPTH, 2, ML_HEADS, ML_DH)
    return (y_p.reshape(bc, lc, D_MODEL), y_s.reshape(bs, ls, D_MODEL), *caches,
            new_ml_c, new_ml_n, jnp.stack(m_states, axis=1))
```

```python
import functools
import math

import numpy as np
import jax
import jax.numpy as jnp
from jax import lax
from jax.experimental import pallas as pl
from jax.experimental.pallas import tpu as pltpu

F32 = jnp.float32
BF16 = jnp.bfloat16

D_MODEL = 1024
DEPTH = 2
GRID_W = 64
NA_HEADS = 4
NA_DH = 64
NA_WIN_R = 8
NA_WIN_C = 16
DF_HEADS = 4
DF_DV = 64
DF_DQK = 32
ML_HEADS = 4
ML_DH = 128
ML_CHUNK = 128
N_EXPERTS = 32
TOP_K = 4
D_EXPERT = 256
ROUTED_SCALE = 2.5
ROPE_BASE = 10000.0
EPS = 1e-6

LANES = 128
IN_COLS = 3600
IN_COLS_PAD = 3712
GATE_COL0 = 3584
MASK_VALUE = -1e30
VMEM_LIMIT = 56 * 1024 * 1024

TM_PROJ = 512
TM_MOE = 1024
MOE_EXPERT_BLOCK = 2
MOE_TILE = 256
MOE_GRANULE = 16
MOE_SLOTS = TOP_K * MOE_TILE + N_EXPERTS * MOE_GRANULE
MOE_ROWS = 512
MOE_DMA_WINDOW = 32
Q_BLOCK_DF = 256
CTX_ATTN_BATCHES = 2
LOOP_UNROLL = 2
NA_Q_ROWS = 4
NA_K_ROWS = NA_Q_ROWS + NA_WIN_R
TM_PROJ_CTX = 256
LOG2E = math.log2(math.e)


def _cparams(n_axes):
    return pltpu.CompilerParams(dimension_semantics=("arbitrary",) * n_axes,
                                vmem_limit_bytes=VMEM_LIMIT)


def _silu(x):
    return x * jax.nn.sigmoid(x)


def _dot(a, b):
    return jnp.dot(a, b, preferred_element_type=F32)


def _dot_nt(a, b):
    return lax.dot_general(a, b, (((1,), (1,)), ((), ())), preferred_element_type=F32)


def _dot_tn(a, b):
    return lax.dot_general(a, b, (((0,), (0,)), ((), ())), preferred_element_type=F32)


def _lane_iota(shape):
    return lax.broadcasted_iota(jnp.int32, shape, 1)


def _mod_kernel(cond_ref, w_ref, b_ref, o_ref):
    s = _silu(cond_ref[...]).astype(BF16)
    o_ref[0] = _dot(s, w_ref[0].astype(BF16)) + b_ref[0]


def _modulation(cond, w_ada, b_ada):
    tn = 1536
    n = 6 * D_MODEL
    return pl.pallas_call(
        _mod_kernel,
        grid=(DEPTH, n // tn),
        in_specs=[pl.BlockSpec((8, D_MODEL), lambda l, j: (0, 0)),
                  pl.BlockSpec((1, D_MODEL, tn), lambda l, j: (l, 0, j)),
                  pl.BlockSpec((1, 1, tn), lambda l, j: (l, 0, j))],
        out_specs=pl.BlockSpec((1, 8, tn), lambda l, j: (l, 0, j)),
        out_shape=jax.ShapeDtypeStruct((DEPTH, 8, n), F32),
        compiler_params=_cparams(2),
        name="adaln_mod",
    )(cond, w_ada, b_ada.reshape(DEPTH, 1, n))


def _group_rms(v, gmat_ref, group):
    sq = v * v
    hi = sq.astype(BF16)
    lo = (sq - hi.astype(F32)).astype(BF16)
    ss = _dot(hi, gmat_ref[...]) + _dot(lo, gmat_ref[...])
    return v * lax.rsqrt(ss * (1.0 / group) + EPS)


def _rope(v, cos, sin_signed):
    first = (_lane_iota(v.shape) % 16) < 8
    partner = jnp.where(first, pltpu.roll(v, v.shape[1] - 8, 1), pltpu.roll(v, 8, 1))
    return v * cos + partner * sin_signed


def _inproj_kernel(*refs, rope, cache_seq, cache_first):
    (x_ref, mod_ref, g1_ref, w_ref, qn_ref, kn_ref, dqn_ref, dkn_ref, gb_ref, g64_ref, g32_ref) = refs[:11]
    pos = 11
    if rope:
        cos_ref, sin_ref = refs[pos:pos + 2]
        pos += 2
    if cache_seq and not cache_first:
        pos += 4
    (naq_ref, nak_ref, nav_ref, dfq_ref, dfk_ref, dfv_ref,
     mq_ref, mk_ref, mv_ref, mo_ref, mg_ref) = refs[pos:pos + 11]
    pos += 11
    x = x_ref[...]
    y = x * lax.rsqrt(jnp.mean(x * x, axis=-1, keepdims=True) + EPS) * g1_ref[...]
    h = (y * (1.0 + mod_ref[0, 1:2, :]) + mod_ref[0, 0:1, :]).astype(BF16)

    def proj(a, b):
        return _dot(h, w_ref[:, a:b])

    naq_ref[...] = _group_rms(proj(0, 256), g64_ref, NA_DH) * qn_ref[...]
    nak = _group_rms(proj(256, 512), g64_ref, NA_DH) * kn_ref[...]
    nav = proj(512, 768)
    dfq = _group_rms(proj(768, 1024), g32_ref, DF_DQK) * dqn_ref[...]
    dfk = _group_rms(proj(1024, 1280), g32_ref, DF_DQK) * dkn_ref[...]
    dfv = proj(1280, 1536)
    if rope:
        dfq = _rope(dfq, cos_ref[...], sin_ref[...])
        dfk = _rope(dfk, cos_ref[...], sin_ref[...])
    nak_ref[...] = nak
    nav_ref[...] = nav
    dfq_ref[...] = dfq
    dfk_ref[...] = dfk
    dfv_ref[...] = dfv
    if cache_seq:
        nk_hm, nv_hm, dk_hm, dv_hm = refs[pos:pos + 4]

        def put(ref, bi, idx, val):
            if cache_first:
                for dl in range(DEPTH):
                    ref[(bi, dl) + idx] = val
            else:
                ref[(bi,) + idx] = val

        for bi in range(x.shape[0] // cache_seq):
            rs = slice(bi * cache_seq, (bi + 1) * cache_seq)
            for hd in range(NA_HEADS):
                put(nk_hm, bi, (hd,), nak[rs, NA_DH * hd:NA_DH * (hd + 1)])
                put(nv_hm, bi, (hd,), nav[rs, NA_DH * hd:NA_DH * (hd + 1)])
                put(dv_hm, bi, (hd,), dfv[rs, DF_DV * hd:DF_DV * (hd + 1)])
                for a in range(2):
                    c0 = (2 * hd + a) * DF_DQK
                    put(dk_hm, bi, (hd, a), dfk[rs, c0:c0 + DF_DQK])
    mq_ref[...] = proj(1536, 2048)
    mk_ref[...] = proj(2048, 2560) * (ML_DH ** -0.5)
    mv_ref[...] = proj(2560, 3072)
    mo_ref[...] = proj(3072, 3584)
    g = proj(GATE_COL0, IN_COLS_PAD) + gb_ref[...]
    lane = _lane_iota(g.shape)
    is_forget = ((lane // ML_HEADS) % 2 == 1) & (lane < 4 * ML_HEADS)
    log_sig = jnp.minimum(g, 0.0) - jnp.log1p(jnp.exp(-jnp.abs(g)))
    mg_ref[...] = jnp.where(is_forget, log_sig, g)


def _inproj(x, mod_p, tokens_per_mod, layer, g1, w_in_bf, qn, kn, dqn, dkn, gate_b, g64, g32, rope_tabs,
            cache_seq=0, cache_prev=None):
    T = x.shape[0]
    tm = TM_PROJ_CTX if cache_seq else TM_PROJ
    rope = rope_tabs is not None
    row = lambda i: (i, 0)
    const = lambda i: (0, 0)
    in_specs = [pl.BlockSpec((tm, D_MODEL), row),
                pl.BlockSpec((1, 6, D_MODEL), lambda i: ((i * tm) // tokens_per_mod, 0, 0)),
                pl.BlockSpec((1, D_MODEL), const),
                pl.BlockSpec((None, D_MODEL, IN_COLS_PAD), lambda i: (layer, 0, 0)),
                pl.BlockSpec((1, 256), const), pl.BlockSpec((1, 256), const),
                pl.BlockSpec((1, 256), const), pl.BlockSpec((1, 256), const),
                pl.BlockSpec((1, LANES), const),
                pl.BlockSpec((256, 256), const), pl.BlockSpec((256, 256), const)]
    args = [x, mod_p, g1, w_in_bf, qn, kn, dqn, dkn, gate_b, g64, g32]
    if rope:
        seq = rope_tabs[0].shape[0]
        tab = lambda i: (i % (seq // tm), 0)
        in_specs += [pl.BlockSpec((tm, 256), tab), pl.BlockSpec((tm, 256), tab)]
        args += list(rope_tabs)
    widths = [256] * 6 + [512] * 4 + [LANES]
    out_specs = [pl.BlockSpec((tm, w), row) for w in widths]
    out_shape = [jax.ShapeDtypeStruct((T, w), F32) for w in widths]
    aliases = {}
    if cache_seq:
        nb = tm // cache_seq
        hm = (NA_HEADS, cache_seq, NA_DH)
        hm_df = (DF_HEADS, 2, cache_seq, DF_DQK)
        for k, tail in enumerate((hm, hm, hm_df, hm)):
            zeros = (0,) * len(tail)
            if cache_prev is None:
                out_specs.append(pl.BlockSpec((nb, DEPTH) + tail, lambda i, z=zeros: (i, 0) + z))
            else:
                out_specs.append(pl.BlockSpec((nb, None) + tail, lambda i, z=zeros: (i, layer) + z))
                aliases[len(args)] = len(widths) + k
                in_specs.append(pl.BlockSpec(memory_space=pl.ANY))
                args.append(cache_prev[k])
            out_shape.append(jax.ShapeDtypeStruct((T // cache_seq, DEPTH) + tail, F32))
    return pl.pallas_call(
        functools.partial(_inproj_kernel, rope=rope, cache_seq=cache_seq, cache_first=cache_prev is None),
        grid=(T // tm,),
        in_specs=in_specs,
        out_specs=out_specs,
        out_shape=out_shape,
        input_output_aliases=aliases,
        compiler_params=_cparams(1),
        name="inproj_rope" if rope else "inproj",
    )(*args)


def _df_lambda(lam_ref, lam_init):
    v = lam_ref[...]
    a = jnp.sum(v[0:1] * v[1:2], axis=1, keepdims=True)
    b = jnp.sum(v[2:3] * v[3:4], axis=1, keepdims=True)
    return jnp.exp(a) - jnp.exp(b) + lam_init


def _softmax_parts(scores):
    es = _softmax_exps(scores)
    tot = functools.reduce(lambda a, b: a + b, [jnp.sum(e, axis=-1, keepdims=True) for e in es])
    return [e.astype(BF16) for e in es], 1.0 / tot


def _softmax_exps(scores):
    m = functools.reduce(jnp.maximum, [jnp.max(s, axis=-1, keepdims=True) for s in scores])
    return [jnp.exp2(s - m) for s in scores]


def _attend_masked_heads(qp, k_segs, v_segs, bias=None):
    lane = _lane_iota(qp.shape)
    v_aug = [jnp.concatenate([v, jnp.ones(v.shape, BF16)], axis=1) for v in v_segs]
    outs = []
    for j in range(2):
        qm = jnp.where(lane // NA_DH == j, qp, 0.0).astype(BF16)
        scores = [_dot_nt(qm, k) for k in k_segs]
        if bias is not None:
            scores[0] = scores[0] + bias[j]
        es = _softmax_exps(scores)
        o = functools.reduce(lambda x, y: x + y, [_dot(e.astype(BF16), v) for e, v in zip(es, v_aug)])
        outs.append(o[:, :LANES] / o[:, LANES:])
    return jnp.where(lane < NA_DH, outs[0], outs[1])


def _subln(o, subln_row, lam_init):
    lane = _lane_iota(o.shape)
    sq = o * o
    s0 = jnp.sum(jnp.where(lane < DF_DV, sq, 0.0), axis=-1, keepdims=True)
    s1 = jnp.sum(jnp.where(lane >= DF_DV, sq, 0.0), axis=-1, keepdims=True)
    ms = jnp.where(lane < DF_DV, s0, s1) * (1.0 / DF_DV)
    return o * lax.rsqrt(ms + EPS) * subln_row * (1.0 - lam_init)


def _diff_attn_tile(q, k_segs, v_segs, lam, t):
    sl = slice(LANES * t, LANES * (t + 1))
    qp = q[:, sl] * (DF_DQK ** -0.5 * LOG2E)
    lane = _lane_iota(qp.shape)
    kps = [k[:, sl].astype(BF16) for k in k_segs]
    vps = [v[:, sl].astype(BF16) for v in v_segs]
    outs = []
    for j in range(2):
        maps = []
        for a in range(2):
            qm = jnp.where(lane // DF_DQK == 2 * j + a, qp, 0.0).astype(BF16)
            maps.append(_softmax_parts([_dot_nt(qm, kp) for kp in kps]))
        (e0, r0), (e1, r1) = maps
        c0 = r0.astype(BF16)
        c1 = (-(lam * r1)).astype(BF16)
        o = functools.reduce(lambda x, y: x + y,
                             [_dot(x0 * c0 + x1 * c1, vp) for x0, x1, vp in zip(e0, e1, vps)])
        outs.append(o)
    return jnp.where(lane < DF_DV, outs[0], outs[1])


def _ctx_attn_kernel(naq_ref, nak_ref, nav_ref, dfq_ref, dfk_ref, dfv_ref, lam_ref, subln_ref,
                     nao_ref, dfo_ref, *, lam_init, seq):
    lam = _df_lambda(lam_ref, lam_init)
    for bi in range(naq_ref.shape[0] // seq):
        rs = slice(bi * seq, (bi + 1) * seq)
        for t in range(2):
            sl = slice(LANES * t, LANES * (t + 1))
            qp = naq_ref[rs, sl] * (NA_DH ** -0.5 * LOG2E)
            kp = nak_ref[rs, sl].astype(BF16)
            vp = nav_ref[rs, sl].astype(BF16)
            nao_ref[rs, sl] = _attend_masked_heads(qp, [kp], [vp])
            o = _diff_attn_tile(dfq_ref[rs, :], [dfk_ref[rs, :]], [dfv_ref[rs, :]], lam, t)
            dfo_ref[rs, sl] = _subln(o, subln_ref[:, sl], lam_init)


def _ctx_attention(naq, nak, nav, dfq, dfk, dfv, df_lam, subln, seq, lam_init):
    T = naq.shape[0]
    rows = CTX_ATTN_BATCHES * seq
    blk = pl.BlockSpec((rows, 256), lambda b: (b, 0))
    return pl.pallas_call(
        functools.partial(_ctx_attn_kernel, lam_init=lam_init, seq=seq),
        grid=(T // rows,),
        in_specs=[blk] * 6 + [pl.BlockSpec((4, DF_DQK), lambda b: (0, 0)),
                              pl.BlockSpec((1, 256), lambda b: (0, 0))],
        out_specs=[blk, blk],
        out_shape=[jax.ShapeDtypeStruct((T, 256), F32)] * 2,
        compiler_params=_cparams(1),
        name="ctx_attention",
    )(naq, nak, nav, dfq, dfk, dfv, df_lam, subln)


def _na_bias_kernel(rpb_ref, o_ref):
    hd = pl.program_id(0)
    shape = (GRID_W, GRID_W)
    qi = lax.broadcasted_iota(jnp.int32, shape, 0)
    kc = lax.broadcasted_iota(jnp.int32, shape, 1)
    cs = jnp.clip(qi - NA_WIN_C // 2, 0, GRID_W - NA_WIN_C)
    inwin = (kc >= cs) & (kc < cs + NA_WIN_C)
    off = jnp.where(inwin, kc - qi + NA_WIN_C - 1, -1)
    n_c = 2 * NA_WIN_C - 1
    n_r = 2 * NA_WIN_R - 1
    for ro in range(n_r):
        val = jnp.full(shape, MASK_VALUE, F32)
        for d in range(n_c):
            val = jnp.where(off == d, rpb_ref[hd * (n_r * n_c) + ro * n_c + d], val)
        o_ref[0, ro] = val * LOG2E


def _na_bias_blocks(rpb):
    n_r = 2 * NA_WIN_R - 1
    n_lh = rpb.shape[0] * rpb.shape[1]
    return pl.pallas_call(
        _na_bias_kernel,
        grid=(n_lh,),
        in_specs=[pl.BlockSpec(memory_space=pltpu.SMEM)],
        out_specs=pl.BlockSpec((1, n_r, GRID_W, GRID_W), lambda h: (h, 0, 0, 0)),
        out_shape=jax.ShapeDtypeStruct((n_lh, n_r, GRID_W, GRID_W), F32),
        compiler_params=_cparams(1),
        name="na_bias",
    )(rpb.reshape(-1))


def _na_block_geometry(rows):
    last_r0 = rows - NA_Q_ROWS
    return ((0, 0), (NA_Q_ROWS, 0), (last_r0, rows - NA_K_ROWS))


def _na_bias_tables(blocks, rows):
    masked = jnp.full(blocks[:, 0].shape, MASK_VALUE * LOG2E, F32)
    tabs = []
    for r0, k0 in _na_block_geometry(rows):
        q_rows = []
        for rq in range(NA_Q_ROWS):
            r = r0 + rq
            rs = min(max(r - NA_WIN_R // 2, 0), rows - NA_WIN_R)
            cols = []
            for kr in range(k0, k0 + NA_K_ROWS):
                cols.append(blocks[:, kr - r + NA_WIN_R - 1] if rs <= kr < rs + NA_WIN_R else masked)
            q_rows.append(jnp.concatenate(cols, axis=-1))
        tabs.append(jnp.concatenate(q_rows, axis=-2))
    return jnp.stack(tabs, axis=0)


def _lat_na_kernel(q_ref, k_ref, v_ref, kc_ref, vc_ref, bias_ref, o_ref, *, rows):
    i = pl.program_id(1)
    k0 = jnp.clip(i * NA_Q_ROWS - NA_WIN_R // 2, 0, rows - NA_K_ROWS)
    krows = pl.ds(pl.multiple_of(k0 * GRID_W, GRID_W), NA_K_ROWS * GRID_W)
    q = q_ref[...] * (NA_DH ** -0.5 * LOG2E)
    for t in range(2):
        sl = slice(LANES * t, LANES * (t + 1))
        qp = q[:, sl]
        kw = k_ref[krows, sl].astype(BF16)
        vw = v_ref[krows, sl].astype(BF16)
        kc = kc_ref[0, :, sl].astype(BF16)
        vc = vc_ref[0, :, sl].astype(BF16)
        bias = [bias_ref[2 * t], bias_ref[2 * t + 1]]
        o_ref[:, sl] = _attend_masked_heads(qp, [kw, kc], [vw, vc], bias)


def _lat_na(q, k, v, kc, vc, bias_tabs, seq):
    T = q.shape[0]
    past = kc.shape[1]
    rows = seq // GRID_W
    nq = rows // NA_Q_ROWS
    qrows = NA_Q_ROWS * GRID_W
    qblk = pl.BlockSpec((qrows, 256), lambda b, i: (b * nq + i, 0))
    kblk = pl.BlockSpec((seq, 256), lambda b, i: (b, 0))
    cblk = pl.BlockSpec((1, past, 256), lambda b, i: (b, 0, 0))
    variant = lambda b, i: (jnp.where(i == 0, 0, jnp.where(i == nq - 1, 2, 1)), 0, 0, 0)
    return pl.pallas_call(
        functools.partial(_lat_na_kernel, rows=rows),
        grid=(T // seq, nq),
        in_specs=[qblk, kblk, kblk, cblk, cblk,
                  pl.BlockSpec((None, NA_HEADS, qrows, NA_K_ROWS * GRID_W), variant)],
        out_specs=qblk,
        out_shape=jax.ShapeDtypeStruct((T, 256), F32),
        compiler_params=_cparams(2),
        name="lat_na",
    )(q, k, v, kc, vc, bias_tabs)


def _lat_df_kernel(q_ref, k_ref, v_ref, kc_ref, vc_ref, lam_ref, subln_ref, o_ref, *, lam_init):
    lam = _df_lambda(lam_ref, lam_init)
    for t in range(2):
        sl = slice(LANES * t, LANES * (t + 1))
        o = _diff_attn_tile(q_ref[...], [k_ref[...], kc_ref[0]], [v_ref[...], vc_ref[0]], lam, t)
        o_ref[:, sl] = _subln(o, subln_ref[:, sl], lam_init)


def _lat_df(q, k, v, kc, vc, df_lam, subln, seq, lam_init):
    T = q.shape[0]
    past = kc.shape[1]
    nq = seq // Q_BLOCK_DF
    qblk = pl.BlockSpec((Q_BLOCK_DF, 256), lambda b, i: (b * nq + i, 0))
    kblk = pl.BlockSpec((seq, 256), lambda b, i: (b, 0))
    cblk = pl.BlockSpec((1, past, 256), lambda b, i: (b, 0, 0))
    return pl.pallas_call(
        functools.partial(_lat_df_kernel, lam_init=lam_init),
        grid=(T // seq, nq),
        in_specs=[qblk, kblk, kblk, cblk, cblk,
                  pl.BlockSpec((4, DF_DQK), lambda b, i: (0, 0)),
                  pl.BlockSpec((1, 256), lambda b, i: (0, 0))],
        out_specs=qblk,
        out_shape=jax.ShapeDtypeStruct((T, 256), F32),
        compiler_params=_cparams(2),
        name="lat_df",
    )(q, k, v, kc, vc, df_lam, subln)


def _mlstm_kernel(*refs, nc, has_init, emit_state, state_first):
    refs = list(refs)
    q_ref, k_ref, v_ref, g_ref = refs[:4]
    pos = 4
    if has_init:
        c0_ref, m0_ref = refs[pos:pos + 2]
        pos += 2
    if emit_state and not state_first:
        pos += 2
    h_ref = refs[pos]
    pos += 1
    if emit_state:
        cn_ref, nn_ref, mn_ref = refs[pos:pos + 3]
        pos += 3
    caug_sc, m_sc = refs[pos:pos + 2]

    if has_init:
        caug_sc[...] = c0_ref[0]
        m_sc[...] = m0_ref[0]
    else:
        caug_sc[...] = jnp.zeros(caug_sc.shape, F32)
        m_sc[...] = jnp.zeros(m_sc.shape, F32)
    h_ref[...] = jnp.zeros(h_ref.shape, F32)

    ti = lax.broadcasted_iota(jnp.int32, (ML_CHUNK, ML_CHUNK), 0)
    si = lax.broadcasted_iota(jnp.int32, (ML_CHUNK, ML_CHUNK), 1)
    masks = (si <= ti, si >= ti)
    ones_blk = jnp.ones((ML_CHUNK, LANES), BF16)
    full = (ML_CHUNK, ML_CHUNK)

    def body(c, carry):
        for d in range(2):
            ci = c if d == 0 else nc - 1 - c
            start = pl.multiple_of(ci * ML_CHUNK, ML_CHUNK)
            rows = pl.ds(start, ML_CHUNK)
            mask = masks[d]
            tri = jnp.where(mask, 1.0, 0.0).astype(BF16)
            gblk = g_ref[rows, :]
            hi = gblk.astype(BF16)
            lo = (gblk - hi.astype(F32)).astype(BF16)
            bc = _dot(tri, hi) + _dot(tri, lo)
            g_t = gblk.T
            bc_t = bc.T
            last = ML_CHUNK - 1 if d == 0 else 0
            for hd in range(ML_HEADS):
                ch_i = 2 * ML_HEADS * d + hd
                ch_f = ch_i + ML_HEADS
                idx = ML_HEADS * d + hd
                hs = slice(ML_DH * hd, ML_DH * (hd + 1))
                b_col = bc[:, ch_f:ch_f + 1]
                a_col = gblk[:, ch_i:ch_i + 1] - b_col
                a_msk = jnp.where(mask, g_t[ch_i:ch_i + 1, :] - bc_t[ch_f:ch_f + 1, :], -jnp.inf)
                m_col = jnp.max(a_msk, axis=1, keepdims=True)
                m_rep = jnp.broadcast_to(m_col, full)
                b_rep = jnp.broadcast_to(b_col, full)
                a_rep = jnp.broadcast_to(a_col, full)
                m_last = m_col[last:last + 1, :]
                b_last = b_col[last:last + 1, :]
                qh = q_ref[rows, hs].astype(BF16)
                kf = k_ref[rows, hs]
                vaug = jnp.concatenate([v_ref[rows, hs].astype(BF16), ones_blk], axis=1)
                s0 = _dot_nt(qh, kf.astype(BF16)) * jnp.exp(a_msk - m_rep)
                sv0 = _dot(s0.astype(BF16), vaug)
                u0 = _dot_tn((jnp.exp(a_rep - m_last) * kf).astype(BF16), vaug)
                m_prev = m_sc[idx][0:1, 0:1]
                g_rep = jnp.maximum(m_prev, m_rep)
                w0 = jnp.exp(m_prev - g_rep)
                cc = jnp.exp(m_rep - g_rep)
                caug = caug_sc[idx]
                qc = _dot(qh, caug.astype(BF16))
                num = w0 * qc[:, :ML_DH] + cc * sv0[:, :ML_DH]
                den = w0 * qc[:, ML_DH:] + cc * sv0[:, ML_DH:]
                h_ref[rows, hs] += num / jnp.maximum(jnp.abs(den), jnp.exp(-b_rep - g_rep))
                g_last = jnp.maximum(m_prev, m_last)
                caug_sc[idx] = jnp.exp(m_prev - g_last) * caug + jnp.exp(m_last - g_last) * u0
                m_sc[idx] = jnp.broadcast_to(b_last + g_last, m_sc.shape[1:])
        return carry

    lax.fori_loop(0, nc, body, 0, unroll=LOOP_UNROLL)
    if emit_state:
        slots = [(0, dl) for dl in range(DEPTH)] if state_first else [(0,)]
        for slot in slots:
            cn_ref[slot] = caug_sc[:, :, :ML_DH]
        for idx in range(2 * ML_HEADS):
            n_row = caug_sc[idx][:, ML_DH:].T[0:1, :]
            for slot in slots:
                nn_ref[slot + (slice(idx, idx + 1), slice(None))] = n_row
        mn_ref[0] = m_sc[...]


def _mlstm(q, k, v, g, seq, init=None, emit_state=False, layer=0, state_prev=None):
    T = q.shape[0]
    n_chain = 2 * ML_HEADS
    blk = pl.BlockSpec((seq, 512), lambda b: (b, 0))
    gblk = pl.BlockSpec((seq, LANES), lambda b: (b, 0))
    cblk = pl.BlockSpec((1, n_chain, ML_DH, 2 * ML_DH), lambda b: (b, 0, 0, 0))
    mblk = pl.BlockSpec((1, n_chain, 8, LANES), lambda b: (b, 0, 0, 0))
    in_specs = [blk, blk, blk, gblk]
    args = [q, k, v, g]
    if init is not None:
        in_specs += [cblk, mblk]
        args += list(init)
    out_specs = [blk]
    out_shape = [jax.ShapeDtypeStruct((T, 512), F32)]
    nb = T // seq
    aliases = {}
    if emit_state:
        if state_prev is None:
            out_specs += [pl.BlockSpec((1, DEPTH, n_chain, ML_DH, ML_DH), lambda b: (b, 0, 0, 0, 0)),
                          pl.BlockSpec((1, DEPTH, n_chain, ML_DH), lambda b: (b, 0, 0, 0)), mblk]
        else:
            out_specs += [pl.BlockSpec((1, None, n_chain, ML_DH, ML_DH), lambda b: (b, layer, 0, 0, 0)),
                          pl.BlockSpec((1, None, n_chain, ML_DH), lambda b: (b, layer, 0, 0)), mblk]
            aliases = {len(args): 1, len(args) + 1: 2}
            in_specs += [pl.BlockSpec(memory_space=pl.ANY)] * 2
            args += list(state_prev)
        out_shape += [jax.ShapeDtypeStruct((nb, DEPTH, n_chain, ML_DH, ML_DH), F32),
                      jax.ShapeDtypeStruct((nb, DEPTH, n_chain, ML_DH), F32),
                      jax.ShapeDtypeStruct((nb, n_chain, 8, LANES), F32)]
    return pl.pallas_call(
        functools.partial(_mlstm_kernel, nc=seq // ML_CHUNK, has_init=init is not None,
                          emit_state=emit_state, state_first=state_prev is None),
        grid=(nb,),
        in_specs=in_specs,
        out_specs=out_specs,
        out_shape=out_shape,
        input_output_aliases=aliases,
        scratch_shapes=[pltpu.VMEM((n_chain, ML_DH, 2 * ML_DH), F32),
                        pltpu.VMEM((n_chain, 8, LANES), F32)],
        compiler_params=_cparams(1),
        name="mlstm_state" if emit_state else "mlstm",
    )(*args)


def _outproj_kernel(x_ref, nao_ref, dfo_ref, mlh_ref, mo_ref, mod_ref, g2_ref, outn_ref, wout_ref,
                    wr_ref, br_ref, x1_ref, h2_ref, gates_ref):
    mlh = mlh_ref[...]
    parts = []
    for hd in range(ML_HEADS):
        seg = mlh[:, ML_DH * hd:ML_DH * (hd + 1)]
        parts.append(seg * lax.rsqrt(jnp.mean(seg * seg, axis=-1, keepdims=True) + EPS))
    ml_o = jnp.concatenate(parts, axis=1) * outn_ref[...] * jax.nn.sigmoid(mo_ref[...])
    mix = (_dot(nao_ref[...].astype(BF16), wout_ref[0:256, :])
           + _dot(dfo_ref[...].astype(BF16), wout_ref[256:512, :])
           + _dot(ml_o.astype(BF16), wout_ref[512:1024, :]))
    x1 = x_ref[...] + mod_ref[0, 2:3, :] * mix
    x1_ref[...] = x1
    y = x1 * lax.rsqrt(jnp.mean(x1 * x1, axis=-1, keepdims=True) + EPS) * g2_ref[...]
    h2 = y * (1.0 + mod_ref[0, 4:5, :]) + mod_ref[0, 3:4, :]
    h2_ref[...] = h2.astype(BF16)

    wr = wr_ref[...]
    h_hi = h2.astype(BF16)
    h_lo = (h2 - h_hi.astype(F32)).astype(BF16)
    w_hi = wr.astype(BF16)
    w_lo = (wr - w_hi.astype(F32)).astype(BF16)
    logits = _dot(h_hi, w_hi) + _dot(h_lo, w_hi) + _dot(h_hi, w_lo)
    s = jax.nn.sigmoid(logits)
    work = s + br_ref[...]
    lane = _lane_iota(work.shape)
    s_sel = jnp.zeros(work.shape, F32)
    for _ in range(TOP_K):
        mx = jnp.max(work, axis=-1, keepdims=True)
        first = jnp.min(jnp.where(work == mx, lane, N_EXPERTS), axis=-1, keepdims=True)
        hit = lane == first
        s_sel = jnp.where(hit, s, s_sel)
        work = jnp.where(hit, -jnp.inf, work)
    gates_ref[...] = s_sel / jnp.sum(s_sel, axis=-1, keepdims=True) * ROUTED_SCALE


def _outproj(x, nao, dfo, mlh, mo, mod_p, tokens_per_mod, layer, g2, outn, w_out_bf, w_router, b_router):
    T = x.shape[0]
    tm = TM_PROJ
    row = lambda i: (i, 0)
    const = lambda i: (0, 0)
    lay = lambda i: (layer, 0, 0)
    return pl.pallas_call(
        _outproj_kernel,
        grid=(T // tm,),
        in_specs=[pl.BlockSpec((tm, D_MODEL), row),
                  pl.BlockSpec((tm, 256), row), pl.BlockSpec((tm, 256), row),
                  pl.BlockSpec((tm, 512), row), pl.BlockSpec((tm, 512), row),
                  pl.BlockSpec((1, 6, D_MODEL), lambda i: ((i * tm) // tokens_per_mod, 0, 0)),
                  pl.BlockSpec((1, D_MODEL), const), pl.BlockSpec((1, 512), const),
                  pl.BlockSpec((None, D_MODEL, D_MODEL), lay),
                  pl.BlockSpec((None, D_MODEL, N_EXPERTS), lay), pl.BlockSpec((1, N_EXPERTS), const)],
        out_specs=[pl.BlockSpec((tm, D_MODEL), row), pl.BlockSpec((tm, D_MODEL), row),
                   pl.BlockSpec((tm, N_EXPERTS), row)],
        out_shape=[jax.ShapeDtypeStruct((T, D_MODEL), F32), jax.ShapeDtypeStruct((T, D_MODEL), BF16),
                   jax.ShapeDtypeStruct((T, N_EXPERTS), F32)],
        compiler_params=_cparams(1),
        name="outproj_router",
    )(x, nao, dfo, mlh, mo, mod_p, g2, outn, w_out_bf, w_router, b_router)


def _moe_kernel(h_ref, gates_ref, x1_ref, mod_ref, w1_ref, w3_ref, w2_ref, s1_ref, s3_ref, s2_ref, o_ref):
    e = pl.program_id(1)
    h = h_ref[...]

    @pl.when(e == 0)
    def _():
        a = _dot(h, s1_ref[...].astype(BF16))
        b = _dot(h, s3_ref[...].astype(BF16))
        o_ref[...] = _dot((_silu(a) * b).astype(BF16), s2_ref[...].astype(BF16))

    gates = gates_ref[...]
    lane = _lane_iota(gates.shape)
    hids = []
    for j in range(MOE_EXPERT_BLOCK):
        a = _dot(h, w1_ref[j].astype(BF16))
        b = _dot(h, w3_ref[j].astype(BF16))
        g = jnp.sum(jnp.where(lane == e * MOE_EXPERT_BLOCK + j, gates, 0.0), axis=-1, keepdims=True)
        hids.append((_silu(a) * b * g).astype(BF16))
    w2 = w2_ref[...].reshape(MOE_EXPERT_BLOCK * D_EXPERT, D_MODEL).astype(BF16)
    o_ref[...] += _dot(jnp.concatenate(hids, axis=1), w2)

    @pl.when(e == N_EXPERTS // MOE_EXPERT_BLOCK - 1)
    def _():
        o_ref[...] = x1_ref[...] + mod_ref[0, 5:6, :] * o_ref[...]


def _moe(h2, gates, x1, mod_p, tokens_per_mod, layer, w1, w3, w2, s1, s3, s2):
    T = h2.shape[0]
    tm = TM_MOE
    eb = MOE_EXPERT_BLOCK
    row = lambda i, e: (i, 0)
    wexp = lambda i, e: (layer, e, 0, 0)
    wsh = lambda i, e: (layer, 0, 0)
    return pl.pallas_call(
        _moe_kernel,
        grid=(T // tm, N_EXPERTS // eb),
        in_specs=[pl.BlockSpec((tm, D_MODEL), row), pl.BlockSpec((tm, N_EXPERTS), row),
                  pl.BlockSpec((tm, D_MODEL), row),
                  pl.BlockSpec((1, 6, D_MODEL), lambda i, e: ((i * tm) // tokens_per_mod, 0, 0)),
                  pl.BlockSpec((None, eb, D_MODEL, D_EXPERT), wexp),
                  pl.BlockSpec((None, eb, D_MODEL, D_EXPERT), wexp),
                  pl.BlockSpec((None, eb, D_EXPERT, D_MODEL), wexp),
                  pl.BlockSpec((None, D_MODEL, D_EXPERT), wsh), pl.BlockSpec((None, D_MODEL, D_EXPERT), wsh),
                  pl.BlockSpec((None, D_EXPERT, D_MODEL), wsh)],
        out_specs=pl.BlockSpec((tm, D_MODEL), row),
        out_shape=jax.ShapeDtypeStruct((T, D_MODEL), F32),
        compiler_params=_cparams(2),
        name="moe_experts",
    )(h2, gates, x1, mod_p, w1, w3, w2, s1, s3, s2)


def _dispatch_kernel(h_ref, gates_ref, xs_ref, cnt_ref, route_ref):
    tt = h_ref.shape[0]
    n_slots = xs_ref.shape[0]
    gates = gates_ref[...]
    sel = jnp.where(gates > 0.0, 1.0, 0.0)
    selb = sel.astype(BF16)
    ti = lax.broadcasted_iota(jnp.int32, (tt, tt), 0)
    si = lax.broadcasted_iota(jnp.int32, (tt, tt), 1)
    earlier = jnp.where(si < ti, 1.0, 0.0).astype(BF16)
    pos = _dot(earlier, selb)
    cnt = jnp.sum(sel, axis=0, keepdims=True)
    seg = jnp.floor((cnt + (MOE_GRANULE - 1.0)) * (1.0 / MOE_GRANULE)) * MOE_GRANULE
    ei = lax.broadcasted_iota(jnp.int32, (N_EXPERTS, N_EXPERTS), 0)
    ej = lax.broadcasted_iota(jnp.int32, (N_EXPERTS, N_EXPERTS), 1)
    before = jnp.where(ei < ej, 1.0, 0.0).astype(BF16)
    off = _dot(jnp.broadcast_to(seg, (16, N_EXPERTS)).astype(BF16), before)[0:1]
    rank = _dot(selb, before)
    slot = off + pos
    lane = _lane_iota((tt, LANES))
    route = jnp.zeros((tt, LANES), F32)
    for k in range(TOP_K):
        hit = jnp.where(rank == k, sel, 0.0)
        has = jnp.sum(hit, axis=1, keepdims=True)
        slot_k = jnp.sum(hit * slot, axis=1, keepdims=True) + has - 1.0
        w_k = jnp.sum(hit * gates, axis=1, keepdims=True)
        route = jnp.where(lane == k, slot_k, jnp.where(lane == TOP_K + k, w_k, route))
    route_ref[...] = route
    cnt_ref[0] = cnt
    slots_t = route.T
    srow = lax.broadcasted_iota(jnp.int32, (n_slots, tt), 0).astype(F32)
    perm = jnp.zeros((n_slots, tt), F32)
    for k in range(TOP_K):
        perm = jnp.where(srow == slots_t[k:k + 1, :], 1.0, perm)
    xs_ref[...] = _dot(perm.astype(BF16), h_ref[...]).astype(BF16)


def _dispatch(h2, gates):
    T = h2.shape[0]
    nt = T // MOE_TILE
    return pl.pallas_call(
        _dispatch_kernel,
        grid=(nt,),
        in_specs=[pl.BlockSpec((MOE_TILE, D_MODEL), lambda i: (i, 0)),
                  pl.BlockSpec((MOE_TILE, N_EXPERTS), lambda i: (i, 0))],
        out_specs=[pl.BlockSpec((MOE_SLOTS, D_MODEL), lambda i: (i, 0)),
                   pl.BlockSpec((1, 1, N_EXPERTS), lambda i: (i, 0, 0)),
                   pl.BlockSpec((MOE_TILE, LANES), lambda i: (i, 0))],
        out_shape=[jax.ShapeDtypeStruct((nt * MOE_SLOTS, D_MODEL), BF16),
                   jax.ShapeDtypeStruct((nt, 1, N_EXPERTS), F32),
                   jax.ShapeDtypeStruct((T, LANES), F32)],
        compiler_params=_cparams(1),
        name="moe_dispatch",
    )(h2, gates)


def _granule_gather_kernel(src_ref, x_hbm, o_hbm, sem, *, n, window):
    def copy(d, s):
        return pltpu.make_async_copy(x_hbm.at[pl.ds(pl.multiple_of(s * MOE_GRANULE, MOE_GRANULE), MOE_GRANULE)],
                                     o_hbm.at[pl.ds(pl.multiple_of(d * MOE_GRANULE, MOE_GRANULE), MOE_GRANULE)],
                                     sem)

    def issue(d, carry):
        copy(d, src_ref[d]).start()

        @pl.when(d >= window)
        def _():
            copy(d - window, 0).wait()
        return carry

    lax.fori_loop(0, n, issue, 0)

    def drain(d, carry):
        copy(d, 0).wait()
        return carry

    lax.fori_loop(n - window, n, drain, 0)


def _granule_gather(src, x, n_out):
    return pl.pallas_call(
        functools.partial(_granule_gather_kernel, n=n_out, window=MOE_DMA_WINDOW),
        grid_spec=pltpu.PrefetchScalarGridSpec(
            num_scalar_prefetch=1, grid=(1,),
            in_specs=[pl.BlockSpec(memory_space=pl.ANY)],
            out_specs=pl.BlockSpec(memory_space=pl.ANY),
            scratch_shapes=[pltpu.SemaphoreType.DMA(())]),
        out_shape=jax.ShapeDtypeStruct((n_out * MOE_GRANULE, x.shape[1]), x.dtype),
        compiler_params=_cparams(1),
        name="moe_granules",
    )(src, x)


def _expert_rows_kernel(te_ref, nact_ref, xs_ref, w1_ref, w3_ref, w2_ref, ys_ref):
    i = pl.program_id(0)

    @pl.when(i < nact_ref[0])
    def _():
        x = xs_ref[...]
        a = _dot(x, w1_ref[...].astype(BF16))
        b = _dot(x, w3_ref[...].astype(BF16))
        ys_ref[...] = _dot((_silu(a) * b).astype(BF16), w2_ref[...].astype(BF16)).astype(BF16)

    @pl.when(i >= nact_ref[0])
    def _():
        ys_ref[...] = jnp.zeros(ys_ref.shape, BF16)


def _expert_rows(tile_expert, n_active, xs, layer, w1, w3, w2):
    n_tiles = xs.shape[0] // MOE_ROWS
    wmap = lambda i, te, na: (layer, te[i], 0, 0)
    return pl.pallas_call(
        _expert_rows_kernel,
        grid_spec=pltpu.PrefetchScalarGridSpec(
            num_scalar_prefetch=2, grid=(n_tiles,),
            in_specs=[pl.BlockSpec((MOE_ROWS, D_MODEL), lambda i, te, na: (i, 0)),
                      pl.BlockSpec((None, None, D_MODEL, D_EXPERT), wmap),
                      pl.BlockSpec((None, None, D_MODEL, D_EXPERT), wmap),
                      pl.BlockSpec((None, None, D_EXPERT, D_MODEL), wmap)],
            out_specs=pl.BlockSpec((MOE_ROWS, D_MODEL), lambda i, te, na: (i, 0))),
        out_shape=jax.ShapeDtypeStruct(xs.shape, BF16),
        compiler_params=_cparams(1),
        name="moe_expert_rows",
    )(tile_expert, n_active, xs, w1, w3, w2)


def _expert_fused_kernel(te_ref, nact_ref, src_ref, dst_ref, xs_hbm, w1_ref, w3_ref, w2_ref, ysz_hbm,
                         ys_hbm, xbuf, ybuf, sem_in, sem_out, *, n_tiles):
    del ysz_hbm
    i = pl.program_id(0)
    n_act = nact_ref[0]
    gr = MOE_ROWS // MOE_GRANULE

    def rows(g):
        if isinstance(g, int):
            return pl.ds(g * MOE_GRANULE, MOE_GRANULE)
        return pl.ds(pl.multiple_of(g * MOE_GRANULE, MOE_GRANULE), MOE_GRANULE)

    def gather(tile, slot, wait):
        for g in range(gr):
            s = 0 if wait else src_ref[tile * gr + g]
            cp = pltpu.make_async_copy(xs_hbm.at[rows(s)], xbuf.at[slot, rows(g)], sem_in.at[slot])
            cp.wait() if wait else cp.start()

    def scatter(tile, slot, wait):
        for g in range(gr):
            dd = dst_ref[tile * gr + g]

            @pl.when(dd >= 0)
            def _():
                cp = pltpu.make_async_copy(ybuf.at[slot, rows(g)], ys_hbm.at[rows(0 if wait else dd)],
                                           sem_out.at[slot])
                cp.wait() if wait else cp.start()

    slot = lax.rem(i, 2)

    @pl.when((i == 0) & (n_act > 0))
    def _():
        gather(0, 0, False)

    @pl.when(i + 1 < n_act)
    def _():
        gather(i + 1, 1 - slot, False)

    @pl.when(i < n_act)
    def _():
        gather(i, slot, True)

        @pl.when(i >= 2)
        def _():
            scatter(i - 2, slot, True)

        x = xbuf[slot]
        a = _dot(x, w1_ref[...].astype(BF16))
        b = _dot(x, w3_ref[...].astype(BF16))
        ybuf[slot] = _dot((_silu(a) * b).astype(BF16), w2_ref[...].astype(BF16)).astype(BF16)
        scatter(i, slot, False)

    @pl.when(i == n_tiles - 1)
    def _():
        for back in (2, 1):
            tile = n_act - back

            @pl.when(tile >= 0)
            def _():
                scatter(tile, lax.rem(tile, 2), True)


def _expert_fused(tile_expert, n_active, src, dst, xs_tiles, layer, w1, w3, w2):
    n_tiles = src.shape[0] // (MOE_ROWS // MOE_GRANULE)
    wmap = lambda i, te, na, s, d: (layer, te[i], 0, 0)
    stage = pltpu.VMEM((2, MOE_ROWS, D_MODEL), BF16)
    return pl.pallas_call(
        functools.partial(_expert_fused_kernel, n_tiles=n_tiles),
        grid_spec=pltpu.PrefetchScalarGridSpec(
            num_scalar_prefetch=4, grid=(n_tiles,),
            in_specs=[pl.BlockSpec(memory_space=pl.ANY),
                      pl.BlockSpec((None, None, D_MODEL, D_EXPERT), wmap),
                      pl.BlockSpec((None, None, D_MODEL, D_EXPERT), wmap),
                      pl.BlockSpec((None, None, D_EXPERT, D_MODEL), wmap),
                      pl.BlockSpec(memory_space=pl.ANY)],
            out_specs=pl.BlockSpec(memory_space=pl.ANY),
            scratch_shapes=[stage, stage, pltpu.SemaphoreType.DMA((2,)), pltpu.SemaphoreType.DMA((2,))]),
        out_shape=jax.ShapeDtypeStruct(xs_tiles.shape, BF16),
        input_output_aliases={8: 0},
        compiler_params=_cparams(1),
        name="moe_expert_rows",
    )(tile_expert, n_active, src, dst, xs_tiles, w1, w3, w2, jnp.zeros(xs_tiles.shape, BF16))


def _combine_kernel(ys_ref, route_ref, h_ref, x1_ref, mod_ref, s1_ref, s3_ref, s2_ref, o_ref):
    route = route_ref[...]
    tt = route.shape[0]
    n_slots = ys_ref.shape[0]
    slot_lane = _lane_iota((tt, n_slots)).astype(F32)
    wp = jnp.zeros((tt, n_slots), F32)
    for k in range(TOP_K):
        wp = jnp.where(slot_lane == route[:, k:k + 1], route[:, TOP_K + k:TOP_K + k + 1], wp)
    w_hi = wp.astype(BF16)
    w_lo = (wp - w_hi.astype(F32)).astype(BF16)
    ys = ys_ref[...]
    routed = _dot(w_hi, ys) + _dot(w_lo, ys)
    h = h_ref[...]
    a = _dot(h, s1_ref[...].astype(BF16))
    b = _dot(h, s3_ref[...].astype(BF16))
    shared = _dot((_silu(a) * b).astype(BF16), s2_ref[...].astype(BF16))
    o_ref[...] = x1_ref[...] + mod_ref[0, 5:6, :] * (routed + shared)


def _combine(ys_tiles, route, h2, x1, mod_p, tokens_per_mod, layer, s1, s3, s2):
    T = h2.shape[0]
    tt = MOE_TILE
    row = lambda i: (i, 0)
    wsh = lambda i: (layer, 0, 0)
    return pl.pallas_call(
        _combine_kernel,
        grid=(T // tt,),
        in_specs=[pl.BlockSpec((MOE_SLOTS, D_MODEL), row), pl.BlockSpec((tt, LANES), row),
                  pl.BlockSpec((tt, D_MODEL), row), pl.BlockSpec((tt, D_MODEL), row),
                  pl.BlockSpec((1, 6, D_MODEL), lambda i: ((i * tt) // tokens_per_mod, 0, 0)),
                  pl.BlockSpec((None, D_MODEL, D_EXPERT), wsh), pl.BlockSpec((None, D_MODEL, D_EXPERT), wsh),
                  pl.BlockSpec((None, D_EXPERT, D_MODEL), wsh)],
        out_specs=pl.BlockSpec((tt, D_MODEL), row),
        out_shape=jax.ShapeDtypeStruct((T, D_MODEL), F32),
        compiler_params=_cparams(1),
        name="moe_combine",
    )(ys_tiles, route, h2, x1, mod_p, s1, s3, s2)


def _excl_cumsum(a, axis):
    return jnp.cumsum(a, axis=axis) - a


def _moe_plan(cnt):
    nt = cnt.shape[0]
    gt = MOE_SLOTS // MOE_GRANULE
    gr = MOE_ROWS // MOE_GRANULE
    ng = _moe_sorted_granules(nt)
    g = (cnt + MOE_GRANULE - 1) // MOE_GRANULE
    loc = _excl_cumsum(g, 1)
    tot = jnp.sum(g, axis=0)
    grp = ((tot + gr - 1) // gr) * gr
    gs = _excl_cumsum(grp, 0)
    within = _excl_cumsum(g, 0)
    d = jnp.arange(ng, dtype=jnp.int32)
    e_d = jnp.sum(gs[None, :] <= d[:, None], axis=1) - 1
    j = d - gs[e_d]
    within_t = within.T
    t_d = jnp.sum(within_t[e_d] <= j[:, None], axis=1) - 1
    local = t_d * gt + loc[t_d, e_d] + (j - within_t[e_d, t_d])
    real = j < tot[e_d]
    src_sorted = jnp.where(real, local, gt - 1)
    dst_sorted = jnp.where(real, local, -1)
    q = jnp.arange(nt * gt, dtype=jnp.int32)
    t_q, l_q = q // gt, q % gt
    e_q = jnp.sum(loc[t_q] <= l_q[:, None], axis=1) - 1
    jj = l_q - loc[t_q, e_q]
    src_local = jnp.where(jj < g[t_q, e_q], gs[e_q] + within[t_q, e_q] + jj, ng - 1)
    n_active = jnp.sum(grp) // gr
    tiles = jnp.arange(ng // gr, dtype=jnp.int32)
    tile_expert = e_d[jnp.minimum(tiles, jnp.maximum(n_active - 1, 0)) * gr]
    return (src_sorted.astype(jnp.int32), dst_sorted.astype(jnp.int32), src_local.astype(jnp.int32),
            tile_expert.astype(jnp.int32), n_active.astype(jnp.int32).reshape(1))


def _moe_sorted_granules(nt):
    gt_used = (TOP_K * MOE_TILE + N_EXPERTS * (MOE_GRANULE - 1)) // MOE_GRANULE
    gr = MOE_ROWS // MOE_GRANULE
    need = nt * gt_used + N_EXPERTS * (gr - 1) + 1
    return -(-need // gr) * gr


def _moe_sparse(h2, gates, x1, mod_p, tokens_per_mod, layer, w1, w3, w2, s1, s3, s2):
    xs_tiles, cnt, route = _dispatch(h2, gates)
    nt = cnt.shape[0]
    src, dst, _, tile_expert, n_active = _moe_plan(cnt.reshape(nt, N_EXPERTS).astype(jnp.int32))
    ys_tiles = _expert_fused(tile_expert, n_active, src, dst, xs_tiles, layer, w1, w3, w2)
    return _combine(ys_tiles, route, h2, x1, mod_p, tokens_per_mod, layer, s1, s3, s2)


def _group_matrix(group):
    i = np.arange(256)
    return jnp.asarray((i[:, None] // group) == (i[None, :] // group), BF16)


def _rope_tables(seq):
    t = np.arange(seq)
    lane = np.arange(256)
    j = lane % DF_DQK
    nf = DF_DQK // 4
    pos = np.where((j < DF_DQK // 2)[None, :], (t // GRID_W)[:, None], (t % GRID_W)[:, None]).astype(np.float32)
    inv = (1.0 / (ROPE_BASE ** (np.arange(nf, dtype=np.float32) / nf))).astype(np.float32)
    ang = pos * inv[j % nf][None, :]
    sign = np.where((lane % (2 * nf)) < nf, -1.0, 1.0).astype(np.float32)
    return jnp.asarray(np.cos(ang), F32), jnp.asarray(np.sin(ang) * sign[None, :], F32)


def _token_major(a):
    b, h, l, d = a.shape
    return a.transpose(0, 2, 1, 3).reshape(b, l, h * d)


def kernel(x_prompt, x_sample, cache_na_k, cache_na_v, cache_df_k, cache_df_v, state_ml_C, state_ml_n,
           state_ml_m, c, c_ctx, w_ada, b_ada, g_norm1, g_norm2, w_in, w_out, na_qn, na_kn, na_rpb,
           df_qn, df_kn, df_lam, df_subln, ml_gate_b, ml_outn, w_router, b_router, w_exp1, w_exp3, w_exp2,
           w_sh1, w_sh3, w_sh2):
    bc, lc, _ = x_prompt.shape
    bs, ls, _ = x_sample.shape
    past = cache_na_k.shape[3]

    cond = jnp.zeros((8, D_MODEL), F32).at[0].set(c_ctx).at[1:1 + bs].set(c)
    mod = _modulation(cond, w_ada, b_ada)
    g64 = _group_matrix(NA_DH)
    g32 = _group_matrix(DF_DQK)
    rope_tabs = _rope_tables(ls)
    w_in_bf = jnp.pad(w_in.astype(BF16), ((0, 0), (0, 0), (0, IN_COLS_PAD - IN_COLS)))
    w_out_bf = w_out.astype(BF16)
    na_blocks = _na_bias_blocks(na_rpb)
    moe_w = (w_exp1, w_exp3, w_exp2, w_sh1, w_sh3, w_sh2)

    y_p = x_prompt.reshape(bc * lc, D_MODEL)
    y_s = x_sample.reshape(bs * ls, D_MODEL)
    caches = states = None
    m_states = []
    for l in range(DEPTH):
        lam_init = 0.8 - 0.6 * math.exp(-0.3 * l)
        mod_l = mod[l].reshape(8, 6, D_MODEL)
        g1 = g_norm1[l][None, :]
        g2 = g_norm2[l][None, :]
        qn = jnp.tile(na_qn[l], NA_HEADS)[None, :]
        kn = jnp.tile(na_kn[l], NA_HEADS)[None, :]
        dqn = jnp.tile(df_qn[l], 2 * DF_HEADS)[None, :]
        dkn = jnp.tile(df_kn[l], 2 * DF_HEADS)[None, :]
        subln = jnp.tile(df_subln[l], DF_HEADS)[None, :]
        outn = jnp.tile(ml_outn[l], ML_HEADS)[None, :]
        gate_b = jnp.pad(ml_gate_b[l].reshape(-1), (0, LANES - 4 * ML_HEADS))[None, :]
        br = b_router[l][None, :]
        proj_w = (l, g1, w_in_bf, qn, kn, dqn, dkn, gate_b, g64, g32)

        mod_c = mod_l[0:1]
        tpm = bc * lc
        res = _inproj(y_p, mod_c, tpm, *proj_w, None, cache_seq=lc, cache_prev=caches)
        naq, nak, nav, dfq, dfk, dfv, mq, mk, mv, mo, mg = res[:11]
        caches = tuple(res[11:])
        nao, dfo = _ctx_attention(naq, nak, nav, dfq, dfk, dfv, df_lam[l], subln, lc, lam_init)
        mlh, c_n, n_n, m_n = _mlstm(mq, mk, mv, mg, lc, emit_state=True, layer=l, state_prev=states)
        states = (c_n, n_n)
        x1, h2, gates = _outproj(y_p, nao, dfo, mlh, mo, mod_c, tpm, l, g2, outn, w_out_bf, w_router, br)
        y_p = _moe_sparse(h2, gates, x1, mod_c, tpm, l, *moe_w)
        m_states.append(m_n[:, :, 0, 0].reshape(bc, 2, ML_HEADS))

        mod_s = mod_l[1:1 + bs]
        naq, nak, nav, dfq, dfk, dfv, mq, mk, mv, mo, mg = _inproj(y_s, mod_s, ls, *proj_w, rope_tabs)
        kc_na = _token_major(cache_na_k[:, l])
        vc_na = _token_major(cache_na_v[:, l])
        kc_df = cache_df_k[:, l].transpose(0, 3, 1, 2, 4).reshape(bs, past, 256)
        vc_df = _token_major(cache_df_v[:, l])
        bias_tabs = _na_bias_tables(na_blocks[l * NA_HEADS:(l + 1) * NA_HEADS], ls // GRID_W)
        nao = _lat_na(naq, nak, nav, kc_na, vc_na, bias_tabs, ls)
        dfo = _lat_df(dfq, dfk, dfv, kc_df, vc_df, df_lam[l], subln, ls, lam_init)
        n0 = jnp.broadcast_to(state_ml_n[:, l][..., None], (bs, 2, ML_HEADS, ML_DH, ML_DH))
        c0 = jnp.concatenate([state_ml_C[:, l], n0], axis=-1)
        c0 = c0.reshape(bs, 2 * ML_HEADS, ML_DH, 2 * ML_DH)
        m0 = jnp.broadcast_to(state_ml_m[:, l].reshape(bs, 2 * ML_HEADS, 1, 1), (bs, 2 * ML_HEADS, 8, LANES))
        (mlh,) = _mlstm(mq, mk, mv, mg, ls, init=(c0, m0))
        x1, h2, gates = _outproj(y_s, nao, dfo, mlh, mo, mod_s, ls, l, g2, outn, w_out_bf, w_router, br)
        y_s = _moe_sparse(h2, gates, x1, mod_s, ls, l, *moe_w)

    new_ml_c = states[0].reshape(bc, DEPTH, 2, ML_HEADS, ML_DH, ML_DH)
    new_ml_n = states[1].reshape(bc, DEPTH, 2, ML_HEADS, ML_DH)
    return (y_p.reshape(bc, lc, D_MODEL), y_s.reshape(bs, ls, D_MODEL), *caches,
            new_ml_c, new_ml_n, jnp.stack(m_states, axis=1))
```

```python
import functools
import math

import numpy as np
import jax
import jax.numpy as jnp
from jax import lax
from jax.experimental import pallas as pl
from jax.experimental.pallas import tpu as pltpu

F32 = jnp.float32
BF16 = jnp.bfloat16

D_MODEL = 1024
DEPTH = 2
GRID_W = 64
NA_HEADS = 4
NA_DH = 64
NA_WIN_R = 8
NA_WIN_C = 16
DF_HEADS = 4
DF_DV = 64
DF_DQK = 32
ML_HEADS = 4
ML_DH = 128
ML_CHUNK = 128
N_EXPERTS = 32
TOP_K = 4
D_EXPERT = 256
ROUTED_SCALE = 2.5
ROPE_BASE = 10000.0
EPS = 1e-6

LANES = 128
IN_COLS = 3600
IN_COLS_PAD = 3712
GATE_COL0 = 3584
MASK_VALUE = -1e30
VMEM_LIMIT = 56 * 1024 * 1024

TM_PROJ = 512
TM_MOE = 1024
MOE_EXPERT_BLOCK = 2
MOE_TILE = 256
MOE_GRANULE = 16
MOE_SLOTS = TOP_K * MOE_TILE + N_EXPERTS * MOE_GRANULE
MOE_ROWS = 512
Q_BLOCK_DF = 256
CTX_ATTN_BATCHES = 2
LOOP_UNROLL = 2
NA_Q_ROWS = 4
NA_K_ROWS = NA_Q_ROWS + NA_WIN_R
TM_PROJ_CTX = 256
LOG2E = math.log2(math.e)


def _cparams(n_axes):
    return pltpu.CompilerParams(dimension_semantics=("arbitrary",) * n_axes,
                                vmem_limit_bytes=VMEM_LIMIT)


def _silu(x):
    return x * jax.nn.sigmoid(x)


def _dot(a, b):
    return jnp.dot(a, b, preferred_element_type=F32)


def _dot_nt(a, b):
    return lax.dot_general(a, b, (((1,), (1,)), ((), ())), preferred_element_type=F32)


def _dot_tn(a, b):
    return lax.dot_general(a, b, (((0,), (0,)), ((), ())), preferred_element_type=F32)


def _lane_iota(shape):
    return lax.broadcasted_iota(jnp.int32, shape, 1)


def _mod_kernel(cond_ref, w_ref, b_ref, o_ref):
    s = _silu(cond_ref[...]).astype(BF16)
    o_ref[0] = _dot(s, w_ref[0].astype(BF16)) + b_ref[0]


def _modulation(cond, w_ada, b_ada):
    tn = 1536
    n = 6 * D_MODEL
    return pl.pallas_call(
        _mod_kernel,
        grid=(DEPTH, n // tn),
        in_specs=[pl.BlockSpec((8, D_MODEL), lambda l, j: (0, 0)),
                  pl.BlockSpec((1, D_MODEL, tn), lambda l, j: (l, 0, j)),
                  pl.BlockSpec((1, 1, tn), lambda l, j: (l, 0, j))],
        out_specs=pl.BlockSpec((1, 8, tn), lambda l, j: (l, 0, j)),
        out_shape=jax.ShapeDtypeStruct((DEPTH, 8, n), F32),
        compiler_params=_cparams(2),
        name="adaln_mod",
    )(cond, w_ada, b_ada.reshape(DEPTH, 1, n))


def _group_rms(v, gmat_ref, group):
    sq = v * v
    hi = sq.astype(BF16)
    lo = (sq - hi.astype(F32)).astype(BF16)
    ss = _dot(hi, gmat_ref[...]) + _dot(lo, gmat_ref[...])
    return v * lax.rsqrt(ss * (1.0 / group) + EPS)


def _rope(v, cos, sin_signed):
    first = (_lane_iota(v.shape) % 16) < 8
    partner = jnp.where(first, pltpu.roll(v, v.shape[1] - 8, 1), pltpu.roll(v, 8, 1))
    return v * cos + partner * sin_signed


def _inproj_kernel(*refs, rope, cache_seq, cache_first):
    (x_ref, mod_ref, g1_ref, w_ref, qn_ref, kn_ref, dqn_ref, dkn_ref, gb_ref, g64_ref, g32_ref) = refs[:11]
    pos = 11
    if rope:
        cos_ref, sin_ref = refs[pos:pos + 2]
        pos += 2
    if cache_seq and not cache_first:
        pos += 4
    (naq_ref, nak_ref, nav_ref, dfq_ref, dfk_ref, dfv_ref,
     mq_ref, mk_ref, mv_ref, mo_ref, mg_ref) = refs[pos:pos + 11]
    pos += 11
    x = x_ref[...]
    y = x * lax.rsqrt(jnp.mean(x * x, axis=-1, keepdims=True) + EPS) * g1_ref[...]
    h = (y * (1.0 + mod_ref[0, 1:2, :]) + mod_ref[0, 0:1, :]).astype(BF16)

    def proj(a, b):
        return _dot(h, w_ref[:, a:b])

    naq_ref[...] = _group_rms(proj(0, 256), g64_ref, NA_DH) * qn_ref[...]
    nak = _group_rms(proj(256, 512), g64_ref, NA_DH) * kn_ref[...]
    nav = proj(512, 768)
    dfq = _group_rms(proj(768, 1024), g32_ref, DF_DQK) * dqn_ref[...]
    dfk = _group_rms(proj(1024, 1280), g32_ref, DF_DQK) * dkn_ref[...]
    dfv = proj(1280, 1536)
    if rope:
        dfq = _rope(dfq, cos_ref[...], sin_ref[...])
        dfk = _rope(dfk, cos_ref[...], sin_ref[...])
    nak_ref[...] = nak
    nav_ref[...] = nav
    dfq_ref[...] = dfq
    dfk_ref[...] = dfk
    dfv_ref[...] = dfv
    if cache_seq:
        nk_hm, nv_hm, dk_hm, dv_hm = refs[pos:pos + 4]

        def put(ref, bi, idx, val):
            if cache_first:
                for dl in range(DEPTH):
                    ref[(bi, dl) + idx] = val
            else:
                ref[(bi,) + idx] = val

        for bi in range(x.shape[0] // cache_seq):
            rs = slice(bi * cache_seq, (bi + 1) * cache_seq)
            for hd in range(NA_HEADS):
                put(nk_hm, bi, (hd,), nak[rs, NA_DH * hd:NA_DH * (hd + 1)])
                put(nv_hm, bi, (hd,), nav[rs, NA_DH * hd:NA_DH * (hd + 1)])
                put(dv_hm, bi, (hd,), dfv[rs, DF_DV * hd:DF_DV * (hd + 1)])
                for a in range(2):
                    c0 = (2 * hd + a) * DF_DQK
                    put(dk_hm, bi, (hd, a), dfk[rs, c0:c0 + DF_DQK])
    mq_ref[...] = proj(1536, 2048)
    mk_ref[...] = proj(2048, 2560) * (ML_DH ** -0.5)
    mv_ref[...] = proj(2560, 3072)
    mo_ref[...] = proj(3072, 3584)
    g = proj(GATE_COL0, IN_COLS_PAD) + gb_ref[...]
    lane = _lane_iota(g.shape)
    is_forget = ((lane // ML_HEADS) % 2 == 1) & (lane < 4 * ML_HEADS)
    log_sig = jnp.minimum(g, 0.0) - jnp.log1p(jnp.exp(-jnp.abs(g)))
    mg_ref[...] = jnp.where(is_forget, log_sig, g)


def _inproj(x, mod_p, tokens_per_mod, layer, g1, w_in_bf, qn, kn, dqn, dkn, gate_b, g64, g32, rope_tabs,
            cache_seq=0, cache_prev=None):
    T = x.shape[0]
    tm = TM_PROJ_CTX if cache_seq else TM_PROJ
    rope = rope_tabs is not None
    row = lambda i: (i, 0)
    const = lambda i: (0, 0)
    in_specs = [pl.BlockSpec((tm, D_MODEL), row),
                pl.BlockSpec((1, 6, D_MODEL), lambda i: ((i * tm) // tokens_per_mod, 0, 0)),
                pl.BlockSpec((1, D_MODEL), const),
                pl.BlockSpec((None, D_MODEL, IN_COLS_PAD), lambda i: (layer, 0, 0)),
                pl.BlockSpec((1, 256), const), pl.BlockSpec((1, 256), const),
                pl.BlockSpec((1, 256), const), pl.BlockSpec((1, 256), const),
                pl.BlockSpec((1, LANES), const),
                pl.BlockSpec((256, 256), const), pl.BlockSpec((256, 256), const)]
    args = [x, mod_p, g1, w_in_bf, qn, kn, dqn, dkn, gate_b, g64, g32]
    if rope:
        seq = rope_tabs[0].shape[0]
        tab = lambda i: (i % (seq // tm), 0)
        in_specs += [pl.BlockSpec((tm, 256), tab), pl.BlockSpec((tm, 256), tab)]
        args += list(rope_tabs)
    widths = [256] * 6 + [512] * 4 + [LANES]
    out_specs = [pl.BlockSpec((tm, w), row) for w in widths]
    out_shape = [jax.ShapeDtypeStruct((T, w), F32) for w in widths]
    aliases = {}
    if cache_seq:
        nb = tm // cache_seq
        hm = (NA_HEADS, cache_seq, NA_DH)
        hm_df = (DF_HEADS, 2, cache_seq, DF_DQK)
        for k, tail in enumerate((hm, hm, hm_df, hm)):
            zeros = (0,) * len(tail)
            if cache_prev is None:
                out_specs.append(pl.BlockSpec((nb, DEPTH) + tail, lambda i, z=zeros: (i, 0) + z))
            else:
                out_specs.append(pl.BlockSpec((nb, None) + tail, lambda i, z=zeros: (i, layer) + z))
                aliases[len(args)] = len(widths) + k
                in_specs.append(pl.BlockSpec(memory_space=pl.ANY))
                args.append(cache_prev[k])
            out_shape.append(jax.ShapeDtypeStruct((T // cache_seq, DEPTH) + tail, F32))
    return pl.pallas_call(
        functools.partial(_inproj_kernel, rope=rope, cache_seq=cache_seq, cache_first=cache_prev is None),
        grid=(T // tm,),
        in_specs=in_specs,
        out_specs=out_specs,
        out_shape=out_shape,
        input_output_aliases=aliases,
        compiler_params=_cparams(1),
        name="inproj_rope" if rope else "inproj",
    )(*args)


def _df_lambda(lam_ref, lam_init):
    v = lam_ref[...]
    a = jnp.sum(v[0:1] * v[1:2], axis=1, keepdims=True)
    b = jnp.sum(v[2:3] * v[3:4], axis=1, keepdims=True)
    return jnp.exp(a) - jnp.exp(b) + lam_init


def _softmax_parts(scores):
    es = _softmax_exps(scores)
    tot = functools.reduce(lambda a, b: a + b, [jnp.sum(e, axis=-1, keepdims=True) for e in es])
    return [e.astype(BF16) for e in es], 1.0 / tot


def _softmax_exps(scores):
    m = functools.reduce(jnp.maximum, [jnp.max(s, axis=-1, keepdims=True) for s in scores])
    return [jnp.exp2(s - m) for s in scores]


def _attend_masked_heads(qp, k_segs, v_segs, bias=None):
    lane = _lane_iota(qp.shape)
    v_aug = [jnp.concatenate([v, jnp.ones(v.shape, BF16)], axis=1) for v in v_segs]
    outs = []
    for j in range(2):
        qm = jnp.where(lane // NA_DH == j, qp, 0.0).astype(BF16)
        scores = [_dot_nt(qm, k) for k in k_segs]
        if bias is not None:
            scores[0] = scores[0] + bias[j]
        es = _softmax_exps(scores)
        o = functools.reduce(lambda x, y: x + y, [_dot(e.astype(BF16), v) for e, v in zip(es, v_aug)])
        outs.append(o[:, :LANES] / o[:, LANES:])
    return jnp.where(lane < NA_DH, outs[0], outs[1])


def _subln(o, subln_row, lam_init):
    lane = _lane_iota(o.shape)
    sq = o * o
    s0 = jnp.sum(jnp.where(lane < DF_DV, sq, 0.0), axis=-1, keepdims=True)
    s1 = jnp.sum(jnp.where(lane >= DF_DV, sq, 0.0), axis=-1, keepdims=True)
    ms = jnp.where(lane < DF_DV, s0, s1) * (1.0 / DF_DV)
    return o * lax.rsqrt(ms + EPS) * subln_row * (1.0 - lam_init)


def _diff_attn_tile(q, k_segs, v_segs, lam, t):
    sl = slice(LANES * t, LANES * (t + 1))
    qp = q[:, sl] * (DF_DQK ** -0.5 * LOG2E)
    lane = _lane_iota(qp.shape)
    kps = [k[:, sl].astype(BF16) for k in k_segs]
    vps = [v[:, sl].astype(BF16) for v in v_segs]
    outs = []
    for j in range(2):
        maps = []
        for a in range(2):
            qm = jnp.where(lane // DF_DQK == 2 * j + a, qp, 0.0).astype(BF16)
            maps.append(_softmax_parts([_dot_nt(qm, kp) for kp in kps]))
        (e0, r0), (e1, r1) = maps
        c0 = r0.astype(BF16)
        c1 = (-(lam * r1)).astype(BF16)
        o = functools.reduce(lambda x, y: x + y,
                             [_dot(x0 * c0 + x1 * c1, vp) for x0, x1, vp in zip(e0, e1, vps)])
        outs.append(o)
    return jnp.where(lane < DF_DV, outs[0], outs[1])


def _ctx_attn_kernel(naq_ref, nak_ref, nav_ref, dfq_ref, dfk_ref, dfv_ref, lam_ref, subln_ref,
                     nao_ref, dfo_ref, *, lam_init, seq):
    lam = _df_lambda(lam_ref, lam_init)
    for bi in range(naq_ref.shape[0] // seq):
        rs = slice(bi * seq, (bi + 1) * seq)
        for t in range(2):
            sl = slice(LANES * t, LANES * (t + 1))
            qp = naq_ref[rs, sl] * (NA_DH ** -0.5 * LOG2E)
            kp = nak_ref[rs, sl].astype(BF16)
            vp = nav_ref[rs, sl].astype(BF16)
            nao_ref[rs, sl] = _attend_masked_heads(qp, [kp], [vp])
            o = _diff_attn_tile(dfq_ref[rs, :], [dfk_ref[rs, :]], [dfv_ref[rs, :]], lam, t)
            dfo_ref[rs, sl] = _subln(o, subln_ref[:, sl], lam_init)


def _ctx_attention(naq, nak, nav, dfq, dfk, dfv, df_lam, subln, seq, lam_init):
    T = naq.shape[0]
    rows = CTX_ATTN_BATCHES * seq
    blk = pl.BlockSpec((rows, 256), lambda b: (b, 0))
    return pl.pallas_call(
        functools.partial(_ctx_attn_kernel, lam_init=lam_init, seq=seq),
        grid=(T // rows,),
        in_specs=[blk] * 6 + [pl.BlockSpec((4, DF_DQK), lambda b: (0, 0)),
                              pl.BlockSpec((1, 256), lambda b: (0, 0))],
        out_specs=[blk, blk],
        out_shape=[jax.ShapeDtypeStruct((T, 256), F32)] * 2,
        compiler_params=_cparams(1),
        name="ctx_attention",
    )(naq, nak, nav, dfq, dfk, dfv, df_lam, subln)


def _na_bias_kernel(rpb_ref, o_ref):
    hd = pl.program_id(0)
    shape = (GRID_W, GRID_W)
    qi = lax.broadcasted_iota(jnp.int32, shape, 0)
    kc = lax.broadcasted_iota(jnp.int32, shape, 1)
    cs = jnp.clip(qi - NA_WIN_C // 2, 0, GRID_W - NA_WIN_C)
    inwin = (kc >= cs) & (kc < cs + NA_WIN_C)
    off = jnp.where(inwin, kc - qi + NA_WIN_C - 1, -1)
    n_c = 2 * NA_WIN_C - 1
    n_r = 2 * NA_WIN_R - 1
    for ro in range(n_r):
        val = jnp.full(shape, MASK_VALUE, F32)
        for d in range(n_c):
            val = jnp.where(off == d, rpb_ref[hd * (n_r * n_c) + ro * n_c + d], val)
        o_ref[0, ro] = val * LOG2E


def _na_bias_blocks(rpb):
    n_r = 2 * NA_WIN_R - 1
    n_lh = rpb.shape[0] * rpb.shape[1]
    return pl.pallas_call(
        _na_bias_kernel,
        grid=(n_lh,),
        in_specs=[pl.BlockSpec(memory_space=pltpu.SMEM)],
        out_specs=pl.BlockSpec((1, n_r, GRID_W, GRID_W), lambda h: (h, 0, 0, 0)),
        out_shape=jax.ShapeDtypeStruct((n_lh, n_r, GRID_W, GRID_W), F32),
        compiler_params=_cparams(1),
        name="na_bias",
    )(rpb.reshape(-1))


def _na_block_geometry(rows):
    last_r0 = rows - NA_Q_ROWS
    return ((0, 0), (NA_Q_ROWS, 0), (last_r0, rows - NA_K_ROWS))


def _na_bias_tables(blocks, rows):
    masked = jnp.full(blocks[:, 0].shape, MASK_VALUE * LOG2E, F32)
    tabs = []
    for r0, k0 in _na_block_geometry(rows):
        q_rows = []
        for rq in range(NA_Q_ROWS):
            r = r0 + rq
            rs = min(max(r - NA_WIN_R // 2, 0), rows - NA_WIN_R)
            cols = []
            for kr in range(k0, k0 + NA_K_ROWS):
                cols.append(blocks[:, kr - r + NA_WIN_R - 1] if rs <= kr < rs + NA_WIN_R else masked)
            q_rows.append(jnp.concatenate(cols, axis=-1))
        tabs.append(jnp.concatenate(q_rows, axis=-2))
    return jnp.stack(tabs, axis=0)


def _lat_na_kernel(q_ref, k_ref, v_ref, kc_ref, vc_ref, bias_ref, o_ref, *, rows):
    i = pl.program_id(1)
    k0 = jnp.clip(i * NA_Q_ROWS - NA_WIN_R // 2, 0, rows - NA_K_ROWS)
    krows = pl.ds(pl.multiple_of(k0 * GRID_W, GRID_W), NA_K_ROWS * GRID_W)
    q = q_ref[...] * (NA_DH ** -0.5 * LOG2E)
    for t in range(2):
        sl = slice(LANES * t, LANES * (t + 1))
        qp = q[:, sl]
        kw = k_ref[krows, sl].astype(BF16)
        vw = v_ref[krows, sl].astype(BF16)
        kc = kc_ref[0, :, sl].astype(BF16)
        vc = vc_ref[0, :, sl].astype(BF16)
        bias = [bias_ref[2 * t], bias_ref[2 * t + 1]]
        o_ref[:, sl] = _attend_masked_heads(qp, [kw, kc], [vw, vc], bias)


def _lat_na(q, k, v, kc, vc, bias_tabs, seq):
    T = q.shape[0]
    past = kc.shape[1]
    rows = seq // GRID_W
    nq = rows // NA_Q_ROWS
    qrows = NA_Q_ROWS * GRID_W
    qblk = pl.BlockSpec((qrows, 256), lambda b, i: (b * nq + i, 0))
    kblk = pl.BlockSpec((seq, 256), lambda b, i: (b, 0))
    cblk = pl.BlockSpec((1, past, 256), lambda b, i: (b, 0, 0))
    variant = lambda b, i: (jnp.where(i == 0, 0, jnp.where(i == nq - 1, 2, 1)), 0, 0, 0)
    return pl.pallas_call(
        functools.partial(_lat_na_kernel, rows=rows),
        grid=(T // seq, nq),
        in_specs=[qblk, kblk, kblk, cblk, cblk,
                  pl.BlockSpec((None, NA_HEADS, qrows, NA_K_ROWS * GRID_W), variant)],
        out_specs=qblk,
        out_shape=jax.ShapeDtypeStruct((T, 256), F32),
        compiler_params=_cparams(2),
        name="lat_na",
    )(q, k, v, kc, vc, bias_tabs)


def _lat_df_kernel(q_ref, k_ref, v_ref, kc_ref, vc_ref, lam_ref, subln_ref, o_ref, *, lam_init):
    lam = _df_lambda(lam_ref, lam_init)
    for t in range(2):
        sl = slice(LANES * t, LANES * (t + 1))
        o = _diff_attn_tile(q_ref[...], [k_ref[...], kc_ref[0]], [v_ref[...], vc_ref[0]], lam, t)
        o_ref[:, sl] = _subln(o, subln_ref[:, sl], lam_init)


def _lat_df(q, k, v, kc, vc, df_lam, subln, seq, lam_init):
    T = q.shape[0]
    past = kc.shape[1]
    nq = seq // Q_BLOCK_DF
    qblk = pl.BlockSpec((Q_BLOCK_DF, 256), lambda b, i: (b * nq + i, 0))
    kblk = pl.BlockSpec((seq, 256), lambda b, i: (b, 0))
    cblk = pl.BlockSpec((1, past, 256), lambda b, i: (b, 0, 0))
    return pl.pallas_call(
        functools.partial(_lat_df_kernel, lam_init=lam_init),
        grid=(T // seq, nq),
        in_specs=[qblk, kblk, kblk, cblk, cblk,
                  pl.BlockSpec((4, DF_DQK), lambda b, i: (0, 0)),
                  pl.BlockSpec((1, 256), lambda b, i: (0, 0))],
        out_specs=qblk,
        out_shape=jax.ShapeDtypeStruct((T, 256), F32),
        compiler_params=_cparams(2),
        name="lat_df",
    )(q, k, v, kc, vc, df_lam, subln)


def _mlstm_kernel(*refs, nc, has_init, emit_state, state_first):
    refs = list(refs)
    q_ref, k_ref, v_ref, g_ref = refs[:4]
    pos = 4
    if has_init:
        c0_ref, m0_ref = refs[pos:pos + 2]
        pos += 2
    if emit_state and not state_first:
        pos += 2
    h_ref = refs[pos]
    pos += 1
    if emit_state:
        cn_ref, nn_ref, mn_ref = refs[pos:pos + 3]
        pos += 3
    caug_sc, m_sc = refs[pos:pos + 2]

    if has_init:
        caug_sc[...] = c0_ref[0]
        m_sc[...] = m0_ref[0]
    else:
        caug_sc[...] = jnp.zeros(caug_sc.shape, F32)
        m_sc[...] = jnp.zeros(m_sc.shape, F32)
    h_ref[...] = jnp.zeros(h_ref.shape, F32)

    ti = lax.broadcasted_iota(jnp.int32, (ML_CHUNK, ML_CHUNK), 0)
    si = lax.broadcasted_iota(jnp.int32, (ML_CHUNK, ML_CHUNK), 1)
    masks = (si <= ti, si >= ti)
    ones_blk = jnp.ones((ML_CHUNK, LANES), BF16)
    full = (ML_CHUNK, ML_CHUNK)

    def body(c, carry):
        for d in range(2):
            ci = c if d == 0 else nc - 1 - c
            start = pl.multiple_of(ci * ML_CHUNK, ML_CHUNK)
            rows = pl.ds(start, ML_CHUNK)
            mask = masks[d]
            tri = jnp.where(mask, 1.0, 0.0).astype(BF16)
            gblk = g_ref[rows, :]
            hi = gblk.astype(BF16)
            lo = (gblk - hi.astype(F32)).astype(BF16)
            bc = _dot(tri, hi) + _dot(tri, lo)
            g_t = gblk.T
            bc_t = bc.T
            last = ML_CHUNK - 1 if d == 0 else 0
            for hd in range(ML_HEADS):
                ch_i = 2 * ML_HEADS * d + hd
                ch_f = ch_i + ML_HEADS
                idx = ML_HEADS * d + hd
                hs = slice(ML_DH * hd, ML_DH * (hd + 1))
                b_col = bc[:, ch_f:ch_f + 1]
                a_col = gblk[:, ch_i:ch_i + 1] - b_col
                a_msk = jnp.where(mask, g_t[ch_i:ch_i + 1, :] - bc_t[ch_f:ch_f + 1, :], -jnp.inf)
                m_col = jnp.max(a_msk, axis=1, keepdims=True)
                m_rep = jnp.broadcast_to(m_col, full)
                b_rep = jnp.broadcast_to(b_col, full)
                a_rep = jnp.broadcast_to(a_col, full)
                m_last = m_col[last:last + 1, :]
                b_last = b_col[last:last + 1, :]
                qh = q_ref[rows, hs].astype(BF16)
                kf = k_ref[rows, hs]
                vaug = jnp.concatenate([v_ref[rows, hs].astype(BF16), ones_blk], axis=1)
                s0 = _dot_nt(qh, kf.astype(BF16)) * jnp.exp(a_msk - m_rep)
                sv0 = _dot(s0.astype(BF16), vaug)
                u0 = _dot_tn((jnp.exp(a_rep - m_last) * kf).astype(BF16), vaug)
                m_prev = m_sc[idx][0:1, 0:1]
                g_rep = jnp.maximum(m_prev, m_rep)
                w0 = jnp.exp(m_prev - g_rep)
                cc = jnp.exp(m_rep - g_rep)
                caug = caug_sc[idx]
                qc = _dot(qh, caug.astype(BF16))
                num = w0 * qc[:, :ML_DH] + cc * sv0[:, :ML_DH]
                den = w0 * qc[:, ML_DH:] + cc * sv0[:, ML_DH:]
                h_ref[rows, hs] += num / jnp.maximum(jnp.abs(den), jnp.exp(-b_rep - g_rep))
                g_last = jnp.maximum(m_prev, m_last)
                caug_sc[idx] = jnp.exp(m_prev - g_last) * caug + jnp.exp(m_last - g_last) * u0
                m_sc[idx] = jnp.broadcast_to(b_last + g_last, m_sc.shape[1:])
        return carry

    lax.fori_loop(0, nc, body, 0, unroll=LOOP_UNROLL)
    if emit_state:
        slots = [(0, dl) for dl in range(DEPTH)] if state_first else [(0,)]
        for slot in slots:
            cn_ref[slot] = caug_sc[:, :, :ML_DH]
        for idx in range(2 * ML_HEADS):
            n_row = caug_sc[idx][:, ML_DH:].T[0:1, :]
            for slot in slots:
                nn_ref[slot + (slice(idx, idx + 1), slice(None))] = n_row
        mn_ref[0] = m_sc[...]


def _mlstm(q, k, v, g, seq, init=None, emit_state=False, layer=0, state_prev=None):
    T = q.shape[0]
    n_chain = 2 * ML_HEADS
    blk = pl.BlockSpec((seq, 512), lambda b: (b, 0))
    gblk = pl.BlockSpec((seq, LANES), lambda b: (b, 0))
    cblk = pl.BlockSpec((1, n_chain, ML_DH, 2 * ML_DH), lambda b: (b, 0, 0, 0))
    mblk = pl.BlockSpec((1, n_chain, 8, LANES), lambda b: (b, 0, 0, 0))
    in_specs = [blk, blk, blk, gblk]
    args = [q, k, v, g]
    if init is not None:
        in_specs += [cblk, mblk]
        args += list(init)
    out_specs = [blk]
    out_shape = [jax.ShapeDtypeStruct((T, 512), F32)]
    nb = T // seq
    aliases = {}
    if emit_state:
        if state_prev is None:
            out_specs += [pl.BlockSpec((1, DEPTH, n_chain, ML_DH, ML_DH), lambda b: (b, 0, 0, 0, 0)),
                          pl.BlockSpec((1, DEPTH, n_chain, ML_DH), lambda b: (b, 0, 0, 0)), mblk]
        else:
            out_specs += [pl.BlockSpec((1, None, n_chain, ML_DH, ML_DH), lambda b: (b, layer, 0, 0, 0)),
                          pl.BlockSpec((1, None, n_chain, ML_DH), lambda b: (b, layer, 0, 0)), mblk]
            aliases = {len(args): 1, len(args) + 1: 2}
            in_specs += [pl.BlockSpec(memory_space=pl.ANY)] * 2
            args += list(state_prev)
        out_shape += [jax.ShapeDtypeStruct((nb, DEPTH, n_chain, ML_DH, ML_DH), F32),
                      jax.ShapeDtypeStruct((nb, DEPTH, n_chain, ML_DH), F32),
                      jax.ShapeDtypeStruct((nb, n_chain, 8, LANES), F32)]
    return pl.pallas_call(
        functools.partial(_mlstm_kernel, nc=seq // ML_CHUNK, has_init=init is not None,
                          emit_state=emit_state, state_first=state_prev is None),
        grid=(nb,),
        in_specs=in_specs,
        out_specs=out_specs,
        out_shape=out_shape,
        input_output_aliases=aliases,
        scratch_shapes=[pltpu.VMEM((n_chain, ML_DH, 2 * ML_DH), F32),
                        pltpu.VMEM((n_chain, 8, LANES), F32)],
        compiler_params=_cparams(1),
        name="mlstm_state" if emit_state else "mlstm",
    )(*args)


def _outproj_kernel(x_ref, nao_ref, dfo_ref, mlh_ref, mo_ref, mod_ref, g2_ref, outn_ref, wout_ref,
                    wr_ref, br_ref, x1_ref, h2_ref, gates_ref):
    mlh = mlh_ref[...]
    parts = []
    for hd in range(ML_HEADS):
        seg = mlh[:, ML_DH * hd:ML_DH * (hd + 1)]
        parts.append(seg * lax.rsqrt(jnp.mean(seg * seg, axis=-1, keepdims=True) + EPS))
    ml_o = jnp.concatenate(parts, axis=1) * outn_ref[...] * jax.nn.sigmoid(mo_ref[...])
    mix = (_dot(nao_ref[...].astype(BF16), wout_ref[0:256, :])
           + _dot(dfo_ref[...].astype(BF16), wout_ref[256:512, :])
           + _dot(ml_o.astype(BF16), wout_ref[512:1024, :]))
    x1 = x_ref[...] + mod_ref[0, 2:3, :] * mix
    x1_ref[...] = x1
    y = x1 * lax.rsqrt(jnp.mean(x1 * x1, axis=-1, keepdims=True) + EPS) * g2_ref[...]
    h2 = y * (1.0 + mod_ref[0, 4:5, :]) + mod_ref[0, 3:4, :]
    h2_ref[...] = h2.astype(BF16)

    wr = wr_ref[...]
    h_hi = h2.astype(BF16)
    h_lo = (h2 - h_hi.astype(F32)).astype(BF16)
    w_hi = wr.astype(BF16)
    w_lo = (wr - w_hi.astype(F32)).astype(BF16)
    logits = _dot(h_hi, w_hi) + _dot(h_lo, w_hi) + _dot(h_hi, w_lo)
    s = jax.nn.sigmoid(logits)
    work = s + br_ref[...]
    lane = _lane_iota(work.shape)
    s_sel = jnp.zeros(work.shape, F32)
    for _ in range(TOP_K):
        mx = jnp.max(work, axis=-1, keepdims=True)
        first = jnp.min(jnp.where(work == mx, lane, N_EXPERTS), axis=-1, keepdims=True)
        hit = lane == first
        s_sel = jnp.where(hit, s, s_sel)
        work = jnp.where(hit, -jnp.inf, work)
    gates_ref[...] = s_sel / jnp.sum(s_sel, axis=-1, keepdims=True) * ROUTED_SCALE


def _outproj(x, nao, dfo, mlh, mo, mod_p, tokens_per_mod, layer, g2, outn, w_out_bf, w_router, b_router):
    T = x.shape[0]
    tm = TM_PROJ
    row = lambda i: (i, 0)
    const = lambda i: (0, 0)
    lay = lambda i: (layer, 0, 0)
    return pl.pallas_call(
        _outproj_kernel,
        grid=(T // tm,),
        in_specs=[pl.BlockSpec((tm, D_MODEL), row),
                  pl.BlockSpec((tm, 256), row), pl.BlockSpec((tm, 256), row),
                  pl.BlockSpec((tm, 512), row), pl.BlockSpec((tm, 512), row),
                  pl.BlockSpec((1, 6, D_MODEL), lambda i: ((i * tm) // tokens_per_mod, 0, 0)),
                  pl.BlockSpec((1, D_MODEL), const), pl.BlockSpec((1, 512), const),
                  pl.BlockSpec((None, D_MODEL, D_MODEL), lay),
                  pl.BlockSpec((None, D_MODEL, N_EXPERTS), lay), pl.BlockSpec((1, N_EXPERTS), const)],
        out_specs=[pl.BlockSpec((tm, D_MODEL), row), pl.BlockSpec((tm, D_MODEL), row),
                   pl.BlockSpec((tm, N_EXPERTS), row)],
        out_shape=[jax.ShapeDtypeStruct((T, D_MODEL), F32), jax.ShapeDtypeStruct((T, D_MODEL), BF16),
                   jax.ShapeDtypeStruct((T, N_EXPERTS), F32)],
        compiler_params=_cparams(1),
        name="outproj_router",
    )(x, nao, dfo, mlh, mo, mod_p, g2, outn, w_out_bf, w_router, b_router)


def _moe_kernel(h_ref, gates_ref, x1_ref, mod_ref, w1_ref, w3_ref, w2_ref, s1_ref, s3_ref, s2_ref, o_ref):
    e = pl.program_id(1)
    h = h_ref[...]

    @pl.when(e == 0)
    def _():
        a = _dot(h, s1_ref[...].astype(BF16))
        b = _dot(h, s3_ref[...].astype(BF16))
        o_ref[...] = _dot((_silu(a) * b).astype(BF16), s2_ref[...].astype(BF16))

    gates = gates_ref[...]
    lane = _lane_iota(gates.shape)
    hids = []
    for j in range(MOE_EXPERT_BLOCK):
        a = _dot(h, w1_ref[j].astype(BF16))
        b = _dot(h, w3_ref[j].astype(BF16))
        g = jnp.sum(jnp.where(lane == e * MOE_EXPERT_BLOCK + j, gates, 0.0), axis=-1, keepdims=True)
        hids.append((_silu(a) * b * g).astype(BF16))
    w2 = w2_ref[...].reshape(MOE_EXPERT_BLOCK * D_EXPERT, D_MODEL).astype(BF16)
    o_ref[...] += _dot(jnp.concatenate(hids, axis=1), w2)

    @pl.when(e == N_EXPERTS // MOE_EXPERT_BLOCK - 1)
    def _():
        o_ref[...] = x1_ref[...] + mod_ref[0, 5:6, :] * o_ref[...]


def _moe(h2, gates, x1, mod_p, tokens_per_mod, layer, w1, w3, w2, s1, s3, s2):
    T = h2.shape[0]
    tm = TM_MOE
    eb = MOE_EXPERT_BLOCK
    row = lambda i, e: (i, 0)
    wexp = lambda i, e: (layer, e, 0, 0)
    wsh = lambda i, e: (layer, 0, 0)
    return pl.pallas_call(
        _moe_kernel,
        grid=(T // tm, N_EXPERTS // eb),
        in_specs=[pl.BlockSpec((tm, D_MODEL), row), pl.BlockSpec((tm, N_EXPERTS), row),
                  pl.BlockSpec((tm, D_MODEL), row),
                  pl.BlockSpec((1, 6, D_MODEL), lambda i, e: ((i * tm) // tokens_per_mod, 0, 0)),
                  pl.BlockSpec((None, eb, D_MODEL, D_EXPERT), wexp),
                  pl.BlockSpec((None, eb, D_MODEL, D_EXPERT), wexp),
                  pl.BlockSpec((None, eb, D_EXPERT, D_MODEL), wexp),
                  pl.BlockSpec((None, D_MODEL, D_EXPERT), wsh), pl.BlockSpec((None, D_MODEL, D_EXPERT), wsh),
                  pl.BlockSpec((None, D_EXPERT, D_MODEL), wsh)],
        out_specs=pl.BlockSpec((tm, D_MODEL), row),
        out_shape=jax.ShapeDtypeStruct((T, D_MODEL), F32),
        compiler_params=_cparams(2),
        name="moe_experts",
    )(h2, gates, x1, mod_p, w1, w3, w2, s1, s3, s2)


def _dispatch_kernel(h_ref, gates_ref, xs_ref, cnt_ref, route_ref, *, n_tiles):
    i = pl.program_id(0)

    @pl.when(i == n_tiles)
    def _():
        xs_ref[...] = jnp.zeros(xs_ref.shape, BF16)

    @pl.when(i < n_tiles)
    def _():
        _dispatch_tile(h_ref, gates_ref, xs_ref, cnt_ref, route_ref)


def _dispatch_tile(h_ref, gates_ref, xs_ref, cnt_ref, route_ref):
    tt = h_ref.shape[0]
    n_slots = xs_ref.shape[0]
    gates = gates_ref[...]
    sel = jnp.where(gates > 0.0, 1.0, 0.0)
    selb = sel.astype(BF16)
    ti = lax.broadcasted_iota(jnp.int32, (tt, tt), 0)
    si = lax.broadcasted_iota(jnp.int32, (tt, tt), 1)
    earlier = jnp.where(si < ti, 1.0, 0.0).astype(BF16)
    pos = _dot(earlier, selb)
    cnt = jnp.sum(sel, axis=0, keepdims=True)
    seg = jnp.floor((cnt + (MOE_GRANULE - 1.0)) * (1.0 / MOE_GRANULE)) * MOE_GRANULE
    ei = lax.broadcasted_iota(jnp.int32, (N_EXPERTS, N_EXPERTS), 0)
    ej = lax.broadcasted_iota(jnp.int32, (N_EXPERTS, N_EXPERTS), 1)
    before = jnp.where(ei < ej, 1.0, 0.0).astype(BF16)
    off = _dot(jnp.broadcast_to(seg, (16, N_EXPERTS)).astype(BF16), before)[0:1]
    rank = _dot(selb, before)
    slot = off + pos
    lane = _lane_iota((tt, LANES))
    route = jnp.zeros((tt, LANES), F32)
    for k in range(TOP_K):
        hit = jnp.where(rank == k, sel, 0.0)
        has = jnp.sum(hit, axis=1, keepdims=True)
        slot_k = jnp.sum(hit * slot, axis=1, keepdims=True) + has - 1.0
        w_k = jnp.sum(hit * gates, axis=1, keepdims=True)
        route = jnp.where(lane == k, slot_k, jnp.where(lane == TOP_K + k, w_k, route))
    route_ref[...] = route
    cnt_ref[0] = cnt
    slots_t = route.T
    srow = lax.broadcasted_iota(jnp.int32, (n_slots, tt), 0).astype(F32)
    perm = jnp.zeros((n_slots, tt), F32)
    for k in range(TOP_K):
        perm = jnp.where(srow == slots_t[k:k + 1, :], 1.0, perm)
    xs_ref[...] = _dot(perm.astype(BF16), h_ref[...]).astype(BF16)


def _dispatch(h2, gates):
    T = h2.shape[0]
    nt = T // MOE_TILE
    tile = lambda i: (jnp.minimum(i, nt - 1), 0)
    return pl.pallas_call(
        functools.partial(_dispatch_kernel, n_tiles=nt),
        grid=(nt + 1,),
        in_specs=[pl.BlockSpec((MOE_TILE, D_MODEL), tile),
                  pl.BlockSpec((MOE_TILE, N_EXPERTS), tile)],
        out_specs=[pl.BlockSpec((MOE_SLOTS, D_MODEL), lambda i: (i, 0)),
                   pl.BlockSpec((1, 1, N_EXPERTS), lambda i: (jnp.minimum(i, nt - 1), 0, 0)),
                   pl.BlockSpec((MOE_TILE, LANES), tile)],
        out_shape=[jax.ShapeDtypeStruct(((nt + 1) * MOE_SLOTS, D_MODEL), BF16),
                   jax.ShapeDtypeStruct((nt, 1, N_EXPERTS), F32),
                   jax.ShapeDtypeStruct((T, LANES), F32)],
        compiler_params=_cparams(1),
        name="moe_dispatch",
    )(h2, gates)


def _expert_fused_kernel(te_ref, nact_ref, src_ref, dst_ref, slots_in, w1_ref, w3_ref, w2_ref,
                         slots_hbm, xbuf, ybuf, sem_in, sem_out, *, n_tiles):
    del slots_in
    xs_hbm = ys_hbm = slots_hbm
    i = pl.program_id(0)
    n_act = nact_ref[0]
    gr = MOE_ROWS // MOE_GRANULE
    spare0 = ys_hbm.shape[0] // MOE_GRANULE - MOE_SLOTS // MOE_GRANULE

    def rows(g):
        if isinstance(g, int):
            return pl.ds(g * MOE_GRANULE, MOE_GRANULE)
        return pl.ds(pl.multiple_of(g * MOE_GRANULE, MOE_GRANULE), MOE_GRANULE)

    zero_granule = MOE_SLOTS // MOE_GRANULE - 1

    def gather_start(tile, slot, dummy=False):
        for g in range(gr):
            s = src_ref[tile * gr + g]
            if dummy is not False:
                s = jnp.where(dummy, zero_granule, s)
            pltpu.make_async_copy(xs_hbm.at[rows(s)], xbuf.at[slot, rows(g)], sem_in.at[slot]).start()

    def gather_wait(slot):
        for g in range(gr):
            pltpu.make_async_copy(xs_hbm.at[rows(0)], xbuf.at[slot, rows(g)], sem_in.at[slot]).wait()

    def scatter_start(tile, slot):
        for g in range(gr):
            dd = dst_ref[tile * gr + g]
            dd = jnp.where(dd < 0, spare0 + slot * gr + g, dd)
            pltpu.make_async_copy(ybuf.at[slot, rows(g)], ys_hbm.at[rows(dd)], sem_out.at[slot]).start()

    def scatter_wait(slot):
        for g in range(gr):
            pltpu.make_async_copy(ybuf.at[slot, rows(g)], ys_hbm.at[rows(0)], sem_out.at[slot]).wait()

    slot = lax.rem(i, 2)

    @pl.when((i == 0) & (n_act > 0))
    def _():
        gather_start(0, 0)

    @pl.when(i < n_act)
    def _():
        gather_start(jnp.minimum(i + 1, n_act - 1), 1 - slot, dummy=i + 1 >= n_act)
        gather_wait(slot)

        @pl.when(i >= 2)
        def _():
            scatter_wait(slot)

        x = xbuf[slot]
        a = _dot(x, w1_ref[...].astype(BF16))
        b = _dot(x, w3_ref[...].astype(BF16))
        ybuf[slot] = _dot((_silu(a) * b).astype(BF16), w2_ref[...].astype(BF16)).astype(BF16)
        scatter_start(i, slot)

    @pl.when((i == n_tiles - 1) & (n_act > 0))
    def _():
        last_slot = lax.rem(n_act - 1, 2)
        gather_wait(1 - last_slot)
        scatter_wait(last_slot)

        @pl.when(n_act > 1)
        def _():
            scatter_wait(1 - last_slot)


def _expert_fused(tile_expert, n_active, src, dst, xs_tiles, layer, w1, w3, w2):
    n_tiles = src.shape[0] // (MOE_ROWS // MOE_GRANULE)
    wmap = lambda i, te, na, s, d: (layer, te[i], 0, 0)
    stage = pltpu.VMEM((2, MOE_ROWS, D_MODEL), BF16)
    assert MOE_SLOTS >= 2 * MOE_ROWS
    return pl.pallas_call(
        functools.partial(_expert_fused_kernel, n_tiles=n_tiles),
        grid_spec=pltpu.PrefetchScalarGridSpec(
            num_scalar_prefetch=4, grid=(n_tiles,),
            in_specs=[pl.BlockSpec(memory_space=pl.ANY),
                      pl.BlockSpec((None, None, D_MODEL, D_EXPERT), wmap),
                      pl.BlockSpec((None, None, D_MODEL, D_EXPERT), wmap),
                      pl.BlockSpec((None, None, D_EXPERT, D_MODEL), wmap)],
            out_specs=pl.BlockSpec(memory_space=pl.ANY),
            scratch_shapes=[stage, stage, pltpu.SemaphoreType.DMA((2,)), pltpu.SemaphoreType.DMA((2,))]),
        out_shape=jax.ShapeDtypeStruct(xs_tiles.shape, BF16),
        input_output_aliases={4: 0},
        compiler_params=_cparams(1),
        name="moe_expert_rows",
    )(tile_expert, n_active, src, dst, xs_tiles, w1, w3, w2)


def _combine_kernel(ys_ref, route_ref, h_ref, x1_ref, mod_ref, s1_ref, s3_ref, s2_ref, o_ref):
    route = route_ref[...]
    tt = route.shape[0]
    n_slots = ys_ref.shape[0]
    slot_lane = _lane_iota((tt, n_slots)).astype(F32)
    wp = jnp.zeros((tt, n_slots), F32)
    for k in range(TOP_K):
        wp = jnp.where(slot_lane == route[:, k:k + 1], route[:, TOP_K + k:TOP_K + k + 1], wp)
    w_hi = wp.astype(BF16)
    w_lo = (wp - w_hi.astype(F32)).astype(BF16)
    ys = ys_ref[...]
    routed = _dot(w_hi, ys) + _dot(w_lo, ys)
    h = h_ref[...]
    a = _dot(h, s1_ref[...].astype(BF16))
    b = _dot(h, s3_ref[...].astype(BF16))
    shared = _dot((_silu(a) * b).astype(BF16), s2_ref[...].astype(BF16))
    o_ref[...] = x1_ref[...] + mod_ref[0, 5:6, :] * (routed + shared)


def _combine(ys_tiles, route, h2, x1, mod_p, tokens_per_mod, layer, s1, s3, s2):
    T = h2.shape[0]
    tt = MOE_TILE
    row = lambda i: (i, 0)
    wsh = lambda i: (layer, 0, 0)
    return pl.pallas_call(
        _combine_kernel,
        grid=(T // tt,),
        in_specs=[pl.BlockSpec((MOE_SLOTS, D_MODEL), row), pl.BlockSpec((tt, LANES), row),
                  pl.BlockSpec((tt, D_MODEL), row), pl.BlockSpec((tt, D_MODEL), row),
                  pl.BlockSpec((1, 6, D_MODEL), lambda i: ((i * tt) // tokens_per_mod, 0, 0)),
                  pl.BlockSpec((None, D_MODEL, D_EXPERT), wsh), pl.BlockSpec((None, D_MODEL, D_EXPERT), wsh),
                  pl.BlockSpec((None, D_EXPERT, D_MODEL), wsh)],
        out_specs=pl.BlockSpec((tt, D_MODEL), row),
        out_shape=jax.ShapeDtypeStruct((T, D_MODEL), F32),
        compiler_params=_cparams(1),
        name="moe_combine",
    )(ys_tiles, route, h2, x1, mod_p, s1, s3, s2)


def _excl_cumsum(a, axis):
    return jnp.cumsum(a, axis=axis) - a


def _moe_plan(cnt):
    nt = cnt.shape[0]
    gt = MOE_SLOTS // MOE_GRANULE
    gr = MOE_ROWS // MOE_GRANULE
    ng = _moe_sorted_granules(nt)
    g = (cnt + MOE_GRANULE - 1) // MOE_GRANULE
    loc = _excl_cumsum(g, 1)
    tot = jnp.sum(g, axis=0)
    grp = ((tot + gr - 1) // gr) * gr
    gs = _excl_cumsum(grp, 0)
    start = gs[None, :] + _excl_cumsum(g, 0)
    base = jnp.arange(nt, dtype=jnp.int32)[:, None] * gt + loc
    d = jnp.arange(ng, dtype=jnp.int32)[:, None, None]
    inseg = (start[None] <= d) & (d < (start + g)[None])
    local = jnp.sum(jnp.where(inseg, base[None] + d - start[None], 0), axis=(1, 2))
    real = jnp.any(inseg, axis=(1, 2))
    src = jnp.where(real, local, gt - 1)
    dst = jnp.where(real, local, -1)
    n_active = jnp.sum(grp) // gr
    first = jnp.minimum(jnp.arange(ng // gr, dtype=jnp.int32), jnp.maximum(n_active - 1, 0)) * gr
    tile_expert = jnp.sum((gs[None, :] <= first[:, None]).astype(jnp.int32), axis=1) - 1
    return (src.astype(jnp.int32), dst.astype(jnp.int32), tile_expert.astype(jnp.int32),
            n_active.astype(jnp.int32).reshape(1))


def _moe_sorted_granules(nt):
    gt_used = (TOP_K * MOE_TILE + N_EXPERTS * (MOE_GRANULE - 1)) // MOE_GRANULE
    gr = MOE_ROWS // MOE_GRANULE
    need = nt * gt_used + N_EXPERTS * (gr - 1) + 1
    return -(-need // gr) * gr


def _moe_sparse(h2, gates, x1, mod_p, tokens_per_mod, layer, w1, w3, w2, s1, s3, s2):
    xs_tiles, cnt, route = _dispatch(h2, gates)
    nt = cnt.shape[0]
    src, dst, tile_expert, n_active = _moe_plan(cnt.reshape(nt, N_EXPERTS).astype(jnp.int32))
    ys_tiles = _expert_fused(tile_expert, n_active, src, dst, xs_tiles, layer, w1, w3, w2)
    return _combine(ys_tiles, route, h2, x1, mod_p, tokens_per_mod, layer, s1, s3, s2)


def _group_matrix(group):
    i = np.arange(256)
    return jnp.asarray((i[:, None] // group) == (i[None, :] // group), BF16)


def _rope_tables(seq):
    t = np.arange(seq)
    lane = np.arange(256)
    j = lane % DF_DQK
    nf = DF_DQK // 4
    pos = np.where((j < DF_DQK // 2)[None, :], (t // GRID_W)[:, None], (t % GRID_W)[:, None]).astype(np.float32)
    inv = (1.0 / (ROPE_BASE ** (np.arange(nf, dtype=np.float32) / nf))).astype(np.float32)
    ang = pos * inv[j % nf][None, :]
    sign = np.where((lane % (2 * nf)) < nf, -1.0, 1.0).astype(np.float32)
    return jnp.asarray(np.cos(ang), F32), jnp.asarray(np.sin(ang) * sign[None, :], F32)


def _token_major(a):
    b, h, l, d = a.shape
    return a.transpose(0, 2, 1, 3).reshape(b, l, h * d)


def kernel(x_prompt, x_sample, cache_na_k, cache_na_v, cache_df_k, cache_df_v, state_ml_C, state_ml_n,
           state_ml_m, c, c_ctx, w_ada, b_ada, g_norm1, g_norm2, w_in, w_out, na_qn, na_kn, na_rpb,
           df_qn, df_kn, df_lam, df_subln, ml_gate_b, ml_outn, w_router, b_router, w_exp1, w_exp3, w_exp2,
           w_sh1, w_sh3, w_sh2):
    bc, lc, _ = x_prompt.shape
    bs, ls, _ = x_sample.shape
    past = cache_na_k.shape[3]

    cond = jnp.zeros((8, D_MODEL), F32).at[0].set(c_ctx).at[1:1 + bs].set(c)
    mod = _modulation(cond, w_ada, b_ada)
    g64 = _group_matrix(NA_DH)
    g32 = _group_matrix(DF_DQK)
    rope_tabs = _rope_tables(ls)
    w_in_bf = jnp.pad(w_in.astype(BF16), ((0, 0), (0, 0), (0, IN_COLS_PAD - IN_COLS)))
    w_out_bf = w_out.astype(BF16)
    na_blocks = _na_bias_blocks(na_rpb)
    moe_w = (w_exp1, w_exp3, w_exp2, w_sh1, w_sh3, w_sh2)

    y_p = x_prompt.reshape(bc * lc, D_MODEL)
    y_s = x_sample.reshape(bs * ls, D_MODEL)
    caches = states = None
    m_states = []
    for l in range(DEPTH):
        lam_init = 0.8 - 0.6 * math.exp(-0.3 * l)
        mod_l = mod[l].reshape(8, 6, D_MODEL)
        g1 = g_norm1[l][None, :]
        g2 = g_norm2[l][None, :]
        qn = jnp.tile(na_qn[l], NA_HEADS)[None, :]
        kn = jnp.tile(na_kn[l], NA_HEADS)[None, :]
        dqn = jnp.tile(df_qn[l], 2 * DF_HEADS)[None, :]
        dkn = jnp.tile(df_kn[l], 2 * DF_HEADS)[None, :]
        subln = jnp.tile(df_subln[l], DF_HEADS)[None, :]
        outn = jnp.tile(ml_outn[l], ML_HEADS)[None, :]
        gate_b = jnp.pad(ml_gate_b[l].reshape(-1), (0, LANES - 4 * ML_HEADS))[None, :]
        br = b_router[l][None, :]
        proj_w = (l, g1, w_in_bf, qn, kn, dqn, dkn, gate_b, g64, g32)

        mod_c = mod_l[0:1]
        tpm = bc * lc
        res = _inproj(y_p, mod_c, tpm, *proj_w, None, cache_seq=lc, cache_prev=caches)
        naq, nak, nav, dfq, dfk, dfv, mq, mk, mv, mo, mg = res[:11]
        caches = tuple(res[11:])
        nao, dfo = _ctx_attention(naq, nak, nav, dfq, dfk, dfv, df_lam[l], subln, lc, lam_init)
        mlh, c_n, n_n, m_n = _mlstm(mq, mk, mv, mg, lc, emit_state=True, layer=l, state_prev=states)
        states = (c_n, n_n)
        x1, h2, gates = _outproj(y_p, nao, dfo, mlh, mo, mod_c, tpm, l, g2, outn, w_out_bf, w_router, br)
        y_p = _moe_sparse(h2, gates, x1, mod_c, tpm, l, *moe_w)
        m_states.append(m_n[:, :, 0, 0].reshape(bc, 2, ML_HEADS))

        mod_s = mod_l[1:1 + bs]
        naq, nak, nav, dfq, dfk, dfv, mq, mk, mv, mo, mg = _inproj(y_s, mod_s, ls, *proj_w, rope_tabs)
        kc_na = _token_major(cache_na_k[:, l])
        vc_na = _token_major(cache_na_v[:, l])
        kc_df = cache_df_k[:, l].transpose(0, 3, 1, 2, 4).reshape(bs, past, 256)
        vc_df = _token_major(cache_df_v[:, l])
        bias_tabs = _na_bias_tables(na_blocks[l * NA_HEADS:(l + 1) * NA_HEADS], ls // GRID_W)
        nao = _lat_na(naq, nak, nav, kc_na, vc_na, bias_tabs, ls)
        dfo = _lat_df(dfq, dfk, dfv, kc_df, vc_df, df_lam[l], subln, ls, lam_init)
        n0 = jnp.broadcast_to(state_ml_n[:, l][..., None], (bs, 2, ML_HEADS, ML_DH, ML_DH))
        c0 = jnp.concatenate([state_ml_C[:, l], n0], axis=-1)
        c0 = c0.reshape(bs, 2 * ML_HEADS, ML_DH, 2 * ML_DH)
        m0 = jnp.broadcast_to(state_ml_m[:, l].reshape(bs, 2 * ML_HEADS, 1, 1), (bs, 2 * ML_HEADS, 8, LANES))
        (mlh,) = _mlstm(mq, mk, mv, mg, ls, init=(c0, m0))
        x1, h2, gates = _outproj(y_s, nao, dfo, mlh, mo, mod_s, ls, l, g2, outn, w_out_bf, w_router, br)
        y_s = _moe_sparse(h2, gates, x1, mod_s, ls, l, *moe_w)

    new_ml_c = states[0].reshape(bc, DEPTH, 2, ML_HEADS, ML_DH, ML_DH)
    new_ml_n = states[1].reshape(bc, DEPTH, 2, ML_HEADS, ML_DH)
    return (y_p.reshape(bc, lc, D_MODEL), y_s.reshape(bs, ls, D_MODEL), *caches,
            new_ml_c, new_ml_n, jnp.stack(m_states, axis=1))
```

```python
import functools
import math

import numpy as np
import jax
import jax.numpy as jnp
from jax import lax
from jax.experimental import pallas as pl
from jax.experimental.pallas import tpu as pltpu

F32 = jnp.float32
BF16 = jnp.bfloat16

D_MODEL = 1024
DEPTH = 2
GRID_W = 64
NA_HEADS = 4
NA_DH = 64
NA_WIN_R = 8
NA_WIN_C = 16
DF_HEADS = 4
DF_DV = 64
DF_DQK = 32
ML_HEADS = 4
ML_DH = 128
ML_CHUNK = 128
N_EXPERTS = 32
TOP_K = 4
D_EXPERT = 256
ROUTED_SCALE = 2.5
ROPE_BASE = 10000.0
EPS = 1e-6

LANES = 128
IN_COLS = 3600
IN_COLS_PAD = 3712
GATE_COL0 = 3584
MASK_VALUE = -1e30
VMEM_LIMIT = 56 * 1024 * 1024

TM_PROJ = 512
MOE_TILE = 256
MOE_GRANULE = 16
MOE_SLOTS = TOP_K * MOE_TILE + N_EXPERTS * MOE_GRANULE
MOE_ROWS = 512
Q_BLOCK_DF = 256
CTX_ATTN_BATCHES = 2
DF_SCORE_LEAD_CTX = 1
DF_SCORE_LEAD_LAT = 3
LOOP_UNROLL = 2
NA_Q_ROWS = 4
NA_K_ROWS = NA_Q_ROWS + NA_WIN_R
TM_PROJ_CTX = 256
LOG2E = math.log2(math.e)


def _cparams(n_axes):
    return pltpu.CompilerParams(dimension_semantics=("arbitrary",) * n_axes,
                                vmem_limit_bytes=VMEM_LIMIT)


def _silu(x):
    return x * jax.nn.sigmoid(x)


def _dot(a, b):
    return jnp.dot(a, b, preferred_element_type=F32)


def _dot_nt(a, b):
    return lax.dot_general(a, b, (((1,), (1,)), ((), ())), preferred_element_type=F32)


def _dot_tn(a, b):
    return lax.dot_general(a, b, (((0,), (0,)), ((), ())), preferred_element_type=F32)


def _lane_iota(shape):
    return lax.broadcasted_iota(jnp.int32, shape, 1)


def _mod_kernel(cond_ref, w_ref, b_ref, o_ref):
    s = _silu(cond_ref[...]).astype(BF16)
    o_ref[0] = _dot(s, w_ref[0].astype(BF16)) + b_ref[0]


def _modulation(cond, w_ada, b_ada):
    tn = 1536
    n = 6 * D_MODEL
    return pl.pallas_call(
        _mod_kernel,
        grid=(DEPTH, n // tn),
        in_specs=[pl.BlockSpec((8, D_MODEL), lambda l, j: (0, 0)),
                  pl.BlockSpec((1, D_MODEL, tn), lambda l, j: (l, 0, j)),
                  pl.BlockSpec((1, 1, tn), lambda l, j: (l, 0, j))],
        out_specs=pl.BlockSpec((1, 8, tn), lambda l, j: (l, 0, j)),
        out_shape=jax.ShapeDtypeStruct((DEPTH, 8, n), F32),
        compiler_params=_cparams(2),
        name="adaln_mod",
    )(cond, w_ada, b_ada.reshape(DEPTH, 1, n))


def _group_rms(v, gmat_ref, group):
    sq = v * v
    hi = sq.astype(BF16)
    lo = (sq - hi.astype(F32)).astype(BF16)
    ss = _dot(hi, gmat_ref[...]) + _dot(lo, gmat_ref[...])
    return v * lax.rsqrt(ss * (1.0 / group) + EPS)


def _rope(v, cos, sin_signed):
    first = (_lane_iota(v.shape) % 16) < 8
    partner = jnp.where(first, pltpu.roll(v, v.shape[1] - 8, 1), pltpu.roll(v, 8, 1))
    return v * cos + partner * sin_signed


def _inproj_kernel(*refs, rope, cache_seq, cache_first):
    (x_ref, mod_ref, g1_ref, w_ref, qn_ref, kn_ref, dqn_ref, dkn_ref, gb_ref, g64_ref, g32_ref) = refs[:11]
    pos = 11
    if rope:
        cos_ref, sin_ref = refs[pos:pos + 2]
        pos += 2
    if cache_seq and not cache_first:
        pos += 4
    (naq_ref, nak_ref, nav_ref, dfq_ref, dfk_ref, dfv_ref,
     mq_ref, mk_ref, mv_ref, mo_ref, mg_ref) = refs[pos:pos + 11]
    pos += 11
    x = x_ref[...]
    y = x * lax.rsqrt(jnp.mean(x * x, axis=-1, keepdims=True) + EPS) * g1_ref[...]
    h = (y * (1.0 + mod_ref[0, 1:2, :]) + mod_ref[0, 0:1, :]).astype(BF16)

    def proj(a, b):
        return _dot(h, w_ref[:, a:b])

    naq_ref[...] = _group_rms(proj(0, 256), g64_ref, NA_DH) * qn_ref[...]
    nak = _group_rms(proj(256, 512), g64_ref, NA_DH) * kn_ref[...]
    nav = proj(512, 768)
    dfq = _group_rms(proj(768, 1024), g32_ref, DF_DQK) * dqn_ref[...]
    dfk = _group_rms(proj(1024, 1280), g32_ref, DF_DQK) * dkn_ref[...]
    dfv = proj(1280, 1536)
    if rope:
        dfq = _rope(dfq, cos_ref[...], sin_ref[...])
        dfk = _rope(dfk, cos_ref[...], sin_ref[...])
    nak_ref[...] = nak
    nav_ref[...] = nav
    dfq_ref[...] = dfq
    dfk_ref[...] = dfk
    dfv_ref[...] = dfv
    if cache_seq:
        nk_hm, nv_hm, dk_hm, dv_hm = refs[pos:pos + 4]

        def put(ref, bi, idx, val):
            if cache_first:
                for dl in range(DEPTH):
                    ref[(bi, dl) + idx] = val
            else:
                ref[(bi,) + idx] = val

        for bi in range(x.shape[0] // cache_seq):
            rs = slice(bi * cache_seq, (bi + 1) * cache_seq)
            for hd in range(NA_HEADS):
                put(nk_hm, bi, (hd,), nak[rs, NA_DH * hd:NA_DH * (hd + 1)])
                put(nv_hm, bi, (hd,), nav[rs, NA_DH * hd:NA_DH * (hd + 1)])
                put(dv_hm, bi, (hd,), dfv[rs, DF_DV * hd:DF_DV * (hd + 1)])
                for a in range(2):
                    c0 = (2 * hd + a) * DF_DQK
                    put(dk_hm, bi, (hd, a), dfk[rs, c0:c0 + DF_DQK])
    mq_ref[...] = proj(1536, 2048)
    mk_ref[...] = proj(2048, 2560) * (ML_DH ** -0.5)
    mv_ref[...] = proj(2560, 3072)
    mo_ref[...] = proj(3072, 3584)
    g = proj(GATE_COL0, IN_COLS_PAD) + gb_ref[...]
    lane = _lane_iota(g.shape)
    is_forget = ((lane // ML_HEADS) % 2 == 1) & (lane < 4 * ML_HEADS)
    log_sig = jnp.minimum(g, 0.0) - jnp.log1p(jnp.exp(-jnp.abs(g)))
    mg_ref[...] = jnp.where(is_forget, log_sig, g)


def _inproj(x, mod_p, tokens_per_mod, layer, g1, w_in_bf, qn, kn, dqn, dkn, gate_b, g64, g32, rope_tabs,
            cache_seq=0, cache_prev=None):
    T = x.shape[0]
    tm = TM_PROJ_CTX if cache_seq else TM_PROJ
    rope = rope_tabs is not None
    row = lambda i: (i, 0)
    const = lambda i: (0, 0)
    in_specs = [pl.BlockSpec((tm, D_MODEL), row),
                pl.BlockSpec((1, 6, D_MODEL), lambda i: ((i * tm) // tokens_per_mod, 0, 0)),
                pl.BlockSpec((1, D_MODEL), const),
                pl.BlockSpec((None, D_MODEL, IN_COLS_PAD), lambda i: (layer, 0, 0)),
                pl.BlockSpec((1, 256), const), pl.BlockSpec((1, 256), const),
                pl.BlockSpec((1, 256), const), pl.BlockSpec((1, 256), const),
                pl.BlockSpec((1, LANES), const),
                pl.BlockSpec((256, 256), const), pl.BlockSpec((256, 256), const)]
    args = [x, mod_p, g1, w_in_bf, qn, kn, dqn, dkn, gate_b, g64, g32]
    if rope:
        seq = rope_tabs[0].shape[0]
        tab = lambda i: (i % (seq // tm), 0)
        in_specs += [pl.BlockSpec((tm, 256), tab), pl.BlockSpec((tm, 256), tab)]
        args += list(rope_tabs)
    widths = [256] * 6 + [512] * 4 + [LANES]
    out_specs = [pl.BlockSpec((tm, w), row) for w in widths]
    out_shape = [jax.ShapeDtypeStruct((T, w), F32) for w in widths]
    aliases = {}
    if cache_seq:
        nb = tm // cache_seq
        hm = (NA_HEADS, cache_seq, NA_DH)
        hm_df = (DF_HEADS, 2, cache_seq, DF_DQK)
        for k, tail in enumerate((hm, hm, hm_df, hm)):
            zeros = (0,) * len(tail)
            if cache_prev is None:
                out_specs.append(pl.BlockSpec((nb, DEPTH) + tail, lambda i, z=zeros: (i, 0) + z))
            else:
                out_specs.append(pl.BlockSpec((nb, None) + tail, lambda i, z=zeros: (i, layer) + z))
                aliases[len(args)] = len(widths) + k
                in_specs.append(pl.BlockSpec(memory_space=pl.ANY))
                args.append(cache_prev[k])
            out_shape.append(jax.ShapeDtypeStruct((T // cache_seq, DEPTH) + tail, F32))
    return pl.pallas_call(
        functools.partial(_inproj_kernel, rope=rope, cache_seq=cache_seq, cache_first=cache_prev is None),
        grid=(T // tm,),
        in_specs=in_specs,
        out_specs=out_specs,
        out_shape=out_shape,
        input_output_aliases=aliases,
        compiler_params=_cparams(1),
        name="inproj_rope" if rope else "inproj",
    )(*args)


def _df_lambda(lam_ref, lam_init):
    v = lam_ref[...]
    a = jnp.sum(v[0:1] * v[1:2], axis=1, keepdims=True)
    b = jnp.sum(v[2:3] * v[3:4], axis=1, keepdims=True)
    return jnp.exp(a) - jnp.exp(b) + lam_init


def _softmax_parts(scores):
    es = _softmax_exps(scores)
    tot = functools.reduce(lambda a, b: a + b, [jnp.sum(e, axis=-1, keepdims=True) for e in es])
    return [e.astype(BF16) for e in es], 1.0 / tot


def _softmax_exps(scores):
    m = functools.reduce(jnp.maximum, [jnp.max(s, axis=-1, keepdims=True) for s in scores])
    return [jnp.exp2(s - m) for s in scores]


def _attend_masked_heads(tiles, lead=1):
    lane = _lane_iota(tiles[0][0].shape)
    v_aug = [[jnp.concatenate([v, jnp.ones(v.shape, BF16)], axis=1) for v in v_segs] for _, _, v_segs, _ in tiles]
    n_heads = 2 * len(tiles)

    def scores(c):
        qp, k_segs, _, bias = tiles[c // 2]
        qm = jnp.where(lane // NA_DH == c % 2, qp, 0.0).astype(BF16)
        s = [_dot_nt(qm, k) for k in k_segs]
        if bias is not None:
            s[0] = s[0] + bias[c % 2]
        return s

    outs = []
    pending = [scores(c) for c in range(min(lead, n_heads))]
    for c in range(n_heads):
        if c + lead < n_heads:
            pending.append(scores(c + lead))
        es = _softmax_exps(pending.pop(0))
        o = functools.reduce(lambda x, y: x + y, [_dot(e.astype(BF16), v) for e, v in zip(es, v_aug[c // 2])])
        outs.append(o[:, :LANES] / o[:, LANES:])
    return [jnp.where(lane < NA_DH, outs[2 * t], outs[2 * t + 1]) for t in range(len(tiles))]


def _subln(o, subln_row, lam_init):
    lane = _lane_iota(o.shape)
    sq = o * o
    s0 = jnp.sum(jnp.where(lane < DF_DV, sq, 0.0), axis=-1, keepdims=True)
    s1 = jnp.sum(jnp.where(lane >= DF_DV, sq, 0.0), axis=-1, keepdims=True)
    ms = jnp.where(lane < DF_DV, s0, s1) * (1.0 / DF_DV)
    return o * lax.rsqrt(ms + EPS) * subln_row * (1.0 - lam_init)


def _diff_attn(q, k_segs, v_segs, lam, lead):
    n_maps = 2 * DF_HEADS
    per_tile = LANES // DF_DQK
    lane = _lane_iota((q.shape[0], LANES))
    tiles = []
    for t in range(n_maps // per_tile):
        sl = slice(LANES * t, LANES * (t + 1))
        tiles.append((q[:, sl] * (DF_DQK ** -0.5 * LOG2E),
                      [k[:, sl].astype(BF16) for k in k_segs], [v[:, sl].astype(BF16) for v in v_segs]))

    def scores(c):
        qp, kps, _ = tiles[c // per_tile]
        qm = jnp.where(lane // DF_DQK == c % per_tile, qp, 0.0).astype(BF16)
        return [_dot_nt(qm, kp) for kp in kps]

    def head_out(m0, m1, vps):
        (e0, r0), (e1, r1) = m0, m1
        c0 = r0.astype(BF16)
        c1 = (-(lam * r1)).astype(BF16)
        return functools.reduce(lambda x, y: x + y,
                                [_dot(x0 * c0 + x1 * c1, vp) for x0, x1, vp in zip(e0, e1, vps)])

    heads = []
    maps = []
    pending = [scores(c) for c in range(lead)]
    for c in range(n_maps):
        if c + lead < n_maps:
            pending.append(scores(c + lead))
        maps.append(_softmax_parts(pending.pop(0)))
        if c % 2 == 1:
            heads.append(head_out(maps[c - 1], maps[c], tiles[c // per_tile][2]))
    return [jnp.where(lane < DF_DV, heads[2 * t], heads[2 * t + 1]) for t in range(len(tiles))]


def _ctx_attn_kernel(naq_ref, nak_ref, nav_ref, dfq_ref, dfk_ref, dfv_ref, lam_ref, subln_ref,
                     nao_ref, dfo_ref, *, lam_init, seq):
    lam = _df_lambda(lam_ref, lam_init)
    for bi in range(naq_ref.shape[0] // seq):
        rs = slice(bi * seq, (bi + 1) * seq)
        tiles = []
        for t in range(2):
            sl = slice(LANES * t, LANES * (t + 1))
            tiles.append((naq_ref[rs, sl] * (NA_DH ** -0.5 * LOG2E), [nak_ref[rs, sl].astype(BF16)],
                          [nav_ref[rs, sl].astype(BF16)], None))
        for t, o in enumerate(_attend_masked_heads(tiles, lead=0)):
            nao_ref[rs, LANES * t:LANES * (t + 1)] = o
        outs = _diff_attn(dfq_ref[rs, :], [dfk_ref[rs, :]], [dfv_ref[rs, :]], lam, DF_SCORE_LEAD_CTX)
        for t, o in enumerate(outs):
            sl = slice(LANES * t, LANES * (t + 1))
            dfo_ref[rs, sl] = _subln(o, subln_ref[:, sl], lam_init)


def _ctx_attention(naq, nak, nav, dfq, dfk, dfv, df_lam, subln, seq, lam_init):
    T = naq.shape[0]
    rows = CTX_ATTN_BATCHES * seq
    blk = pl.BlockSpec((rows, 256), lambda b: (b, 0))
    return pl.pallas_call(
        functools.partial(_ctx_attn_kernel, lam_init=lam_init, seq=seq),
        grid=(T // rows,),
        in_specs=[blk] * 6 + [pl.BlockSpec((4, DF_DQK), lambda b: (0, 0)),
                              pl.BlockSpec((1, 256), lambda b: (0, 0))],
        out_specs=[blk, blk],
        out_shape=[jax.ShapeDtypeStruct((T, 256), F32)] * 2,
        compiler_params=_cparams(1),
        name="ctx_attention",
    )(naq, nak, nav, dfq, dfk, dfv, df_lam, subln)


def _na_bias_kernel(rpb_ref, o_ref):
    hd = pl.program_id(0)
    shape = (GRID_W, GRID_W)
    qi = lax.broadcasted_iota(jnp.int32, shape, 0)
    kc = lax.broadcasted_iota(jnp.int32, shape, 1)
    cs = jnp.clip(qi - NA_WIN_C // 2, 0, GRID_W - NA_WIN_C)
    inwin = (kc >= cs) & (kc < cs + NA_WIN_C)
    off = jnp.where(inwin, kc - qi + NA_WIN_C - 1, -1)
    n_c = 2 * NA_WIN_C - 1
    n_r = 2 * NA_WIN_R - 1
    for ro in range(n_r):
        val = jnp.full(shape, MASK_VALUE, F32)
        for d in range(n_c):
            val = jnp.where(off == d, rpb_ref[hd * (n_r * n_c) + ro * n_c + d], val)
        o_ref[0, ro] = val * LOG2E


def _na_bias_blocks(rpb):
    n_r = 2 * NA_WIN_R - 1
    n_lh = rpb.shape[0] * rpb.shape[1]
    return pl.pallas_call(
        _na_bias_kernel,
        grid=(n_lh,),
        in_specs=[pl.BlockSpec(memory_space=pltpu.SMEM)],
        out_specs=pl.BlockSpec((1, n_r, GRID_W, GRID_W), lambda h: (h, 0, 0, 0)),
        out_shape=jax.ShapeDtypeStruct((n_lh, n_r, GRID_W, GRID_W), F32),
        compiler_params=_cparams(1),
        name="na_bias",
    )(rpb.reshape(-1))


def _na_block_geometry(rows):
    last_r0 = rows - NA_Q_ROWS
    return ((0, 0), (NA_Q_ROWS, 0), (last_r0, rows - NA_K_ROWS))


def _na_bias_tables(blocks, rows):
    masked = jnp.full(blocks[:, 0].shape, MASK_VALUE * LOG2E, F32)
    tabs = []
    for r0, k0 in _na_block_geometry(rows):
        q_rows = []
        for rq in range(NA_Q_ROWS):
            r = r0 + rq
            rs = min(max(r - NA_WIN_R // 2, 0), rows - NA_WIN_R)
            cols = []
            for kr in range(k0, k0 + NA_K_ROWS):
                cols.append(blocks[:, kr - r + NA_WIN_R - 1] if rs <= kr < rs + NA_WIN_R else masked)
            q_rows.append(jnp.concatenate(cols, axis=-1))
        tabs.append(jnp.concatenate(q_rows, axis=-2))
    return jnp.stack(tabs, axis=0)


def _lat_na_kernel(q_ref, k_ref, v_ref, kc_ref, vc_ref, bias_ref, o_ref, *, rows):
    i = pl.program_id(1)
    k0 = jnp.clip(i * NA_Q_ROWS - NA_WIN_R // 2, 0, rows - NA_K_ROWS)
    krows = pl.ds(pl.multiple_of(k0 * GRID_W, GRID_W), NA_K_ROWS * GRID_W)
    q = q_ref[...] * (NA_DH ** -0.5 * LOG2E)
    tiles = []
    for t in range(2):
        sl = slice(LANES * t, LANES * (t + 1))
        tiles.append((q[:, sl],
                      [k_ref[krows, sl].astype(BF16), kc_ref[0, :, sl].astype(BF16)],
                      [v_ref[krows, sl].astype(BF16), vc_ref[0, :, sl].astype(BF16)],
                      [bias_ref[2 * t], bias_ref[2 * t + 1]]))
    for t, o in enumerate(_attend_masked_heads(tiles)):
        o_ref[:, LANES * t:LANES * (t + 1)] = o


def _lat_na(q, k, v, kc, vc, bias_tabs, seq):
    T = q.shape[0]
    past = kc.shape[1]
    rows = seq // GRID_W
    nq = rows // NA_Q_ROWS
    qrows = NA_Q_ROWS * GRID_W
    qblk = pl.BlockSpec((qrows, 256), lambda b, i: (b * nq + i, 0))
    kblk = pl.BlockSpec((seq, 256), lambda b, i: (b, 0))
    cblk = pl.BlockSpec((1, past, 256), lambda b, i: (b, 0, 0))
    variant = lambda b, i: (jnp.where(i == 0, 0, jnp.where(i == nq - 1, 2, 1)), 0, 0, 0)
    return pl.pallas_call(
        functools.partial(_lat_na_kernel, rows=rows),
        grid=(T // seq, nq),
        in_specs=[qblk, kblk, kblk, cblk, cblk,
                  pl.BlockSpec((None, NA_HEADS, qrows, NA_K_ROWS * GRID_W), variant)],
        out_specs=qblk,
        out_shape=jax.ShapeDtypeStruct((T, 256), F32),
        compiler_params=_cparams(2),
        name="lat_na",
    )(q, k, v, kc, vc, bias_tabs)


def _lat_df_kernel(q_ref, k_ref, v_ref, kc_ref, vc_ref, lam_ref, subln_ref, o_ref, *, lam_init):
    lam = _df_lambda(lam_ref, lam_init)
    outs = _diff_attn(q_ref[...], [k_ref[...], kc_ref[0]], [v_ref[...], vc_ref[0]], lam, DF_SCORE_LEAD_LAT)
    for t, o in enumerate(outs):
        sl = slice(LANES * t, LANES * (t + 1))
        o_ref[:, sl] = _subln(o, subln_ref[:, sl], lam_init)


def _lat_df(q, k, v, kc, vc, df_lam, subln, seq, lam_init):
    T = q.shape[0]
    past = kc.shape[1]
    nq = seq // Q_BLOCK_DF
    qblk = pl.BlockSpec((Q_BLOCK_DF, 256), lambda b, i: (b * nq + i, 0))
    kblk = pl.BlockSpec((seq, 256), lambda b, i: (b, 0))
    cblk = pl.BlockSpec((1, past, 256), lambda b, i: (b, 0, 0))
    return pl.pallas_call(
        functools.partial(_lat_df_kernel, lam_init=lam_init),
        grid=(T // seq, nq),
        in_specs=[qblk, kblk, kblk, cblk, cblk,
                  pl.BlockSpec((4, DF_DQK), lambda b, i: (0, 0)),
                  pl.BlockSpec((1, 256), lambda b, i: (0, 0))],
        out_specs=qblk,
        out_shape=jax.ShapeDtypeStruct((T, 256), F32),
        compiler_params=_cparams(2),
        name="lat_df",
    )(q, k, v, kc, vc, df_lam, subln)


def _mlstm_kernel(*refs, nc, has_init, emit_state, state_first):
    refs = list(refs)
    q_ref, k_ref, v_ref, g_ref = refs[:4]
    pos = 4
    if has_init:
        c0_ref, m0_ref = refs[pos:pos + 2]
        pos += 2
    if emit_state and not state_first:
        pos += 2
    h_ref = refs[pos]
    pos += 1
    if emit_state:
        cn_ref, nn_ref, mn_ref = refs[pos:pos + 3]
        pos += 3
    caug_sc, m_sc = refs[pos:pos + 2]

    if has_init:
        caug_sc[...] = c0_ref[0]
        m_sc[...] = m0_ref[0]
    else:
        caug_sc[...] = jnp.zeros(caug_sc.shape, F32)
        m_sc[...] = jnp.zeros(m_sc.shape, F32)
    h_ref[...] = jnp.zeros(h_ref.shape, F32)

    ti = lax.broadcasted_iota(jnp.int32, (ML_CHUNK, ML_CHUNK), 0)
    si = lax.broadcasted_iota(jnp.int32, (ML_CHUNK, ML_CHUNK), 1)
    masks = (si <= ti, si >= ti)
    ones_blk = jnp.ones((ML_CHUNK, LANES), BF16)
    full = (ML_CHUNK, ML_CHUNK)

    def body(c, carry):
        for d in range(2):
            ci = c if d == 0 else nc - 1 - c
            start = pl.multiple_of(ci * ML_CHUNK, ML_CHUNK)
            rows = pl.ds(start, ML_CHUNK)
            mask = masks[d]
            tri = jnp.where(mask, 1.0, 0.0).astype(BF16)
            gblk = g_ref[rows, :]
            hi = gblk.astype(BF16)
            lo = (gblk - hi.astype(F32)).astype(BF16)
            bc = _dot(tri, hi) + _dot(tri, lo)
            g_t = gblk.T
            bc_t = bc.T
            last = ML_CHUNK - 1 if d == 0 else 0
            qk = [_dot_nt(q_ref[rows, ML_DH * hd:ML_DH * (hd + 1)].astype(BF16),
                          k_ref[rows, ML_DH * hd:ML_DH * (hd + 1)].astype(BF16)) for hd in range(ML_HEADS)]
            for hd in range(ML_HEADS):
                ch_i = 2 * ML_HEADS * d + hd
                ch_f = ch_i + ML_HEADS
                idx = ML_HEADS * d + hd
                hs = slice(ML_DH * hd, ML_DH * (hd + 1))
                b_col = bc[:, ch_f:ch_f + 1]
                a_col = gblk[:, ch_i:ch_i + 1] - b_col
                a_msk = jnp.where(mask, g_t[ch_i:ch_i + 1, :] - bc_t[ch_f:ch_f + 1, :], -jnp.inf)
                m_col = jnp.max(a_msk, axis=1, keepdims=True)
                m_rep = jnp.broadcast_to(m_col, full)
                b_rep = jnp.broadcast_to(b_col, full)
                a_rep = jnp.broadcast_to(a_col, full)
                m_last = m_col[last:last + 1, :]
                b_last = b_col[last:last + 1, :]
                qh = q_ref[rows, hs].astype(BF16)
                kf = k_ref[rows, hs]
                vaug = jnp.concatenate([v_ref[rows, hs].astype(BF16), ones_blk], axis=1)
                s0 = qk[hd] * jnp.exp(a_msk - m_rep)
                sv0 = _dot(s0.astype(BF16), vaug)
                u0 = _dot_tn((jnp.exp(a_rep - m_last) * kf).astype(BF16), vaug)
                m_prev = m_sc[idx][0:1, 0:1]
                g_rep = jnp.maximum(m_prev, m_rep)
                w0 = jnp.exp(m_prev - g_rep)
                cc = jnp.exp(m_rep - g_rep)
                caug = caug_sc[idx]
                qc = _dot(qh, caug.astype(BF16))
                num = w0 * qc[:, :ML_DH] + cc * sv0[:, :ML_DH]
                den = w0 * qc[:, ML_DH:] + cc * sv0[:, ML_DH:]
                h_ref[rows, hs] += num / jnp.maximum(jnp.abs(den), jnp.exp(-b_rep - g_rep))
                g_last = jnp.maximum(m_prev, m_last)
                caug_sc[idx] = jnp.exp(m_prev - g_last) * caug + jnp.exp(m_last - g_last) * u0
                m_sc[idx] = jnp.broadcast_to(b_last + g_last, m_sc.shape[1:])
        return carry

    lax.fori_loop(0, nc, body, 0, unroll=LOOP_UNROLL)
    if emit_state:
        slots = [(0, dl) for dl in range(DEPTH)] if state_first else [(0,)]
        for slot in slots:
            cn_ref[slot] = caug_sc[:, :, :ML_DH]
        for idx in range(2 * ML_HEADS):
            n_row = caug_sc[idx][:, ML_DH:].T[0:1, :]
            for slot in slots:
                nn_ref[slot + (slice(idx, idx + 1), slice(None))] = n_row
        mn_ref[0] = m_sc[...]


def _mlstm(q, k, v, g, seq, init=None, emit_state=False, layer=0, state_prev=None):
    T = q.shape[0]
    n_chain = 2 * ML_HEADS
    blk = pl.BlockSpec((seq, 512), lambda b: (b, 0))
    gblk = pl.BlockSpec((seq, LANES), lambda b: (b, 0))
    cblk = pl.BlockSpec((1, n_chain, ML_DH, 2 * ML_DH), lambda b: (b, 0, 0, 0))
    mblk = pl.BlockSpec((1, n_chain, 8, LANES), lambda b: (b, 0, 0, 0))
    in_specs = [blk, blk, blk, gblk]
    args = [q, k, v, g]
    if init is not None:
        in_specs += [cblk, mblk]
        args += list(init)
    out_specs = [blk]
    out_shape = [jax.ShapeDtypeStruct((T, 512), F32)]
    nb = T // seq
    aliases = {}
    if emit_state:
        if state_prev is None:
            out_specs += [pl.BlockSpec((1, DEPTH, n_chain, ML_DH, ML_DH), lambda b: (b, 0, 0, 0, 0)),
                          pl.BlockSpec((1, DEPTH, n_chain, ML_DH), lambda b: (b, 0, 0, 0)), mblk]
        else:
            out_specs += [pl.BlockSpec((1, None, n_chain, ML_DH, ML_DH), lambda b: (b, layer, 0, 0, 0)),
                          pl.BlockSpec((1, None, n_chain, ML_DH), lambda b: (b, layer, 0, 0)), mblk]
            aliases = {len(args): 1, len(args) + 1: 2}
            in_specs += [pl.BlockSpec(memory_space=pl.ANY)] * 2
            args += list(state_prev)
        out_shape += [jax.ShapeDtypeStruct((nb, DEPTH, n_chain, ML_DH, ML_DH), F32),
                      jax.ShapeDtypeStruct((nb, DEPTH, n_chain, ML_DH), F32),
                      jax.ShapeDtypeStruct((nb, n_chain, 8, LANES), F32)]
    return pl.pallas_call(
        functools.partial(_mlstm_kernel, nc=seq // ML_CHUNK, has_init=init is not None,
                          emit_state=emit_state, state_first=state_prev is None),
        grid=(nb,),
        in_specs=in_specs,
        out_specs=out_specs,
        out_shape=out_shape,
        input_output_aliases=aliases,
        scratch_shapes=[pltpu.VMEM((n_chain, ML_DH, 2 * ML_DH), F32),
                        pltpu.VMEM((n_chain, 8, LANES), F32)],
        compiler_params=_cparams(1),
        name="mlstm_state" if emit_state else "mlstm",
    )(*args)


def _outproj_kernel(x_ref, nao_ref, dfo_ref, mlh_ref, mo_ref, mod_ref, g2_ref, outn_ref, wout_ref,
                    wr_ref, br_ref, x1_ref, h2_ref, gates_ref):
    mlh = mlh_ref[...]
    parts = []
    for hd in range(ML_HEADS):
        seg = mlh[:, ML_DH * hd:ML_DH * (hd + 1)]
        parts.append(seg * lax.rsqrt(jnp.mean(seg * seg, axis=-1, keepdims=True) + EPS))
    ml_o = jnp.concatenate(parts, axis=1) * outn_ref[...] * jax.nn.sigmoid(mo_ref[...])
    mix = (_dot(nao_ref[...].astype(BF16), wout_ref[0:256, :])
           + _dot(dfo_ref[...].astype(BF16), wout_ref[256:512, :])
           + _dot(ml_o.astype(BF16), wout_ref[512:1024, :]))
    x1 = x_ref[...] + mod_ref[0, 2:3, :] * mix
    x1_ref[...] = x1
    y = x1 * lax.rsqrt(jnp.mean(x1 * x1, axis=-1, keepdims=True) + EPS) * g2_ref[...]
    h2 = y * (1.0 + mod_ref[0, 4:5, :]) + mod_ref[0, 3:4, :]
    h2_ref[...] = h2.astype(BF16)

    wr = wr_ref[...]
    h_hi = h2.astype(BF16)
    h_lo = (h2 - h_hi.astype(F32)).astype(BF16)
    w_hi = wr.astype(BF16)
    w_lo = (wr - w_hi.astype(F32)).astype(BF16)
    logits = _dot(h_hi, w_hi) + _dot(h_lo, w_hi) + _dot(h_hi, w_lo)
    s = jax.nn.sigmoid(logits)
    work = s + br_ref[...]
    lane = _lane_iota(work.shape)
    s_sel = jnp.zeros(work.shape, F32)
    for _ in range(TOP_K):
        mx = jnp.max(work, axis=-1, keepdims=True)
        first = jnp.min(jnp.where(work == mx, lane, N_EXPERTS), axis=-1, keepdims=True)
        hit = lane == first
        s_sel = jnp.where(hit, s, s_sel)
        work = jnp.where(hit, -jnp.inf, work)
    gates_ref[...] = s_sel / jnp.sum(s_sel, axis=-1, keepdims=True) * ROUTED_SCALE


def _outproj(x, nao, dfo, mlh, mo, mod_p, tokens_per_mod, layer, g2, outn, w_out_bf, w_router, b_router):
    T = x.shape[0]
    tm = TM_PROJ
    row = lambda i: (i, 0)
    const = lambda i: (0, 0)
    lay = lambda i: (layer, 0, 0)
    return pl.pallas_call(
        _outproj_kernel,
        grid=(T // tm,),
        in_specs=[pl.BlockSpec((tm, D_MODEL), row),
                  pl.BlockSpec((tm, 256), row), pl.BlockSpec((tm, 256), row),
                  pl.BlockSpec((tm, 512), row), pl.BlockSpec((tm, 512), row),
                  pl.BlockSpec((1, 6, D_MODEL), lambda i: ((i * tm) // tokens_per_mod, 0, 0)),
                  pl.BlockSpec((1, D_MODEL), const), pl.BlockSpec((1, 512), const),
                  pl.BlockSpec((None, D_MODEL, D_MODEL), lay),
                  pl.BlockSpec((None, D_MODEL, N_EXPERTS), lay), pl.BlockSpec((1, N_EXPERTS), const)],
        out_specs=[pl.BlockSpec((tm, D_MODEL), row), pl.BlockSpec((tm, D_MODEL), row),
                   pl.BlockSpec((tm, N_EXPERTS), row)],
        out_shape=[jax.ShapeDtypeStruct((T, D_MODEL), F32), jax.ShapeDtypeStruct((T, D_MODEL), BF16),
                   jax.ShapeDtypeStruct((T, N_EXPERTS), F32)],
        compiler_params=_cparams(1),
        name="outproj_router",
    )(x, nao, dfo, mlh, mo, mod_p, g2, outn, w_out_bf, w_router, b_router)


def _dispatch_kernel(h_ref, gates_ref, xs_ref, cnt_ref, route_ref, *, n_tiles):
    i = pl.program_id(0)

    @pl.when(i == n_tiles)
    def _():
        xs_ref[...] = jnp.zeros(xs_ref.shape, BF16)

    @pl.when(i < n_tiles)
    def _():
        _dispatch_tile(h_ref, gates_ref, xs_ref, cnt_ref, route_ref)


def _dispatch_tile(h_ref, gates_ref, xs_ref, cnt_ref, route_ref):
    tt = h_ref.shape[0]
    n_slots = xs_ref.shape[0]
    gates = gates_ref[...]
    sel = jnp.where(gates > 0.0, 1.0, 0.0)
    selb = sel.astype(BF16)
    ti = lax.broadcasted_iota(jnp.int32, (tt, tt), 0)
    si = lax.broadcasted_iota(jnp.int32, (tt, tt), 1)
    earlier = jnp.where(si < ti, 1.0, 0.0).astype(BF16)
    pos = _dot(earlier, selb)
    cnt = jnp.sum(sel, axis=0, keepdims=True)
    seg = jnp.floor((cnt + (MOE_GRANULE - 1.0)) * (1.0 / MOE_GRANULE)) * MOE_GRANULE
    ei = lax.broadcasted_iota(jnp.int32, (N_EXPERTS, N_EXPERTS), 0)
    ej = lax.broadcasted_iota(jnp.int32, (N_EXPERTS, N_EXPERTS), 1)
    before = jnp.where(ei < ej, 1.0, 0.0).astype(BF16)
    off = _dot(jnp.broadcast_to(seg, (16, N_EXPERTS)).astype(BF16), before)[0:1]
    rank = _dot(selb, before)
    slot = off + pos
    lane = _lane_iota((tt, LANES))
    route = jnp.zeros((tt, LANES), F32)
    for k in range(TOP_K):
        hit = jnp.where(rank == k, sel, 0.0)
        has = jnp.sum(hit, axis=1, keepdims=True)
        slot_k = jnp.sum(hit * slot, axis=1, keepdims=True) + has - 1.0
        w_k = jnp.sum(hit * gates, axis=1, keepdims=True)
        route = jnp.where(lane == k, slot_k, jnp.where(lane == TOP_K + k, w_k, route))
    route_ref[...] = route
    cnt_ref[0] = cnt
    slots_t = route.T
    srow = lax.broadcasted_iota(jnp.int32, (n_slots, tt), 0).astype(F32)
    perm = jnp.zeros((n_slots, tt), F32)
    for k in range(TOP_K):
        perm = jnp.where(srow == slots_t[k:k + 1, :], 1.0, perm)
    xs_ref[...] = _dot(perm.astype(BF16), h_ref[...]).astype(BF16)


def _dispatch(h2, gates):
    T = h2.shape[0]
    nt = T // MOE_TILE
    tile = lambda i: (jnp.minimum(i, nt - 1), 0)
    return pl.pallas_call(
        functools.partial(_dispatch_kernel, n_tiles=nt),
        grid=(nt + 1,),
        in_specs=[pl.BlockSpec((MOE_TILE, D_MODEL), tile),
                  pl.BlockSpec((MOE_TILE, N_EXPERTS), tile)],
        out_specs=[pl.BlockSpec((MOE_SLOTS, D_MODEL), lambda i: (i, 0)),
                   pl.BlockSpec((1, 1, N_EXPERTS), lambda i: (jnp.minimum(i, nt - 1), 0, 0)),
                   pl.BlockSpec((MOE_TILE, LANES), tile)],
        out_shape=[jax.ShapeDtypeStruct(((nt + 1) * MOE_SLOTS, D_MODEL), BF16),
                   jax.ShapeDtypeStruct((nt, 1, N_EXPERTS), F32),
                   jax.ShapeDtypeStruct((T, LANES), F32)],
        compiler_params=_cparams(1),
        name="moe_dispatch",
    )(h2, gates)


def _expert_fused_kernel(te_ref, nact_ref, src_ref, dst_ref, slots_in, w1_ref, w3_ref, w2_ref,
                         slots_hbm, xbuf, ybuf, sem_in, sem_out, *, n_tiles):
    del slots_in
    xs_hbm = ys_hbm = slots_hbm
    i = pl.program_id(0)
    n_act = nact_ref[0]
    gr = MOE_ROWS // MOE_GRANULE
    spare0 = ys_hbm.shape[0] // MOE_GRANULE - MOE_SLOTS // MOE_GRANULE

    def rows(g):
        if isinstance(g, int):
            return pl.ds(g * MOE_GRANULE, MOE_GRANULE)
        return pl.ds(pl.multiple_of(g * MOE_GRANULE, MOE_GRANULE), MOE_GRANULE)

    zero_granule = MOE_SLOTS // MOE_GRANULE - 1

    def gather_start(tile, slot, dummy=False):
        for g in range(gr):
            s = src_ref[tile * gr + g]
            if dummy is not False:
                s = jnp.where(dummy, zero_granule, s)
            pltpu.make_async_copy(xs_hbm.at[rows(s)], xbuf.at[slot, rows(g)], sem_in.at[slot]).start()

    def gather_wait(slot):
        for g in range(gr):
            pltpu.make_async_copy(xs_hbm.at[rows(0)], xbuf.at[slot, rows(g)], sem_in.at[slot]).wait()

    def scatter_start(tile, slot):
        for g in range(gr):
            dd = dst_ref[tile * gr + g]
            dd = jnp.where(dd < 0, spare0 + slot * gr + g, dd)
            pltpu.make_async_copy(ybuf.at[slot, rows(g)], ys_hbm.at[rows(dd)], sem_out.at[slot]).start()

    def scatter_wait(slot):
        for g in range(gr):
            pltpu.make_async_copy(ybuf.at[slot, rows(g)], ys_hbm.at[rows(0)], sem_out.at[slot]).wait()

    slot = lax.rem(i, 2)

    @pl.when((i == 0) & (n_act > 0))
    def _():
        gather_start(0, 0)

    @pl.when(i < n_act)
    def _():
        gather_start(jnp.minimum(i + 1, n_act - 1), 1 - slot, dummy=i + 1 >= n_act)
        gather_wait(slot)

        @pl.when(i >= 2)
        def _():
            scatter_wait(slot)

        x = xbuf[slot]
        a = _dot(x, w1_ref[...].astype(BF16))
        b = _dot(x, w3_ref[...].astype(BF16))
        ybuf[slot] = _dot((_silu(a) * b).astype(BF16), w2_ref[...].astype(BF16)).astype(BF16)
        scatter_start(i, slot)

    @pl.when((i == n_tiles - 1) & (n_act > 0))
    def _():
        last_slot = lax.rem(n_act - 1, 2)
        gather_wait(1 - last_slot)
        scatter_wait(last_slot)

        @pl.when(n_act > 1)
        def _():
            scatter_wait(1 - last_slot)


def _expert_fused(tile_expert, n_active, src, dst, xs_tiles, layer, w1, w3, w2):
    n_tiles = src.shape[0] // (MOE_ROWS // MOE_GRANULE)
    wmap = lambda i, te, na, s, d: (layer, te[i], 0, 0)
    stage = pltpu.VMEM((2, MOE_ROWS, D_MODEL), BF16)
    assert MOE_SLOTS >= 2 * MOE_ROWS
    return pl.pallas_call(
        functools.partial(_expert_fused_kernel, n_tiles=n_tiles),
        grid_spec=pltpu.PrefetchScalarGridSpec(
            num_scalar_prefetch=4, grid=(n_tiles,),
            in_specs=[pl.BlockSpec(memory_space=pl.ANY),
                      pl.BlockSpec((None, None, D_MODEL, D_EXPERT), wmap),
                      pl.BlockSpec((None, None, D_MODEL, D_EXPERT), wmap),
                      pl.BlockSpec((None, None, D_EXPERT, D_MODEL), wmap)],
            out_specs=pl.BlockSpec(memory_space=pl.ANY),
            scratch_shapes=[stage, stage, pltpu.SemaphoreType.DMA((2,)), pltpu.SemaphoreType.DMA((2,))]),
        out_shape=jax.ShapeDtypeStruct(xs_tiles.shape, BF16),
        input_output_aliases={4: 0},
        compiler_params=_cparams(1),
        name="moe_expert_rows",
    )(tile_expert, n_active, src, dst, xs_tiles, w1, w3, w2)


def _combine_kernel(ys_ref, route_ref, h_ref, x1_ref, mod_ref, s1_ref, s3_ref, s2_ref, o_ref):
    route = route_ref[...]
    tt = route.shape[0]
    n_slots = ys_ref.shape[0]
    slot_lane = _lane_iota((tt, n_slots)).astype(F32)
    wp = jnp.zeros((tt, n_slots), F32)
    for k in range(TOP_K):
        wp = jnp.where(slot_lane == route[:, k:k + 1], route[:, TOP_K + k:TOP_K + k + 1], wp)
    routed = _dot(wp.astype(BF16), ys_ref[...])
    h = h_ref[...]
    a = _dot(h, s1_ref[...].astype(BF16))
    b = _dot(h, s3_ref[...].astype(BF16))
    shared = _dot((_silu(a) * b).astype(BF16), s2_ref[...].astype(BF16))
    o_ref[...] = x1_ref[...] + mod_ref[0, 5:6, :] * (routed + shared)


def _combine(ys_tiles, route, h2, x1, mod_p, tokens_per_mod, layer, s1, s3, s2):
    T = h2.shape[0]
    tt = MOE_TILE
    row = lambda i: (i, 0)
    wsh = lambda i: (layer, 0, 0)
    return pl.pallas_call(
        _combine_kernel,
        grid=(T // tt,),
        in_specs=[pl.BlockSpec((MOE_SLOTS, D_MODEL), row), pl.BlockSpec((tt, LANES), row),
                  pl.BlockSpec((tt, D_MODEL), row), pl.BlockSpec((tt, D_MODEL), row),
                  pl.BlockSpec((1, 6, D_MODEL), lambda i: ((i * tt) // tokens_per_mod, 0, 0)),
                  pl.BlockSpec((None, D_MODEL, D_EXPERT), wsh), pl.BlockSpec((None, D_MODEL, D_EXPERT), wsh),
                  pl.BlockSpec((None, D_EXPERT, D_MODEL), wsh)],
        out_specs=pl.BlockSpec((tt, D_MODEL), row),
        out_shape=jax.ShapeDtypeStruct((T, D_MODEL), F32),
        compiler_params=_cparams(1),
        name="moe_combine",
    )(ys_tiles, route, h2, x1, mod_p, s1, s3, s2)


def _excl_cumsum(a, axis):
    return jnp.cumsum(a, axis=axis) - a


def _moe_plan(cnt):
    nt = cnt.shape[0]
    gt = MOE_SLOTS // MOE_GRANULE
    gr = MOE_ROWS // MOE_GRANULE
    ng = _moe_sorted_granules(nt)
    g = (cnt + MOE_GRANULE - 1) // MOE_GRANULE
    loc = _excl_cumsum(g, 1)
    tot = jnp.sum(g, axis=0)
    grp = ((tot + gr - 1) // gr) * gr
    gs = _excl_cumsum(grp, 0)
    start = gs[None, :] + _excl_cumsum(g, 0)
    base = jnp.arange(nt, dtype=jnp.int32)[:, None] * gt + loc
    d = jnp.arange(ng, dtype=jnp.int32)[:, None, None]
    inseg = (start[None] <= d) & (d < (start + g)[None])
    local = jnp.sum(jnp.where(inseg, base[None] + d - start[None], 0), axis=(1, 2))
    real = jnp.any(inseg, axis=(1, 2))
    src = jnp.where(real, local, gt - 1)
    dst = jnp.where(real, local, -1)
    n_active = jnp.sum(grp) // gr
    first = jnp.minimum(jnp.arange(ng // gr, dtype=jnp.int32), jnp.maximum(n_active - 1, 0)) * gr
    tile_expert = jnp.sum((gs[None, :] <= first[:, None]).astype(jnp.int32), axis=1) - 1
    return (src.astype(jnp.int32), dst.astype(jnp.int32), tile_expert.astype(jnp.int32),
            n_active.astype(jnp.int32).reshape(1))


def _moe_sorted_granules(nt):
    gt_used = (TOP_K * MOE_TILE + N_EXPERTS * (MOE_GRANULE - 1)) // MOE_GRANULE
    gr = MOE_ROWS // MOE_GRANULE
    need = nt * gt_used + N_EXPERTS * (gr - 1) + 1
    return -(-need // gr) * gr


def _moe_sparse(h2, gates, x1, mod_p, tokens_per_mod, layer, w1, w3, w2, s1, s3, s2):
    xs_tiles, cnt, route = _dispatch(h2, gates)
    nt = cnt.shape[0]
    src, dst, tile_expert, n_active = _moe_plan(cnt.reshape(nt, N_EXPERTS).astype(jnp.int32))
    ys_tiles = _expert_fused(tile_expert, n_active, src, dst, xs_tiles, layer, w1, w3, w2)
    return _combine(ys_tiles, route, h2, x1, mod_p, tokens_per_mod, layer, s1, s3, s2)


def _group_matrix(group):
    i = np.arange(256)
    return jnp.asarray((i[:, None] // group) == (i[None, :] // group), BF16)


def _rope_tables(seq):
    t = np.arange(seq)
    lane = np.arange(256)
    j = lane % DF_DQK
    nf = DF_DQK // 4
    pos = np.where((j < DF_DQK // 2)[None, :], (t // GRID_W)[:, None], (t % GRID_W)[:, None]).astype(np.float32)
    inv = (1.0 / (ROPE_BASE ** (np.arange(nf, dtype=np.float32) / nf))).astype(np.float32)
    ang = pos * inv[j % nf][None, :]
    sign = np.where((lane % (2 * nf)) < nf, -1.0, 1.0).astype(np.float32)
    return jnp.asarray(np.cos(ang), F32), jnp.asarray(np.sin(ang) * sign[None, :], F32)


def _token_major(a):
    b, h, l, d = a.shape
    return a.transpose(0, 2, 1, 3).reshape(b, l, h * d)


def kernel(x_prompt, x_sample, cache_na_k, cache_na_v, cache_df_k, cache_df_v, state_ml_C, state_ml_n,
           state_ml_m, c, c_ctx, w_ada, b_ada, g_norm1, g_norm2, w_in, w_out, na_qn, na_kn, na_rpb,
           df_qn, df_kn, df_lam, df_subln, ml_gate_b, ml_outn, w_router, b_router, w_exp1, w_exp3, w_exp2,
           w_sh1, w_sh3, w_sh2):
    bc, lc, _ = x_prompt.shape
    bs, ls, _ = x_sample.shape
    past = cache_na_k.shape[3]

    cond = jnp.zeros((8, D_MODEL), F32).at[0].set(c_ctx).at[1:1 + bs].set(c)
    mod = _modulation(cond, w_ada, b_ada)
    g64 = _group_matrix(NA_DH)
    g32 = _group_matrix(DF_DQK)
    rope_tabs = _rope_tables(ls)
    w_in_bf = jnp.pad(w_in.astype(BF16), ((0, 0), (0, 0), (0, IN_COLS_PAD - IN_COLS)))
    w_out_bf = w_out.astype(BF16)
    na_blocks = _na_bias_blocks(na_rpb)
    moe_w = (w_exp1, w_exp3, w_exp2, w_sh1, w_sh3, w_sh2)

    y_p = x_prompt.reshape(bc * lc, D_MODEL)
    y_s = x_sample.reshape(bs * ls, D_MODEL)
    caches = states = None
    m_states = []
    for l in range(DEPTH):
        lam_init = 0.8 - 0.6 * math.exp(-0.3 * l)
        mod_l = mod[l].reshape(8, 6, D_MODEL)
        g1 = g_norm1[l][None, :]
        g2 = g_norm2[l][None, :]
        qn = jnp.tile(na_qn[l], NA_HEADS)[None, :]
        kn = jnp.tile(na_kn[l], NA_HEADS)[None, :]
        dqn = jnp.tile(df_qn[l], 2 * DF_HEADS)[None, :]
        dkn = jnp.tile(df_kn[l], 2 * DF_HEADS)[None, :]
        subln = jnp.tile(df_subln[l], DF_HEADS)[None, :]
        outn = jnp.tile(ml_outn[l], ML_HEADS)[None, :]
        gate_b = jnp.pad(ml_gate_b[l].reshape(-1), (0, LANES - 4 * ML_HEADS))[None, :]
        br = b_router[l][None, :]
        proj_w = (l, g1, w_in_bf, qn, kn, dqn, dkn, gate_b, g64, g32)

        mod_c = mod_l[0:1]
        tpm = bc * lc
        res = _inproj(y_p, mod_c, tpm, *proj_w, None, cache_seq=lc, cache_prev=caches)
        naq, nak, nav, dfq, dfk, dfv, mq, mk, mv, mo, mg = res[:11]
        caches = tuple(res[11:])
        nao, dfo = _ctx_attention(naq, nak, nav, dfq, dfk, dfv, df_lam[l], subln, lc, lam_init)
        mlh, c_n, n_n, m_n = _mlstm(mq, mk, mv, mg, lc, emit_state=True, layer=l, state_prev=states)
        states = (c_n, n_n)
        x1, h2, gates = _outproj(y_p, nao, dfo, mlh, mo, mod_c, tpm, l, g2, outn, w_out_bf, w_router, br)
        y_p = _moe_sparse(h2, gates, x1, mod_c, tpm, l, *moe_w)
        m_states.append(m_n[:, :, 0, 0].reshape(bc, 2, ML_HEADS))

        mod_s = mod_l[1:1 + bs]
        naq, nak, nav, dfq, dfk, dfv, mq, mk, mv, mo, mg = _inproj(y_s, mod_s, ls, *proj_w, rope_tabs)
        kc_na = _token_major(cache_na_k[:, l])
        vc_na = _token_major(cache_na_v[:, l])
        kc_df = cache_df_k[:, l].transpose(0, 3, 1, 2, 4).reshape(bs, past, 256)
        vc_df = _token_major(cache_df_v[:, l])
        bias_tabs = _na_bias_tables(na_blocks[l * NA_HEADS:(l + 1) * NA_HEADS], ls // GRID_W)
        nao = _lat_na(naq, nak, nav, kc_na, vc_na, bias_tabs, ls)
        dfo = _lat_df(dfq, dfk, dfv, kc_df, vc_df, df_lam[l], subln, ls, lam_init)
        n0 = jnp.broadcast_to(state_ml_n[:, l][..., None], (bs, 2, ML_HEADS, ML_DH, ML_DH))
        c0 = jnp.concatenate([state_ml_C[:, l], n0], axis=-1)
        c0 = c0.reshape(bs, 2 * ML_HEADS, ML_DH, 2 * ML_DH)
        m0 = jnp.broadcast_to(state_ml_m[:, l].reshape(bs, 2 * ML_HEADS, 1, 1), (bs, 2 * ML_HEADS, 8, LANES))
        (mlh,) = _mlstm(mq, mk, mv, mg, ls, init=(c0, m0))
        x1, h2, gates = _outproj(y_s, nao, dfo, mlh, mo, mod_s, ls, l, g2, outn, w_out_bf, w_router, br)
        y_s = _moe_sparse(h2, gates, x1, mod_s, ls, l, *moe_w)

    new_ml_c = states[0].reshape(bc, DEPTH, 2, ML_HEADS, ML_DH, ML_DH)
    new_ml_n = states[1].reshape(bc, DEPTH, 2, ML_HEADS, ML_DH)
    return (y_p.reshape(bc, lc, D_MODEL), y_s.reshape(bs, ls, D_MODEL), *caches,
            new_ml_c, new_ml_n, jnp.stack(m_states, axis=1))
```

```python
import functools
import math

import numpy as np
import jax
import jax.numpy as jnp
from jax import lax
from jax.experimental import pallas as pl
from jax.experimental.pallas import tpu as pltpu

F32 = jnp.float32
BF16 = jnp.bfloat16

D_MODEL = 1024
DEPTH = 2
GRID_W = 64
NA_HEADS = 4
NA_DH = 64
NA_WIN_R = 8
NA_WIN_C = 16
DF_HEADS = 4
DF_DV = 64
DF_DQK = 32
ML_HEADS = 4
ML_DH = 128
ML_CHUNK = 128
N_EXPERTS = 32
TOP_K = 4
D_EXPERT = 256
ROUTED_SCALE = 2.5
ROPE_BASE = 10000.0
EPS = 1e-6

LANES = 128
IN_COLS = 3600
IN_COLS_PAD = 3712
GATE_COL0 = 3584
MASK_VALUE = -1e30
VMEM_LIMIT = 56 * 1024 * 1024

TM_PROJ = 512
MOE_TILE = 256
MOE_GRANULE = 16
MOE_SLOTS = TOP_K * MOE_TILE + N_EXPERTS * MOE_GRANULE
MOE_ROWS = 512
Q_BLOCK_DF = 256
CTX_ATTN_BATCHES = 2
DF_SCORE_LEAD_CTX = 1
DF_SCORE_LEAD_LAT = 3
LOOP_UNROLL = 2
NA_Q_ROWS = 4
NA_K_ROWS = NA_Q_ROWS + NA_WIN_R
LOG2E = math.log2(math.e)


def _cparams(n_axes):
    return pltpu.CompilerParams(dimension_semantics=("arbitrary",) * n_axes,
                                vmem_limit_bytes=VMEM_LIMIT)


def _silu(x):
    return x * jax.nn.sigmoid(x)


def _dot(a, b):
    return jnp.dot(a, b, preferred_element_type=F32)


def _dot_nt(a, b):
    return lax.dot_general(a, b, (((1,), (1,)), ((), ())), preferred_element_type=F32)


def _dot_tn(a, b):
    return lax.dot_general(a, b, (((0,), (0,)), ((), ())), preferred_element_type=F32)


def _lane_iota(shape):
    return lax.broadcasted_iota(jnp.int32, shape, 1)


def _mod_kernel(cond_ref, w_ref, b_ref, o_ref):
    s = _silu(cond_ref[...]).astype(BF16)
    o_ref[0] = _dot(s, w_ref[0].astype(BF16)) + b_ref[0]


def _modulation(cond, w_ada, b_ada):
    tn = 1536
    n = 6 * D_MODEL
    return pl.pallas_call(
        _mod_kernel,
        grid=(DEPTH, n // tn),
        in_specs=[pl.BlockSpec((8, D_MODEL), lambda l, j: (0, 0)),
                  pl.BlockSpec((1, D_MODEL, tn), lambda l, j: (l, 0, j)),
                  pl.BlockSpec((1, 1, tn), lambda l, j: (l, 0, j))],
        out_specs=pl.BlockSpec((1, 8, tn), lambda l, j: (l, 0, j)),
        out_shape=jax.ShapeDtypeStruct((DEPTH, 8, n), F32),
        compiler_params=_cparams(2),
        name="adaln_mod",
    )(cond, w_ada, b_ada.reshape(DEPTH, 1, n))


def _group_rms(v, gmat_ref, group):
    ss = _dot((v * v).astype(BF16), gmat_ref[...])
    return v * lax.rsqrt(ss * (1.0 / group) + EPS)


def _rope(v, cos, sin_signed):
    first = (_lane_iota(v.shape) % 16) < 8
    partner = jnp.where(first, pltpu.roll(v, v.shape[1] - 8, 1), pltpu.roll(v, 8, 1))
    return v * cos + partner * sin_signed


def _inproj_kernel(*refs, rope, cache_seq, cache_first):
    (x_ref, mod_ref, g1_ref, w_ref, qn_ref, kn_ref, dqn_ref, dkn_ref, gb_ref, g64_ref, g32_ref) = refs[:11]
    pos = 11
    if rope:
        cos_ref, sin_ref = refs[pos:pos + 2]
        pos += 2
    if cache_seq and not cache_first:
        pos += 4
    (naq_ref, nak_ref, nav_ref, dfq_ref, dfk_ref, dfv_ref,
     mq_ref, mk_ref, mv_ref, mo_ref, mg_ref) = refs[pos:pos + 11]
    pos += 11
    x = x_ref[...]
    y = x * lax.rsqrt(jnp.mean(x * x, axis=-1, keepdims=True) + EPS) * g1_ref[...]
    h = (y * (1.0 + mod_ref[0, 1:2, :]) + mod_ref[0, 0:1, :]).astype(BF16)

    def proj(a, b):
        return _dot(h, w_ref[:, a:b])

    naq_ref[...] = _group_rms(proj(0, 256), g64_ref, NA_DH) * qn_ref[...]
    nak = _group_rms(proj(256, 512), g64_ref, NA_DH) * kn_ref[...]
    nav = proj(512, 768)
    dfq = _group_rms(proj(768, 1024), g32_ref, DF_DQK) * dqn_ref[...]
    dfk = _group_rms(proj(1024, 1280), g32_ref, DF_DQK) * dkn_ref[...]
    dfv = proj(1280, 1536)
    if rope:
        dfq = _rope(dfq, cos_ref[...], sin_ref[...])
        dfk = _rope(dfk, cos_ref[...], sin_ref[...])
    nak_ref[...] = nak
    nav_ref[...] = nav
    dfq_ref[...] = dfq
    dfk_ref[...] = dfk
    dfv_ref[...] = dfv
    if cache_seq:
        nk_hm, nv_hm, dk_hm, dv_hm = refs[pos:pos + 4]

        def put(ref, bi, idx, val):
            if cache_first:
                for dl in range(DEPTH):
                    ref[(bi, dl) + idx] = val
            else:
                ref[(bi,) + idx] = val

        for bi in range(x.shape[0] // cache_seq):
            rs = slice(bi * cache_seq, (bi + 1) * cache_seq)
            for hd in range(NA_HEADS):
                put(nk_hm, bi, (hd,), nak[rs, NA_DH * hd:NA_DH * (hd + 1)])
                put(nv_hm, bi, (hd,), nav[rs, NA_DH * hd:NA_DH * (hd + 1)])
                put(dv_hm, bi, (hd,), dfv[rs, DF_DV * hd:DF_DV * (hd + 1)])
                for a in range(2):
                    c0 = (2 * hd + a) * DF_DQK
                    put(dk_hm, bi, (hd, a), dfk[rs, c0:c0 + DF_DQK])
    mq_ref[...] = proj(1536, 2048)
    mk_ref[...] = proj(2048, 2560) * (ML_DH ** -0.5)
    mv_ref[...] = proj(2560, 3072)
    mo_ref[...] = proj(3072, 3584)
    g = proj(GATE_COL0, IN_COLS_PAD) + gb_ref[...]
    lane = _lane_iota(g.shape)
    is_forget = ((lane // ML_HEADS) % 2 == 1) & (lane < 4 * ML_HEADS)
    log_sig = jnp.minimum(g, 0.0) - jnp.log1p(jnp.exp(-jnp.abs(g)))
    mg_ref[...] = jnp.where(is_forget, log_sig, g)


def _inproj(x, mod_p, tokens_per_mod, layer, g1, w_in_bf, qn, kn, dqn, dkn, gate_b, g64, g32, rope_tabs,
            cache_seq=0, cache_prev=None):
    T = x.shape[0]
    tm = TM_PROJ
    rope = rope_tabs is not None
    row = lambda i: (i, 0)
    const = lambda i: (0, 0)
    in_specs = [pl.BlockSpec((tm, D_MODEL), row),
                pl.BlockSpec((1, 6, D_MODEL), lambda i: ((i * tm) // tokens_per_mod, 0, 0)),
                pl.BlockSpec((1, D_MODEL), const),
                pl.BlockSpec((None, D_MODEL, IN_COLS_PAD), lambda i: (layer, 0, 0)),
                pl.BlockSpec((1, 256), const), pl.BlockSpec((1, 256), const),
                pl.BlockSpec((1, 256), const), pl.BlockSpec((1, 256), const),
                pl.BlockSpec((1, LANES), const),
                pl.BlockSpec((256, 256), const), pl.BlockSpec((256, 256), const)]
    args = [x, mod_p, g1, w_in_bf, qn, kn, dqn, dkn, gate_b, g64, g32]
    if rope:
        seq = rope_tabs[0].shape[0]
        tab = lambda i: (i % (seq // tm), 0)
        in_specs += [pl.BlockSpec((tm, 256), tab), pl.BlockSpec((tm, 256), tab)]
        args += list(rope_tabs)
    widths = [256] * 6 + [512] * 4 + [LANES]
    out_specs = [pl.BlockSpec((tm, w), row) for w in widths]
    out_shape = [jax.ShapeDtypeStruct((T, w), F32) for w in widths]
    aliases = {}
    if cache_seq:
        nb = tm // cache_seq
        hm = (NA_HEADS, cache_seq, NA_DH)
        hm_df = (DF_HEADS, 2, cache_seq, DF_DQK)
        for k, tail in enumerate((hm, hm, hm_df, hm)):
            zeros = (0,) * len(tail)
            if cache_prev is None:
                out_specs.append(pl.BlockSpec((nb, DEPTH) + tail, lambda i, z=zeros: (i, 0) + z))
            else:
                out_specs.append(pl.BlockSpec((nb, None) + tail, lambda i, z=zeros: (i, layer) + z))
                aliases[len(args)] = len(widths) + k
                in_specs.append(pl.BlockSpec(memory_space=pl.ANY))
                args.append(cache_prev[k])
            out_shape.append(jax.ShapeDtypeStruct((T // cache_seq, DEPTH) + tail, F32))
    return pl.pallas_call(
        functools.partial(_inproj_kernel, rope=rope, cache_seq=cache_seq, cache_first=cache_prev is None),
        grid=(T // tm,),
        in_specs=in_specs,
        out_specs=out_specs,
        out_shape=out_shape,
        input_output_aliases=aliases,
        compiler_params=_cparams(1),
        name="inproj_rope" if rope else "inproj",
    )(*args)


def _df_lambda(lam_ref, lam_init):
    v = lam_ref[...]
    a = jnp.sum(v[0:1] * v[1:2], axis=1, keepdims=True)
    b = jnp.sum(v[2:3] * v[3:4], axis=1, keepdims=True)
    return jnp.exp(a) - jnp.exp(b) + lam_init


def _softmax_parts(scores):
    es = _softmax_exps(scores)
    tot = functools.reduce(lambda a, b: a + b, [jnp.sum(e, axis=-1, keepdims=True) for e in es])
    return [e.astype(BF16) for e in es], 1.0 / tot


def _softmax_exps(scores):
    m = functools.reduce(jnp.maximum, [jnp.max(s, axis=-1, keepdims=True) for s in scores])
    return [jnp.exp2(s - m) for s in scores]


def _attend_masked_heads(tiles, lead=1):
    lane = _lane_iota(tiles[0][0].shape)
    v_aug = [[jnp.concatenate([v, jnp.ones(v.shape, BF16)], axis=1) for v in v_segs] for _, _, v_segs, _ in tiles]
    n_heads = 2 * len(tiles)

    def scores(c):
        qp, k_segs, _, bias = tiles[c // 2]
        qm = jnp.where(lane // NA_DH == c % 2, qp, 0.0).astype(BF16)
        s = [_dot_nt(qm, k) for k in k_segs]
        if bias is not None:
            s[0] = s[0] + bias[c % 2]
        return s

    outs = []
    pending = [scores(c) for c in range(min(lead, n_heads))]
    for c in range(n_heads):
        if c + lead < n_heads:
            pending.append(scores(c + lead))
        es = _softmax_exps(pending.pop(0))
        o = functools.reduce(lambda x, y: x + y, [_dot(e.astype(BF16), v) for e, v in zip(es, v_aug[c // 2])])
        outs.append(o[:, :LANES] / o[:, LANES:])
    return [jnp.where(lane < NA_DH, outs[2 * t], outs[2 * t + 1]) for t in range(len(tiles))]


def _subln(o, subln_row, lam_init):
    lane = _lane_iota(o.shape)
    sq = o * o
    s0 = jnp.sum(jnp.where(lane < DF_DV, sq, 0.0), axis=-1, keepdims=True)
    s1 = jnp.sum(jnp.where(lane >= DF_DV, sq, 0.0), axis=-1, keepdims=True)
    ms = jnp.where(lane < DF_DV, s0, s1) * (1.0 / DF_DV)
    return o * lax.rsqrt(ms + EPS) * subln_row * (1.0 - lam_init)


def _diff_attn(q, k_segs, v_segs, lam, lead):
    n_maps = 2 * DF_HEADS
    per_tile = LANES // DF_DQK
    lane = _lane_iota((q.shape[0], LANES))
    tiles = []
    for t in range(n_maps // per_tile):
        sl = slice(LANES * t, LANES * (t + 1))
        tiles.append((q[:, sl] * (DF_DQK ** -0.5 * LOG2E),
                      [k[:, sl].astype(BF16) for k in k_segs], [v[:, sl].astype(BF16) for v in v_segs]))

    def scores(c):
        qp, kps, _ = tiles[c // per_tile]
        qm = jnp.where(lane // DF_DQK == c % per_tile, qp, 0.0).astype(BF16)
        return [_dot_nt(qm, kp) for kp in kps]

    def head_out(m0, m1, vps):
        (e0, r0), (e1, r1) = m0, m1
        c0 = r0.astype(BF16)
        c1 = (-(lam * r1)).astype(BF16)
        return functools.reduce(lambda x, y: x + y,
                                [_dot(x0 * c0 + x1 * c1, vp) for x0, x1, vp in zip(e0, e1, vps)])

    heads = []
    maps = []
    pending = [scores(c) for c in range(lead)]
    for c in range(n_maps):
        if c + lead < n_maps:
            pending.append(scores(c + lead))
        maps.append(_softmax_parts(pending.pop(0)))
        if c % 2 == 1:
            heads.append(head_out(maps[c - 1], maps[c], tiles[c // per_tile][2]))
    return [jnp.where(lane < DF_DV, heads[2 * t], heads[2 * t + 1]) for t in range(len(tiles))]


def _ctx_attn_kernel(naq_ref, nak_ref, nav_ref, dfq_ref, dfk_ref, dfv_ref, lam_ref, subln_ref,
                     nao_ref, dfo_ref, *, lam_init, seq):
    lam = _df_lambda(lam_ref, lam_init)
    for bi in range(naq_ref.shape[0] // seq):
        rs = slice(bi * seq, (bi + 1) * seq)
        tiles = []
        for t in range(2):
            sl = slice(LANES * t, LANES * (t + 1))
            tiles.append((naq_ref[rs, sl] * (NA_DH ** -0.5 * LOG2E), [nak_ref[rs, sl].astype(BF16)],
                          [nav_ref[rs, sl].astype(BF16)], None))
        for t, o in enumerate(_attend_masked_heads(tiles, lead=0)):
            nao_ref[rs, LANES * t:LANES * (t + 1)] = o
        outs = _diff_attn(dfq_ref[rs, :], [dfk_ref[rs, :]], [dfv_ref[rs, :]], lam, DF_SCORE_LEAD_CTX)
        for t, o in enumerate(outs):
            sl = slice(LANES * t, LANES * (t + 1))
            dfo_ref[rs, sl] = _subln(o, subln_ref[:, sl], lam_init)


def _ctx_attention(naq, nak, nav, dfq, dfk, dfv, df_lam, subln, seq, lam_init):
    T = naq.shape[0]
    rows = CTX_ATTN_BATCHES * seq
    blk = pl.BlockSpec((rows, 256), lambda b: (b, 0))
    return pl.pallas_call(
        functools.partial(_ctx_attn_kernel, lam_init=lam_init, seq=seq),
        grid=(T // rows,),
        in_specs=[blk] * 6 + [pl.BlockSpec((4, DF_DQK), lambda b: (0, 0)),
                              pl.BlockSpec((1, 256), lambda b: (0, 0))],
        out_specs=[blk, blk],
        out_shape=[jax.ShapeDtypeStruct((T, 256), F32)] * 2,
        compiler_params=_cparams(1),
        name="ctx_attention",
    )(naq, nak, nav, dfq, dfk, dfv, df_lam, subln)


def _na_bias_kernel(rpb_ref, o_ref):
    hd = pl.program_id(0)
    shape = (GRID_W, GRID_W)
    qi = lax.broadcasted_iota(jnp.int32, shape, 0)
    kc = lax.broadcasted_iota(jnp.int32, shape, 1)
    cs = jnp.clip(qi - NA_WIN_C // 2, 0, GRID_W - NA_WIN_C)
    inwin = (kc >= cs) & (kc < cs + NA_WIN_C)
    off = jnp.where(inwin, kc - qi + NA_WIN_C - 1, -1)
    n_c = 2 * NA_WIN_C - 1
    n_r = 2 * NA_WIN_R - 1
    for ro in range(n_r):
        val = jnp.full(shape, MASK_VALUE, F32)
        for d in range(n_c):
            val = jnp.where(off == d, rpb_ref[hd * (n_r * n_c) + ro * n_c + d], val)
        o_ref[0, ro] = val * LOG2E


def _na_bias_blocks(rpb):
    n_r = 2 * NA_WIN_R - 1
    n_lh = rpb.shape[0] * rpb.shape[1]
    return pl.pallas_call(
        _na_bias_kernel,
        grid=(n_lh,),
        in_specs=[pl.BlockSpec(memory_space=pltpu.SMEM)],
        out_specs=pl.BlockSpec((1, n_r, GRID_W, GRID_W), lambda h: (h, 0, 0, 0)),
        out_shape=jax.ShapeDtypeStruct((n_lh, n_r, GRID_W, GRID_W), F32),
        compiler_params=_cparams(1),
        name="na_bias",
    )(rpb.reshape(-1))


def _na_block_geometry(rows):
    last_r0 = rows - NA_Q_ROWS
    return ((0, 0), (NA_Q_ROWS, 0), (last_r0, rows - NA_K_ROWS))


def _na_bias_tables(blocks, rows):
    masked = jnp.full(blocks[:, 0].shape, MASK_VALUE * LOG2E, F32)
    tabs = []
    for r0, k0 in _na_block_geometry(rows):
        q_rows = []
        for rq in range(NA_Q_ROWS):
            r = r0 + rq
            rs = min(max(r - NA_WIN_R // 2, 0), rows - NA_WIN_R)
            cols = []
            for kr in range(k0, k0 + NA_K_ROWS):
                cols.append(blocks[:, kr - r + NA_WIN_R - 1] if rs <= kr < rs + NA_WIN_R else masked)
            q_rows.append(jnp.concatenate(cols, axis=-1))
        tabs.append(jnp.concatenate(q_rows, axis=-2))
    return jnp.stack(tabs, axis=0)


def _lat_na_kernel(q_ref, k_ref, v_ref, kc_ref, vc_ref, bias_ref, o_ref, *, rows):
    i = pl.program_id(1)
    k0 = jnp.clip(i * NA_Q_ROWS - NA_WIN_R // 2, 0, rows - NA_K_ROWS)
    krows = pl.ds(pl.multiple_of(k0 * GRID_W, GRID_W), NA_K_ROWS * GRID_W)
    q = q_ref[...] * (NA_DH ** -0.5 * LOG2E)
    tiles = []
    for t in range(2):
        sl = slice(LANES * t, LANES * (t + 1))
        tiles.append((q[:, sl],
                      [k_ref[krows, sl].astype(BF16), kc_ref[0, :, sl].astype(BF16)],
                      [v_ref[krows, sl].astype(BF16), vc_ref[0, :, sl].astype(BF16)],
                      [bias_ref[2 * t], bias_ref[2 * t + 1]]))
    for t, o in enumerate(_attend_masked_heads(tiles)):
        o_ref[:, LANES * t:LANES * (t + 1)] = o


def _lat_na(q, k, v, kc, vc, bias_tabs, seq):
    T = q.shape[0]
    past = kc.shape[1]
    rows = seq // GRID_W
    nq = rows // NA_Q_ROWS
    qrows = NA_Q_ROWS * GRID_W
    qblk = pl.BlockSpec((qrows, 256), lambda b, i: (b * nq + i, 0))
    kblk = pl.BlockSpec((seq, 256), lambda b, i: (b, 0))
    cblk = pl.BlockSpec((1, past, 256), lambda b, i: (b, 0, 0))
    variant = lambda b, i: (jnp.where(i == 0, 0, jnp.where(i == nq - 1, 2, 1)), 0, 0, 0)
    return pl.pallas_call(
        functools.partial(_lat_na_kernel, rows=rows),
        grid=(T // seq, nq),
        in_specs=[qblk, kblk, kblk, cblk, cblk,
                  pl.BlockSpec((None, NA_HEADS, qrows, NA_K_ROWS * GRID_W), variant)],
        out_specs=qblk,
        out_shape=jax.ShapeDtypeStruct((T, 256), F32),
        compiler_params=_cparams(2),
        name="lat_na",
    )(q, k, v, kc, vc, bias_tabs)


def _lat_df_kernel(q_ref, k_ref, v_ref, kc_ref, vc_ref, lam_ref, subln_ref, o_ref, *, lam_init):
    lam = _df_lambda(lam_ref, lam_init)
    outs = _diff_attn(q_ref[...], [k_ref[...], kc_ref[0]], [v_ref[...], vc_ref[0]], lam, DF_SCORE_LEAD_LAT)
    for t, o in enumerate(outs):
        sl = slice(LANES * t, LANES * (t + 1))
        o_ref[:, sl] = _subln(o, subln_ref[:, sl], lam_init)


def _lat_df(q, k, v, kc, vc, df_lam, subln, seq, lam_init):
    T = q.shape[0]
    past = kc.shape[1]
    nq = seq // Q_BLOCK_DF
    qblk = pl.BlockSpec((Q_BLOCK_DF, 256), lambda b, i: (b * nq + i, 0))
    kblk = pl.BlockSpec((seq, 256), lambda b, i: (b, 0))
    cblk = pl.BlockSpec((1, past, 256), lambda b, i: (b, 0, 0))
    return pl.pallas_call(
        functools.partial(_lat_df_kernel, lam_init=lam_init),
        grid=(T // seq, nq),
        in_specs=[qblk, kblk, kblk, cblk, cblk,
                  pl.BlockSpec((4, DF_DQK), lambda b, i: (0, 0)),
                  pl.BlockSpec((1, 256), lambda b, i: (0, 0))],
        out_specs=qblk,
        out_shape=jax.ShapeDtypeStruct((T, 256), F32),
        compiler_params=_cparams(2),
        name="lat_df",
    )(q, k, v, kc, vc, df_lam, subln)


def _mlstm_kernel(*refs, nc, has_init, emit_state, state_first):
    refs = list(refs)
    q_ref, k_ref, v_ref, g_ref = refs[:4]
    pos = 4
    if has_init:
        c0_ref, m0_ref = refs[pos:pos + 2]
        pos += 2
    if emit_state and not state_first:
        pos += 2
    h_ref = refs[pos]
    pos += 1
    if emit_state:
        cn_ref, nn_ref, mn_ref = refs[pos:pos + 3]
        pos += 3
    caug_sc, m_sc = refs[pos:pos + 2]

    if has_init:
        caug_sc[...] = c0_ref[0]
        m_sc[...] = m0_ref[0]
    else:
        caug_sc[...] = jnp.zeros(caug_sc.shape, F32)
        m_sc[...] = jnp.zeros(m_sc.shape, F32)
    static_chunks = nc <= LOOP_UNROLL
    if not static_chunks:
        h_ref[...] = jnp.zeros(h_ref.shape, F32)
    seen = set()

    ti = lax.broadcasted_iota(jnp.int32, (ML_CHUNK, ML_CHUNK), 0)
    si = lax.broadcasted_iota(jnp.int32, (ML_CHUNK, ML_CHUNK), 1)
    masks = (si <= ti, si >= ti)
    ones_blk = jnp.ones((ML_CHUNK, LANES), BF16)
    full = (ML_CHUNK, ML_CHUNK)

    def body(c, carry):
        for d in range(2):
            ci = c if d == 0 else nc - 1 - c
            if static_chunks:
                rows = pl.ds(ci * ML_CHUNK, ML_CHUNK)
                accumulate = ci in seen
                seen.add(ci)
            else:
                rows = pl.ds(pl.multiple_of(ci * ML_CHUNK, ML_CHUNK), ML_CHUNK)
                accumulate = True
            mask = masks[d]
            tri = jnp.where(mask, 1.0, 0.0).astype(BF16)
            gblk = g_ref[rows, :]
            hi = gblk.astype(BF16)
            lo = (gblk - hi.astype(F32)).astype(BF16)
            bc = _dot(tri, hi) + _dot(tri, lo)
            g_t = gblk.T
            bc_t = bc.T
            last = ML_CHUNK - 1 if d == 0 else 0
            qk = [_dot_nt(q_ref[rows, ML_DH * hd:ML_DH * (hd + 1)].astype(BF16),
                          k_ref[rows, ML_DH * hd:ML_DH * (hd + 1)].astype(BF16)) for hd in range(ML_HEADS)]
            for hd in range(ML_HEADS):
                ch_i = 2 * ML_HEADS * d + hd
                ch_f = ch_i + ML_HEADS
                idx = ML_HEADS * d + hd
                hs = slice(ML_DH * hd, ML_DH * (hd + 1))
                b_col = bc[:, ch_f:ch_f + 1]
                a_col = gblk[:, ch_i:ch_i + 1] - b_col
                a_msk = jnp.where(mask, g_t[ch_i:ch_i + 1, :] - bc_t[ch_f:ch_f + 1, :], -jnp.inf)
                m_col = jnp.max(a_msk, axis=1, keepdims=True)
                m_rep = jnp.broadcast_to(m_col, full)
                b_rep = jnp.broadcast_to(b_col, full)
                a_rep = jnp.broadcast_to(a_col, full)
                m_last = m_col[last:last + 1, :]
                b_last = b_col[last:last + 1, :]
                qh = q_ref[rows, hs].astype(BF16)
                kf = k_ref[rows, hs]
                vaug = jnp.concatenate([v_ref[rows, hs].astype(BF16), ones_blk], axis=1)
                s0 = qk[hd] * jnp.exp(a_msk - m_rep)
                sv0 = _dot(s0.astype(BF16), vaug)
                u0 = _dot_tn((jnp.exp(a_rep - m_last) * kf).astype(BF16), vaug)
                m_prev = m_sc[idx][0:1, 0:1]
                g_rep = jnp.maximum(m_prev, m_rep)
                w0 = jnp.exp(m_prev - g_rep)
                cc = jnp.exp(m_rep - g_rep)
                caug = caug_sc[idx]
                qc = _dot(qh, caug.astype(BF16))
                num = w0 * qc[:, :ML_DH] + cc * sv0[:, :ML_DH]
                den = w0 * qc[:, ML_DH:] + cc * sv0[:, ML_DH:]
                h_dir = num / jnp.maximum(jnp.abs(den), jnp.exp(-b_rep - g_rep))
                if accumulate:
                    h_ref[rows, hs] += h_dir
                else:
                    h_ref[rows, hs] = h_dir
                g_last = jnp.maximum(m_prev, m_last)
                caug_sc[idx] = jnp.exp(m_prev - g_last) * caug + jnp.exp(m_last - g_last) * u0
                m_sc[idx] = jnp.broadcast_to(b_last + g_last, m_sc.shape[1:])
        return carry

    if static_chunks:
        for c in range(nc):
            body(c, 0)
    else:
        lax.fori_loop(0, nc, body, 0, unroll=LOOP_UNROLL)
    if emit_state:
        slots = [(0, dl) for dl in range(DEPTH)] if state_first else [(0,)]
        for slot in slots:
            cn_ref[slot] = caug_sc[:, :, :ML_DH]
        for idx in range(2 * ML_HEADS):
            n_row = caug_sc[idx][:, ML_DH:].T[0:1, :]
            for slot in slots:
                nn_ref[slot + (slice(idx, idx + 1), slice(None))] = n_row
        mn_ref[0] = m_sc[...]


def _mlstm(q, k, v, g, seq, init=None, emit_state=False, layer=0, state_prev=None):
    T = q.shape[0]
    n_chain = 2 * ML_HEADS
    blk = pl.BlockSpec((seq, 512), lambda b: (b, 0))
    gblk = pl.BlockSpec((seq, LANES), lambda b: (b, 0))
    cblk = pl.BlockSpec((1, n_chain, ML_DH, 2 * ML_DH), lambda b: (b, 0, 0, 0))
    mblk = pl.BlockSpec((1, n_chain, 8, LANES), lambda b: (b, 0, 0, 0))
    in_specs = [blk, blk, blk, gblk]
    args = [q, k, v, g]
    if init is not None:
        in_specs += [cblk, mblk]
        args += list(init)
    out_specs = [blk]
    out_shape = [jax.ShapeDtypeStruct((T, 512), F32)]
    nb = T // seq
    aliases = {}
    if emit_state:
        if state_prev is None:
            out_specs += [pl.BlockSpec((1, DEPTH, n_chain, ML_DH, ML_DH), lambda b: (b, 0, 0, 0, 0)),
                          pl.BlockSpec((1, DEPTH, n_chain, ML_DH), lambda b: (b, 0, 0, 0)), mblk]
        else:
            out_specs += [pl.BlockSpec((1, None, n_chain, ML_DH, ML_DH), lambda b: (b, layer, 0, 0, 0)),
                          pl.BlockSpec((1, None, n_chain, ML_DH), lambda b: (b, layer, 0, 0)), mblk]
            aliases = {len(args): 1, len(args) + 1: 2}
            in_specs += [pl.BlockSpec(memory_space=pl.ANY)] * 2
            args += list(state_prev)
        out_shape += [jax.ShapeDtypeStruct((nb, DEPTH, n_chain, ML_DH, ML_DH), F32),
                      jax.ShapeDtypeStruct((nb, DEPTH, n_chain, ML_DH), F32),
                      jax.ShapeDtypeStruct((nb, n_chain, 8, LANES), F32)]
    return pl.pallas_call(
        functools.partial(_mlstm_kernel, nc=seq // ML_CHUNK, has_init=init is not None,
                          emit_state=emit_state, state_first=state_prev is None),
        grid=(nb,),
        in_specs=in_specs,
        out_specs=out_specs,
        out_shape=out_shape,
        input_output_aliases=aliases,
        scratch_shapes=[pltpu.VMEM((n_chain, ML_DH, 2 * ML_DH), F32),
                        pltpu.VMEM((n_chain, 8, LANES), F32)],
        compiler_params=_cparams(1),
        name="mlstm_state" if emit_state else "mlstm",
    )(*args)


def _outproj_kernel(x_ref, nao_ref, dfo_ref, mlh_ref, mo_ref, mod_ref, g2_ref, outn_ref, wout_ref,
                    wr_ref, br_ref, x1_ref, h2_ref, gates_ref):
    mlh = mlh_ref[...]
    parts = []
    for hd in range(ML_HEADS):
        seg = mlh[:, ML_DH * hd:ML_DH * (hd + 1)]
        parts.append(seg * lax.rsqrt(jnp.mean(seg * seg, axis=-1, keepdims=True) + EPS))
    ml_o = jnp.concatenate(parts, axis=1) * outn_ref[...] * jax.nn.sigmoid(mo_ref[...])
    mix = (_dot(nao_ref[...].astype(BF16), wout_ref[0:256, :])
           + _dot(dfo_ref[...].astype(BF16), wout_ref[256:512, :])
           + _dot(ml_o.astype(BF16), wout_ref[512:1024, :]))
    x1 = x_ref[...] + mod_ref[0, 2:3, :] * mix
    x1_ref[...] = x1
    y = x1 * lax.rsqrt(jnp.mean(x1 * x1, axis=-1, keepdims=True) + EPS) * g2_ref[...]
    h2 = y * (1.0 + mod_ref[0, 4:5, :]) + mod_ref[0, 3:4, :]
    h2_ref[...] = h2.astype(BF16)

    wr = wr_ref[...]
    h_hi = h2.astype(BF16)
    h_lo = (h2 - h_hi.astype(F32)).astype(BF16)
    w_hi = wr.astype(BF16)
    w_lo = (wr - w_hi.astype(F32)).astype(BF16)
    logits = _dot(h_hi, w_hi) + _dot(h_lo, w_hi) + _dot(h_hi, w_lo)
    s = jax.nn.sigmoid(logits)
    work = s + br_ref[...]
    lane = _lane_iota(work.shape)
    s_sel = jnp.zeros(work.shape, F32)
    for _ in range(TOP_K):
        mx = jnp.max(work, axis=-1, keepdims=True)
        first = jnp.min(jnp.where(work == mx, lane, N_EXPERTS), axis=-1, keepdims=True)
        hit = lane == first
        s_sel = jnp.where(hit, s, s_sel)
        work = jnp.where(hit, -jnp.inf, work)
    gates_ref[...] = s_sel / jnp.sum(s_sel, axis=-1, keepdims=True) * ROUTED_SCALE


def _outproj(x, nao, dfo, mlh, mo, mod_p, tokens_per_mod, layer, g2, outn, w_out_bf, w_router, b_router):
    T = x.shape[0]
    tm = TM_PROJ
    row = lambda i: (i, 0)
    const = lambda i: (0, 0)
    lay = lambda i: (layer, 0, 0)
    return pl.pallas_call(
        _outproj_kernel,
        grid=(T // tm,),
        in_specs=[pl.BlockSpec((tm, D_MODEL), row),
                  pl.BlockSpec((tm, 256), row), pl.BlockSpec((tm, 256), row),
                  pl.BlockSpec((tm, 512), row), pl.BlockSpec((tm, 512), row),
                  pl.BlockSpec((1, 6, D_MODEL), lambda i: ((i * tm) // tokens_per_mod, 0, 0)),
                  pl.BlockSpec((1, D_MODEL), const), pl.BlockSpec((1, 512), const),
                  pl.BlockSpec((None, D_MODEL, D_MODEL), lay),
                  pl.BlockSpec((None, D_MODEL, N_EXPERTS), lay), pl.BlockSpec((1, N_EXPERTS), const)],
        out_specs=[pl.BlockSpec((tm, D_MODEL), row), pl.BlockSpec((tm, D_MODEL), row),
                   pl.BlockSpec((tm, N_EXPERTS), row)],
        out_shape=[jax.ShapeDtypeStruct((T, D_MODEL), F32), jax.ShapeDtypeStruct((T, D_MODEL), BF16),
                   jax.ShapeDtypeStruct((T, N_EXPERTS), F32)],
        compiler_params=_cparams(1),
        name="outproj_router",
    )(x, nao, dfo, mlh, mo, mod_p, g2, outn, w_out_bf, w_router, b_router)


def _dispatch_kernel(h_ref, gates_ref, xs_ref, cnt_ref, route_ref, *, n_tiles):
    i = pl.program_id(0)

    @pl.when(i == n_tiles)
    def _():
        xs_ref[...] = jnp.zeros(xs_ref.shape, BF16)

    @pl.when(i < n_tiles)
    def _():
        _dispatch_tile(h_ref, gates_ref, xs_ref, cnt_ref, route_ref)


def _dispatch_tile(h_ref, gates_ref, xs_ref, cnt_ref, route_ref):
    tt = h_ref.shape[0]
    n_slots = xs_ref.shape[0]
    gates = gates_ref[...]
    sel = jnp.where(gates > 0.0, 1.0, 0.0)
    selb = sel.astype(BF16)
    ti = lax.broadcasted_iota(jnp.int32, (tt, tt), 0)
    si = lax.broadcasted_iota(jnp.int32, (tt, tt), 1)
    earlier = jnp.where(si < ti, 1.0, 0.0).astype(BF16)
    pos = _dot(earlier, selb)
    cnt = jnp.sum(sel, axis=0, keepdims=True)
    seg = jnp.floor((cnt + (MOE_GRANULE - 1.0)) * (1.0 / MOE_GRANULE)) * MOE_GRANULE
    ei = lax.broadcasted_iota(jnp.int32, (N_EXPERTS, N_EXPERTS), 0)
    ej = lax.broadcasted_iota(jnp.int32, (N_EXPERTS, N_EXPERTS), 1)
    before = jnp.where(ei < ej, 1.0, 0.0).astype(BF16)
    off = _dot(jnp.broadcast_to(seg, (16, N_EXPERTS)).astype(BF16), before)[0:1]
    rank = _dot(selb, before)
    slot = off + pos
    lane = _lane_iota((tt, LANES))
    route = jnp.zeros((tt, LANES), F32)
    for k in range(TOP_K):
        hit = jnp.where(rank == k, sel, 0.0)
        has = jnp.sum(hit, axis=1, keepdims=True)
        slot_k = jnp.sum(hit * slot, axis=1, keepdims=True) + has - 1.0
        w_k = jnp.sum(hit * gates, axis=1, keepdims=True)
        route = jnp.where(lane == k, slot_k, jnp.where(lane == TOP_K + k, w_k, route))
    route_ref[...] = route
    cnt_ref[0] = cnt
    slots_t = route.T
    srow = lax.broadcasted_iota(jnp.int32, (n_slots, tt), 0).astype(F32)
    perm = jnp.zeros((n_slots, tt), F32)
    for k in range(TOP_K):
        perm = jnp.where(srow == slots_t[k:k + 1, :], 1.0, perm)
    xs_ref[...] = _dot(perm.astype(BF16), h_ref[...]).astype(BF16)


def _dispatch(h2, gates):
    T = h2.shape[0]
    nt = T // MOE_TILE
    tile = lambda i: (jnp.minimum(i, nt - 1), 0)
    return pl.pallas_call(
        functools.partial(_dispatch_kernel, n_tiles=nt),
        grid=(nt + 1,),
        in_specs=[pl.BlockSpec((MOE_TILE, D_MODEL), tile),
                  pl.BlockSpec((MOE_TILE, N_EXPERTS), tile)],
        out_specs=[pl.BlockSpec((MOE_SLOTS, D_MODEL), lambda i: (i, 0)),
                   pl.BlockSpec((1, 1, N_EXPERTS), lambda i: (jnp.minimum(i, nt - 1), 0, 0)),
                   pl.BlockSpec((MOE_TILE, LANES), tile)],
        out_shape=[jax.ShapeDtypeStruct(((nt + 1) * MOE_SLOTS, D_MODEL), BF16),
                   jax.ShapeDtypeStruct((nt, 1, N_EXPERTS), F32),
                   jax.ShapeDtypeStruct((T, LANES), F32)],
        compiler_params=_cparams(1),
        name="moe_dispatch",
    )(h2, gates)


def _expert_fused_kernel(te_ref, nact_ref, src_ref, dst_ref, slots_in, w1_ref, w3_ref, w2_ref,
                         slots_hbm, xbuf, ybuf, sem_in, sem_out, *, n_tiles):
    del slots_in
    xs_hbm = ys_hbm = slots_hbm
    i = pl.program_id(0)
    n_act = nact_ref[0]
    gr = MOE_ROWS // MOE_GRANULE
    spare0 = ys_hbm.shape[0] // MOE_GRANULE - MOE_SLOTS // MOE_GRANULE

    def rows(g):
        if isinstance(g, int):
            return pl.ds(g * MOE_GRANULE, MOE_GRANULE)
        return pl.ds(pl.multiple_of(g * MOE_GRANULE, MOE_GRANULE), MOE_GRANULE)

    zero_granule = MOE_SLOTS // MOE_GRANULE - 1

    def gather_start(tile, slot, dummy=False):
        for g in range(gr):
            s = src_ref[tile * gr + g]
            if dummy is not False:
                s = jnp.where(dummy, zero_granule, s)
            pltpu.make_async_copy(xs_hbm.at[rows(s)], xbuf.at[slot, rows(g)], sem_in.at[slot]).start()

    def gather_wait(slot):
        for g in range(gr):
            pltpu.make_async_copy(xs_hbm.at[rows(0)], xbuf.at[slot, rows(g)], sem_in.at[slot]).wait()

    def scatter_start(tile, slot):
        for g in range(gr):
            dd = dst_ref[tile * gr + g]
            dd = jnp.where(dd < 0, spare0 + slot * gr + g, dd)
            pltpu.make_async_copy(ybuf.at[slot, rows(g)], ys_hbm.at[rows(dd)], sem_out.at[slot]).start()

    def scatter_wait(slot):
        for g in range(gr):
            pltpu.make_async_copy(ybuf.at[slot, rows(g)], ys_hbm.at[rows(0)], sem_out.at[slot]).wait()

    slot = lax.rem(i, 2)

    @pl.when((i == 0) & (n_act > 0))
    def _():
        gather_start(0, 0)

    @pl.when(i < n_act)
    def _():
        gather_start(jnp.minimum(i + 1, n_act - 1), 1 - slot, dummy=i + 1 >= n_act)
        gather_wait(slot)

        @pl.when(i >= 2)
        def _():
            scatter_wait(slot)

        x = xbuf[slot]
        a = _dot(x, w1_ref[...].astype(BF16))
        b = _dot(x, w3_ref[...].astype(BF16))
        ybuf[slot] = _dot((_silu(a) * b).astype(BF16), w2_ref[...].astype(BF16)).astype(BF16)
        scatter_start(i, slot)

    @pl.when((i == n_tiles - 1) & (n_act > 0))
    def _():
        last_slot = lax.rem(n_act - 1, 2)
        gather_wait(1 - last_slot)
        scatter_wait(last_slot)

        @pl.when(n_act > 1)
        def _():
            scatter_wait(1 - last_slot)


def _expert_fused(tile_expert, n_active, src, dst, xs_tiles, layer, w1, w3, w2):
    n_tiles = src.shape[0] // (MOE_ROWS // MOE_GRANULE)
    wmap = lambda i, te, na, s, d: (layer, te[i], 0, 0)
    stage = pltpu.VMEM((2, MOE_ROWS, D_MODEL), BF16)
    assert MOE_SLOTS >= 2 * MOE_ROWS
    return pl.pallas_call(
        functools.partial(_expert_fused_kernel, n_tiles=n_tiles),
        grid_spec=pltpu.PrefetchScalarGridSpec(
            num_scalar_prefetch=4, grid=(n_tiles,),
            in_specs=[pl.BlockSpec(memory_space=pl.ANY),
                      pl.BlockSpec((None, None, D_MODEL, D_EXPERT), wmap),
                      pl.BlockSpec((None, None, D_MODEL, D_EXPERT), wmap),
                      pl.BlockSpec((None, None, D_EXPERT, D_MODEL), wmap)],
            out_specs=pl.BlockSpec(memory_space=pl.ANY),
            scratch_shapes=[stage, stage, pltpu.SemaphoreType.DMA((2,)), pltpu.SemaphoreType.DMA((2,))]),
        out_shape=jax.ShapeDtypeStruct(xs_tiles.shape, BF16),
        input_output_aliases={4: 0},
        compiler_params=_cparams(1),
        name="moe_expert_rows",
    )(tile_expert, n_active, src, dst, xs_tiles, w1, w3, w2)


def _combine_kernel(ys_ref, route_ref, h_ref, x1_ref, mod_ref, s1_ref, s3_ref, s2_ref, o_ref):
    route = route_ref[...]
    tt = route.shape[0]
    n_slots = ys_ref.shape[0]
    slot_lane = _lane_iota((tt, n_slots)).astype(F32)
    wp = jnp.zeros((tt, n_slots), F32)
    for k in range(TOP_K):
        wp = jnp.where(slot_lane == route[:, k:k + 1], route[:, TOP_K + k:TOP_K + k + 1], wp)
    routed = _dot(wp.astype(BF16), ys_ref[...])
    h = h_ref[...]
    a = _dot(h, s1_ref[...].astype(BF16))
    b = _dot(h, s3_ref[...].astype(BF16))
    shared = _dot((_silu(a) * b).astype(BF16), s2_ref[...].astype(BF16))
    o_ref[...] = x1_ref[...] + mod_ref[0, 5:6, :] * (routed + shared)


def _combine(ys_tiles, route, h2, x1, mod_p, tokens_per_mod, layer, s1, s3, s2):
    T = h2.shape[0]
    tt = MOE_TILE
    row = lambda i: (i, 0)
    wsh = lambda i: (layer, 0, 0)
    return pl.pallas_call(
        _combine_kernel,
        grid=(T // tt,),
        in_specs=[pl.BlockSpec((MOE_SLOTS, D_MODEL), row), pl.BlockSpec((tt, LANES), row),
                  pl.BlockSpec((tt, D_MODEL), row), pl.BlockSpec((tt, D_MODEL), row),
                  pl.BlockSpec((1, 6, D_MODEL), lambda i: ((i * tt) // tokens_per_mod, 0, 0)),
                  pl.BlockSpec((None, D_MODEL, D_EXPERT), wsh), pl.BlockSpec((None, D_MODEL, D_EXPERT), wsh),
                  pl.BlockSpec((None, D_EXPERT, D_MODEL), wsh)],
        out_specs=pl.BlockSpec((tt, D_MODEL), row),
        out_shape=jax.ShapeDtypeStruct((T, D_MODEL), F32),
        compiler_params=_cparams(1),
        name="moe_combine",
    )(ys_tiles, route, h2, x1, mod_p, s1, s3, s2)


def _excl_cumsum(a, axis):
    return jnp.cumsum(a, axis=axis) - a


def _moe_plan(cnt):
    nt = cnt.shape[0]
    gt = MOE_SLOTS // MOE_GRANULE
    gr = MOE_ROWS // MOE_GRANULE
    ng = _moe_sorted_granules(nt)
    g = (cnt + MOE_GRANULE - 1) // MOE_GRANULE
    loc = _excl_cumsum(g, 1)
    tot = jnp.sum(g, axis=0)
    grp = ((tot + gr - 1) // gr) * gr
    gs = _excl_cumsum(grp, 0)
    start = gs[None, :] + _excl_cumsum(g, 0)
    base = jnp.arange(nt, dtype=jnp.int32)[:, None] * gt + loc
    d = jnp.arange(ng, dtype=jnp.int32)[:, None, None]
    inseg = (start[None] <= d) & (d < (start + g)[None])
    local = jnp.sum(jnp.where(inseg, base[None] + d - start[None], 0), axis=(1, 2))
    real = jnp.any(inseg, axis=(1, 2))
    src = jnp.where(real, local, gt - 1)
    dst = jnp.where(real, local, -1)
    n_active = jnp.sum(grp) // gr
    first = jnp.minimum(jnp.arange(ng // gr, dtype=jnp.int32), jnp.maximum(n_active - 1, 0)) * gr
    tile_expert = jnp.sum((gs[None, :] <= first[:, None]).astype(jnp.int32), axis=1) - 1
    return (src.astype(jnp.int32), dst.astype(jnp.int32), tile_expert.astype(jnp.int32),
            n_active.astype(jnp.int32).reshape(1))


def _moe_sorted_granules(nt):
    gt_used = (TOP_K * MOE_TILE + N_EXPERTS * (MOE_GRANULE - 1)) // MOE_GRANULE
    gr = MOE_ROWS // MOE_GRANULE
    need = nt * gt_used + N_EXPERTS * (gr - 1) + 1
    return -(-need // gr) * gr


def _moe_sparse(h2, gates, x1, mod_p, tokens_per_mod, layer, w1, w3, w2, s1, s3, s2):
    xs_tiles, cnt, route = _dispatch(h2, gates)
    nt = cnt.shape[0]
    src, dst, tile_expert, n_active = _moe_plan(cnt.reshape(nt, N_EXPERTS).astype(jnp.int32))
    ys_tiles = _expert_fused(tile_expert, n_active, src, dst, xs_tiles, layer, w1, w3, w2)
    return _combine(ys_tiles, route, h2, x1, mod_p, tokens_per_mod, layer, s1, s3, s2)


def _group_matrix(group):
    i = np.arange(256)
    return jnp.asarray((i[:, None] // group) == (i[None, :] // group), BF16)


def _rope_tables(seq):
    t = np.arange(seq)
    lane = np.arange(256)
    j = lane % DF_DQK
    nf = DF_DQK // 4
    pos = np.where((j < DF_DQK // 2)[None, :], (t // GRID_W)[:, None], (t % GRID_W)[:, None]).astype(np.float32)
    inv = (1.0 / (ROPE_BASE ** (np.arange(nf, dtype=np.float32) / nf))).astype(np.float32)
    ang = pos * inv[j % nf][None, :]
    sign = np.where((lane % (2 * nf)) < nf, -1.0, 1.0).astype(np.float32)
    return jnp.asarray(np.cos(ang), F32), jnp.asarray(np.sin(ang) * sign[None, :], F32)


def _token_major(a):
    b, h, l, d = a.shape
    return a.transpose(0, 2, 1, 3).reshape(b, l, h * d)


def kernel(x_prompt, x_sample, cache_na_k, cache_na_v, cache_df_k, cache_df_v, state_ml_C, state_ml_n,
           state_ml_m, c, c_ctx, w_ada, b_ada, g_norm1, g_norm2, w_in, w_out, na_qn, na_kn, na_rpb,
           df_qn, df_kn, df_lam, df_subln, ml_gate_b, ml_outn, w_router, b_router, w_exp1, w_exp3, w_exp2,
           w_sh1, w_sh3, w_sh2):
    bc, lc, _ = x_prompt.shape
    bs, ls, _ = x_sample.shape
    past = cache_na_k.shape[3]

    cond = jnp.zeros((8, D_MODEL), F32).at[0].set(c_ctx).at[1:1 + bs].set(c)
    mod = _modulation(cond, w_ada, b_ada)
    g64 = _group_matrix(NA_DH)
    g32 = _group_matrix(DF_DQK)
    rope_tabs = _rope_tables(ls)
    w_in_bf = jnp.pad(w_in.astype(BF16), ((0, 0), (0, 0), (0, IN_COLS_PAD - IN_COLS)))
    w_out_bf = w_out.astype(BF16)
    na_blocks = _na_bias_blocks(na_rpb)
    moe_w = (w_exp1, w_exp3, w_exp2, w_sh1, w_sh3, w_sh2)

    y_p = x_prompt.reshape(bc * lc, D_MODEL)
    y_s = x_sample.reshape(bs * ls, D_MODEL)
    caches = states = None
    m_states = []
    for l in range(DEPTH):
        lam_init = 0.8 - 0.6 * math.exp(-0.3 * l)
        mod_l = mod[l].reshape(8, 6, D_MODEL)
        g1 = g_norm1[l][None, :]
        g2 = g_norm2[l][None, :]
        qn = jnp.tile(na_qn[l], NA_HEADS)[None, :]
        kn = jnp.tile(na_kn[l], NA_HEADS)[None, :]
        dqn = jnp.tile(df_qn[l], 2 * DF_HEADS)[None, :]
        dkn = jnp.tile(df_kn[l], 2 * DF_HEADS)[None, :]
        subln = jnp.tile(df_subln[l], DF_HEADS)[None, :]
        outn = jnp.tile(ml_outn[l], ML_HEADS)[None, :]
        gate_b = jnp.pad(ml_gate_b[l].reshape(-1), (0, LANES - 4 * ML_HEADS))[None, :]
        br = b_router[l][None, :]
        proj_w = (l, g1, w_in_bf, qn, kn, dqn, dkn, gate_b, g64, g32)

        mod_c = mod_l[0:1]
        tpm = bc * lc
        res = _inproj(y_p, mod_c, tpm, *proj_w, None, cache_seq=lc, cache_prev=caches)
        naq, nak, nav, dfq, dfk, dfv, mq, mk, mv, mo, mg = res[:11]
        caches = tuple(res[11:])
        nao, dfo = _ctx_attention(naq, nak, nav, dfq, dfk, dfv, df_lam[l], subln, lc, lam_init)
        mlh, c_n, n_n, m_n = _mlstm(mq, mk, mv, mg, lc, emit_state=True, layer=l, state_prev=states)
        states = (c_n, n_n)
        x1, h2, gates = _outproj(y_p, nao, dfo, mlh, mo, mod_c, tpm, l, g2, outn, w_out_bf, w_router, br)
        y_p = _moe_sparse(h2, gates, x1, mod_c, tpm, l, *moe_w)
        m_states.append(m_n[:, :, 0, 0].reshape(bc, 2, ML_HEADS))

        mod_s = mod_l[1:1 + bs]
        naq, nak, nav, dfq, dfk, dfv, mq, mk, mv, mo, mg = _inproj(y_s, mod_s, ls, *proj_w, rope_tabs)
        kc_na = _token_major(cache_na_k[:, l])
        vc_na = _token_major(cache_na_v[:, l])
        kc_df = cache_df_k[:, l].transpose(0, 3, 1, 2, 4).reshape(bs, past, 256)
        vc_df = _token_major(cache_df_v[:, l])
        bias_tabs = _na_bias_tables(na_blocks[l * NA_HEADS:(l + 1) * NA_HEADS], ls // GRID_W)
        nao = _lat_na(naq, nak, nav, kc_na, vc_na, bias_tabs, ls)
        dfo = _lat_df(dfq, dfk, dfv, kc_df, vc_df, df_lam[l], subln, ls, lam_init)
        n0 = jnp.broadcast_to(state_ml_n[:, l][..., None], (bs, 2, ML_HEADS, ML_DH, ML_DH))
        c0 = jnp.concatenate([state_ml_C[:, l], n0], axis=-1)
        c0 = c0.reshape(bs, 2 * ML_HEADS, ML_DH, 2 * ML_DH)
        m0 = jnp.broadcast_to(state_ml_m[:, l].reshape(bs, 2 * ML_HEADS, 1, 1), (bs, 2 * ML_HEADS, 8, LANES))
        (mlh,) = _mlstm(mq, mk, mv, mg, ls, init=(c0, m0))
        x1, h2, gates = _outproj(y_s, nao, dfo, mlh, mo, mod_s, ls, l, g2, outn, w_out_bf, w_router, br)
        y_s = _moe_sparse(h2, gates, x1, mod_s, ls, l, *moe_w)

    new_ml_c = states[0].reshape(bc, DEPTH, 2, ML_HEADS, ML_DH, ML_DH)
    new_ml_n = states[1].reshape(bc, DEPTH, 2, ML_HEADS, ML_DH)
    return (y_p.reshape(bc, lc, D_MODEL), y_s.reshape(bs, ls, D_MODEL), *caches,
            new_ml_c, new_ml_n, jnp.stack(m_states, axis=1))
```

```python
import functools
import math

import numpy as np
import jax
import jax.numpy as jnp
from jax import lax
from jax.experimental import pallas as pl
from jax.experimental.pallas import tpu as pltpu

F32 = jnp.float32
BF16 = jnp.bfloat16

D_MODEL = 1024
DEPTH = 2
GRID_W = 64
NA_HEADS = 4
NA_DH = 64
NA_WIN_R = 8
NA_WIN_C = 16
DF_HEADS = 4
DF_DV = 64
DF_DQK = 32
ML_HEADS = 4
ML_DH = 128
ML_CHUNK = 128
N_EXPERTS = 32
TOP_K = 4
D_EXPERT = 256
ROUTED_SCALE = 2.5
ROPE_BASE = 10000.0
EPS = 1e-6

LANES = 128
IN_COLS = 3600
IN_COLS_PAD = 3712
GATE_COL0 = 3584
MASK_VALUE = -1e30
VMEM_LIMIT = 56 * 1024 * 1024

TM_PROJ = 512
MOE_TILE = 256
MOE_GRANULE = 16
MOE_SLOTS = TOP_K * MOE_TILE + N_EXPERTS * MOE_GRANULE
MOE_ROWS = 512
Q_BLOCK_DF = 256
CTX_ATTN_BATCHES = 2
DF_SCORE_LEAD_CTX = 1
DF_SCORE_LEAD_LAT = 3
LOOP_UNROLL = 2
NA_Q_ROWS = 4
NA_K_ROWS = NA_Q_ROWS + NA_WIN_R
LOG2E = math.log2(math.e)


def _cparams(n_axes):
    return pltpu.CompilerParams(dimension_semantics=("arbitrary",) * n_axes,
                                vmem_limit_bytes=VMEM_LIMIT)


def _silu(x):
    return x * jax.nn.sigmoid(x)


def _dot(a, b):
    return jnp.dot(a, b, preferred_element_type=F32)


def _dot_nt(a, b):
    return lax.dot_general(a, b, (((1,), (1,)), ((), ())), preferred_element_type=F32)


def _dot_tn(a, b):
    return lax.dot_general(a, b, (((0,), (0,)), ((), ())), preferred_element_type=F32)


def _lane_iota(shape):
    return lax.broadcasted_iota(jnp.int32, shape, 1)


def _mod_kernel(cond_ref, w_ref, b_ref, o_ref):
    s = _silu(cond_ref[...]).astype(BF16)
    o_ref[0] = _dot(s, w_ref[0].astype(BF16)) + b_ref[0]


def _modulation(cond, w_ada, b_ada):
    tn = 3072
    n = 6 * D_MODEL
    return pl.pallas_call(
        _mod_kernel,
        grid=(DEPTH, n // tn),
        in_specs=[pl.BlockSpec((8, D_MODEL), lambda l, j: (0, 0)),
                  pl.BlockSpec((1, D_MODEL, tn), lambda l, j: (l, 0, j)),
                  pl.BlockSpec((1, 1, tn), lambda l, j: (l, 0, j))],
        out_specs=pl.BlockSpec((1, 8, tn), lambda l, j: (l, 0, j)),
        out_shape=jax.ShapeDtypeStruct((DEPTH, 8, n), F32),
        compiler_params=_cparams(2),
        name="adaln_mod",
    )(cond, w_ada, b_ada.reshape(DEPTH, 1, n))


def _group_rms(v, gmat_ref, group):
    ss = _dot((v * v).astype(BF16), gmat_ref[...])
    return v * lax.rsqrt(ss * (1.0 / group) + EPS)


def _rope(v, cos, sin_signed):
    first = (_lane_iota(v.shape) % 16) < 8
    partner = jnp.where(first, pltpu.roll(v, v.shape[1] - 8, 1), pltpu.roll(v, 8, 1))
    return v * cos + partner * sin_signed


def _inproj_kernel(*refs, rope, cache_seq, cache_first):
    (x_ref, mod_ref, g1_ref, w_ref, qn_ref, kn_ref, dqn_ref, dkn_ref, gb_ref, g64_ref, g32_ref) = refs[:11]
    pos = 11
    if rope:
        cos_ref, sin_ref = refs[pos:pos + 2]
        pos += 2
    if cache_seq and not cache_first:
        pos += 4
    (naq_ref, nak_ref, nav_ref, dfq_ref, dfk_ref, dfv_ref,
     mq_ref, mk_ref, mv_ref, mo_ref, mg_ref) = refs[pos:pos + 11]
    pos += 11
    x = x_ref[...]
    y = x * lax.rsqrt(jnp.mean(x * x, axis=-1, keepdims=True) + EPS) * g1_ref[...]
    h = (y * (1.0 + mod_ref[0, 1:2, :]) + mod_ref[0, 0:1, :]).astype(BF16)

    def proj(a, b):
        return _dot(h, w_ref[:, a:b])

    naq_ref[...] = _group_rms(proj(0, 256), g64_ref, NA_DH) * qn_ref[...]
    nak = _group_rms(proj(256, 512), g64_ref, NA_DH) * kn_ref[...]
    nav = proj(512, 768)
    dfq = _group_rms(proj(768, 1024), g32_ref, DF_DQK) * dqn_ref[...]
    dfk = _group_rms(proj(1024, 1280), g32_ref, DF_DQK) * dkn_ref[...]
    dfv = proj(1280, 1536)
    if rope:
        dfq = _rope(dfq, cos_ref[...], sin_ref[...])
        dfk = _rope(dfk, cos_ref[...], sin_ref[...])
    nak_ref[...] = nak
    nav_ref[...] = nav
    dfq_ref[...] = dfq
    dfk_ref[...] = dfk
    dfv_ref[...] = dfv
    if cache_seq:
        nk_hm, nv_hm, dk_hm, dv_hm = refs[pos:pos + 4]

        def put(ref, bi, idx, val):
            if cache_first:
                for dl in range(DEPTH):
                    ref[(bi, dl) + idx] = val
            else:
                ref[(bi,) + idx] = val

        for bi in range(x.shape[0] // cache_seq):
            rs = slice(bi * cache_seq, (bi + 1) * cache_seq)
            for hd in range(NA_HEADS):
                put(nk_hm, bi, (hd,), nak[rs, NA_DH * hd:NA_DH * (hd + 1)])
                put(nv_hm, bi, (hd,), nav[rs, NA_DH * hd:NA_DH * (hd + 1)])
                put(dv_hm, bi, (hd,), dfv[rs, DF_DV * hd:DF_DV * (hd + 1)])
                for a in range(2):
                    c0 = (2 * hd + a) * DF_DQK
                    put(dk_hm, bi, (hd, a), dfk[rs, c0:c0 + DF_DQK])
    mq_ref[...] = proj(1536, 2048)
    mk_ref[...] = proj(2048, 2560) * (ML_DH ** -0.5)
    mv_ref[...] = proj(2560, 3072)
    mo_ref[...] = proj(3072, 3584)
    g = proj(GATE_COL0, IN_COLS_PAD) + gb_ref[...]
    lane = _lane_iota(g.shape)
    is_forget = ((lane // ML_HEADS) % 2 == 1) & (lane < 4 * ML_HEADS)
    log_sig = jnp.minimum(g, 0.0) - jnp.log1p(jnp.exp(-jnp.abs(g)))
    mg_ref[...] = jnp.where(is_forget, log_sig, g)


def _inproj(x, mod_p, tokens_per_mod, layer, g1, w_in_bf, qn, kn, dqn, dkn, gate_b, g64, g32, rope_tabs,
            cache_seq=0, cache_prev=None):
    T = x.shape[0]
    tm = TM_PROJ
    rope = rope_tabs is not None
    row = lambda i: (i, 0)
    const = lambda i: (0, 0)
    in_specs = [pl.BlockSpec((tm, D_MODEL), row),
                pl.BlockSpec((1, 6, D_MODEL), lambda i: ((i * tm) // tokens_per_mod, 0, 0)),
                pl.BlockSpec((1, D_MODEL), const),
                pl.BlockSpec((None, D_MODEL, IN_COLS_PAD), lambda i: (layer, 0, 0)),
                pl.BlockSpec((1, 256), const), pl.BlockSpec((1, 256), const),
                pl.BlockSpec((1, 256), const), pl.BlockSpec((1, 256), const),
                pl.BlockSpec((1, LANES), const),
                pl.BlockSpec((256, 256), const), pl.BlockSpec((256, 256), const)]
    args = [x, mod_p, g1, w_in_bf, qn, kn, dqn, dkn, gate_b, g64, g32]
    if rope:
        seq = rope_tabs[0].shape[0]
        tab = lambda i: (i % (seq // tm), 0)
        in_specs += [pl.BlockSpec((tm, 256), tab), pl.BlockSpec((tm, 256), tab)]
        args += list(rope_tabs)
    widths = [256] * 6 + [512] * 4 + [LANES]
    out_specs = [pl.BlockSpec((tm, w), row) for w in widths]
    out_shape = [jax.ShapeDtypeStruct((T, w), F32) for w in widths]
    aliases = {}
    if cache_seq:
        nb = tm // cache_seq
        hm = (NA_HEADS, cache_seq, NA_DH)
        hm_df = (DF_HEADS, 2, cache_seq, DF_DQK)
        for k, tail in enumerate((hm, hm, hm_df, hm)):
            zeros = (0,) * len(tail)
            if cache_prev is None:
                out_specs.append(pl.BlockSpec((nb, DEPTH) + tail, lambda i, z=zeros: (i, 0) + z))
            else:
                out_specs.append(pl.BlockSpec((nb, None) + tail, lambda i, z=zeros: (i, layer) + z))
                aliases[len(args)] = len(widths) + k
                in_specs.append(pl.BlockSpec(memory_space=pl.ANY))
                args.append(cache_prev[k])
            out_shape.append(jax.ShapeDtypeStruct((T // cache_seq, DEPTH) + tail, F32))
    return pl.pallas_call(
        functools.partial(_inproj_kernel, rope=rope, cache_seq=cache_seq, cache_first=cache_prev is None),
        grid=(T // tm,),
        in_specs=in_specs,
        out_specs=out_specs,
        out_shape=out_shape,
        input_output_aliases=aliases,
        compiler_params=_cparams(1),
        name="inproj_rope" if rope else "inproj",
    )(*args)


def _df_lambda(lam_ref, lam_init):
    v = lam_ref[...]
    a = jnp.sum(v[0:1] * v[1:2], axis=1, keepdims=True)
    b = jnp.sum(v[2:3] * v[3:4], axis=1, keepdims=True)
    return jnp.exp(a) - jnp.exp(b) + lam_init


def _softmax_parts(scores):
    es = _softmax_exps(scores)
    tot = functools.reduce(lambda a, b: a + b, [jnp.sum(e, axis=-1, keepdims=True) for e in es])
    return [e.astype(BF16) for e in es], 1.0 / tot


def _softmax_exps(scores):
    m = functools.reduce(jnp.maximum, [jnp.max(s, axis=-1, keepdims=True) for s in scores])
    return [jnp.exp2(s - m) for s in scores]


def _attend_masked_heads(tiles, lead=1):
    lane = _lane_iota(tiles[0][0].shape)
    v_aug = [[jnp.concatenate([v, jnp.ones(v.shape, BF16)], axis=1) for v in v_segs] for _, _, v_segs, _ in tiles]
    n_heads = 2 * len(tiles)

    def scores(c):
        qp, k_segs, _, bias = tiles[c // 2]
        qm = jnp.where(lane // NA_DH == c % 2, qp, 0.0).astype(BF16)
        s = [_dot_nt(qm, k) for k in k_segs]
        if bias is not None:
            s[0] = s[0] + bias[c % 2]
        return s

    outs = []
    pending = [scores(c) for c in range(min(lead, n_heads))]
    for c in range(n_heads):
        if c + lead < n_heads:
            pending.append(scores(c + lead))
        es = _softmax_exps(pending.pop(0))
        o = functools.reduce(lambda x, y: x + y, [_dot(e.astype(BF16), v) for e, v in zip(es, v_aug[c // 2])])
        outs.append(o[:, :LANES] / o[:, LANES:])
    return [jnp.where(lane < NA_DH, outs[2 * t], outs[2 * t + 1]) for t in range(len(tiles))]


def _subln(o, subln_row, lam_init):
    lane = _lane_iota(o.shape)
    sq = o * o
    s0 = jnp.sum(jnp.where(lane < DF_DV, sq, 0.0), axis=-1, keepdims=True)
    s1 = jnp.sum(jnp.where(lane >= DF_DV, sq, 0.0), axis=-1, keepdims=True)
    ms = jnp.where(lane < DF_DV, s0, s1) * (1.0 / DF_DV)
    return o * lax.rsqrt(ms + EPS) * subln_row * (1.0 - lam_init)


def _diff_attn(q, k_segs, v_segs, lam, lead):
    n_maps = 2 * DF_HEADS
    per_tile = LANES // DF_DQK
    lane = _lane_iota((q.shape[0], LANES))
    tiles = []
    for t in range(n_maps // per_tile):
        sl = slice(LANES * t, LANES * (t + 1))
        tiles.append((q[:, sl] * (DF_DQK ** -0.5 * LOG2E),
                      [k[:, sl].astype(BF16) for k in k_segs], [v[:, sl].astype(BF16) for v in v_segs]))

    def scores(c):
        qp, kps, _ = tiles[c // per_tile]
        qm = jnp.where(lane // DF_DQK == c % per_tile, qp, 0.0).astype(BF16)
        return [_dot_nt(qm, kp) for kp in kps]

    def head_out(m0, m1, vps):
        (e0, r0), (e1, r1) = m0, m1
        c0 = r0.astype(BF16)
        c1 = (-(lam * r1)).astype(BF16)
        return functools.reduce(lambda x, y: x + y,
                                [_dot(x0 * c0 + x1 * c1, vp) for x0, x1, vp in zip(e0, e1, vps)])

    heads = []
    maps = []
    pending = [scores(c) for c in range(lead)]
    for c in range(n_maps):
        if c + lead < n_maps:
            pending.append(scores(c + lead))
        maps.append(_softmax_parts(pending.pop(0)))
        if c % 2 == 1:
            heads.append(head_out(maps[c - 1], maps[c], tiles[c // per_tile][2]))
    return [jnp.where(lane < DF_DV, heads[2 * t], heads[2 * t + 1]) for t in range(len(tiles))]


def _ctx_attn_kernel(naq_ref, nak_ref, nav_ref, dfq_ref, dfk_ref, dfv_ref, lam_ref, subln_ref,
                     nao_ref, dfo_ref, *, lam_init, seq):
    lam = _df_lambda(lam_ref, lam_init)
    for bi in range(naq_ref.shape[0] // seq):
        rs = slice(bi * seq, (bi + 1) * seq)
        tiles = []
        for t in range(2):
            sl = slice(LANES * t, LANES * (t + 1))
            tiles.append((naq_ref[rs, sl] * (NA_DH ** -0.5 * LOG2E), [nak_ref[rs, sl].astype(BF16)],
                          [nav_ref[rs, sl].astype(BF16)], None))
        for t, o in enumerate(_attend_masked_heads(tiles, lead=0)):
            nao_ref[rs, LANES * t:LANES * (t + 1)] = o
        outs = _diff_attn(dfq_ref[rs, :], [dfk_ref[rs, :]], [dfv_ref[rs, :]], lam, DF_SCORE_LEAD_CTX)
        for t, o in enumerate(outs):
            sl = slice(LANES * t, LANES * (t + 1))
            dfo_ref[rs, sl] = _subln(o, subln_ref[:, sl], lam_init)


def _ctx_attention(naq, nak, nav, dfq, dfk, dfv, df_lam, subln, seq, lam_init):
    T = naq.shape[0]
    rows = CTX_ATTN_BATCHES * seq
    blk = pl.BlockSpec((rows, 256), lambda b: (b, 0))
    return pl.pallas_call(
        functools.partial(_ctx_attn_kernel, lam_init=lam_init, seq=seq),
        grid=(T // rows,),
        in_specs=[blk] * 6 + [pl.BlockSpec((4, DF_DQK), lambda b: (0, 0)),
                              pl.BlockSpec((1, 256), lambda b: (0, 0))],
        out_specs=[blk, blk],
        out_shape=[jax.ShapeDtypeStruct((T, 256), F32)] * 2,
        compiler_params=_cparams(1),
        name="ctx_attention",
    )(naq, nak, nav, dfq, dfk, dfv, df_lam, subln)


def _na_bias_kernel(rpb_ref, o_ref):
    hd = pl.program_id(0)
    shape = (GRID_W, GRID_W)
    qi = lax.broadcasted_iota(jnp.int32, shape, 0)
    kc = lax.broadcasted_iota(jnp.int32, shape, 1)
    cs = jnp.clip(qi - NA_WIN_C // 2, 0, GRID_W - NA_WIN_C)
    inwin = (kc >= cs) & (kc < cs + NA_WIN_C)
    off = jnp.where(inwin, kc - qi + NA_WIN_C - 1, -1)
    n_c = 2 * NA_WIN_C - 1
    n_r = 2 * NA_WIN_R - 1
    for ro in range(n_r):
        val = jnp.full(shape, MASK_VALUE, F32)
        for d in range(n_c):
            val = jnp.where(off == d, rpb_ref[hd * (n_r * n_c) + ro * n_c + d], val)
        o_ref[0, ro] = val * LOG2E


def _na_bias_blocks(rpb):
    n_r = 2 * NA_WIN_R - 1
    n_lh = rpb.shape[0] * rpb.shape[1]
    return pl.pallas_call(
        _na_bias_kernel,
        grid=(n_lh,),
        in_specs=[pl.BlockSpec(memory_space=pltpu.SMEM)],
        out_specs=pl.BlockSpec((1, n_r, GRID_W, GRID_W), lambda h: (h, 0, 0, 0)),
        out_shape=jax.ShapeDtypeStruct((n_lh, n_r, GRID_W, GRID_W), F32),
        compiler_params=_cparams(1),
        name="na_bias",
    )(rpb.reshape(-1))


def _na_block_geometry(rows):
    last_r0 = rows - NA_Q_ROWS
    return ((0, 0), (NA_Q_ROWS, 0), (last_r0, rows - NA_K_ROWS))


def _na_bias_tables(blocks, rows):
    masked = jnp.full(blocks[:, 0].shape, MASK_VALUE * LOG2E, F32)
    tabs = []
    for r0, k0 in _na_block_geometry(rows):
        q_rows = []
        for rq in range(NA_Q_ROWS):
            r = r0 + rq
            rs = min(max(r - NA_WIN_R // 2, 0), rows - NA_WIN_R)
            cols = []
            for kr in range(k0, k0 + NA_K_ROWS):
                cols.append(blocks[:, kr - r + NA_WIN_R - 1] if rs <= kr < rs + NA_WIN_R else masked)
            q_rows.append(jnp.concatenate(cols, axis=-1))
        tabs.append(jnp.concatenate(q_rows, axis=-2))
    return jnp.stack(tabs, axis=0)


def _lat_na_kernel(q_ref, k_ref, v_ref, kc_ref, vc_ref, bias_ref, o_ref, *, rows):
    i = pl.program_id(1)
    k0 = jnp.clip(i * NA_Q_ROWS - NA_WIN_R // 2, 0, rows - NA_K_ROWS)
    krows = pl.ds(pl.multiple_of(k0 * GRID_W, GRID_W), NA_K_ROWS * GRID_W)
    q = q_ref[...] * (NA_DH ** -0.5 * LOG2E)
    tiles = []
    for t in range(2):
        sl = slice(LANES * t, LANES * (t + 1))
        tiles.append((q[:, sl],
                      [k_ref[krows, sl].astype(BF16), kc_ref[0, :, sl].astype(BF16)],
                      [v_ref[krows, sl].astype(BF16), vc_ref[0, :, sl].astype(BF16)],
                      [bias_ref[2 * t], bias_ref[2 * t + 1]]))
    for t, o in enumerate(_attend_masked_heads(tiles)):
        o_ref[:, LANES * t:LANES * (t + 1)] = o


def _lat_na(q, k, v, kc, vc, bias_tabs, seq):
    T = q.shape[0]
    past = kc.shape[1]
    rows = seq // GRID_W
    nq = rows // NA_Q_ROWS
    qrows = NA_Q_ROWS * GRID_W
    qblk = pl.BlockSpec((qrows, 256), lambda b, i: (b * nq + i, 0))
    kblk = pl.BlockSpec((seq, 256), lambda b, i: (b, 0))
    cblk = pl.BlockSpec((1, past, 256), lambda b, i: (b, 0, 0))
    variant = lambda b, i: (jnp.where(i == 0, 0, jnp.where(i == nq - 1, 2, 1)), 0, 0, 0)
    return pl.pallas_call(
        functools.partial(_lat_na_kernel, rows=rows),
        grid=(T // seq, nq),
        in_specs=[qblk, kblk, kblk, cblk, cblk,
                  pl.BlockSpec((None, NA_HEADS, qrows, NA_K_ROWS * GRID_W), variant)],
        out_specs=qblk,
        out_shape=jax.ShapeDtypeStruct((T, 256), F32),
        compiler_params=_cparams(2),
        name="lat_na",
    )(q, k, v, kc, vc, bias_tabs)


def _lat_df_kernel(q_ref, k_ref, v_ref, kc_ref, vc_ref, lam_ref, subln_ref, o_ref, *, lam_init):
    lam = _df_lambda(lam_ref, lam_init)
    outs = _diff_attn(q_ref[...], [k_ref[...], kc_ref[0]], [v_ref[...], vc_ref[0]], lam, DF_SCORE_LEAD_LAT)
    for t, o in enumerate(outs):
        sl = slice(LANES * t, LANES * (t + 1))
        o_ref[:, sl] = _subln(o, subln_ref[:, sl], lam_init)


def _lat_df(q, k, v, kc, vc, df_lam, subln, seq, lam_init):
    T = q.shape[0]
    past = kc.shape[1]
    nq = seq // Q_BLOCK_DF
    qblk = pl.BlockSpec((Q_BLOCK_DF, 256), lambda b, i: (b * nq + i, 0))
    kblk = pl.BlockSpec((seq, 256), lambda b, i: (b, 0))
    cblk = pl.BlockSpec((1, past, 256), lambda b, i: (b, 0, 0))
    return pl.pallas_call(
        functools.partial(_lat_df_kernel, lam_init=lam_init),
        grid=(T // seq, nq),
        in_specs=[qblk, kblk, kblk, cblk, cblk,
                  pl.BlockSpec((4, DF_DQK), lambda b, i: (0, 0)),
                  pl.BlockSpec((1, 256), lambda b, i: (0, 0))],
        out_specs=qblk,
        out_shape=jax.ShapeDtypeStruct((T, 256), F32),
        compiler_params=_cparams(2),
        name="lat_df",
    )(q, k, v, kc, vc, df_lam, subln)


def _mlstm_kernel(*refs, nc, has_init, emit_state, state_first):
    refs = list(refs)
    q_ref, k_ref, v_ref, g_ref = refs[:4]
    pos = 4
    if has_init:
        c0_ref, m0_ref = refs[pos:pos + 2]
        pos += 2
    if emit_state and not state_first:
        pos += 2
    h_ref = refs[pos]
    pos += 1
    if emit_state:
        cn_ref, nn_ref, mn_ref = refs[pos:pos + 3]
        pos += 3
    caug_sc, m_sc = refs[pos:pos + 2]

    if has_init:
        caug_sc[...] = c0_ref[0]
        m_sc[...] = m0_ref[0]
    else:
        caug_sc[...] = jnp.zeros(caug_sc.shape, F32)
        m_sc[...] = jnp.zeros(m_sc.shape, F32)
    static_chunks = nc <= LOOP_UNROLL
    if not static_chunks:
        h_ref[...] = jnp.zeros(h_ref.shape, F32)
    seen = set()

    ti = lax.broadcasted_iota(jnp.int32, (ML_CHUNK, ML_CHUNK), 0)
    si = lax.broadcasted_iota(jnp.int32, (ML_CHUNK, ML_CHUNK), 1)
    masks = (si <= ti, si >= ti)
    ones_blk = jnp.ones((ML_CHUNK, LANES), BF16)
    full = (ML_CHUNK, ML_CHUNK)

    def body(c, carry):
        for d in range(2):
            ci = c if d == 0 else nc - 1 - c
            if static_chunks:
                rows = pl.ds(ci * ML_CHUNK, ML_CHUNK)
                accumulate = ci in seen
                seen.add(ci)
            else:
                rows = pl.ds(pl.multiple_of(ci * ML_CHUNK, ML_CHUNK), ML_CHUNK)
                accumulate = True
            mask = masks[d]
            tri = jnp.where(mask, 1.0, 0.0).astype(BF16)
            gblk = g_ref[rows, :]
            hi = gblk.astype(BF16)
            lo = (gblk - hi.astype(F32)).astype(BF16)
            bc = _dot(tri, hi) + _dot(tri, lo)
            g_t = gblk.T
            bc_t = bc.T
            last = ML_CHUNK - 1 if d == 0 else 0
            qk = [_dot_nt(q_ref[rows, ML_DH * hd:ML_DH * (hd + 1)].astype(BF16),
                          k_ref[rows, ML_DH * hd:ML_DH * (hd + 1)].astype(BF16)) for hd in range(ML_HEADS)]
            for hd in range(ML_HEADS):
                ch_i = 2 * ML_HEADS * d + hd
                ch_f = ch_i + ML_HEADS
                idx = ML_HEADS * d + hd
                hs = slice(ML_DH * hd, ML_DH * (hd + 1))
                b_col = bc[:, ch_f:ch_f + 1]
                a_col = gblk[:, ch_i:ch_i + 1] - b_col
                a_msk = jnp.where(mask, g_t[ch_i:ch_i + 1, :] - bc_t[ch_f:ch_f + 1, :], -jnp.inf)
                m_col = jnp.max(a_msk, axis=1, keepdims=True)
                m_rep = jnp.broadcast_to(m_col, full)
                b_rep = jnp.broadcast_to(b_col, full)
                a_rep = jnp.broadcast_to(a_col, full)
                m_last = m_col[last:last + 1, :]
                b_last = b_col[last:last + 1, :]
                qh = q_ref[rows, hs].astype(BF16)
                kf = k_ref[rows, hs]
                vaug = jnp.concatenate([v_ref[rows, hs].astype(BF16), ones_blk], axis=1)
                s0 = qk[hd] * jnp.exp(a_msk - m_rep)
                sv0 = _dot(s0.astype(BF16), vaug)
                u0 = _dot_tn((jnp.exp(a_rep - m_last) * kf).astype(BF16), vaug)
                m_prev = m_sc[idx][0:1, 0:1]
                g_rep = jnp.maximum(m_prev, m_rep)
                w0 = jnp.exp(m_prev - g_rep)
                cc = jnp.exp(m_rep - g_rep)
                caug = caug_sc[idx]
                qc = _dot(qh, caug.astype(BF16))
                num = w0 * qc[:, :ML_DH] + cc * sv0[:, :ML_DH]
                den = w0 * qc[:, ML_DH:] + cc * sv0[:, ML_DH:]
                h_dir = num / jnp.maximum(jnp.abs(den), jnp.exp(-b_rep - g_rep))
                if accumulate:
                    h_ref[rows, hs] += h_dir
                else:
                    h_ref[rows, hs] = h_dir
                g_last = jnp.maximum(m_prev, m_last)
                caug_sc[idx] = jnp.exp(m_prev - g_last) * caug + jnp.exp(m_last - g_last) * u0
                m_sc[idx] = jnp.broadcast_to(b_last + g_last, m_sc.shape[1:])
        return carry

    if static_chunks:
        for c in range(nc):
            body(c, 0)
    else:
        lax.fori_loop(0, nc, body, 0, unroll=LOOP_UNROLL)
    if emit_state:
        slots = [(0, dl) for dl in range(DEPTH)] if state_first else [(0,)]
        for slot in slots:
            cn_ref[slot] = caug_sc[:, :, :ML_DH]
        for idx in range(2 * ML_HEADS):
            n_row = caug_sc[idx][:, ML_DH:].T[0:1, :]
            for slot in slots:
                nn_ref[slot + (slice(idx, idx + 1), slice(None))] = n_row
        mn_ref[0] = m_sc[...]


def _mlstm(q, k, v, g, seq, init=None, emit_state=False, layer=0, state_prev=None):
    T = q.shape[0]
    n_chain = 2 * ML_HEADS
    blk = pl.BlockSpec((seq, 512), lambda b: (b, 0))
    gblk = pl.BlockSpec((seq, LANES), lambda b: (b, 0))
    cblk = pl.BlockSpec((1, n_chain, ML_DH, 2 * ML_DH), lambda b: (b, 0, 0, 0))
    mblk = pl.BlockSpec((1, n_chain, 8, LANES), lambda b: (b, 0, 0, 0))
    in_specs = [blk, blk, blk, gblk]
    args = [q, k, v, g]
    if init is not None:
        in_specs += [cblk, mblk]
        args += list(init)
    out_specs = [blk]
    out_shape = [jax.ShapeDtypeStruct((T, 512), F32)]
    nb = T // seq
    aliases = {}
    if emit_state:
        if state_prev is None:
            out_specs += [pl.BlockSpec((1, DEPTH, n_chain, ML_DH, ML_DH), lambda b: (b, 0, 0, 0, 0)),
                          pl.BlockSpec((1, DEPTH, n_chain, ML_DH), lambda b: (b, 0, 0, 0)), mblk]
        else:
            out_specs += [pl.BlockSpec((1, None, n_chain, ML_DH, ML_DH), lambda b: (b, layer, 0, 0, 0)),
                          pl.BlockSpec((1, None, n_chain, ML_DH), lambda b: (b, layer, 0, 0)), mblk]
            aliases = {len(args): 1, len(args) + 1: 2}
            in_specs += [pl.BlockSpec(memory_space=pl.ANY)] * 2
            args += list(state_prev)
        out_shape += [jax.ShapeDtypeStruct((nb, DEPTH, n_chain, ML_DH, ML_DH), F32),
                      jax.ShapeDtypeStruct((nb, DEPTH, n_chain, ML_DH), F32),
                      jax.ShapeDtypeStruct((nb, n_chain, 8, LANES), F32)]
    return pl.pallas_call(
        functools.partial(_mlstm_kernel, nc=seq // ML_CHUNK, has_init=init is not None,
                          emit_state=emit_state, state_first=state_prev is None),
        grid=(nb,),
        in_specs=in_specs,
        out_specs=out_specs,
        out_shape=out_shape,
        input_output_aliases=aliases,
        scratch_shapes=[pltpu.VMEM((n_chain, ML_DH, 2 * ML_DH), F32),
                        pltpu.VMEM((n_chain, 8, LANES), F32)],
        compiler_params=_cparams(1),
        name="mlstm_state" if emit_state else "mlstm",
    )(*args)


def _outproj_kernel(x_ref, nao_ref, dfo_ref, mlh_ref, mo_ref, mod_ref, g2_ref, outn_ref, wout_ref,
                    wr_ref, br_ref, x1_ref, h2_ref, gates_ref):
    mlh = mlh_ref[...]
    parts = []
    for hd in range(ML_HEADS):
        seg = mlh[:, ML_DH * hd:ML_DH * (hd + 1)]
        parts.append(seg * lax.rsqrt(jnp.mean(seg * seg, axis=-1, keepdims=True) + EPS))
    ml_o = jnp.concatenate(parts, axis=1) * outn_ref[...] * jax.nn.sigmoid(mo_ref[...])
    mix = (_dot(nao_ref[...].astype(BF16), wout_ref[0:256, :])
           + _dot(dfo_ref[...].astype(BF16), wout_ref[256:512, :])
           + _dot(ml_o.astype(BF16), wout_ref[512:1024, :]))
    x1 = x_ref[...] + mod_ref[0, 2:3, :] * mix
    x1_ref[...] = x1
    y = x1 * lax.rsqrt(jnp.mean(x1 * x1, axis=-1, keepdims=True) + EPS) * g2_ref[...]
    h2 = y * (1.0 + mod_ref[0, 4:5, :]) + mod_ref[0, 3:4, :]
    h2_ref[...] = h2.astype(BF16)

    wr = wr_ref[...]
    h_hi = h2.astype(BF16)
    h_lo = (h2 - h_hi.astype(F32)).astype(BF16)
    w_hi = wr.astype(BF16)
    w_lo = (wr - w_hi.astype(F32)).astype(BF16)
    logits = _dot(h_hi, w_hi) + _dot(h_lo, w_hi) + _dot(h_hi, w_lo)
    tm = logits.shape[0]
    logits_t = jnp.concatenate([logits, jnp.zeros((tm, LANES - N_EXPERTS), F32)], axis=1).T[:N_EXPERTS]
    s = jax.nn.sigmoid(logits_t)
    work = s + br_ref[...]
    row = lax.broadcasted_iota(jnp.int32, work.shape, 0)
    s_sel = jnp.zeros(work.shape, F32)
    for _ in range(TOP_K):
        mx = jnp.max(work, axis=0, keepdims=True)
        first = jnp.min(jnp.where(work == mx, row, N_EXPERTS), axis=0, keepdims=True)
        hit = row == first
        s_sel = jnp.where(hit, s, s_sel)
        work = jnp.where(hit, -jnp.inf, work)
    gates_t = s_sel / jnp.sum(s_sel, axis=0, keepdims=True) * ROUTED_SCALE
    gates_ref[...] = jnp.concatenate([gates_t, jnp.zeros((LANES - N_EXPERTS, tm), F32)], axis=0).T[:, :N_EXPERTS]


def _outproj(x, nao, dfo, mlh, mo, mod_p, tokens_per_mod, layer, g2, outn, w_out_bf, w_router, b_router):
    T = x.shape[0]
    tm = TM_PROJ
    row = lambda i: (i, 0)
    const = lambda i: (0, 0)
    lay = lambda i: (layer, 0, 0)
    return pl.pallas_call(
        _outproj_kernel,
        grid=(T // tm,),
        in_specs=[pl.BlockSpec((tm, D_MODEL), row),
                  pl.BlockSpec((tm, 256), row), pl.BlockSpec((tm, 256), row),
                  pl.BlockSpec((tm, 512), row), pl.BlockSpec((tm, 512), row),
                  pl.BlockSpec((1, 6, D_MODEL), lambda i: ((i * tm) // tokens_per_mod, 0, 0)),
                  pl.BlockSpec((1, D_MODEL), const), pl.BlockSpec((1, 512), const),
                  pl.BlockSpec((None, D_MODEL, D_MODEL), lay),
                  pl.BlockSpec((None, D_MODEL, N_EXPERTS), lay), pl.BlockSpec((N_EXPERTS, 1), const)],
        out_specs=[pl.BlockSpec((tm, D_MODEL), row), pl.BlockSpec((tm, D_MODEL), row),
                   pl.BlockSpec((tm, N_EXPERTS), row)],
        out_shape=[jax.ShapeDtypeStruct((T, D_MODEL), F32), jax.ShapeDtypeStruct((T, D_MODEL), BF16),
                   jax.ShapeDtypeStruct((T, N_EXPERTS), F32)],
        compiler_params=_cparams(1),
        name="outproj_router",
    )(x, nao, dfo, mlh, mo, mod_p, g2, outn, w_out_bf, w_router, b_router)


def _dispatch_kernel(h_ref, gates_ref, xs_ref, cnt_ref, route_ref, *, n_tiles):
    i = pl.program_id(0)

    @pl.when(i == n_tiles)
    def _():
        xs_ref[...] = jnp.zeros(xs_ref.shape, BF16)

    @pl.when(i < n_tiles)
    def _():
        _dispatch_tile(h_ref, gates_ref, xs_ref, cnt_ref, route_ref)


def _dispatch_tile(h_ref, gates_ref, xs_ref, cnt_ref, route_ref):
    tt = h_ref.shape[0]
    n_slots = xs_ref.shape[0]
    gates = gates_ref[...]
    sel = jnp.where(gates > 0.0, 1.0, 0.0)
    selb = sel.astype(BF16)
    ti = lax.broadcasted_iota(jnp.int32, (tt, tt), 0)
    si = lax.broadcasted_iota(jnp.int32, (tt, tt), 1)
    earlier = jnp.where(si < ti, 1.0, 0.0).astype(BF16)
    pos = _dot(earlier, selb)
    cnt = jnp.sum(sel, axis=0, keepdims=True)
    seg = jnp.floor((cnt + (MOE_GRANULE - 1.0)) * (1.0 / MOE_GRANULE)) * MOE_GRANULE
    ei = lax.broadcasted_iota(jnp.int32, (N_EXPERTS, N_EXPERTS), 0)
    ej = lax.broadcasted_iota(jnp.int32, (N_EXPERTS, N_EXPERTS), 1)
    before = jnp.where(ei < ej, 1.0, 0.0).astype(BF16)
    off = _dot(jnp.broadcast_to(seg, (16, N_EXPERTS)).astype(BF16), before)[0:1]
    rank = _dot(selb, before)
    slot = off + pos
    lane = _lane_iota((tt, LANES))
    route = jnp.zeros((tt, LANES), F32)
    for k in range(TOP_K):
        hit = jnp.where(rank == k, sel, 0.0)
        has = jnp.sum(hit, axis=1, keepdims=True)
        slot_k = jnp.sum(hit * slot, axis=1, keepdims=True) + has - 1.0
        w_k = jnp.sum(hit * gates, axis=1, keepdims=True)
        route = jnp.where(lane == k, slot_k, jnp.where(lane == TOP_K + k, w_k, route))
    route_ref[...] = route
    cnt_ref[0] = cnt
    slots_t = route.T
    srow = lax.broadcasted_iota(jnp.int32, (n_slots, tt), 0).astype(F32)
    perm = jnp.zeros((n_slots, tt), F32)
    for k in range(TOP_K):
        perm = jnp.where(srow == slots_t[k:k + 1, :], 1.0, perm)
    xs_ref[...] = _dot(perm.astype(BF16), h_ref[...]).astype(BF16)


def _dispatch(h2, gates):
    T = h2.shape[0]
    nt = T // MOE_TILE
    tile = lambda i: (jnp.minimum(i, nt - 1), 0)
    return pl.pallas_call(
        functools.partial(_dispatch_kernel, n_tiles=nt),
        grid=(nt + 1,),
        in_specs=[pl.BlockSpec((MOE_TILE, D_MODEL), tile),
                  pl.BlockSpec((MOE_TILE, N_EXPERTS), tile)],
        out_specs=[pl.BlockSpec((MOE_SLOTS, D_MODEL), lambda i: (i, 0)),
                   pl.BlockSpec((1, 1, N_EXPERTS), lambda i: (jnp.minimum(i, nt - 1), 0, 0)),
                   pl.BlockSpec((MOE_TILE, LANES), tile)],
        out_shape=[jax.ShapeDtypeStruct(((nt + 1) * MOE_SLOTS, D_MODEL), BF16),
                   jax.ShapeDtypeStruct((nt, 1, N_EXPERTS), F32),
                   jax.ShapeDtypeStruct((T, LANES), F32)],
        compiler_params=_cparams(1),
        name="moe_dispatch",
    )(h2, gates)


def _expert_fused_kernel(te_ref, nact_ref, src_ref, dst_ref, slots_in, w1_ref, w3_ref, w2_ref,
                         slots_hbm, xbuf, ybuf, sem_in, sem_out, *, n_tiles):
    del slots_in
    xs_hbm = ys_hbm = slots_hbm
    i = pl.program_id(0)
    n_act = nact_ref[0]
    gr = MOE_ROWS // MOE_GRANULE
    spare0 = ys_hbm.shape[0] // MOE_GRANULE - MOE_SLOTS // MOE_GRANULE

    def rows(g):
        if isinstance(g, int):
            return pl.ds(g * MOE_GRANULE, MOE_GRANULE)
        return pl.ds(pl.multiple_of(g * MOE_GRANULE, MOE_GRANULE), MOE_GRANULE)

    zero_granule = MOE_SLOTS // MOE_GRANULE - 1

    def gather_start(tile, slot, dummy=False):
        for g in range(gr):
            s = src_ref[tile * gr + g]
            if dummy is not False:
                s = jnp.where(dummy, zero_granule, s)
            pltpu.make_async_copy(xs_hbm.at[rows(s)], xbuf.at[slot, rows(g)], sem_in.at[slot]).start()

    def gather_wait(slot):
        for g in range(gr):
            pltpu.make_async_copy(xs_hbm.at[rows(0)], xbuf.at[slot, rows(g)], sem_in.at[slot]).wait()

    def scatter_start(tile, slot):
        for g in range(gr):
            dd = dst_ref[tile * gr + g]
            dd = jnp.where(dd < 0, spare0 + slot * gr + g, dd)
            pltpu.make_async_copy(ybuf.at[slot, rows(g)], ys_hbm.at[rows(dd)], sem_out.at[slot]).start()

    def scatter_wait(slot):
        for g in range(gr):
            pltpu.make_async_copy(ybuf.at[slot, rows(g)], ys_hbm.at[rows(0)], sem_out.at[slot]).wait()

    slot = lax.rem(i, 2)

    @pl.when((i == 0) & (n_act > 0))
    def _():
        gather_start(0, 0)

    @pl.when(i < n_act)
    def _():
        gather_start(jnp.minimum(i + 1, n_act - 1), 1 - slot, dummy=i + 1 >= n_act)
        gather_wait(slot)

        @pl.when(i >= 2)
        def _():
            scatter_wait(slot)

        x = xbuf[slot]
        a = _dot(x, w1_ref[...].astype(BF16))
        b = _dot(x, w3_ref[...].astype(BF16))
        ybuf[slot] = _dot((_silu(a) * b).astype(BF16), w2_ref[...].astype(BF16)).astype(BF16)
        scatter_start(i, slot)

    @pl.when((i == n_tiles - 1) & (n_act > 0))
    def _():
        last_slot = lax.rem(n_act - 1, 2)
        gather_wait(1 - last_slot)
        scatter_wait(last_slot)

        @pl.when(n_act > 1)
        def _():
            scatter_wait(1 - last_slot)


def _expert_fused(tile_expert, n_active, src, dst, xs_tiles, layer, w1, w3, w2):
    n_tiles = src.shape[0] // (MOE_ROWS // MOE_GRANULE)
    wmap = lambda i, te, na, s, d: (layer, te[i], 0, 0)
    stage = pltpu.VMEM((2, MOE_ROWS, D_MODEL), BF16)
    assert MOE_SLOTS >= 2 * MOE_ROWS
    return pl.pallas_call(
        functools.partial(_expert_fused_kernel, n_tiles=n_tiles),
        grid_spec=pltpu.PrefetchScalarGridSpec(
            num_scalar_prefetch=4, grid=(n_tiles,),
            in_specs=[pl.BlockSpec(memory_space=pl.ANY),
                      pl.BlockSpec((None, None, D_MODEL, D_EXPERT), wmap),
                      pl.BlockSpec((None, None, D_MODEL, D_EXPERT), wmap),
                      pl.BlockSpec((None, None, D_EXPERT, D_MODEL), wmap)],
            out_specs=pl.BlockSpec(memory_space=pl.ANY),
            scratch_shapes=[stage, stage, pltpu.SemaphoreType.DMA((2,)), pltpu.SemaphoreType.DMA((2,))]),
        out_shape=jax.ShapeDtypeStruct(xs_tiles.shape, BF16),
        input_output_aliases={4: 0},
        compiler_params=_cparams(1),
        name="moe_expert_rows",
    )(tile_expert, n_active, src, dst, xs_tiles, w1, w3, w2)


def _combine_kernel(ys_ref, route_ref, h_ref, x1_ref, mod_ref, s1_ref, s3_ref, s2_ref, o_ref):
    route = route_ref[...]
    tt = route.shape[0]
    n_slots = ys_ref.shape[0]
    slot_lane = _lane_iota((tt, n_slots)).astype(F32)
    wp = jnp.zeros((tt, n_slots), F32)
    for k in range(TOP_K):
        wp = jnp.where(slot_lane == route[:, k:k + 1], route[:, TOP_K + k:TOP_K + k + 1], wp)
    routed = _dot(wp.astype(BF16), ys_ref[...])
    h = h_ref[...]
    a = _dot(h, s1_ref[...].astype(BF16))
    b = _dot(h, s3_ref[...].astype(BF16))
    shared = _dot((_silu(a) * b).astype(BF16), s2_ref[...].astype(BF16))
    o_ref[...] = x1_ref[...] + mod_ref[0, 5:6, :] * (routed + shared)


def _combine(ys_tiles, route, h2, x1, mod_p, tokens_per_mod, layer, s1, s3, s2):
    T = h2.shape[0]
    tt = MOE_TILE
    row = lambda i: (i, 0)
    wsh = lambda i: (layer, 0, 0)
    return pl.pallas_call(
        _combine_kernel,
        grid=(T // tt,),
        in_specs=[pl.BlockSpec((MOE_SLOTS, D_MODEL), row), pl.BlockSpec((tt, LANES), row),
                  pl.BlockSpec((tt, D_MODEL), row), pl.BlockSpec((tt, D_MODEL), row),
                  pl.BlockSpec((1, 6, D_MODEL), lambda i: ((i * tt) // tokens_per_mod, 0, 0)),
                  pl.BlockSpec((None, D_MODEL, D_EXPERT), wsh), pl.BlockSpec((None, D_MODEL, D_EXPERT), wsh),
                  pl.BlockSpec((None, D_EXPERT, D_MODEL), wsh)],
        out_specs=pl.BlockSpec((tt, D_MODEL), row),
        out_shape=jax.ShapeDtypeStruct((T, D_MODEL), F32),
        compiler_params=_cparams(1),
        name="moe_combine",
    )(ys_tiles, route, h2, x1, mod_p, s1, s3, s2)


def _excl_cumsum(a, axis):
    return jnp.cumsum(a, axis=axis) - a


def _moe_plan(cnt):
    nt = cnt.shape[0]
    gt = MOE_SLOTS // MOE_GRANULE
    gr = MOE_ROWS // MOE_GRANULE
    ng = _moe_sorted_granules(nt)
    g = (cnt + MOE_GRANULE - 1) // MOE_GRANULE
    loc = _excl_cumsum(g, 1)
    tot = jnp.sum(g, axis=0)
    grp = ((tot + gr - 1) // gr) * gr
    gs = _excl_cumsum(grp, 0)
    start = gs[None, :] + _excl_cumsum(g, 0)
    base = jnp.arange(nt, dtype=jnp.int32)[:, None] * gt + loc
    d = jnp.arange(ng, dtype=jnp.int32)[:, None, None]
    inseg = (start[None] <= d) & (d < (start + g)[None])
    local = jnp.sum(jnp.where(inseg, base[None] + d - start[None], 0), axis=(1, 2))
    real = jnp.any(inseg, axis=(1, 2))
    src = jnp.where(real, local, gt - 1)
    dst = jnp.where(real, local, -1)
    n_active = jnp.sum(grp) // gr
    first = jnp.minimum(jnp.arange(ng // gr, dtype=jnp.int32), jnp.maximum(n_active - 1, 0)) * gr
    tile_expert = jnp.sum((gs[None, :] <= first[:, None]).astype(jnp.int32), axis=1) - 1
    return (src.astype(jnp.int32), dst.astype(jnp.int32), tile_expert.astype(jnp.int32),
            n_active.astype(jnp.int32).reshape(1))


def _moe_sorted_granules(nt):
    gt_used = (TOP_K * MOE_TILE + N_EXPERTS * (MOE_GRANULE - 1)) // MOE_GRANULE
    gr = MOE_ROWS // MOE_GRANULE
    need = nt * gt_used + N_EXPERTS * (gr - 1) + 1
    return -(-need // gr) * gr


def _moe_sparse(h2, gates, x1, mod_p, tokens_per_mod, layer, w1, w3, w2, s1, s3, s2):
    xs_tiles, cnt, route = _dispatch(h2, gates)
    nt = cnt.shape[0]
    src, dst, tile_expert, n_active = _moe_plan(cnt.reshape(nt, N_EXPERTS).astype(jnp.int32))
    ys_tiles = _expert_fused(tile_expert, n_active, src, dst, xs_tiles, layer, w1, w3, w2)
    return _combine(ys_tiles, route, h2, x1, mod_p, tokens_per_mod, layer, s1, s3, s2)


def _group_matrix(group):
    i = np.arange(256)
    return jnp.asarray((i[:, None] // group) == (i[None, :] // group), BF16)


def _rope_tables(seq):
    t = np.arange(seq)
    lane = np.arange(256)
    j = lane % DF_DQK
    nf = DF_DQK // 4
    pos = np.where((j < DF_DQK // 2)[None, :], (t // GRID_W)[:, None], (t % GRID_W)[:, None]).astype(np.float32)
    inv = (1.0 / (ROPE_BASE ** (np.arange(nf, dtype=np.float32) / nf))).astype(np.float32)
    ang = pos * inv[j % nf][None, :]
    sign = np.where((lane % (2 * nf)) < nf, -1.0, 1.0).astype(np.float32)
    return jnp.asarray(np.cos(ang), F32), jnp.asarray(np.sin(ang) * sign[None, :], F32)


def _token_major(a):
    b, h, l, d = a.shape
    return a.transpose(0, 2, 1, 3).reshape(b, l, h * d)


def kernel(x_prompt, x_sample, cache_na_k, cache_na_v, cache_df_k, cache_df_v, state_ml_C, state_ml_n,
           state_ml_m, c, c_ctx, w_ada, b_ada, g_norm1, g_norm2, w_in, w_out, na_qn, na_kn, na_rpb,
           df_qn, df_kn, df_lam, df_subln, ml_gate_b, ml_outn, w_router, b_router, w_exp1, w_exp3, w_exp2,
           w_sh1, w_sh3, w_sh2):
    bc, lc, _ = x_prompt.shape
    bs, ls, _ = x_sample.shape
    past = cache_na_k.shape[3]

    cond = jnp.zeros((8, D_MODEL), F32).at[0].set(c_ctx).at[1:1 + bs].set(c)
    mod = _modulation(cond, w_ada, b_ada)
    g64 = _group_matrix(NA_DH)
    g32 = _group_matrix(DF_DQK)
    rope_tabs = _rope_tables(ls)
    w_in_bf = jnp.pad(w_in.astype(BF16), ((0, 0), (0, 0), (0, IN_COLS_PAD - IN_COLS)))
    w_out_bf = w_out.astype(BF16)
    na_blocks = _na_bias_blocks(na_rpb)
    moe_w = (w_exp1, w_exp3, w_exp2, w_sh1, w_sh3, w_sh2)

    y_p = x_prompt.reshape(bc * lc, D_MODEL)
    y_s = x_sample.reshape(bs * ls, D_MODEL)
    caches = states = None
    m_states = []
    for l in range(DEPTH):
        lam_init = 0.8 - 0.6 * math.exp(-0.3 * l)
        mod_l = mod[l].reshape(8, 6, D_MODEL)
        g1 = g_norm1[l][None, :]
        g2 = g_norm2[l][None, :]
        qn = jnp.tile(na_qn[l], NA_HEADS)[None, :]
        kn = jnp.tile(na_kn[l], NA_HEADS)[None, :]
        dqn = jnp.tile(df_qn[l], 2 * DF_HEADS)[None, :]
        dkn = jnp.tile(df_kn[l], 2 * DF_HEADS)[None, :]
        subln = jnp.tile(df_subln[l], DF_HEADS)[None, :]
        outn = jnp.tile(ml_outn[l], ML_HEADS)[None, :]
        gate_b = jnp.pad(ml_gate_b[l].reshape(-1), (0, LANES - 4 * ML_HEADS))[None, :]
        br = b_router[l][:, None]
        proj_w = (l, g1, w_in_bf, qn, kn, dqn, dkn, gate_b, g64, g32)

        mod_c = mod_l[0:1]
        tpm = bc * lc
        res = _inproj(y_p, mod_c, tpm, *proj_w, None, cache_seq=lc, cache_prev=caches)
        naq, nak, nav, dfq, dfk, dfv, mq, mk, mv, mo, mg = res[:11]
        caches = tuple(res[11:])
        nao, dfo = _ctx_attention(naq, nak, nav, dfq, dfk, dfv, df_lam[l], subln, lc, lam_init)
        mlh, c_n, n_n, m_n = _mlstm(mq, mk, mv, mg, lc, emit_state=True, layer=l, state_prev=states)
        states = (c_n, n_n)
        x1, h2, gates = _outproj(y_p, nao, dfo, mlh, mo, mod_c, tpm, l, g2, outn, w_out_bf, w_router, br)
        y_p = _moe_sparse(h2, gates, x1, mod_c, tpm, l, *moe_w)
        m_states.append(m_n[:, :, 0, 0].reshape(bc, 2, ML_HEADS))

        mod_s = mod_l[1:1 + bs]
        naq, nak, nav, dfq, dfk, dfv, mq, mk, mv, mo, mg = _inproj(y_s, mod_s, ls, *proj_w, rope_tabs)
        kc_na = _token_major(cache_na_k[:, l])
        vc_na = _token_major(cache_na_v[:, l])
        kc_df = cache_df_k[:, l].transpose(0, 3, 1, 2, 4).reshape(bs, past, 256)
        vc_df = _token_major(cache_df_v[:, l])
        bias_tabs = _na_bias_tables(na_blocks[l * NA_HEADS:(l + 1) * NA_HEADS], ls // GRID_W)
        nao = _lat_na(naq, nak, nav, kc_na, vc_na, bias_tabs, ls)
        dfo = _lat_df(dfq, dfk, dfv, kc_df, vc_df, df_lam[l], subln, ls, lam_init)
        n0 = jnp.broadcast_to(state_ml_n[:, l][..., None], (bs, 2, ML_HEADS, ML_DH, ML_DH))
        c0 = jnp.concatenate([state_ml_C[:, l], n0], axis=-1)
        c0 = c0.reshape(bs, 2 * ML_HEADS, ML_DH, 2 * ML_DH)
        m0 = jnp.broadcast_to(state_ml_m[:, l].reshape(bs, 2 * ML_HEADS, 1, 1), (bs, 2 * ML_HEADS, 8, LANES))
        (mlh,) = _mlstm(mq, mk, mv, mg, ls, init=(c0, m0))
        x1, h2, gates = _outproj(y_s, nao, dfo, mlh, mo, mod_s, ls, l, g2, outn, w_out_bf, w_router, br)
        y_s = _moe_sparse(h2, gates, x1, mod_s, ls, l, *moe_w)

    new_ml_c = states[0].reshape(bc, DEPTH, 2, ML_HEADS, ML_DH, ML_DH)
    new_ml_n = states[1].reshape(bc, DEPTH, 2, ML_HEADS, ML_DH)
    return (y_p.reshape(bc, lc, D_MODEL), y_s.reshape(bs, ls, D_MODEL), *caches,
            new_ml_c, new_ml_n, jnp.stack(m_states, axis=1))
```

```python
import functools
import math

import numpy as np
import jax
import jax.numpy as jnp
from jax import lax
from jax.experimental import pallas as pl
from jax.experimental.pallas import tpu as pltpu

F32 = jnp.float32
BF16 = jnp.bfloat16

D_MODEL = 1024
DEPTH = 2
GRID_W = 64
NA_HEADS = 4
NA_DH = 64
NA_WIN_R = 8
NA_WIN_C = 16
DF_HEADS = 4
DF_DV = 64
DF_DQK = 32
ML_HEADS = 4
ML_DH = 128
ML_CHUNK = 128
N_EXPERTS = 32
TOP_K = 4
D_EXPERT = 256
ROUTED_SCALE = 2.5
ROPE_BASE = 10000.0
EPS = 1e-6

LANES = 128
IN_COLS = 3600
IN_COLS_PAD = 3712
GATE_COL0 = 3584
MASK_VALUE = -1e30
VMEM_LIMIT = 56 * 1024 * 1024

TM_PROJ = 512
MOE_TILE = 256
MOE_GRANULE = 16
MOE_SLOTS = TOP_K * MOE_TILE + N_EXPERTS * MOE_GRANULE
MOE_ROWS = 512
Q_BLOCK_DF = 256
CTX_ATTN_BATCHES = 2
DF_SCORE_LEAD_CTX = 1
DF_SCORE_LEAD_LAT = 3
LOOP_UNROLL = 2
NA_Q_ROWS = 4
NA_K_ROWS = NA_Q_ROWS + NA_WIN_R
LOG2E = math.log2(math.e)


def _cparams(n_axes):
    return pltpu.CompilerParams(dimension_semantics=("arbitrary",) * n_axes,
                                vmem_limit_bytes=VMEM_LIMIT)


def _silu(x):
    return x * jax.nn.sigmoid(x)


def _dot(a, b):
    return jnp.dot(a, b, preferred_element_type=F32)


def _dot_nt(a, b):
    return lax.dot_general(a, b, (((1,), (1,)), ((), ())), preferred_element_type=F32)


def _dot_tn(a, b):
    return lax.dot_general(a, b, (((0,), (0,)), ((), ())), preferred_element_type=F32)


def _lane_iota(shape):
    return lax.broadcasted_iota(jnp.int32, shape, 1)


def _mod_kernel(cond_ref, w_ref, b_ref, o_ref):
    s = _silu(cond_ref[...]).astype(BF16)
    o_ref[0] = _dot(s, w_ref[0].astype(BF16)) + b_ref[0]


def _modulation(cond, w_ada, b_ada):
    tn = 3072
    n = 6 * D_MODEL
    return pl.pallas_call(
        _mod_kernel,
        grid=(DEPTH, n // tn),
        in_specs=[pl.BlockSpec((8, D_MODEL), lambda l, j: (0, 0)),
                  pl.BlockSpec((1, D_MODEL, tn), lambda l, j: (l, 0, j)),
                  pl.BlockSpec((1, 1, tn), lambda l, j: (l, 0, j))],
        out_specs=pl.BlockSpec((1, 8, tn), lambda l, j: (l, 0, j)),
        out_shape=jax.ShapeDtypeStruct((DEPTH, 8, n), F32),
        compiler_params=_cparams(2),
        name="adaln_mod",
    )(cond, w_ada, b_ada.reshape(DEPTH, 1, n))


def _group_rms(v, gmat_ref, group):
    ss = _dot((v * v).astype(BF16), gmat_ref[...])
    return v * lax.rsqrt(ss * (1.0 / group) + EPS)


def _rope(v, cos, sin_signed):
    first = (_lane_iota(v.shape) % 16) < 8
    partner = jnp.where(first, pltpu.roll(v, v.shape[1] - 8, 1), pltpu.roll(v, 8, 1))
    return v * cos + partner * sin_signed


def _inproj_kernel(*refs, rope, cache_seq, cache_first):
    (x_ref, mod_ref, g1_ref, w_ref, qn_ref, kn_ref, dqn_ref, dkn_ref, gb_ref, g64_ref, g32_ref) = refs[:11]
    pos = 11
    if rope:
        cos_ref, sin_ref = refs[pos:pos + 2]
        pos += 2
    if cache_seq and not cache_first:
        pos += 4
    (naq_ref, nak_ref, nav_ref, dfq_ref, dfk_ref, dfv_ref,
     mq_ref, mk_ref, mv_ref, mo_ref, mg_ref) = refs[pos:pos + 11]
    pos += 11
    x = x_ref[...]
    y = x * lax.rsqrt(jnp.mean(x * x, axis=-1, keepdims=True) + EPS) * g1_ref[...]
    h = (y * (1.0 + mod_ref[0, 1:2, :]) + mod_ref[0, 0:1, :]).astype(BF16)

    def proj(a, b):
        return _dot(h, w_ref[:, a:b])

    naq_ref[...] = _group_rms(proj(0, 256), g64_ref, NA_DH) * qn_ref[...]
    nak = _group_rms(proj(256, 512), g64_ref, NA_DH) * kn_ref[...]
    nav = proj(512, 768)
    dfq = _group_rms(proj(768, 1024), g32_ref, DF_DQK) * dqn_ref[...]
    dfk = _group_rms(proj(1024, 1280), g32_ref, DF_DQK) * dkn_ref[...]
    dfv = proj(1280, 1536)
    if rope:
        dfq = _rope(dfq, cos_ref[...], sin_ref[...])
        dfk = _rope(dfk, cos_ref[...], sin_ref[...])
    nak_ref[...] = nak
    nav_ref[...] = nav
    dfq_ref[...] = dfq
    dfk_ref[...] = dfk
    dfv_ref[...] = dfv
    if cache_seq:
        nk_hm, nv_hm, dk_hm, dv_hm = refs[pos:pos + 4]

        def put(ref, bi, idx, val):
            if cache_first:
                for dl in range(DEPTH):
                    ref[(bi, dl) + idx] = val
            else:
                ref[(bi,) + idx] = val

        for bi in range(x.shape[0] // cache_seq):
            rs = slice(bi * cache_seq, (bi + 1) * cache_seq)
            for hd in range(NA_HEADS):
                put(nk_hm, bi, (hd,), nak[rs, NA_DH * hd:NA_DH * (hd + 1)])
                put(nv_hm, bi, (hd,), nav[rs, NA_DH * hd:NA_DH * (hd + 1)])
                put(dv_hm, bi, (hd,), dfv[rs, DF_DV * hd:DF_DV * (hd + 1)])
                for a in range(2):
                    c0 = (2 * hd + a) * DF_DQK
                    put(dk_hm, bi, (hd, a), dfk[rs, c0:c0 + DF_DQK])
    mq_ref[...] = proj(1536, 2048)
    mk_ref[...] = proj(2048, 2560) * (ML_DH ** -0.5)
    mv_ref[...] = proj(2560, 3072)
    mo_ref[...] = proj(3072, 3584)
    g = proj(GATE_COL0, IN_COLS_PAD) + gb_ref[...]
    lane = _lane_iota(g.shape)
    is_forget = ((lane // ML_HEADS) % 2 == 1) & (lane < 4 * ML_HEADS)
    log_sig = jnp.minimum(g, 0.0) - jnp.log1p(jnp.exp(-jnp.abs(g)))
    mg_ref[...] = jnp.where(is_forget, log_sig, g)


def _inproj(x, mod_p, tokens_per_mod, layer, g1, w_in_bf, qn, kn, dqn, dkn, gate_b, g64, g32, rope_tabs,
            cache_seq=0, cache_prev=None):
    T = x.shape[0]
    tm = TM_PROJ
    rope = rope_tabs is not None
    row = lambda i: (i, 0)
    const = lambda i: (0, 0)
    in_specs = [pl.BlockSpec((tm, D_MODEL), row),
                pl.BlockSpec((1, 6, D_MODEL), lambda i: ((i * tm) // tokens_per_mod, 0, 0)),
                pl.BlockSpec((1, D_MODEL), const),
                pl.BlockSpec((None, D_MODEL, IN_COLS_PAD), lambda i: (layer, 0, 0)),
                pl.BlockSpec((1, 256), const), pl.BlockSpec((1, 256), const),
                pl.BlockSpec((1, 256), const), pl.BlockSpec((1, 256), const),
                pl.BlockSpec((1, LANES), const),
                pl.BlockSpec((256, 256), const), pl.BlockSpec((256, 256), const)]
    args = [x, mod_p, g1, w_in_bf, qn, kn, dqn, dkn, gate_b, g64, g32]
    if rope:
        seq = rope_tabs[0].shape[0]
        tab = lambda i: (i % (seq // tm), 0)
        in_specs += [pl.BlockSpec((tm, 256), tab), pl.BlockSpec((tm, 256), tab)]
        args += list(rope_tabs)
    widths = [256] * 6 + [512] * 4 + [LANES]
    out_specs = [pl.BlockSpec((tm, w), row) for w in widths]
    out_shape = [jax.ShapeDtypeStruct((T, w), F32) for w in widths]
    aliases = {}
    if cache_seq:
        nb = tm // cache_seq
        hm = (NA_HEADS, cache_seq, NA_DH)
        hm_df = (DF_HEADS, 2, cache_seq, DF_DQK)
        for k, tail in enumerate((hm, hm, hm_df, hm)):
            zeros = (0,) * len(tail)
            if cache_prev is None:
                out_specs.append(pl.BlockSpec((nb, DEPTH) + tail, lambda i, z=zeros: (i, 0) + z))
            else:
                out_specs.append(pl.BlockSpec((nb, None) + tail, lambda i, z=zeros: (i, layer) + z))
                aliases[len(args)] = len(widths) + k
                in_specs.append(pl.BlockSpec(memory_space=pl.ANY))
                args.append(cache_prev[k])
            out_shape.append(jax.ShapeDtypeStruct((T // cache_seq, DEPTH) + tail, F32))
    return pl.pallas_call(
        functools.partial(_inproj_kernel, rope=rope, cache_seq=cache_seq, cache_first=cache_prev is None),
        grid=(T // tm,),
        in_specs=in_specs,
        out_specs=out_specs,
        out_shape=out_shape,
        input_output_aliases=aliases,
        compiler_params=_cparams(1),
        name="inproj_rope" if rope else "inproj",
    )(*args)


def _df_lambda(lam_ref, lam_init):
    v = lam_ref[...]
    a = jnp.sum(v[0:1] * v[1:2], axis=1, keepdims=True)
    b = jnp.sum(v[2:3] * v[3:4], axis=1, keepdims=True)
    return jnp.exp(a) - jnp.exp(b) + lam_init


def _softmax_parts(scores):
    es = _softmax_exps(scores)
    tot = functools.reduce(lambda a, b: a + b, [jnp.sum(e, axis=-1, keepdims=True) for e in es])
    return [e.astype(BF16) for e in es], 1.0 / tot


def _softmax_exps(scores):
    m = functools.reduce(jnp.maximum, [jnp.max(s, axis=-1, keepdims=True) for s in scores])
    return [jnp.exp2(s - m) for s in scores]


def _attend_masked_heads(tiles, lead=1):
    lane = _lane_iota(tiles[0][0].shape)
    v_aug = [[jnp.concatenate([v, jnp.ones(v.shape, BF16)], axis=1) for v in v_segs] for _, _, v_segs, _ in tiles]
    n_heads = 2 * len(tiles)

    def scores(c):
        qp, k_segs, _, bias = tiles[c // 2]
        qm = jnp.where(lane // NA_DH == c % 2, qp, 0.0).astype(BF16)
        s = [_dot_nt(qm, k) for k in k_segs]
        if bias is not None:
            s[0] = s[0] + bias[c % 2]
        return s

    outs = []
    pending = [scores(c) for c in range(min(lead, n_heads))]
    for c in range(n_heads):
        if c + lead < n_heads:
            pending.append(scores(c + lead))
        es = _softmax_exps(pending.pop(0))
        o = functools.reduce(lambda x, y: x + y, [_dot(e.astype(BF16), v) for e, v in zip(es, v_aug[c // 2])])
        outs.append(o[:, :LANES] / o[:, LANES:])
    return [jnp.where(lane < NA_DH, outs[2 * t], outs[2 * t + 1]) for t in range(len(tiles))]


def _subln(o, subln_row, lam_init):
    lane = _lane_iota(o.shape)
    sq = o * o
    s0 = jnp.sum(jnp.where(lane < DF_DV, sq, 0.0), axis=-1, keepdims=True)
    s1 = jnp.sum(jnp.where(lane >= DF_DV, sq, 0.0), axis=-1, keepdims=True)
    ms = jnp.where(lane < DF_DV, s0, s1) * (1.0 / DF_DV)
    return o * lax.rsqrt(ms + EPS) * subln_row * (1.0 - lam_init)


def _diff_attn(q, k_segs, v_segs, lam, lead):
    n_maps = 2 * DF_HEADS
    per_tile = LANES // DF_DQK
    lane = _lane_iota((q.shape[0], LANES))
    tiles = []
    for t in range(n_maps // per_tile):
        sl = slice(LANES * t, LANES * (t + 1))
        tiles.append((q[:, sl] * (DF_DQK ** -0.5 * LOG2E),
                      [k[:, sl].astype(BF16) for k in k_segs], [v[:, sl].astype(BF16) for v in v_segs]))

    def scores(c):
        qp, kps, _ = tiles[c // per_tile]
        qm = jnp.where(lane // DF_DQK == c % per_tile, qp, 0.0).astype(BF16)
        return [_dot_nt(qm, kp) for kp in kps]

    def head_out(m0, m1, vps):
        (e0, r0), (e1, r1) = m0, m1
        c0 = r0.astype(BF16)
        c1 = (-(lam * r1)).astype(BF16)
        return functools.reduce(lambda x, y: x + y,
                                [_dot(x0 * c0 + x1 * c1, vp) for x0, x1, vp in zip(e0, e1, vps)])

    heads = []
    maps = []
    pending = [scores(c) for c in range(lead)]
    for c in range(n_maps):
        if c + lead < n_maps:
            pending.append(scores(c + lead))
        maps.append(_softmax_parts(pending.pop(0)))
        if c % 2 == 1:
            heads.append(head_out(maps[c - 1], maps[c], tiles[c // per_tile][2]))
    return [jnp.where(lane < DF_DV, heads[2 * t], heads[2 * t + 1]) for t in range(len(tiles))]


def _ctx_attn_kernel(naq_ref, nak_ref, nav_ref, dfq_ref, dfk_ref, dfv_ref, lam_ref, subln_ref,
                     nao_ref, dfo_ref, *, lam_init, seq):
    lam = _df_lambda(lam_ref, lam_init)
    for bi in range(naq_ref.shape[0] // seq):
        rs = slice(bi * seq, (bi + 1) * seq)
        tiles = []
        for t in range(2):
            sl = slice(LANES * t, LANES * (t + 1))
            tiles.append((naq_ref[rs, sl] * (NA_DH ** -0.5 * LOG2E), [nak_ref[rs, sl].astype(BF16)],
                          [nav_ref[rs, sl].astype(BF16)], None))
        for t, o in enumerate(_attend_masked_heads(tiles, lead=0)):
            nao_ref[rs, LANES * t:LANES * (t + 1)] = o
        outs = _diff_attn(dfq_ref[rs, :], [dfk_ref[rs, :]], [dfv_ref[rs, :]], lam, DF_SCORE_LEAD_CTX)
        for t, o in enumerate(outs):
            sl = slice(LANES * t, LANES * (t + 1))
            dfo_ref[rs, sl] = _subln(o, subln_ref[:, sl], lam_init)


def _ctx_attention(naq, nak, nav, dfq, dfk, dfv, df_lam, subln, seq, lam_init):
    T = naq.shape[0]
    rows = CTX_ATTN_BATCHES * seq
    blk = pl.BlockSpec((rows, 256), lambda b: (b, 0))
    return pl.pallas_call(
        functools.partial(_ctx_attn_kernel, lam_init=lam_init, seq=seq),
        grid=(T // rows,),
        in_specs=[blk] * 6 + [pl.BlockSpec((4, DF_DQK), lambda b: (0, 0)),
                              pl.BlockSpec((1, 256), lambda b: (0, 0))],
        out_specs=[blk, blk],
        out_shape=[jax.ShapeDtypeStruct((T, 256), F32)] * 2,
        compiler_params=_cparams(1),
        name="ctx_attention",
    )(naq, nak, nav, dfq, dfk, dfv, df_lam, subln)


def _na_block_geometry(rows):
    last_r0 = rows - NA_Q_ROWS
    return ((0, 0), (NA_Q_ROWS, 0), (last_r0, rows - NA_K_ROWS))


def _na_bias_kernel(rpb_ref, o_ref, *, rows):
    hd = pl.program_id(0)
    shape = (GRID_W, GRID_W)
    qi = lax.broadcasted_iota(jnp.int32, shape, 0)
    kc = lax.broadcasted_iota(jnp.int32, shape, 1)
    cs = jnp.clip(qi - NA_WIN_C // 2, 0, GRID_W - NA_WIN_C)
    inwin = (kc >= cs) & (kc < cs + NA_WIN_C)
    off = jnp.where(inwin, kc - qi + NA_WIN_C - 1, -1)
    n_c = 2 * NA_WIN_C - 1
    n_r = 2 * NA_WIN_R - 1
    blocks = []
    for ro in range(n_r):
        val = jnp.full(shape, MASK_VALUE, F32)
        for d in range(n_c):
            val = jnp.where(off == d, rpb_ref[hd * (n_r * n_c) + ro * n_c + d], val)
        blocks.append(val * LOG2E)
    masked = jnp.full(shape, MASK_VALUE * LOG2E, F32)
    for v, (r0, k0) in enumerate(_na_block_geometry(rows)):
        for rq in range(NA_Q_ROWS):
            r = r0 + rq
            rs = min(max(r - NA_WIN_R // 2, 0), rows - NA_WIN_R)
            for j, kr in enumerate(range(k0, k0 + NA_K_ROWS)):
                blk = blocks[kr - r + NA_WIN_R - 1] if rs <= kr < rs + NA_WIN_R else masked
                o_ref[0, v, GRID_W * rq:GRID_W * (rq + 1), GRID_W * j:GRID_W * (j + 1)] = blk


def _na_bias_tables(rpb, rows):
    n_lh = rpb.shape[0] * rpb.shape[1]
    blk = (3, NA_Q_ROWS * GRID_W, NA_K_ROWS * GRID_W)
    tabs = pl.pallas_call(
        functools.partial(_na_bias_kernel, rows=rows),
        grid=(n_lh,),
        in_specs=[pl.BlockSpec(memory_space=pltpu.SMEM)],
        out_specs=pl.BlockSpec((1,) + blk, lambda h: (h, 0, 0, 0)),
        out_shape=jax.ShapeDtypeStruct((n_lh,) + blk, F32),
        compiler_params=_cparams(1),
        name="na_bias",
    )(rpb.reshape(-1))
    return tabs.reshape(rpb.shape[:2] + blk)


def _lat_na_kernel(q_ref, k_ref, v_ref, kc_ref, vc_ref, bias_ref, o_ref, *, rows):
    i = pl.program_id(1)
    k0 = jnp.clip(i * NA_Q_ROWS - NA_WIN_R // 2, 0, rows - NA_K_ROWS)
    krows = pl.ds(pl.multiple_of(k0 * GRID_W, GRID_W), NA_K_ROWS * GRID_W)
    q = q_ref[...] * (NA_DH ** -0.5 * LOG2E)
    tiles = []
    for t in range(2):
        sl = slice(LANES * t, LANES * (t + 1))
        tiles.append((q[:, sl],
                      [k_ref[krows, sl].astype(BF16), kc_ref[0, :, sl].astype(BF16)],
                      [v_ref[krows, sl].astype(BF16), vc_ref[0, :, sl].astype(BF16)],
                      [bias_ref[2 * t], bias_ref[2 * t + 1]]))
    for t, o in enumerate(_attend_masked_heads(tiles)):
        o_ref[:, LANES * t:LANES * (t + 1)] = o


def _lat_na(q, k, v, kc, vc, bias_tabs, layer, seq):
    T = q.shape[0]
    past = kc.shape[1]
    rows = seq // GRID_W
    nq = rows // NA_Q_ROWS
    qrows = NA_Q_ROWS * GRID_W
    qblk = pl.BlockSpec((qrows, 256), lambda b, i: (b * nq + i, 0))
    kblk = pl.BlockSpec((seq, 256), lambda b, i: (b, 0))
    cblk = pl.BlockSpec((1, past, 256), lambda b, i: (b, 0, 0))
    variant = lambda b, i: (layer, 0, jnp.where(i == 0, 0, jnp.where(i == nq - 1, 2, 1)), 0, 0)
    return pl.pallas_call(
        functools.partial(_lat_na_kernel, rows=rows),
        grid=(T // seq, nq),
        in_specs=[qblk, kblk, kblk, cblk, cblk,
                  pl.BlockSpec((None, NA_HEADS, None, qrows, NA_K_ROWS * GRID_W), variant)],
        out_specs=qblk,
        out_shape=jax.ShapeDtypeStruct((T, 256), F32),
        compiler_params=_cparams(2),
        name="lat_na",
    )(q, k, v, kc, vc, bias_tabs)


def _lat_df_kernel(q_ref, k_ref, v_ref, kc_ref, vc_ref, lam_ref, subln_ref, o_ref, *, lam_init):
    lam = _df_lambda(lam_ref, lam_init)
    outs = _diff_attn(q_ref[...], [k_ref[...], kc_ref[0]], [v_ref[...], vc_ref[0]], lam, DF_SCORE_LEAD_LAT)
    for t, o in enumerate(outs):
        sl = slice(LANES * t, LANES * (t + 1))
        o_ref[:, sl] = _subln(o, subln_ref[:, sl], lam_init)


def _lat_df(q, k, v, kc, vc, df_lam, subln, seq, lam_init):
    T = q.shape[0]
    past = kc.shape[1]
    nq = seq // Q_BLOCK_DF
    qblk = pl.BlockSpec((Q_BLOCK_DF, 256), lambda b, i: (b * nq + i, 0))
    kblk = pl.BlockSpec((seq, 256), lambda b, i: (b, 0))
    cblk = pl.BlockSpec((1, past, 256), lambda b, i: (b, 0, 0))
    return pl.pallas_call(
        functools.partial(_lat_df_kernel, lam_init=lam_init),
        grid=(T // seq, nq),
        in_specs=[qblk, kblk, kblk, cblk, cblk,
                  pl.BlockSpec((4, DF_DQK), lambda b, i: (0, 0)),
                  pl.BlockSpec((1, 256), lambda b, i: (0, 0))],
        out_specs=qblk,
        out_shape=jax.ShapeDtypeStruct((T, 256), F32),
        compiler_params=_cparams(2),
        name="lat_df",
    )(q, k, v, kc, vc, df_lam, subln)


def _mlstm_kernel(*refs, nc, has_init, emit_state, state_first):
    refs = list(refs)
    q_ref, k_ref, v_ref, g_ref = refs[:4]
    pos = 4
    if has_init:
        c0_ref, m0_ref = refs[pos:pos + 2]
        pos += 2
    if emit_state and not state_first:
        pos += 2
    h_ref = refs[pos]
    pos += 1
    if emit_state:
        cn_ref, nn_ref, mn_ref = refs[pos:pos + 3]
        pos += 3
    caug_sc, m_sc = refs[pos:pos + 2]

    if has_init:
        caug_sc[...] = c0_ref[0]
        m_sc[...] = m0_ref[0]
    else:
        caug_sc[...] = jnp.zeros(caug_sc.shape, F32)
        m_sc[...] = jnp.zeros(m_sc.shape, F32)
    static_chunks = nc <= LOOP_UNROLL
    if not static_chunks:
        h_ref[...] = jnp.zeros(h_ref.shape, F32)
    seen = set()

    ti = lax.broadcasted_iota(jnp.int32, (ML_CHUNK, ML_CHUNK), 0)
    si = lax.broadcasted_iota(jnp.int32, (ML_CHUNK, ML_CHUNK), 1)
    masks = (si <= ti, si >= ti)
    ones_blk = jnp.ones((ML_CHUNK, LANES), BF16)
    full = (ML_CHUNK, ML_CHUNK)

    def body(c, carry):
        for d in range(2):
            ci = c if d == 0 else nc - 1 - c
            if static_chunks:
                rows = pl.ds(ci * ML_CHUNK, ML_CHUNK)
                accumulate = ci in seen
                seen.add(ci)
            else:
                rows = pl.ds(pl.multiple_of(ci * ML_CHUNK, ML_CHUNK), ML_CHUNK)
                accumulate = True
            mask = masks[d]
            tri = jnp.where(mask, 1.0, 0.0).astype(BF16)
            gblk = g_ref[rows, :]
            hi = gblk.astype(BF16)
            lo = (gblk - hi.astype(F32)).astype(BF16)
            bc = _dot(tri, hi) + _dot(tri, lo)
            g_t = gblk.T
            bc_t = bc.T
            last = ML_CHUNK - 1 if d == 0 else 0
            def qk_scores(hd, rows=rows):
                return _dot_nt(q_ref[rows, ML_DH * hd:ML_DH * (hd + 1)].astype(BF16),
                               k_ref[rows, ML_DH * hd:ML_DH * (hd + 1)].astype(BF16))

            qk = None if static_chunks else [qk_scores(hd) for hd in range(ML_HEADS)]
            for hd in range(ML_HEADS):
                ch_i = 2 * ML_HEADS * d + hd
                ch_f = ch_i + ML_HEADS
                idx = ML_HEADS * d + hd
                hs = slice(ML_DH * hd, ML_DH * (hd + 1))
                b_col = bc[:, ch_f:ch_f + 1]
                a_col = gblk[:, ch_i:ch_i + 1] - b_col
                a_msk = jnp.where(mask, g_t[ch_i:ch_i + 1, :] - bc_t[ch_f:ch_f + 1, :], -jnp.inf)
                m_col = jnp.max(a_msk, axis=1, keepdims=True)
                m_rep = jnp.broadcast_to(m_col, full)
                b_rep = jnp.broadcast_to(b_col, full)
                a_rep = jnp.broadcast_to(a_col, full)
                m_last = m_col[last:last + 1, :]
                b_last = b_col[last:last + 1, :]
                qh = q_ref[rows, hs].astype(BF16)
                kf = k_ref[rows, hs]
                vaug = jnp.concatenate([v_ref[rows, hs].astype(BF16), ones_blk], axis=1)
                s0 = (qk_scores(hd) if qk is None else qk[hd]) * jnp.exp(a_msk - m_rep)
                sv0 = _dot(s0.astype(BF16), vaug)
                u0 = _dot_tn((jnp.exp(a_rep - m_last) * kf).astype(BF16), vaug)
                m_prev = m_sc[idx][0:1, 0:1]
                g_rep = jnp.maximum(m_prev, m_rep)
                w0 = jnp.exp(m_prev - g_rep)
                cc = jnp.exp(m_rep - g_rep)
                caug = caug_sc[idx]
                qc = _dot(qh, caug.astype(BF16))
                num = w0 * qc[:, :ML_DH] + cc * sv0[:, :ML_DH]
                den = w0 * qc[:, ML_DH:] + cc * sv0[:, ML_DH:]
                h_dir = num / jnp.maximum(jnp.abs(den), jnp.exp(-b_rep - g_rep))
                if accumulate:
                    h_ref[rows, hs] += h_dir
                else:
                    h_ref[rows, hs] = h_dir
                g_last = jnp.maximum(m_prev, m_last)
                caug_sc[idx] = jnp.exp(m_prev - g_last) * caug + jnp.exp(m_last - g_last) * u0
                m_sc[idx] = jnp.broadcast_to(b_last + g_last, m_sc.shape[1:])
        return carry

    if static_chunks:
        for c in range(nc):
            body(c, 0)
    else:
        lax.fori_loop(0, nc, body, 0, unroll=LOOP_UNROLL)
    if emit_state:
        slots = [(0, dl) for dl in range(DEPTH)] if state_first else [(0,)]
        for slot in slots:
            cn_ref[slot] = caug_sc[:, :, :ML_DH]
        for idx in range(2 * ML_HEADS):
            n_row = caug_sc[idx][:, ML_DH:].T[0:1, :]
            for slot in slots:
                nn_ref[slot + (slice(idx, idx + 1), slice(None))] = n_row
        mn_ref[0] = m_sc[...]


def _mlstm(q, k, v, g, seq, init=None, emit_state=False, layer=0, state_prev=None):
    T = q.shape[0]
    n_chain = 2 * ML_HEADS
    blk = pl.BlockSpec((seq, 512), lambda b: (b, 0))
    gblk = pl.BlockSpec((seq, LANES), lambda b: (b, 0))
    cblk = pl.BlockSpec((1, n_chain, ML_DH, 2 * ML_DH), lambda b: (b, 0, 0, 0))
    mblk = pl.BlockSpec((1, n_chain, 8, LANES), lambda b: (b, 0, 0, 0))
    in_specs = [blk, blk, blk, gblk]
    args = [q, k, v, g]
    if init is not None:
        in_specs += [cblk, mblk]
        args += list(init)
    out_specs = [blk]
    out_shape = [jax.ShapeDtypeStruct((T, 512), F32)]
    nb = T // seq
    aliases = {}
    if emit_state:
        if state_prev is None:
            out_specs += [pl.BlockSpec((1, DEPTH, n_chain, ML_DH, ML_DH), lambda b: (b, 0, 0, 0, 0)),
                          pl.BlockSpec((1, DEPTH, n_chain, ML_DH), lambda b: (b, 0, 0, 0)), mblk]
        else:
            out_specs += [pl.BlockSpec((1, None, n_chain, ML_DH, ML_DH), lambda b: (b, layer, 0, 0, 0)),
                          pl.BlockSpec((1, None, n_chain, ML_DH), lambda b: (b, layer, 0, 0)), mblk]
            aliases = {len(args): 1, len(args) + 1: 2}
            in_specs += [pl.BlockSpec(memory_space=pl.ANY)] * 2
            args += list(state_prev)
        out_shape += [jax.ShapeDtypeStruct((nb, DEPTH, n_chain, ML_DH, ML_DH), F32),
                      jax.ShapeDtypeStruct((nb, DEPTH, n_chain, ML_DH), F32),
                      jax.ShapeDtypeStruct((nb, n_chain, 8, LANES), F32)]
    return pl.pallas_call(
        functools.partial(_mlstm_kernel, nc=seq // ML_CHUNK, has_init=init is not None,
                          emit_state=emit_state, state_first=state_prev is None),
        grid=(nb,),
        in_specs=in_specs,
        out_specs=out_specs,
        out_shape=out_shape,
        input_output_aliases=aliases,
        scratch_shapes=[pltpu.VMEM((n_chain, ML_DH, 2 * ML_DH), F32),
                        pltpu.VMEM((n_chain, 8, LANES), F32)],
        compiler_params=_cparams(1),
        name="mlstm_state" if emit_state else "mlstm",
    )(*args)


def _outproj_kernel(x_ref, nao_ref, dfo_ref, mlh_ref, mo_ref, mod_ref, g2_ref, outn_ref, wout_ref,
                    wr_ref, br_ref, x1_ref, h2_ref, gates_ref):
    mlh = mlh_ref[...]
    parts = []
    for hd in range(ML_HEADS):
        seg = mlh[:, ML_DH * hd:ML_DH * (hd + 1)]
        parts.append(seg * lax.rsqrt(jnp.mean(seg * seg, axis=-1, keepdims=True) + EPS))
    ml_o = jnp.concatenate(parts, axis=1) * outn_ref[...] * jax.nn.sigmoid(mo_ref[...])
    mix = (_dot(nao_ref[...].astype(BF16), wout_ref[0:256, :])
           + _dot(dfo_ref[...].astype(BF16), wout_ref[256:512, :])
           + _dot(ml_o.astype(BF16), wout_ref[512:1024, :]))
    x1 = x_ref[...] + mod_ref[0, 2:3, :] * mix
    x1_ref[...] = x1
    y = x1 * lax.rsqrt(jnp.mean(x1 * x1, axis=-1, keepdims=True) + EPS) * g2_ref[...]
    h2 = y * (1.0 + mod_ref[0, 4:5, :]) + mod_ref[0, 3:4, :]
    h2_ref[...] = h2.astype(BF16)

    wr = wr_ref[...]
    h_hi = h2.astype(BF16)
    h_lo = (h2 - h_hi.astype(F32)).astype(BF16)
    w_hi = wr.astype(BF16)
    w_lo = (wr - w_hi.astype(F32)).astype(BF16)
    logits = _dot(h_hi, w_hi) + _dot(h_lo, w_hi) + _dot(h_hi, w_lo)
    tm = logits.shape[0]
    logits_t = jnp.concatenate([logits, jnp.zeros((tm, LANES - N_EXPERTS), F32)], axis=1).T[:N_EXPERTS]
    s = jax.nn.sigmoid(logits_t)
    work = s + br_ref[...]
    row = lax.broadcasted_iota(jnp.int32, work.shape, 0)
    s_sel = jnp.zeros(work.shape, F32)
    for _ in range(TOP_K):
        mx = jnp.max(work, axis=0, keepdims=True)
        first = jnp.min(jnp.where(work == mx, row, N_EXPERTS), axis=0, keepdims=True)
        hit = row == first
        s_sel = jnp.where(hit, s, s_sel)
        work = jnp.where(hit, -jnp.inf, work)
    gates_t = s_sel / jnp.sum(s_sel, axis=0, keepdims=True) * ROUTED_SCALE
    gates_ref[...] = jnp.concatenate([gates_t, jnp.zeros((LANES - N_EXPERTS, tm), F32)], axis=0).T[:, :N_EXPERTS]


def _outproj(x, nao, dfo, mlh, mo, mod_p, tokens_per_mod, layer, g2, outn, w_out_bf, w_router, b_router):
    T = x.shape[0]
    tm = TM_PROJ
    row = lambda i: (i, 0)
    const = lambda i: (0, 0)
    lay = lambda i: (layer, 0, 0)
    return pl.pallas_call(
        _outproj_kernel,
        grid=(T // tm,),
        in_specs=[pl.BlockSpec((tm, D_MODEL), row),
                  pl.BlockSpec((tm, 256), row), pl.BlockSpec((tm, 256), row),
                  pl.BlockSpec((tm, 512), row), pl.BlockSpec((tm, 512), row),
                  pl.BlockSpec((1, 6, D_MODEL), lambda i: ((i * tm) // tokens_per_mod, 0, 0)),
                  pl.BlockSpec((1, D_MODEL), const), pl.BlockSpec((1, 512), const),
                  pl.BlockSpec((None, D_MODEL, D_MODEL), lay),
                  pl.BlockSpec((None, D_MODEL, N_EXPERTS), lay), pl.BlockSpec((N_EXPERTS, 1), const)],
        out_specs=[pl.BlockSpec((tm, D_MODEL), row), pl.BlockSpec((tm, D_MODEL), row),
                   pl.BlockSpec((tm, N_EXPERTS), row)],
        out_shape=[jax.ShapeDtypeStruct((T, D_MODEL), F32), jax.ShapeDtypeStruct((T, D_MODEL), BF16),
                   jax.ShapeDtypeStruct((T, N_EXPERTS), F32)],
        compiler_params=_cparams(1),
        name="outproj_router",
    )(x, nao, dfo, mlh, mo, mod_p, g2, outn, w_out_bf, w_router, b_router)


def _dispatch_kernel(h_ref, gates_ref, xs_ref, cnt_ref, route_ref, *, n_tiles):
    i = pl.program_id(0)

    @pl.when(i == n_tiles)
    def _():
        xs_ref[...] = jnp.zeros(xs_ref.shape, BF16)

    @pl.when(i < n_tiles)
    def _():
        _dispatch_tile(h_ref, gates_ref, xs_ref, cnt_ref, route_ref)


def _dispatch_tile(h_ref, gates_ref, xs_ref, cnt_ref, route_ref):
    tt = h_ref.shape[0]
    n_slots = xs_ref.shape[0]
    gates = gates_ref[...]
    sel = jnp.where(gates > 0.0, 1.0, 0.0)
    selb = sel.astype(BF16)
    ti = lax.broadcasted_iota(jnp.int32, (tt, tt), 0)
    si = lax.broadcasted_iota(jnp.int32, (tt, tt), 1)
    earlier = jnp.where(si < ti, 1.0, 0.0).astype(BF16)
    pos = _dot(earlier, selb)
    cnt = jnp.sum(sel, axis=0, keepdims=True)
    seg = jnp.floor((cnt + (MOE_GRANULE - 1.0)) * (1.0 / MOE_GRANULE)) * MOE_GRANULE
    ei = lax.broadcasted_iota(jnp.int32, (N_EXPERTS, N_EXPERTS), 0)
    ej = lax.broadcasted_iota(jnp.int32, (N_EXPERTS, N_EXPERTS), 1)
    before = jnp.where(ei < ej, 1.0, 0.0).astype(BF16)
    off = _dot(jnp.broadcast_to(seg, (16, N_EXPERTS)).astype(BF16), before)[0:1]
    rank = _dot(selb, before)
    slot = off + pos
    lane = _lane_iota((tt, LANES))
    route = jnp.zeros((tt, LANES), F32)
    for k in range(TOP_K):
        hit = jnp.where(rank == k, sel, 0.0)
        has = jnp.sum(hit, axis=1, keepdims=True)
        slot_k = jnp.sum(hit * slot, axis=1, keepdims=True) + has - 1.0
        w_k = jnp.sum(hit * gates, axis=1, keepdims=True)
        route = jnp.where(lane == k, slot_k, jnp.where(lane == TOP_K + k, w_k, route))
    route_ref[...] = route
    cnt_ref[0] = cnt
    slots_t = route.T
    srow = lax.broadcasted_iota(jnp.int32, (n_slots, tt), 0).astype(F32)
    perm = jnp.zeros((n_slots, tt), F32)
    for k in range(TOP_K):
        perm = jnp.where(srow == slots_t[k:k + 1, :], 1.0, perm)
    xs_ref[...] = _dot(perm.astype(BF16), h_ref[...]).astype(BF16)


def _dispatch(h2, gates):
    T = h2.shape[0]
    nt = T // MOE_TILE
    tile = lambda i: (jnp.minimum(i, nt - 1), 0)
    return pl.pallas_call(
        functools.partial(_dispatch_kernel, n_tiles=nt),
        grid=(nt + 1,),
        in_specs=[pl.BlockSpec((MOE_TILE, D_MODEL), tile),
                  pl.BlockSpec((MOE_TILE, N_EXPERTS), tile)],
        out_specs=[pl.BlockSpec((MOE_SLOTS, D_MODEL), lambda i: (i, 0)),
                   pl.BlockSpec((1, 1, N_EXPERTS), lambda i: (jnp.minimum(i, nt - 1), 0, 0)),
                   pl.BlockSpec((MOE_TILE, LANES), tile)],
        out_shape=[jax.ShapeDtypeStruct(((nt + 1) * MOE_SLOTS, D_MODEL), BF16),
                   jax.ShapeDtypeStruct((nt, 1, N_EXPERTS), F32),
                   jax.ShapeDtypeStruct((T, LANES), F32)],
        compiler_params=_cparams(1),
        name="moe_dispatch",
    )(h2, gates)


def _expert_fused_kernel(te_ref, nact_ref, src_ref, dst_ref, slots_in, w1_ref, w3_ref, w2_ref,
                         slots_hbm, xbuf, ybuf, sem_in, sem_out, *, n_tiles):
    del slots_in
    xs_hbm = ys_hbm = slots_hbm
    i = pl.program_id(0)
    n_act = nact_ref[0]
    gr = MOE_ROWS // MOE_GRANULE
    spare0 = ys_hbm.shape[0] // MOE_GRANULE - MOE_SLOTS // MOE_GRANULE

    def rows(g):
        if isinstance(g, int):
            return pl.ds(g * MOE_GRANULE, MOE_GRANULE)
        return pl.ds(pl.multiple_of(g * MOE_GRANULE, MOE_GRANULE), MOE_GRANULE)

    zero_granule = MOE_SLOTS // MOE_GRANULE - 1

    def gather_start(tile, slot, dummy=False):
        for g in range(gr):
            s = src_ref[tile * gr + g]
            if dummy is not False:
                s = jnp.where(dummy, zero_granule, s)
            pltpu.make_async_copy(xs_hbm.at[rows(s)], xbuf.at[slot, rows(g)], sem_in.at[slot]).start()

    def gather_wait(slot):
        for g in range(gr):
            pltpu.make_async_copy(xs_hbm.at[rows(0)], xbuf.at[slot, rows(g)], sem_in.at[slot]).wait()

    def scatter_start(tile, slot):
        for g in range(gr):
            dd = dst_ref[tile * gr + g]
            dd = jnp.where(dd < 0, spare0 + slot * gr + g, dd)
            pltpu.make_async_copy(ybuf.at[slot, rows(g)], ys_hbm.at[rows(dd)], sem_out.at[slot]).start()

    def scatter_wait(slot):
        for g in range(gr):
            pltpu.make_async_copy(ybuf.at[slot, rows(g)], ys_hbm.at[rows(0)], sem_out.at[slot]).wait()

    slot = lax.rem(i, 2)

    @pl.when((i == 0) & (n_act > 0))
    def _():
        gather_start(0, 0)

    @pl.when(i < n_act)
    def _():
        gather_start(jnp.minimum(i + 1, n_act - 1), 1 - slot, dummy=i + 1 >= n_act)
        gather_wait(slot)

        @pl.when(i >= 2)
        def _():
            scatter_wait(slot)

        x = xbuf[slot]
        a = _dot(x, w1_ref[...].astype(BF16))
        b = _dot(x, w3_ref[...].astype(BF16))
        ybuf[slot] = _dot((_silu(a) * b).astype(BF16), w2_ref[...].astype(BF16)).astype(BF16)
        scatter_start(i, slot)

    @pl.when((i == n_tiles - 1) & (n_act > 0))
    def _():
        last_slot = lax.rem(n_act - 1, 2)
        gather_wait(1 - last_slot)
        scatter_wait(last_slot)

        @pl.when(n_act > 1)
        def _():
            scatter_wait(1 - last_slot)


def _expert_fused(tile_expert, n_active, src, dst, xs_tiles, layer, w1, w3, w2):
    n_tiles = src.shape[0] // (MOE_ROWS // MOE_GRANULE)
    wmap = lambda i, te, na, s, d: (layer, te[i], 0, 0)
    stage = pltpu.VMEM((2, MOE_ROWS, D_MODEL), BF16)
    assert MOE_SLOTS >= 2 * MOE_ROWS
    return pl.pallas_call(
        functools.partial(_expert_fused_kernel, n_tiles=n_tiles),
        grid_spec=pltpu.PrefetchScalarGridSpec(
            num_scalar_prefetch=4, grid=(n_tiles,),
            in_specs=[pl.BlockSpec(memory_space=pl.ANY),
                      pl.BlockSpec((None, None, D_MODEL, D_EXPERT), wmap),
                      pl.BlockSpec((None, None, D_MODEL, D_EXPERT), wmap),
                      pl.BlockSpec((None, None, D_EXPERT, D_MODEL), wmap)],
            out_specs=pl.BlockSpec(memory_space=pl.ANY),
            scratch_shapes=[stage, stage, pltpu.SemaphoreType.DMA((2,)), pltpu.SemaphoreType.DMA((2,))]),
        out_shape=jax.ShapeDtypeStruct(xs_tiles.shape, BF16),
        input_output_aliases={4: 0},
        compiler_params=_cparams(1),
        name="moe_expert_rows",
    )(tile_expert, n_active, src, dst, xs_tiles, w1, w3, w2)


def _combine_kernel(ys_ref, route_ref, h_ref, x1_ref, mod_ref, s1_ref, s3_ref, s2_ref, o_ref):
    route = route_ref[...]
    tt = route.shape[0]
    n_slots = ys_ref.shape[0]
    slot_lane = _lane_iota((tt, n_slots)).astype(F32)
    wp = jnp.zeros((tt, n_slots), F32)
    for k in range(TOP_K):
        wp = jnp.where(slot_lane == route[:, k:k + 1], route[:, TOP_K + k:TOP_K + k + 1], wp)
    routed = _dot(wp.astype(BF16), ys_ref[...])
    h = h_ref[...]
    a = _dot(h, s1_ref[...].astype(BF16))
    b = _dot(h, s3_ref[...].astype(BF16))
    shared = _dot((_silu(a) * b).astype(BF16), s2_ref[...].astype(BF16))
    o_ref[...] = x1_ref[...] + mod_ref[0, 5:6, :] * (routed + shared)


def _combine(ys_tiles, route, h2, x1, mod_p, tokens_per_mod, layer, s1, s3, s2):
    T = h2.shape[0]
    tt = MOE_TILE
    row = lambda i: (i, 0)
    wsh = lambda i: (layer, 0, 0)
    return pl.pallas_call(
        _combine_kernel,
        grid=(T // tt,),
        in_specs=[pl.BlockSpec((MOE_SLOTS, D_MODEL), row), pl.BlockSpec((tt, LANES), row),
                  pl.BlockSpec((tt, D_MODEL), row), pl.BlockSpec((tt, D_MODEL), row),
                  pl.BlockSpec((1, 6, D_MODEL), lambda i: ((i * tt) // tokens_per_mod, 0, 0)),
                  pl.BlockSpec((None, D_MODEL, D_EXPERT), wsh), pl.BlockSpec((None, D_MODEL, D_EXPERT), wsh),
                  pl.BlockSpec((None, D_EXPERT, D_MODEL), wsh)],
        out_specs=pl.BlockSpec((tt, D_MODEL), row),
        out_shape=jax.ShapeDtypeStruct((T, D_MODEL), F32),
        compiler_params=_cparams(1),
        name="moe_combine",
    )(ys_tiles, route, h2, x1, mod_p, s1, s3, s2)


def _excl_cumsum(a, axis):
    return jnp.cumsum(a, axis=axis) - a


def _moe_plan(cnt):
    nt = cnt.shape[0]
    gt = MOE_SLOTS // MOE_GRANULE
    gr = MOE_ROWS // MOE_GRANULE
    ng = _moe_sorted_granules(nt)
    g = (cnt + MOE_GRANULE - 1) // MOE_GRANULE
    loc = _excl_cumsum(g, 1)
    tot = jnp.sum(g, axis=0)
    grp = ((tot + gr - 1) // gr) * gr
    gs = _excl_cumsum(grp, 0)
    start = gs[None, :] + _excl_cumsum(g, 0)
    base = jnp.arange(nt, dtype=jnp.int32)[:, None] * gt + loc
    d = jnp.arange(ng, dtype=jnp.int32)[:, None, None]
    inseg = (start[None] <= d) & (d < (start + g)[None])
    local = jnp.sum(jnp.where(inseg, base[None] + d - start[None], 0), axis=(1, 2))
    real = jnp.any(inseg, axis=(1, 2))
    src = jnp.where(real, local, gt - 1)
    dst = jnp.where(real, local, -1)
    n_active = jnp.sum(grp) // gr
    first = jnp.minimum(jnp.arange(ng // gr, dtype=jnp.int32), jnp.maximum(n_active - 1, 0)) * gr
    tile_expert = jnp.sum((gs[None, :] <= first[:, None]).astype(jnp.int32), axis=1) - 1
    return (src.astype(jnp.int32), dst.astype(jnp.int32), tile_expert.astype(jnp.int32),
            n_active.astype(jnp.int32).reshape(1))


def _moe_sorted_granules(nt):
    gt_used = (TOP_K * MOE_TILE + N_EXPERTS * (MOE_GRANULE - 1)) // MOE_GRANULE
    gr = MOE_ROWS // MOE_GRANULE
    need = nt * gt_used + N_EXPERTS * (gr - 1) + 1
    return -(-need // gr) * gr


def _moe_sparse(h2, gates, x1, mod_p, tokens_per_mod, layer, w1, w3, w2, s1, s3, s2):
    xs_tiles, cnt, route = _dispatch(h2, gates)
    nt = cnt.shape[0]
    src, dst, tile_expert, n_active = _moe_plan(cnt.reshape(nt, N_EXPERTS).astype(jnp.int32))
    ys_tiles = _expert_fused(tile_expert, n_active, src, dst, xs_tiles, layer, w1, w3, w2)
    return _combine(ys_tiles, route, h2, x1, mod_p, tokens_per_mod, layer, s1, s3, s2)


def _group_matrix(group):
    i = np.arange(256)
    return jnp.asarray((i[:, None] // group) == (i[None, :] // group), BF16)


def _rope_tables(seq):
    t = np.arange(seq)
    lane = np.arange(256)
    j = lane % DF_DQK
    nf = DF_DQK // 4
    pos = np.where((j < DF_DQK // 2)[None, :], (t // GRID_W)[:, None], (t % GRID_W)[:, None]).astype(np.float32)
    inv = (1.0 / (ROPE_BASE ** (np.arange(nf, dtype=np.float32) / nf))).astype(np.float32)
    ang = pos * inv[j % nf][None, :]
    sign = np.where((lane % (2 * nf)) < nf, -1.0, 1.0).astype(np.float32)
    return jnp.asarray(np.cos(ang), F32), jnp.asarray(np.sin(ang) * sign[None, :], F32)


def _token_major(a):
    b, h, l, d = a.shape
    return a.transpose(0, 2, 1, 3).reshape(b, l, h * d)


def kernel(x_prompt, x_sample, cache_na_k, cache_na_v, cache_df_k, cache_df_v, state_ml_C, state_ml_n,
           state_ml_m, c, c_ctx, w_ada, b_ada, g_norm1, g_norm2, w_in, w_out, na_qn, na_kn, na_rpb,
           df_qn, df_kn, df_lam, df_subln, ml_gate_b, ml_outn, w_router, b_router, w_exp1, w_exp3, w_exp2,
           w_sh1, w_sh3, w_sh2):
    bc, lc, _ = x_prompt.shape
    bs, ls, _ = x_sample.shape
    past = cache_na_k.shape[3]

    cond = jnp.zeros((8, D_MODEL), F32).at[0].set(c_ctx).at[1:1 + bs].set(c)
    mod = _modulation(cond, w_ada, b_ada)
    g64 = _group_matrix(NA_DH)
    g32 = _group_matrix(DF_DQK)
    rope_tabs = _rope_tables(ls)
    w_in_bf = jnp.pad(w_in.astype(BF16), ((0, 0), (0, 0), (0, IN_COLS_PAD - IN_COLS)))
    w_out_bf = w_out.astype(BF16)
    bias_tabs = _na_bias_tables(na_rpb, ls // GRID_W)
    moe_w = (w_exp1, w_exp3, w_exp2, w_sh1, w_sh3, w_sh2)

    y_p = x_prompt.reshape(bc * lc, D_MODEL)
    y_s = x_sample.reshape(bs * ls, D_MODEL)
    caches = states = None
    m_states = []
    for l in range(DEPTH):
        lam_init = 0.8 - 0.6 * math.exp(-0.3 * l)
        mod_l = mod[l].reshape(8, 6, D_MODEL)
        g1 = g_norm1[l][None, :]
        g2 = g_norm2[l][None, :]
        qn = jnp.tile(na_qn[l], NA_HEADS)[None, :]
        kn = jnp.tile(na_kn[l], NA_HEADS)[None, :]
        dqn = jnp.tile(df_qn[l], 2 * DF_HEADS)[None, :]
        dkn = jnp.tile(df_kn[l], 2 * DF_HEADS)[None, :]
        subln = jnp.tile(df_subln[l], DF_HEADS)[None, :]
        outn = jnp.tile(ml_outn[l], ML_HEADS)[None, :]
        gate_b = jnp.pad(ml_gate_b[l].reshape(-1), (0, LANES - 4 * ML_HEADS))[None, :]
        br = b_router[l][:, None]
        proj_w = (l, g1, w_in_bf, qn, kn, dqn, dkn, gate_b, g64, g32)

        mod_c = mod_l[0:1]
        tpm = bc * lc
        res = _inproj(y_p, mod_c, tpm, *proj_w, None, cache_seq=lc, cache_prev=caches)
        naq, nak, nav, dfq, dfk, dfv, mq, mk, mv, mo, mg = res[:11]
        caches = tuple(res[11:])
        nao, dfo = _ctx_attention(naq, nak, nav, dfq, dfk, dfv, df_lam[l], subln, lc, lam_init)
        mlh, c_n, n_n, m_n = _mlstm(mq, mk, mv, mg, lc, emit_state=True, layer=l, state_prev=states)
        states = (c_n, n_n)
        x1, h2, gates = _outproj(y_p, nao, dfo, mlh, mo, mod_c, tpm, l, g2, outn, w_out_bf, w_router, br)
        y_p = _moe_sparse(h2, gates, x1, mod_c, tpm, l, *moe_w)
        m_states.append(m_n[:, :, 0, 0].reshape(bc, 2, ML_HEADS))

        mod_s = mod_l[1:1 + bs]
        naq, nak, nav, dfq, dfk, dfv, mq, mk, mv, mo, mg = _inproj(y_s, mod_s, ls, *proj_w, rope_tabs)
        kc_na = _token_major(cache_na_k[:, l])
        vc_na = _token_major(cache_na_v[:, l])
        kc_df = cache_df_k[:, l].transpose(0, 3, 1, 2, 4).reshape(bs, past, 256)
        vc_df = _token_major(cache_df_v[:, l])
        nao = _lat_na(naq, nak, nav, kc_na, vc_na, bias_tabs, l, ls)
        dfo = _lat_df(dfq, dfk, dfv, kc_df, vc_df, df_lam[l], subln, ls, lam_init)
        n0 = jnp.broadcast_to(state_ml_n[:, l][..., None], (bs, 2, ML_HEADS, ML_DH, ML_DH))
        c0 = jnp.concatenate([state_ml_C[:, l], n0], axis=-1)
        c0 = c0.reshape(bs, 2 * ML_HEADS, ML_DH, 2 * ML_DH)
        m0 = jnp.broadcast_to(state_ml_m[:, l].reshape(bs, 2 * ML_HEADS, 1, 1), (bs, 2 * ML_HEADS, 8, LANES))
        (mlh,) = _mlstm(mq, mk, mv, mg, ls, init=(c0, m0))
        x1, h2, gates = _outproj(y_s, nao, dfo, mlh, mo, mod_s, ls, l, g2, outn, w_out_bf, w_router, br)
        y_s = _moe_sparse(h2, gates, x1, mod_s, ls, l, *moe_w)

    new_ml_c = states[0].reshape(bc, DEPTH, 2, ML_HEADS, ML_DH, ML_DH)
    new_ml_n = states[1].reshape(bc, DEPTH, 2, ML_HEADS, ML_DH)
    return (y_p.reshape(bc, lc, D_MODEL), y_s.reshape(bs, ls, D_MODEL), *caches,
            new_ml_c, new_ml_n, jnp.stack(m_states, axis=1))
```

```python
import functools
import math

import numpy as np
import jax
import jax.numpy as jnp
from jax import lax
from jax.experimental import pallas as pl
from jax.experimental.pallas import tpu as pltpu

F32 = jnp.float32
BF16 = jnp.bfloat16

D_MODEL = 1024
DEPTH = 2
GRID_W = 64
NA_HEADS = 4
NA_DH = 64
NA_WIN_R = 8
NA_WIN_C = 16
DF_HEADS = 4
DF_DV = 64
DF_DQK = 32
ML_HEADS = 4
ML_DH = 128
ML_CHUNK = 128
N_EXPERTS = 32
TOP_K = 4
D_EXPERT = 256
ROUTED_SCALE = 2.5
ROPE_BASE = 10000.0
EPS = 1e-6

LANES = 128
IN_COLS = 3600
IN_COLS_PAD = 3712
GATE_COL0 = 3584
MASK_VALUE = -1e30
VMEM_LIMIT = 56 * 1024 * 1024

TM_PROJ = 512
MOE_TILE = 256
MOE_GRANULE = 16
MOE_SLOTS = TOP_K * MOE_TILE + N_EXPERTS * MOE_GRANULE
MOE_ROWS = 512
Q_BLOCK_DF = 512
CTX_ATTN_BATCHES = 2
DF_SCORE_LEAD_CTX = 1
DF_SCORE_LEAD_LAT = 3
LOOP_UNROLL = 2
NA_Q_ROWS = 4
NA_K_ROWS = NA_Q_ROWS + NA_WIN_R
LOG2E = math.log2(math.e)


def _cparams(n_axes):
    return pltpu.CompilerParams(dimension_semantics=("arbitrary",) * n_axes,
                                vmem_limit_bytes=VMEM_LIMIT)


def _silu(x):
    return x * jax.nn.sigmoid(x)


def _dot(a, b):
    return jnp.dot(a, b, preferred_element_type=F32)


def _dot_nt(a, b):
    return lax.dot_general(a, b, (((1,), (1,)), ((), ())), preferred_element_type=F32)


def _dot_tn(a, b):
    return lax.dot_general(a, b, (((0,), (0,)), ((), ())), preferred_element_type=F32)


def _lane_iota(shape):
    return lax.broadcasted_iota(jnp.int32, shape, 1)


def _mod_kernel(cond_ref, w_ref, b_ref, o_ref):
    s = _silu(cond_ref[...]).astype(BF16)
    o_ref[0] = _dot(s, w_ref[0].astype(BF16)) + b_ref[0]


def _modulation(cond, w_ada, b_ada):
    tn = 3072
    n = 6 * D_MODEL
    return pl.pallas_call(
        _mod_kernel,
        grid=(DEPTH, n // tn),
        in_specs=[pl.BlockSpec((8, D_MODEL), lambda l, j: (0, 0)),
                  pl.BlockSpec((1, D_MODEL, tn), lambda l, j: (l, 0, j)),
                  pl.BlockSpec((1, 1, tn), lambda l, j: (l, 0, j))],
        out_specs=pl.BlockSpec((1, 8, tn), lambda l, j: (l, 0, j)),
        out_shape=jax.ShapeDtypeStruct((DEPTH, 8, n), F32),
        compiler_params=_cparams(2),
        name="adaln_mod",
    )(cond, w_ada, b_ada.reshape(DEPTH, 1, n))


def _group_rms(v, gmat_ref, group):
    ss = _dot((v * v).astype(BF16), gmat_ref[...])
    return v * lax.rsqrt(ss * (1.0 / group) + EPS)


def _rope(v, cos, sin_signed):
    first = (_lane_iota(v.shape) % 16) < 8
    partner = jnp.where(first, pltpu.roll(v, v.shape[1] - 8, 1), pltpu.roll(v, 8, 1))
    return v * cos + partner * sin_signed


def _inproj_kernel(*refs, rope, cache_seq, cache_first):
    (x_ref, mod_ref, g1_ref, w_ref, qn_ref, kn_ref, dqn_ref, dkn_ref, gb_ref, g64_ref, g32_ref) = refs[:11]
    pos = 11
    if rope:
        cos_ref, sin_ref = refs[pos:pos + 2]
        pos += 2
    if cache_seq and not cache_first:
        pos += 4
    (naq_ref, nak_ref, nav_ref, dfq_ref, dfk_ref, dfv_ref,
     mq_ref, mk_ref, mv_ref, mo_ref, mg_ref) = refs[pos:pos + 11]
    pos += 11
    x = x_ref[...]
    y = x * lax.rsqrt(jnp.mean(x * x, axis=-1, keepdims=True) + EPS) * g1_ref[...]
    h = (y * (1.0 + mod_ref[0, 1:2, :]) + mod_ref[0, 0:1, :]).astype(BF16)

    def proj(a, b):
        return _dot(h, w_ref[:, a:b])

    naq_ref[...] = _group_rms(proj(0, 256), g64_ref, NA_DH) * qn_ref[...]
    nak = _group_rms(proj(256, 512), g64_ref, NA_DH) * kn_ref[...]
    nav = proj(512, 768)
    dfq = _group_rms(proj(768, 1024), g32_ref, DF_DQK) * dqn_ref[...]
    dfk = _group_rms(proj(1024, 1280), g32_ref, DF_DQK) * dkn_ref[...]
    dfv = proj(1280, 1536)
    if rope:
        dfq = _rope(dfq, cos_ref[...], sin_ref[...])
        dfk = _rope(dfk, cos_ref[...], sin_ref[...])
    nak_ref[...] = nak
    nav_ref[...] = nav
    dfq_ref[...] = dfq
    dfk_ref[...] = dfk
    dfv_ref[...] = dfv
    if cache_seq:
        nk_hm, nv_hm, dk_hm, dv_hm = refs[pos:pos + 4]

        def put(ref, bi, idx, val):
            if cache_first:
                for dl in range(DEPTH):
                    ref[(bi, dl) + idx] = val
            else:
                ref[(bi,) + idx] = val

        for bi in range(x.shape[0] // cache_seq):
            rs = slice(bi * cache_seq, (bi + 1) * cache_seq)
            for hd in range(NA_HEADS):
                put(nk_hm, bi, (hd,), nak[rs, NA_DH * hd:NA_DH * (hd + 1)])
                put(nv_hm, bi, (hd,), nav[rs, NA_DH * hd:NA_DH * (hd + 1)])
                put(dv_hm, bi, (hd,), dfv[rs, DF_DV * hd:DF_DV * (hd + 1)])
                for a in range(2):
                    c0 = (2 * hd + a) * DF_DQK
                    put(dk_hm, bi, (hd, a), dfk[rs, c0:c0 + DF_DQK])
    mq_ref[...] = proj(1536, 2048)
    mk_ref[...] = proj(2048, 2560) * (ML_DH ** -0.5)
    mv_ref[...] = proj(2560, 3072)
    mo_ref[...] = proj(3072, 3584)
    g = proj(GATE_COL0, IN_COLS_PAD) + gb_ref[...]
    lane = _lane_iota(g.shape)
    is_forget = ((lane // ML_HEADS) % 2 == 1) & (lane < 4 * ML_HEADS)
    log_sig = jnp.minimum(g, 0.0) - jnp.log1p(jnp.exp(-jnp.abs(g)))
    mg_ref[...] = jnp.where(is_forget, log_sig, g)


def _inproj(x, mod_p, tokens_per_mod, layer, g1, w_in_bf, qn, kn, dqn, dkn, gate_b, g64, g32, rope_tabs,
            cache_seq=0, cache_prev=None):
    T = x.shape[0]
    tm = TM_PROJ
    rope = rope_tabs is not None
    row = lambda i: (i, 0)
    const = lambda i: (0, 0)
    in_specs = [pl.BlockSpec((tm, D_MODEL), row),
                pl.BlockSpec((1, 6, D_MODEL), lambda i: ((i * tm) // tokens_per_mod, 0, 0)),
                pl.BlockSpec((1, D_MODEL), const),
                pl.BlockSpec((None, D_MODEL, IN_COLS_PAD), lambda i: (layer, 0, 0)),
                pl.BlockSpec((1, 256), const), pl.BlockSpec((1, 256), const),
                pl.BlockSpec((1, 256), const), pl.BlockSpec((1, 256), const),
                pl.BlockSpec((1, LANES), const),
                pl.BlockSpec((256, 256), const), pl.BlockSpec((256, 256), const)]
    args = [x, mod_p, g1, w_in_bf, qn, kn, dqn, dkn, gate_b, g64, g32]
    if rope:
        seq = rope_tabs[0].shape[0]
        tab = lambda i: (i % (seq // tm), 0)
        in_specs += [pl.BlockSpec((tm, 256), tab), pl.BlockSpec((tm, 256), tab)]
        args += list(rope_tabs)
    widths = [256] * 6 + [512] * 4 + [LANES]
    out_specs = [pl.BlockSpec((tm, w), row) for w in widths]
    out_shape = [jax.ShapeDtypeStruct((T, w), F32) for w in widths]
    aliases = {}
    if cache_seq:
        nb = tm // cache_seq
        hm = (NA_HEADS, cache_seq, NA_DH)
        hm_df = (DF_HEADS, 2, cache_seq, DF_DQK)
        for k, tail in enumerate((hm, hm, hm_df, hm)):
            zeros = (0,) * len(tail)
            if cache_prev is None:
                out_specs.append(pl.BlockSpec((nb, DEPTH) + tail, lambda i, z=zeros: (i, 0) + z))
            else:
                out_specs.append(pl.BlockSpec((nb, None) + tail, lambda i, z=zeros: (i, layer) + z))
                aliases[len(args)] = len(widths) + k
                in_specs.append(pl.BlockSpec(memory_space=pl.ANY))
                args.append(cache_prev[k])
            out_shape.append(jax.ShapeDtypeStruct((T // cache_seq, DEPTH) + tail, F32))
    return pl.pallas_call(
        functools.partial(_inproj_kernel, rope=rope, cache_seq=cache_seq, cache_first=cache_prev is None),
        grid=(T // tm,),
        in_specs=in_specs,
        out_specs=out_specs,
        out_shape=out_shape,
        input_output_aliases=aliases,
        compiler_params=_cparams(1),
        name="inproj_rope" if rope else "inproj",
    )(*args)


def _df_lambda(lam_ref, lam_init):
    v = lam_ref[...]
    a = jnp.sum(v[0:1] * v[1:2], axis=1, keepdims=True)
    b = jnp.sum(v[2:3] * v[3:4], axis=1, keepdims=True)
    return jnp.exp(a) - jnp.exp(b) + lam_init


def _softmax_parts(scores):
    es = _softmax_exps(scores)
    tot = functools.reduce(lambda a, b: a + b, [jnp.sum(e, axis=-1, keepdims=True) for e in es])
    return [e.astype(BF16) for e in es], 1.0 / tot


def _softmax_exps(scores):
    m = functools.reduce(jnp.maximum, [jnp.max(s, axis=-1, keepdims=True) for s in scores])
    return [jnp.exp2(s - m) for s in scores]


def _attend_masked_heads(tiles, lead=1):
    lane = _lane_iota(tiles[0][0].shape)
    v_aug = [[jnp.concatenate([v, jnp.ones(v.shape, BF16)], axis=1) for v in v_segs] for _, _, v_segs, _ in tiles]
    n_heads = 2 * len(tiles)

    def scores(c):
        qp, k_segs, _, bias = tiles[c // 2]
        qm = jnp.where(lane // NA_DH == c % 2, qp, 0.0).astype(BF16)
        s = [_dot_nt(qm, k) for k in k_segs]
        if bias is not None:
            s[0] = s[0] + bias[c % 2]
        return s

    outs = []
    pending = [scores(c) for c in range(min(lead, n_heads))]
    for c in range(n_heads):
        if c + lead < n_heads:
            pending.append(scores(c + lead))
        es = _softmax_exps(pending.pop(0))
        o = functools.reduce(lambda x, y: x + y, [_dot(e.astype(BF16), v) for e, v in zip(es, v_aug[c // 2])])
        outs.append(o[:, :LANES] / o[:, LANES:])
    return [jnp.where(lane < NA_DH, outs[2 * t], outs[2 * t + 1]) for t in range(len(tiles))]


def _subln(o, subln_row, lam_init):
    lane = _lane_iota(o.shape)
    sq = o * o
    s0 = jnp.sum(jnp.where(lane < DF_DV, sq, 0.0), axis=-1, keepdims=True)
    s1 = jnp.sum(jnp.where(lane >= DF_DV, sq, 0.0), axis=-1, keepdims=True)
    ms = jnp.where(lane < DF_DV, s0, s1) * (1.0 / DF_DV)
    return o * lax.rsqrt(ms + EPS) * subln_row * (1.0 - lam_init)


def _diff_attn(q, k_segs, v_segs, lam, lead):
    n_maps = 2 * DF_HEADS
    per_tile = LANES // DF_DQK
    lane = _lane_iota((q.shape[0], LANES))
    tiles = []
    for t in range(n_maps // per_tile):
        sl = slice(LANES * t, LANES * (t + 1))
        tiles.append((q[:, sl] * (DF_DQK ** -0.5 * LOG2E),
                      [k[:, sl].astype(BF16) for k in k_segs], [v[:, sl].astype(BF16) for v in v_segs]))

    def scores(c):
        qp, kps, _ = tiles[c // per_tile]
        qm = jnp.where(lane // DF_DQK == c % per_tile, qp, 0.0).astype(BF16)
        return [_dot_nt(qm, kp) for kp in kps]

    def head_out(m0, m1, vps):
        (e0, r0), (e1, r1) = m0, m1
        c0 = r0.astype(BF16)
        c1 = (-(lam * r1)).astype(BF16)
        return functools.reduce(lambda x, y: x + y,
                                [_dot(x0 * c0 + x1 * c1, vp) for x0, x1, vp in zip(e0, e1, vps)])

    heads = []
    maps = []
    pending = [scores(c) for c in range(lead)]
    for c in range(n_maps):
        if c + lead < n_maps:
            pending.append(scores(c + lead))
        maps.append(_softmax_parts(pending.pop(0)))
        if c % 2 == 1:
            heads.append(head_out(maps[c - 1], maps[c], tiles[c // per_tile][2]))
    return [jnp.where(lane < DF_DV, heads[2 * t], heads[2 * t + 1]) for t in range(len(tiles))]


def _ctx_attn_kernel(naq_ref, nak_ref, nav_ref, dfq_ref, dfk_ref, dfv_ref, lam_ref, subln_ref,
                     nao_ref, dfo_ref, *, lam_init, seq):
    lam = _df_lambda(lam_ref, lam_init)
    for bi in range(naq_ref.shape[0] // seq):
        rs = slice(bi * seq, (bi + 1) * seq)
        tiles = []
        for t in range(2):
            sl = slice(LANES * t, LANES * (t + 1))
            tiles.append((naq_ref[rs, sl] * (NA_DH ** -0.5 * LOG2E), [nak_ref[rs, sl].astype(BF16)],
                          [nav_ref[rs, sl].astype(BF16)], None))
        for t, o in enumerate(_attend_masked_heads(tiles, lead=0)):
            nao_ref[rs, LANES * t:LANES * (t + 1)] = o
        outs = _diff_attn(dfq_ref[rs, :], [dfk_ref[rs, :]], [dfv_ref[rs, :]], lam, DF_SCORE_LEAD_CTX)
        for t, o in enumerate(outs):
            sl = slice(LANES * t, LANES * (t + 1))
            dfo_ref[rs, sl] = _subln(o, subln_ref[:, sl], lam_init)


def _ctx_attention(naq, nak, nav, dfq, dfk, dfv, df_lam, subln, seq, lam_init):
    T = naq.shape[0]
    rows = CTX_ATTN_BATCHES * seq
    blk = pl.BlockSpec((rows, 256), lambda b: (b, 0))
    return pl.pallas_call(
        functools.partial(_ctx_attn_kernel, lam_init=lam_init, seq=seq),
        grid=(T // rows,),
        in_specs=[blk] * 6 + [pl.BlockSpec((4, DF_DQK), lambda b: (0, 0)),
                              pl.BlockSpec((1, 256), lambda b: (0, 0))],
        out_specs=[blk, blk],
        out_shape=[jax.ShapeDtypeStruct((T, 256), F32)] * 2,
        compiler_params=_cparams(1),
        name="ctx_attention",
    )(naq, nak, nav, dfq, dfk, dfv, df_lam, subln)


def _na_block_geometry(rows):
    last_r0 = rows - NA_Q_ROWS
    return ((0, 0), (NA_Q_ROWS, 0), (last_r0, rows - NA_K_ROWS))


def _na_bias_kernel(rpb_ref, o_ref, *, rows):
    hd = pl.program_id(0)
    shape = (GRID_W, GRID_W)
    qi = lax.broadcasted_iota(jnp.int32, shape, 0)
    kc = lax.broadcasted_iota(jnp.int32, shape, 1)
    cs = jnp.clip(qi - NA_WIN_C // 2, 0, GRID_W - NA_WIN_C)
    inwin = (kc >= cs) & (kc < cs + NA_WIN_C)
    off = jnp.where(inwin, kc - qi + NA_WIN_C - 1, -1)
    n_c = 2 * NA_WIN_C - 1
    n_r = 2 * NA_WIN_R - 1
    blocks = []
    for ro in range(n_r):
        val = jnp.full(shape, MASK_VALUE, F32)
        for d in range(n_c):
            val = jnp.where(off == d, rpb_ref[hd * (n_r * n_c) + ro * n_c + d], val)
        blocks.append(val * LOG2E)
    masked = jnp.full(shape, MASK_VALUE * LOG2E, F32)
    for v, (r0, k0) in enumerate(_na_block_geometry(rows)):
        for rq in range(NA_Q_ROWS):
            r = r0 + rq
            rs = min(max(r - NA_WIN_R // 2, 0), rows - NA_WIN_R)
            for j, kr in enumerate(range(k0, k0 + NA_K_ROWS)):
                blk = blocks[kr - r + NA_WIN_R - 1] if rs <= kr < rs + NA_WIN_R else masked
                o_ref[0, v, GRID_W * rq:GRID_W * (rq + 1), GRID_W * j:GRID_W * (j + 1)] = blk


def _na_bias_tables(rpb, rows):
    n_lh = rpb.shape[0] * rpb.shape[1]
    blk = (3, NA_Q_ROWS * GRID_W, NA_K_ROWS * GRID_W)
    tabs = pl.pallas_call(
        functools.partial(_na_bias_kernel, rows=rows),
        grid=(n_lh,),
        in_specs=[pl.BlockSpec(memory_space=pltpu.SMEM)],
        out_specs=pl.BlockSpec((1,) + blk, lambda h: (h, 0, 0, 0)),
        out_shape=jax.ShapeDtypeStruct((n_lh,) + blk, F32),
        compiler_params=_cparams(1),
        name="na_bias",
    )(rpb.reshape(-1))
    return tabs.reshape(rpb.shape[:2] + blk)


def _lat_na_kernel(q_ref, k_ref, v_ref, kc_ref, vc_ref, bias_ref, o_ref, *, rows):
    i = pl.program_id(1)
    k0 = jnp.clip(i * NA_Q_ROWS - NA_WIN_R // 2, 0, rows - NA_K_ROWS)
    krows = pl.ds(pl.multiple_of(k0 * GRID_W, GRID_W), NA_K_ROWS * GRID_W)
    q = q_ref[...] * (NA_DH ** -0.5 * LOG2E)
    tiles = []
    for t in range(2):
        sl = slice(LANES * t, LANES * (t + 1))
        tiles.append((q[:, sl],
                      [k_ref[krows, sl].astype(BF16), kc_ref[0, :, sl].astype(BF16)],
                      [v_ref[krows, sl].astype(BF16), vc_ref[0, :, sl].astype(BF16)],
                      [bias_ref[2 * t], bias_ref[2 * t + 1]]))
    for t, o in enumerate(_attend_masked_heads(tiles)):
        o_ref[:, LANES * t:LANES * (t + 1)] = o


def _lat_na(q, k, v, kc, vc, bias_tabs, layer, seq):
    T = q.shape[0]
    past = kc.shape[1]
    rows = seq // GRID_W
    nq = rows // NA_Q_ROWS
    qrows = NA_Q_ROWS * GRID_W
    qblk = pl.BlockSpec((qrows, 256), lambda b, i: (b * nq + i, 0))
    kblk = pl.BlockSpec((seq, 256), lambda b, i: (b, 0))
    cblk = pl.BlockSpec((1, past, 256), lambda b, i: (b, 0, 0))
    variant = lambda b, i: (layer, 0, jnp.where(i == 0, 0, jnp.where(i == nq - 1, 2, 1)), 0, 0)
    return pl.pallas_call(
        functools.partial(_lat_na_kernel, rows=rows),
        grid=(T // seq, nq),
        in_specs=[qblk, kblk, kblk, cblk, cblk,
                  pl.BlockSpec((None, NA_HEADS, None, qrows, NA_K_ROWS * GRID_W), variant)],
        out_specs=qblk,
        out_shape=jax.ShapeDtypeStruct((T, 256), F32),
        compiler_params=_cparams(2),
        name="lat_na",
    )(q, k, v, kc, vc, bias_tabs)


def _lat_df_kernel(q_ref, k_ref, v_ref, kc_ref, vc_ref, lam_ref, subln_ref, o_ref, *, lam_init):
    lam = _df_lambda(lam_ref, lam_init)
    outs = _diff_attn(q_ref[...], [k_ref[...], kc_ref[0]], [v_ref[...], vc_ref[0]], lam, DF_SCORE_LEAD_LAT)
    for t, o in enumerate(outs):
        sl = slice(LANES * t, LANES * (t + 1))
        o_ref[:, sl] = _subln(o, subln_ref[:, sl], lam_init)


def _lat_df(q, k, v, kc, vc, df_lam, subln, seq, lam_init):
    T = q.shape[0]
    past = kc.shape[1]
    nq = seq // Q_BLOCK_DF
    qblk = pl.BlockSpec((Q_BLOCK_DF, 256), lambda b, i: (b * nq + i, 0))
    kblk = pl.BlockSpec((seq, 256), lambda b, i: (b, 0))
    cblk = pl.BlockSpec((1, past, 256), lambda b, i: (b, 0, 0))
    return pl.pallas_call(
        functools.partial(_lat_df_kernel, lam_init=lam_init),
        grid=(T // seq, nq),
        in_specs=[qblk, kblk, kblk, cblk, cblk,
                  pl.BlockSpec((4, DF_DQK), lambda b, i: (0, 0)),
                  pl.BlockSpec((1, 256), lambda b, i: (0, 0))],
        out_specs=qblk,
        out_shape=jax.ShapeDtypeStruct((T, 256), F32),
        compiler_params=_cparams(2),
        name="lat_df",
    )(q, k, v, kc, vc, df_lam, subln)


def _mlstm_kernel(*refs, nc, has_init, emit_state, state_first):
    refs = list(refs)
    q_ref, k_ref, v_ref, g_ref = refs[:4]
    pos = 4
    if has_init:
        c0_ref, m0_ref = refs[pos:pos + 2]
        pos += 2
    if emit_state and not state_first:
        pos += 2
    h_ref = refs[pos]
    pos += 1
    if emit_state:
        cn_ref, nn_ref, mn_ref = refs[pos:pos + 3]
        pos += 3
    caug_sc, m_sc = refs[pos:pos + 2]

    if has_init:
        caug_sc[...] = c0_ref[0]
        m_sc[...] = m0_ref[0]
    else:
        caug_sc[...] = jnp.zeros(caug_sc.shape, F32)
        m_sc[...] = jnp.zeros(m_sc.shape, F32)
    static_chunks = nc <= LOOP_UNROLL
    if not static_chunks:
        h_ref[...] = jnp.zeros(h_ref.shape, F32)
    seen = set()

    ti = lax.broadcasted_iota(jnp.int32, (ML_CHUNK, ML_CHUNK), 0)
    si = lax.broadcasted_iota(jnp.int32, (ML_CHUNK, ML_CHUNK), 1)
    masks = (si <= ti, si >= ti)
    ones_blk = jnp.ones((ML_CHUNK, LANES), BF16)
    full = (ML_CHUNK, ML_CHUNK)

    def body(c, carry):
        for d in range(2):
            ci = c if d == 0 else nc - 1 - c
            if static_chunks:
                rows = pl.ds(ci * ML_CHUNK, ML_CHUNK)
                accumulate = ci in seen
                seen.add(ci)
            else:
                rows = pl.ds(pl.multiple_of(ci * ML_CHUNK, ML_CHUNK), ML_CHUNK)
                accumulate = True
            mask = masks[d]
            tri = jnp.where(mask, 1.0, 0.0).astype(BF16)
            gblk = g_ref[rows, :]
            hi = gblk.astype(BF16)
            lo = (gblk - hi.astype(F32)).astype(BF16)
            bc = _dot(tri, hi) + _dot(tri, lo)
            g_t = gblk.T
            bc_t = bc.T
            last = ML_CHUNK - 1 if d == 0 else 0
            def qk_scores(hd, rows=rows):
                return _dot_nt(q_ref[rows, ML_DH * hd:ML_DH * (hd + 1)].astype(BF16),
                               k_ref[rows, ML_DH * hd:ML_DH * (hd + 1)].astype(BF16))

            qk = None if static_chunks else [qk_scores(hd) for hd in range(ML_HEADS)]
            for hd in range(ML_HEADS):
                ch_i = 2 * ML_HEADS * d + hd
                ch_f = ch_i + ML_HEADS
                idx = ML_HEADS * d + hd
                hs = slice(ML_DH * hd, ML_DH * (hd + 1))
                b_col = bc[:, ch_f:ch_f + 1]
                a_col = gblk[:, ch_i:ch_i + 1] - b_col
                a_msk = jnp.where(mask, g_t[ch_i:ch_i + 1, :] - bc_t[ch_f:ch_f + 1, :], -jnp.inf)
                m_col = jnp.max(a_msk, axis=1, keepdims=True)
                m_rep = jnp.broadcast_to(m_col, full)
                b_rep = jnp.broadcast_to(b_col, full)
                a_rep = jnp.broadcast_to(a_col, full)
                m_last = m_col[last:last + 1, :]
                b_last = b_col[last:last + 1, :]
                qh = q_ref[rows, hs].astype(BF16)
                kf = k_ref[rows, hs]
                vaug = jnp.concatenate([v_ref[rows, hs].astype(BF16), ones_blk], axis=1)
                s0 = (qk_scores(hd) if qk is None else qk[hd]) * jnp.exp(a_msk - m_rep)
                sv0 = _dot(s0.astype(BF16), vaug)
                u0 = _dot_tn((jnp.exp(a_rep - m_last) * kf).astype(BF16), vaug)
                m_prev = m_sc[idx][0:1, 0:1]
                g_rep = jnp.maximum(m_prev, m_rep)
                w0 = jnp.exp(m_prev - g_rep)
                cc = jnp.exp(m_rep - g_rep)
                caug = caug_sc[idx]
                qc = _dot(qh, caug.astype(BF16))
                num = w0 * qc[:, :ML_DH] + cc * sv0[:, :ML_DH]
                den = w0 * qc[:, ML_DH:] + cc * sv0[:, ML_DH:]
                h_dir = num / jnp.maximum(jnp.abs(den), jnp.exp(-b_rep - g_rep))
                if accumulate:
                    h_ref[rows, hs] += h_dir
                else:
                    h_ref[rows, hs] = h_dir
                g_last = jnp.maximum(m_prev, m_last)
                caug_sc[idx] = jnp.exp(m_prev - g_last) * caug + jnp.exp(m_last - g_last) * u0
                m_sc[idx] = jnp.broadcast_to(b_last + g_last, m_sc.shape[1:])
        return carry

    if static_chunks:
        for c in range(nc):
            body(c, 0)
    else:
        lax.fori_loop(0, nc, body, 0, unroll=LOOP_UNROLL)
    if emit_state:
        slots = [(0, dl) for dl in range(DEPTH)] if state_first else [(0,)]
        for slot in slots:
            cn_ref[slot] = caug_sc[:, :, :ML_DH]
        for idx in range(2 * ML_HEADS):
            n_row = caug_sc[idx][:, ML_DH:].T[0:1, :]
            for slot in slots:
                nn_ref[slot + (slice(idx, idx + 1), slice(None))] = n_row
        mn_ref[0] = m_sc[...]


def _mlstm(q, k, v, g, seq, init=None, emit_state=False, layer=0, state_prev=None):
    T = q.shape[0]
    n_chain = 2 * ML_HEADS
    blk = pl.BlockSpec((seq, 512), lambda b: (b, 0))
    gblk = pl.BlockSpec((seq, LANES), lambda b: (b, 0))
    cblk = pl.BlockSpec((1, n_chain, ML_DH, 2 * ML_DH), lambda b: (b, 0, 0, 0))
    mblk = pl.BlockSpec((1, n_chain, 8, LANES), lambda b: (b, 0, 0, 0))
    in_specs = [blk, blk, blk, gblk]
    args = [q, k, v, g]
    if init is not None:
        in_specs += [cblk, mblk]
        args += list(init)
    out_specs = [blk]
    out_shape = [jax.ShapeDtypeStruct((T, 512), F32)]
    nb = T // seq
    aliases = {}
    if emit_state:
        if state_prev is None:
            out_specs += [pl.BlockSpec((1, DEPTH, n_chain, ML_DH, ML_DH), lambda b: (b, 0, 0, 0, 0)),
                          pl.BlockSpec((1, DEPTH, n_chain, ML_DH), lambda b: (b, 0, 0, 0)), mblk]
        else:
            out_specs += [pl.BlockSpec((1, None, n_chain, ML_DH, ML_DH), lambda b: (b, layer, 0, 0, 0)),
                          pl.BlockSpec((1, None, n_chain, ML_DH), lambda b: (b, layer, 0, 0)), mblk]
            aliases = {len(args): 1, len(args) + 1: 2}
            in_specs += [pl.BlockSpec(memory_space=pl.ANY)] * 2
            args += list(state_prev)
        out_shape += [jax.ShapeDtypeStruct((nb, DEPTH, n_chain, ML_DH, ML_DH), F32),
                      jax.ShapeDtypeStruct((nb, DEPTH, n_chain, ML_DH), F32),
                      jax.ShapeDtypeStruct((nb, n_chain, 8, LANES), F32)]
    return pl.pallas_call(
        functools.partial(_mlstm_kernel, nc=seq // ML_CHUNK, has_init=init is not None,
                          emit_state=emit_state, state_first=state_prev is None),
        grid=(nb,),
        in_specs=in_specs,
        out_specs=out_specs,
        out_shape=out_shape,
        input_output_aliases=aliases,
        scratch_shapes=[pltpu.VMEM((n_chain, ML_DH, 2 * ML_DH), F32),
                        pltpu.VMEM((n_chain, 8, LANES), F32)],
        compiler_params=_cparams(1),
        name="mlstm_state" if emit_state else "mlstm",
    )(*args)


def _outproj_kernel(x_ref, nao_ref, dfo_ref, mlh_ref, mo_ref, mod_ref, g2_ref, outn_ref, wout_ref,
                    wr_ref, br_ref, x1_ref, h2_ref, gates_ref):
    mlh = mlh_ref[...]
    parts = []
    for hd in range(ML_HEADS):
        seg = mlh[:, ML_DH * hd:ML_DH * (hd + 1)]
        parts.append(seg * lax.rsqrt(jnp.mean(seg * seg, axis=-1, keepdims=True) + EPS))
    ml_o = jnp.concatenate(parts, axis=1) * outn_ref[...] * jax.nn.sigmoid(mo_ref[...])
    mix = (_dot(nao_ref[...].astype(BF16), wout_ref[0:256, :])
           + _dot(dfo_ref[...].astype(BF16), wout_ref[256:512, :])
           + _dot(ml_o.astype(BF16), wout_ref[512:1024, :]))
    x1 = x_ref[...] + mod_ref[0, 2:3, :] * mix
    x1_ref[...] = x1
    y = x1 * lax.rsqrt(jnp.mean(x1 * x1, axis=-1, keepdims=True) + EPS) * g2_ref[...]
    h2 = y * (1.0 + mod_ref[0, 4:5, :]) + mod_ref[0, 3:4, :]
    h2_ref[...] = h2.astype(BF16)

    wr = wr_ref[...]
    h_hi = h2.astype(BF16)
    h_lo = (h2 - h_hi.astype(F32)).astype(BF16)
    w_hi = wr.astype(BF16)
    w_lo = (wr - w_hi.astype(F32)).astype(BF16)
    logits = _dot(h_hi, w_hi) + _dot(h_lo, w_hi) + _dot(h_hi, w_lo)
    tm = logits.shape[0]
    logits_t = jnp.concatenate([logits, jnp.zeros((tm, LANES - N_EXPERTS), F32)], axis=1).T[:N_EXPERTS]
    s = jax.nn.sigmoid(logits_t)
    work = s + br_ref[...]
    row = lax.broadcasted_iota(jnp.int32, work.shape, 0)
    s_sel = jnp.zeros(work.shape, F32)
    for _ in range(TOP_K):
        mx = jnp.max(work, axis=0, keepdims=True)
        first = jnp.min(jnp.where(work == mx, row, N_EXPERTS), axis=0, keepdims=True)
        hit = row == first
        s_sel = jnp.where(hit, s, s_sel)
        work = jnp.where(hit, -jnp.inf, work)
    gates_t = s_sel / jnp.sum(s_sel, axis=0, keepdims=True) * ROUTED_SCALE
    gates_ref[...] = jnp.concatenate([gates_t, jnp.zeros((LANES - N_EXPERTS, tm), F32)], axis=0).T[:, :N_EXPERTS]


def _outproj(x, nao, dfo, mlh, mo, mod_p, tokens_per_mod, layer, g2, outn, w_out_bf, w_router, b_router):
    T = x.shape[0]
    tm = TM_PROJ
    row = lambda i: (i, 0)
    const = lambda i: (0, 0)
    lay = lambda i: (layer, 0, 0)
    return pl.pallas_call(
        _outproj_kernel,
        grid=(T // tm,),
        in_specs=[pl.BlockSpec((tm, D_MODEL), row),
                  pl.BlockSpec((tm, 256), row), pl.BlockSpec((tm, 256), row),
                  pl.BlockSpec((tm, 512), row), pl.BlockSpec((tm, 512), row),
                  pl.BlockSpec((1, 6, D_MODEL), lambda i: ((i * tm) // tokens_per_mod, 0, 0)),
                  pl.BlockSpec((1, D_MODEL), const), pl.BlockSpec((1, 512), const),
                  pl.BlockSpec((None, D_MODEL, D_MODEL), lay),
                  pl.BlockSpec((None, D_MODEL, N_EXPERTS), lay), pl.BlockSpec((N_EXPERTS, 1), const)],
        out_specs=[pl.BlockSpec((tm, D_MODEL), row), pl.BlockSpec((tm, D_MODEL), row),
                   pl.BlockSpec((tm, N_EXPERTS), row)],
        out_shape=[jax.ShapeDtypeStruct((T, D_MODEL), F32), jax.ShapeDtypeStruct((T, D_MODEL), BF16),
                   jax.ShapeDtypeStruct((T, N_EXPERTS), F32)],
        compiler_params=_cparams(1),
        name="outproj_router",
    )(x, nao, dfo, mlh, mo, mod_p, g2, outn, w_out_bf, w_router, b_router)


def _dispatch_kernel(h_ref, gates_ref, xs_ref, cnt_ref, route_ref, *, n_tiles):
    i = pl.program_id(0)

    @pl.when(i == n_tiles)
    def _():
        xs_ref[...] = jnp.zeros(xs_ref.shape, BF16)

    @pl.when(i < n_tiles)
    def _():
        _dispatch_tile(h_ref, gates_ref, xs_ref, cnt_ref, route_ref)


def _dispatch_tile(h_ref, gates_ref, xs_ref, cnt_ref, route_ref):
    tt = h_ref.shape[0]
    n_slots = xs_ref.shape[0]
    gates = gates_ref[...]
    sel = jnp.where(gates > 0.0, 1.0, 0.0)
    selb = sel.astype(BF16)
    ti = lax.broadcasted_iota(jnp.int32, (tt, tt), 0)
    si = lax.broadcasted_iota(jnp.int32, (tt, tt), 1)
    earlier = jnp.where(si < ti, 1.0, 0.0).astype(BF16)
    pos = _dot(earlier, selb)
    cnt = jnp.sum(sel, axis=0, keepdims=True)
    seg = jnp.floor((cnt + (MOE_GRANULE - 1.0)) * (1.0 / MOE_GRANULE)) * MOE_GRANULE
    ei = lax.broadcasted_iota(jnp.int32, (N_EXPERTS, N_EXPERTS), 0)
    ej = lax.broadcasted_iota(jnp.int32, (N_EXPERTS, N_EXPERTS), 1)
    before = jnp.where(ei < ej, 1.0, 0.0).astype(BF16)
    off = _dot(jnp.broadcast_to(seg, (16, N_EXPERTS)).astype(BF16), before)[0:1]
    rank = _dot(selb, before)
    slot = off + pos
    lane = _lane_iota((tt, LANES))
    route = jnp.zeros((tt, LANES), F32)
    for k in range(TOP_K):
        hit = jnp.where(rank == k, sel, 0.0)
        has = jnp.sum(hit, axis=1, keepdims=True)
        slot_k = jnp.sum(hit * slot, axis=1, keepdims=True) + has - 1.0
        w_k = jnp.sum(hit * gates, axis=1, keepdims=True)
        route = jnp.where(lane == k, slot_k, jnp.where(lane == TOP_K + k, w_k, route))
    route_ref[...] = route
    cnt_ref[0] = cnt
    slots_t = route.T
    srow = lax.broadcasted_iota(jnp.int32, (n_slots, tt), 0).astype(F32)
    perm = jnp.zeros((n_slots, tt), F32)
    for k in range(TOP_K):
        perm = jnp.where(srow == slots_t[k:k + 1, :], 1.0, perm)
    xs_ref[...] = _dot(perm.astype(BF16), h_ref[...]).astype(BF16)


def _dispatch(h2, gates):
    T = h2.shape[0]
    nt = T // MOE_TILE
    tile = lambda i: (jnp.minimum(i, nt - 1), 0)
    return pl.pallas_call(
        functools.partial(_dispatch_kernel, n_tiles=nt),
        grid=(nt + 1,),
        in_specs=[pl.BlockSpec((MOE_TILE, D_MODEL), tile),
                  pl.BlockSpec((MOE_TILE, N_EXPERTS), tile)],
        out_specs=[pl.BlockSpec((MOE_SLOTS, D_MODEL), lambda i: (i, 0)),
                   pl.BlockSpec((1, 1, N_EXPERTS), lambda i: (jnp.minimum(i, nt - 1), 0, 0)),
                   pl.BlockSpec((MOE_TILE, LANES), tile)],
        out_shape=[jax.ShapeDtypeStruct(((nt + 1) * MOE_SLOTS, D_MODEL), BF16),
                   jax.ShapeDtypeStruct((nt, 1, N_EXPERTS), F32),
                   jax.ShapeDtypeStruct((T, LANES), F32)],
        compiler_params=_cparams(1),
        name="moe_dispatch",
    )(h2, gates)


def _expert_fused_kernel(te_ref, nact_ref, src_ref, dst_ref, slots_in, w1_ref, w3_ref, w2_ref,
                         slots_hbm, xbuf, ybuf, sem_in, sem_out, *, n_tiles):
    del slots_in
    xs_hbm = ys_hbm = slots_hbm
    i = pl.program_id(0)
    n_act = nact_ref[0]
    gr = MOE_ROWS // MOE_GRANULE
    spare0 = ys_hbm.shape[0] // MOE_GRANULE - MOE_SLOTS // MOE_GRANULE

    def rows(g):
        if isinstance(g, int):
            return pl.ds(g * MOE_GRANULE, MOE_GRANULE)
        return pl.ds(pl.multiple_of(g * MOE_GRANULE, MOE_GRANULE), MOE_GRANULE)

    zero_granule = MOE_SLOTS // MOE_GRANULE - 1

    def gather_start(tile, slot, dummy=False):
        for g in range(gr):
            s = src_ref[tile * gr + g]
            if dummy is not False:
                s = jnp.where(dummy, zero_granule, s)
            pltpu.make_async_copy(xs_hbm.at[rows(s)], xbuf.at[slot, rows(g)], sem_in.at[slot]).start()

    def gather_wait(slot):
        for g in range(gr):
            pltpu.make_async_copy(xs_hbm.at[rows(0)], xbuf.at[slot, rows(g)], sem_in.at[slot]).wait()

    def scatter_start(tile, slot):
        for g in range(gr):
            dd = dst_ref[tile * gr + g]
            dd = jnp.where(dd < 0, spare0 + slot * gr + g, dd)
            pltpu.make_async_copy(ybuf.at[slot, rows(g)], ys_hbm.at[rows(dd)], sem_out.at[slot]).start()

    def scatter_wait(slot):
        for g in range(gr):
            pltpu.make_async_copy(ybuf.at[slot, rows(g)], ys_hbm.at[rows(0)], sem_out.at[slot]).wait()

    slot = lax.rem(i, 2)

    @pl.when((i == 0) & (n_act > 0))
    def _():
        gather_start(0, 0)

    @pl.when(i < n_act)
    def _():
        gather_start(jnp.minimum(i + 1, n_act - 1), 1 - slot, dummy=i + 1 >= n_act)
        gather_wait(slot)

        @pl.when(i >= 2)
        def _():
            scatter_wait(slot)

        x = xbuf[slot]
        a = _dot(x, w1_ref[...].astype(BF16))
        b = _dot(x, w3_ref[...].astype(BF16))
        ybuf[slot] = _dot((_silu(a) * b).astype(BF16), w2_ref[...].astype(BF16)).astype(BF16)
        scatter_start(i, slot)

    @pl.when((i == n_tiles - 1) & (n_act > 0))
    def _():
        last_slot = lax.rem(n_act - 1, 2)
        gather_wait(1 - last_slot)
        scatter_wait(last_slot)

        @pl.when(n_act > 1)
        def _():
            scatter_wait(1 - last_slot)


def _expert_fused(tile_expert, n_active, src, dst, xs_tiles, layer, w1, w3, w2):
    n_tiles = src.shape[0] // (MOE_ROWS // MOE_GRANULE)
    wmap = lambda i, te, na, s, d: (layer, te[i], 0, 0)
    stage = pltpu.VMEM((2, MOE_ROWS, D_MODEL), BF16)
    assert MOE_SLOTS >= 2 * MOE_ROWS
    return pl.pallas_call(
        functools.partial(_expert_fused_kernel, n_tiles=n_tiles),
        grid_spec=pltpu.PrefetchScalarGridSpec(
            num_scalar_prefetch=4, grid=(n_tiles,),
            in_specs=[pl.BlockSpec(memory_space=pl.ANY),
                      pl.BlockSpec((None, None, D_MODEL, D_EXPERT), wmap),
                      pl.BlockSpec((None, None, D_MODEL, D_EXPERT), wmap),
                      pl.BlockSpec((None, None, D_EXPERT, D_MODEL), wmap)],
            out_specs=pl.BlockSpec(memory_space=pl.ANY),
            scratch_shapes=[stage, stage, pltpu.SemaphoreType.DMA((2,)), pltpu.SemaphoreType.DMA((2,))]),
        out_shape=jax.ShapeDtypeStruct(xs_tiles.shape, BF16),
        input_output_aliases={4: 0},
        compiler_params=_cparams(1),
        name="moe_expert_rows",
    )(tile_expert, n_active, src, dst, xs_tiles, w1, w3, w2)


def _combine_kernel(ys_ref, route_ref, h_ref, x1_ref, mod_ref, s1_ref, s3_ref, s2_ref, o_ref):
    route = route_ref[...]
    tt = route.shape[0]
    n_slots = ys_ref.shape[0]
    slot_lane = _lane_iota((tt, n_slots)).astype(F32)
    wp = jnp.zeros((tt, n_slots), F32)
    for k in range(TOP_K):
        wp = jnp.where(slot_lane == route[:, k:k + 1], route[:, TOP_K + k:TOP_K + k + 1], wp)
    routed = _dot(wp.astype(BF16), ys_ref[...])
    h = h_ref[...]
    a = _dot(h, s1_ref[...].astype(BF16))
    b = _dot(h, s3_ref[...].astype(BF16))
    shared = _dot((_silu(a) * b).astype(BF16), s2_ref[...].astype(BF16))
    o_ref[...] = x1_ref[...] + mod_ref[0, 5:6, :] * (routed + shared)


def _combine(ys_tiles, route, h2, x1, mod_p, tokens_per_mod, layer, s1, s3, s2):
    T = h2.shape[0]
    tt = MOE_TILE
    row = lambda i: (i, 0)
    wsh = lambda i: (layer, 0, 0)
    return pl.pallas_call(
        _combine_kernel,
        grid=(T // tt,),
        in_specs=[pl.BlockSpec((MOE_SLOTS, D_MODEL), row), pl.BlockSpec((tt, LANES), row),
                  pl.BlockSpec((tt, D_MODEL), row), pl.BlockSpec((tt, D_MODEL), row),
                  pl.BlockSpec((1, 6, D_MODEL), lambda i: ((i * tt) // tokens_per_mod, 0, 0)),
                  pl.BlockSpec((None, D_MODEL, D_EXPERT), wsh), pl.BlockSpec((None, D_MODEL, D_EXPERT), wsh),
                  pl.BlockSpec((None, D_EXPERT, D_MODEL), wsh)],
        out_specs=pl.BlockSpec((tt, D_MODEL), row),
        out_shape=jax.ShapeDtypeStruct((T, D_MODEL), F32),
        compiler_params=_cparams(1),
        name="moe_combine",
    )(ys_tiles, route, h2, x1, mod_p, s1, s3, s2)


def _excl_cumsum(a, axis):
    return jnp.cumsum(a, axis=axis) - a


def _moe_plan(cnt):
    nt = cnt.shape[0]
    gt = MOE_SLOTS // MOE_GRANULE
    gr = MOE_ROWS // MOE_GRANULE
    ng = _moe_sorted_granules(nt)
    g = (cnt + MOE_GRANULE - 1) // MOE_GRANULE
    loc = _excl_cumsum(g, 1)
    tot = jnp.sum(g, axis=0)
    grp = ((tot + gr - 1) // gr) * gr
    gs = _excl_cumsum(grp, 0)
    start = gs[None, :] + _excl_cumsum(g, 0)
    base = jnp.arange(nt, dtype=jnp.int32)[:, None] * gt + loc
    d = jnp.arange(ng, dtype=jnp.int32)[:, None, None]
    inseg = (start[None] <= d) & (d < (start + g)[None])
    local = jnp.sum(jnp.where(inseg, base[None] + d - start[None], 0), axis=(1, 2))
    real = jnp.any(inseg, axis=(1, 2))
    src = jnp.where(real, local, gt - 1)
    dst = jnp.where(real, local, -1)
    n_active = jnp.sum(grp) // gr
    first = jnp.minimum(jnp.arange(ng // gr, dtype=jnp.int32), jnp.maximum(n_active - 1, 0)) * gr
    tile_expert = jnp.sum((gs[None, :] <= first[:, None]).astype(jnp.int32), axis=1) - 1
    return (src.astype(jnp.int32), dst.astype(jnp.int32), tile_expert.astype(jnp.int32),
            n_active.astype(jnp.int32).reshape(1))


def _moe_sorted_granules(nt):
    gt_used = (TOP_K * MOE_TILE + N_EXPERTS * (MOE_GRANULE - 1)) // MOE_GRANULE
    gr = MOE_ROWS // MOE_GRANULE
    need = nt * gt_used + N_EXPERTS * (gr - 1) + 1
    return -(-need // gr) * gr


def _moe_sparse(h2, gates, x1, mod_p, tokens_per_mod, layer, w1, w3, w2, s1, s3, s2):
    xs_tiles, cnt, route = _dispatch(h2, gates)
    nt = cnt.shape[0]
    src, dst, tile_expert, n_active = _moe_plan(cnt.reshape(nt, N_EXPERTS).astype(jnp.int32))
    ys_tiles = _expert_fused(tile_expert, n_active, src, dst, xs_tiles, layer, w1, w3, w2)
    return _combine(ys_tiles, route, h2, x1, mod_p, tokens_per_mod, layer, s1, s3, s2)


def _group_matrix(group):
    i = np.arange(256)
    return jnp.asarray((i[:, None] // group) == (i[None, :] // group), BF16)


def _rope_tables(seq):
    t = np.arange(seq)
    lane = np.arange(256)
    j = lane % DF_DQK
    nf = DF_DQK // 4
    pos = np.where((j < DF_DQK // 2)[None, :], (t // GRID_W)[:, None], (t % GRID_W)[:, None]).astype(np.float32)
    inv = (1.0 / (ROPE_BASE ** (np.arange(nf, dtype=np.float32) / nf))).astype(np.float32)
    ang = pos * inv[j % nf][None, :]
    sign = np.where((lane % (2 * nf)) < nf, -1.0, 1.0).astype(np.float32)
    return jnp.asarray(np.cos(ang), F32), jnp.asarray(np.sin(ang) * sign[None, :], F32)


def _token_major(a):
    b, h, l, d = a.shape
    return a.transpose(0, 2, 1, 3).reshape(b, l, h * d)


def kernel(x_prompt, x_sample, cache_na_k, cache_na_v, cache_df_k, cache_df_v, state_ml_C, state_ml_n,
           state_ml_m, c, c_ctx, w_ada, b_ada, g_norm1, g_norm2, w_in, w_out, na_qn, na_kn, na_rpb,
           df_qn, df_kn, df_lam, df_subln, ml_gate_b, ml_outn, w_router, b_router, w_exp1, w_exp3, w_exp2,
           w_sh1, w_sh3, w_sh2):
    bc, lc, _ = x_prompt.shape
    bs, ls, _ = x_sample.shape
    past = cache_na_k.shape[3]

    cond = jnp.zeros((8, D_MODEL), F32).at[0].set(c_ctx).at[1:1 + bs].set(c)
    mod = _modulation(cond, w_ada, b_ada)
    g64 = _group_matrix(NA_DH)
    g32 = _group_matrix(DF_DQK)
    rope_tabs = _rope_tables(ls)
    w_in_bf = jnp.pad(w_in.astype(BF16), ((0, 0), (0, 0), (0, IN_COLS_PAD - IN_COLS)))
    w_out_bf = w_out.astype(BF16)
    bias_tabs = _na_bias_tables(na_rpb, ls // GRID_W)
    moe_w = (w_exp1, w_exp3, w_exp2, w_sh1, w_sh3, w_sh2)

    y_p = x_prompt.reshape(bc * lc, D_MODEL)
    y_s = x_sample.reshape(bs * ls, D_MODEL)
    caches = states = None
    m_states = []
    for l in range(DEPTH):
        lam_init = 0.8 - 0.6 * math.exp(-0.3 * l)
        mod_l = mod[l].reshape(8, 6, D_MODEL)
        g1 = g_norm1[l][None, :]
        g2 = g_norm2[l][None, :]
        qn = jnp.tile(na_qn[l], NA_HEADS)[None, :]
        kn = jnp.tile(na_kn[l], NA_HEADS)[None, :]
        dqn = jnp.tile(df_qn[l], 2 * DF_HEADS)[None, :]
        dkn = jnp.tile(df_kn[l], 2 * DF_HEADS)[None, :]
        subln = jnp.tile(df_subln[l], DF_HEADS)[None, :]
        outn = jnp.tile(ml_outn[l], ML_HEADS)[None, :]
        gate_b = jnp.pad(ml_gate_b[l].reshape(-1), (0, LANES - 4 * ML_HEADS))[None, :]
        br = b_router[l][:, None]
        proj_w = (l, g1, w_in_bf, qn, kn, dqn, dkn, gate_b, g64, g32)

        mod_c = mod_l[0:1]
        tpm = bc * lc
        res = _inproj(y_p, mod_c, tpm, *proj_w, None, cache_seq=lc, cache_prev=caches)
        naq, nak, nav, dfq, dfk, dfv, mq, mk, mv, mo, mg = res[:11]
        caches = tuple(res[11:])
        nao, dfo = _ctx_attention(naq, nak, nav, dfq, dfk, dfv, df_lam[l], subln, lc, lam_init)
        mlh, c_n, n_n, m_n = _mlstm(mq, mk, mv, mg, lc, emit_state=True, layer=l, state_prev=states)
        states = (c_n, n_n)
        x1, h2, gates = _outproj(y_p, nao, dfo, mlh, mo, mod_c, tpm, l, g2, outn, w_out_bf, w_router, br)
        y_p = _moe_sparse(h2, gates, x1, mod_c, tpm, l, *moe_w)
        m_states.append(m_n[:, :, 0, 0].reshape(bc, 2, ML_HEADS))

        mod_s = mod_l[1:1 + bs]
        naq, nak, nav, dfq, dfk, dfv, mq, mk, mv, mo, mg = _inproj(y_s, mod_s, ls, *proj_w, rope_tabs)
        kc_na = _token_major(cache_na_k[:, l])
        vc_na = _token_major(cache_na_v[:, l])
        kc_df = cache_df_k[:, l].transpose(0, 3, 1, 2, 4).reshape(bs, past, 256)
        vc_df = _token_major(cache_df_v[:, l])
        nao = _lat_na(naq, nak, nav, kc_na, vc_na, bias_tabs, l, ls)
        dfo = _lat_df(dfq, dfk, dfv, kc_df, vc_df, df_lam[l], subln, ls, lam_init)
        n0 = jnp.broadcast_to(state_ml_n[:, l][..., None], (bs, 2, ML_HEADS, ML_DH, ML_DH))
        c0 = jnp.concatenate([state_ml_C[:, l], n0], axis=-1)
        c0 = c0.reshape(bs, 2 * ML_HEADS, ML_DH, 2 * ML_DH)
        m0 = jnp.broadcast_to(state_ml_m[:, l].reshape(bs, 2 * ML_HEADS, 1, 1), (bs, 2 * ML_HEADS, 8, LANES))
        (mlh,) = _mlstm(mq, mk, mv, mg, ls, init=(c0, m0))
        x1, h2, gates = _outproj(y_s, nao, dfo, mlh, mo, mod_s, ls, l, g2, outn, w_out_bf, w_router, br)
        y_s = _moe_sparse(h2, gates, x1, mod_s, ls, l, *moe_w)

    new_ml_c = states[0].reshape(bc, DEPTH, 2, ML_HEADS, ML_DH, ML_DH)
    new_ml_n = states[1].reshape(bc, DEPTH, 2, ML_HEADS, ML_DH)
    return (y_p.reshape(bc, lc, D_MODEL), y_s.reshape(bs, ls, D_MODEL), *caches,
            new_ml_c, new_ml_n, jnp.stack(m_states, axis=1))
```

```python
import functools
import math

import numpy as np
import jax
import jax.numpy as jnp
from jax import lax
from jax.experimental import pallas as pl
from jax.experimental.pallas import tpu as pltpu

F32 = jnp.float32
BF16 = jnp.bfloat16

D_MODEL = 1024
DEPTH = 2
GRID_W = 64
NA_HEADS = 4
NA_DH = 64
NA_WIN_R = 8
NA_WIN_C = 16
DF_HEADS = 4
DF_DV = 64
DF_DQK = 32
ML_HEADS = 4
ML_DH = 128
ML_CHUNK = 128
N_EXPERTS = 32
TOP_K = 4
D_EXPERT = 256
ROUTED_SCALE = 2.5
ROPE_BASE = 10000.0
EPS = 1e-6

LANES = 128
IN_COLS = 3600
IN_COLS_PAD = 3712
GATE_COL0 = 3584
MASK_VALUE = -1e30
VMEM_LIMIT = 56 * 1024 * 1024

TM_PROJ = 512
MOE_TILE = 256
MOE_GRANULE = 16
MOE_SLOTS = TOP_K * MOE_TILE + N_EXPERTS * MOE_GRANULE
MOE_ROWS = 512
Q_BLOCK_DF = 512
CTX_ATTN_BATCHES = 2
DF_SCORE_LEAD_CTX = 1
DF_SCORE_LEAD_LAT = 3
LOOP_UNROLL = 2
NA_Q_ROWS = 4
NA_K_ROWS = NA_Q_ROWS + NA_WIN_R
LOG2E = math.log2(math.e)


def _cparams(n_axes):
    return pltpu.CompilerParams(dimension_semantics=("arbitrary",) * n_axes,
                                vmem_limit_bytes=VMEM_LIMIT)


def _silu(x):
    return x * jax.nn.sigmoid(x)


def _dot(a, b):
    return jnp.dot(a, b, preferred_element_type=F32)


def _dot_nt(a, b):
    return lax.dot_general(a, b, (((1,), (1,)), ((), ())), preferred_element_type=F32)


def _dot_tn(a, b):
    return lax.dot_general(a, b, (((0,), (0,)), ((), ())), preferred_element_type=F32)


def _lane_iota(shape):
    return lax.broadcasted_iota(jnp.int32, shape, 1)


def _mod_kernel(cond_ref, w_ref, b_ref, o_ref):
    s = _silu(cond_ref[...]).astype(BF16)
    o_ref[0] = _dot(s, w_ref[0].astype(BF16)) + b_ref[0]


def _modulation(cond, w_ada, b_ada):
    tn = 3072
    n = 6 * D_MODEL
    return pl.pallas_call(
        _mod_kernel,
        grid=(DEPTH, n // tn),
        in_specs=[pl.BlockSpec((8, D_MODEL), lambda l, j: (0, 0)),
                  pl.BlockSpec((1, D_MODEL, tn), lambda l, j: (l, 0, j)),
                  pl.BlockSpec((1, 1, tn), lambda l, j: (l, 0, j))],
        out_specs=pl.BlockSpec((1, 8, tn), lambda l, j: (l, 0, j)),
        out_shape=jax.ShapeDtypeStruct((DEPTH, 8, n), F32),
        compiler_params=_cparams(2),
        name="adaln_mod",
    )(cond, w_ada, b_ada.reshape(DEPTH, 1, n))


def _group_rms(v, gmat_ref, group):
    ss = _dot((v * v).astype(BF16), gmat_ref[...])
    return v * lax.rsqrt(ss * (1.0 / group) + EPS)


def _rope(v, cos, sin_signed):
    first = (_lane_iota(v.shape) % 16) < 8
    partner = jnp.where(first, pltpu.roll(v, v.shape[1] - 8, 1), pltpu.roll(v, 8, 1))
    return v * cos + partner * sin_signed


def _inproj_kernel(*refs, rope, cache_seq, cache_first):
    (x_ref, mod_ref, g1_ref, w_ref, qn_ref, kn_ref, dqn_ref, dkn_ref, gb_ref, g64_ref, g32_ref) = refs[:11]
    pos = 11
    if rope:
        cos_ref, sin_ref = refs[pos:pos + 2]
        pos += 2
    if cache_seq and not cache_first:
        pos += 4
    (naq_ref, nak_ref, nav_ref, dfq_ref, dfk_ref, dfv_ref,
     mq_ref, mk_ref, mv_ref, mo_ref, mg_ref) = refs[pos:pos + 11]
    pos += 11
    x = x_ref[...]
    y = x * lax.rsqrt(jnp.mean(x * x, axis=-1, keepdims=True) + EPS) * g1_ref[...]
    h = (y * (1.0 + mod_ref[0, 1:2, :]) + mod_ref[0, 0:1, :]).astype(BF16)

    def proj(a, b):
        return _dot(h, w_ref[:, a:b])

    naq_ref[...] = _group_rms(proj(0, 256), g64_ref, NA_DH) * qn_ref[...]
    nak = _group_rms(proj(256, 512), g64_ref, NA_DH) * kn_ref[...]
    nav = proj(512, 768)
    dfq = _group_rms(proj(768, 1024), g32_ref, DF_DQK) * dqn_ref[...]
    dfk = _group_rms(proj(1024, 1280), g32_ref, DF_DQK) * dkn_ref[...]
    dfv = proj(1280, 1536)
    if rope:
        dfq = _rope(dfq, cos_ref[...], sin_ref[...])
        dfk = _rope(dfk, cos_ref[...], sin_ref[...])
    nak_ref[...] = nak
    nav_ref[...] = nav
    dfq_ref[...] = dfq
    dfk_ref[...] = dfk
    dfv_ref[...] = dfv
    if cache_seq:
        nk_hm, nv_hm, dk_hm, dv_hm = refs[pos:pos + 4]

        def put(ref, bi, idx, val):
            if cache_first:
                for dl in range(DEPTH):
                    ref[(bi, dl) + idx] = val
            else:
                ref[(bi,) + idx] = val

        for bi in range(x.shape[0] // cache_seq):
            rs = slice(bi * cache_seq, (bi + 1) * cache_seq)
            for hd in range(NA_HEADS):
                put(nk_hm, bi, (hd,), nak[rs, NA_DH * hd:NA_DH * (hd + 1)])
                put(nv_hm, bi, (hd,), nav[rs, NA_DH * hd:NA_DH * (hd + 1)])
                put(dv_hm, bi, (hd,), dfv[rs, DF_DV * hd:DF_DV * (hd + 1)])
                for a in range(2):
                    c0 = (2 * hd + a) * DF_DQK
                    put(dk_hm, bi, (hd, a), dfk[rs, c0:c0 + DF_DQK])
    mq_ref[...] = proj(1536, 2048)
    mk_ref[...] = proj(2048, 2560) * (ML_DH ** -0.5)
    mv_ref[...] = proj(2560, 3072)
    mo_ref[...] = proj(3072, 3584)
    g = proj(GATE_COL0, IN_COLS_PAD) + gb_ref[...]
    lane = _lane_iota(g.shape)
    is_forget = ((lane // ML_HEADS) % 2 == 1) & (lane < 4 * ML_HEADS)
    log_sig = jnp.minimum(g, 0.0) - jnp.log1p(jnp.exp(-jnp.abs(g)))
    mg_ref[...] = jnp.where(is_forget, log_sig, g)


def _inproj(x, mod_p, tokens_per_mod, layer, g1, w_in_bf, qn, kn, dqn, dkn, gate_b, g64, g32, rope_tabs,
            cache_seq=0, cache_prev=None):
    T = x.shape[0]
    tm = TM_PROJ
    rope = rope_tabs is not None
    row = lambda i: (i, 0)
    const = lambda i: (0, 0)
    in_specs = [pl.BlockSpec((tm, D_MODEL), row),
                pl.BlockSpec((1, 6, D_MODEL), lambda i: ((i * tm) // tokens_per_mod, 0, 0)),
                pl.BlockSpec((1, D_MODEL), const),
                pl.BlockSpec((None, D_MODEL, IN_COLS_PAD), lambda i: (layer, 0, 0)),
                pl.BlockSpec((1, 256), const), pl.BlockSpec((1, 256), const),
                pl.BlockSpec((1, 256), const), pl.BlockSpec((1, 256), const),
                pl.BlockSpec((1, LANES), const),
                pl.BlockSpec((256, 256), const), pl.BlockSpec((256, 256), const)]
    args = [x, mod_p, g1, w_in_bf, qn, kn, dqn, dkn, gate_b, g64, g32]
    if rope:
        seq = rope_tabs[0].shape[0]
        tab = lambda i: (i % (seq // tm), 0)
        in_specs += [pl.BlockSpec((tm, 256), tab), pl.BlockSpec((tm, 256), tab)]
        args += list(rope_tabs)
    widths = [256] * 6 + [512] * 4 + [LANES]
    out_specs = [pl.BlockSpec((tm, w), row) for w in widths]
    out_shape = [jax.ShapeDtypeStruct((T, w), F32) for w in widths]
    aliases = {}
    if cache_seq:
        nb = tm // cache_seq
        hm = (NA_HEADS, cache_seq, NA_DH)
        hm_df = (DF_HEADS, 2, cache_seq, DF_DQK)
        for k, tail in enumerate((hm, hm, hm_df, hm)):
            zeros = (0,) * len(tail)
            if cache_prev is None:
                out_specs.append(pl.BlockSpec((nb, DEPTH) + tail, lambda i, z=zeros: (i, 0) + z))
            else:
                out_specs.append(pl.BlockSpec((nb, None) + tail, lambda i, z=zeros: (i, layer) + z))
                aliases[len(args)] = len(widths) + k
                in_specs.append(pl.BlockSpec(memory_space=pl.ANY))
                args.append(cache_prev[k])
            out_shape.append(jax.ShapeDtypeStruct((T // cache_seq, DEPTH) + tail, F32))
    return pl.pallas_call(
        functools.partial(_inproj_kernel, rope=rope, cache_seq=cache_seq, cache_first=cache_prev is None),
        grid=(T // tm,),
        in_specs=in_specs,
        out_specs=out_specs,
        out_shape=out_shape,
        input_output_aliases=aliases,
        compiler_params=_cparams(1),
        name="inproj_rope" if rope else "inproj",
    )(*args)


def _df_lambda(lam_ref, lam_init):
    v = lam_ref[...]
    a = jnp.sum(v[0:1] * v[1:2], axis=1, keepdims=True)
    b = jnp.sum(v[2:3] * v[3:4], axis=1, keepdims=True)
    return jnp.exp(a) - jnp.exp(b) + lam_init


def _softmax_parts(scores):
    es = _softmax_exps(scores)
    tot = functools.reduce(lambda a, b: a + b, [jnp.sum(e, axis=-1, keepdims=True) for e in es])
    return [e.astype(BF16) for e in es], 1.0 / tot


def _softmax_exps(scores):
    m = functools.reduce(jnp.maximum, [jnp.max(s, axis=-1, keepdims=True) for s in scores])
    return [jnp.exp2(s - m) for s in scores]


def _attend_masked_heads(tiles, lead=1):
    lane = _lane_iota(tiles[0][0].shape)
    v_aug = [[jnp.concatenate([v, jnp.ones(v.shape, BF16)], axis=1) for v in v_segs] for _, _, v_segs, _ in tiles]
    n_heads = 2 * len(tiles)

    def scores(c):
        qp, k_segs, _, bias = tiles[c // 2]
        qm = jnp.where(lane // NA_DH == c % 2, qp, 0.0).astype(BF16)
        s = [_dot_nt(qm, k) for k in k_segs]
        if bias is not None:
            s[0] = s[0] + bias[c % 2]
        return s

    outs = []
    pending = [scores(c) for c in range(min(lead, n_heads))]
    for c in range(n_heads):
        if c + lead < n_heads:
            pending.append(scores(c + lead))
        es = _softmax_exps(pending.pop(0))
        o = functools.reduce(lambda x, y: x + y, [_dot(e.astype(BF16), v) for e, v in zip(es, v_aug[c // 2])])
        outs.append(o[:, :LANES] / o[:, LANES:])
    return [jnp.where(lane < NA_DH, outs[2 * t], outs[2 * t + 1]) for t in range(len(tiles))]


def _subln(o, subln_row, lam_init):
    lane = _lane_iota(o.shape)
    sq = o * o
    s0 = jnp.sum(jnp.where(lane < DF_DV, sq, 0.0), axis=-1, keepdims=True)
    s1 = jnp.sum(jnp.where(lane >= DF_DV, sq, 0.0), axis=-1, keepdims=True)
    ms = jnp.where(lane < DF_DV, s0, s1) * (1.0 / DF_DV)
    return o * lax.rsqrt(ms + EPS) * subln_row * (1.0 - lam_init)


def _diff_attn(q, k_segs, v_segs, lam, lead):
    n_maps = 2 * DF_HEADS
    per_tile = LANES // DF_DQK
    lane = _lane_iota((q.shape[0], LANES))
    tiles = []
    for t in range(n_maps // per_tile):
        sl = slice(LANES * t, LANES * (t + 1))
        tiles.append((q[:, sl] * (DF_DQK ** -0.5 * LOG2E),
                      [k[:, sl].astype(BF16) for k in k_segs], [v[:, sl].astype(BF16) for v in v_segs]))

    def scores(c):
        qp, kps, _ = tiles[c // per_tile]
        qm = jnp.where(lane // DF_DQK == c % per_tile, qp, 0.0).astype(BF16)
        return [_dot_nt(qm, kp) for kp in kps]

    def head_out(m0, m1, vps):
        (e0, r0), (e1, r1) = m0, m1
        c0 = r0.astype(BF16)
        c1 = (-(lam * r1)).astype(BF16)
        return functools.reduce(lambda x, y: x + y,
                                [_dot(x0 * c0 + x1 * c1, vp) for x0, x1, vp in zip(e0, e1, vps)])

    heads = []
    maps = []
    pending = [scores(c) for c in range(lead)]
    for c in range(n_maps):
        if c + lead < n_maps:
            pending.append(scores(c + lead))
        maps.append(_softmax_parts(pending.pop(0)))
        if c % 2 == 1:
            heads.append(head_out(maps[c - 1], maps[c], tiles[c // per_tile][2]))
    return [jnp.where(lane < DF_DV, heads[2 * t], heads[2 * t + 1]) for t in range(len(tiles))]


def _ctx_attn_kernel(naq_ref, nak_ref, nav_ref, dfq_ref, dfk_ref, dfv_ref, lam_ref, subln_ref,
                     nao_ref, dfo_ref, *, lam_init, seq):
    lam = _df_lambda(lam_ref, lam_init)
    for bi in range(naq_ref.shape[0] // seq):
        rs = slice(bi * seq, (bi + 1) * seq)
        tiles = []
        for t in range(2):
            sl = slice(LANES * t, LANES * (t + 1))
            tiles.append((naq_ref[rs, sl] * (NA_DH ** -0.5 * LOG2E), [nak_ref[rs, sl].astype(BF16)],
                          [nav_ref[rs, sl].astype(BF16)], None))
        for t, o in enumerate(_attend_masked_heads(tiles, lead=0)):
            nao_ref[rs, LANES * t:LANES * (t + 1)] = o
        outs = _diff_attn(dfq_ref[rs, :], [dfk_ref[rs, :]], [dfv_ref[rs, :]], lam, DF_SCORE_LEAD_CTX)
        for t, o in enumerate(outs):
            sl = slice(LANES * t, LANES * (t + 1))
            dfo_ref[rs, sl] = _subln(o, subln_ref[:, sl], lam_init)


def _ctx_attention(naq, nak, nav, dfq, dfk, dfv, df_lam, subln, seq, lam_init):
    T = naq.shape[0]
    rows = CTX_ATTN_BATCHES * seq
    blk = pl.BlockSpec((rows, 256), lambda b: (b, 0))
    return pl.pallas_call(
        functools.partial(_ctx_attn_kernel, lam_init=lam_init, seq=seq),
        grid=(T // rows,),
        in_specs=[blk] * 6 + [pl.BlockSpec((4, DF_DQK), lambda b: (0, 0)),
                              pl.BlockSpec((1, 256), lambda b: (0, 0))],
        out_specs=[blk, blk],
        out_shape=[jax.ShapeDtypeStruct((T, 256), F32)] * 2,
        compiler_params=_cparams(1),
        name="ctx_attention",
    )(naq, nak, nav, dfq, dfk, dfv, df_lam, subln)


def _na_block_geometry(rows):
    last_r0 = rows - NA_Q_ROWS
    return ((0, 0), (NA_Q_ROWS, 0), (last_r0, rows - NA_K_ROWS))


def _na_bias_kernel(rpb_ref, o_ref, *, rows):
    hd = pl.program_id(0)
    shape = (GRID_W, GRID_W)
    qi = lax.broadcasted_iota(jnp.int32, shape, 0)
    kc = lax.broadcasted_iota(jnp.int32, shape, 1)
    cs = jnp.clip(qi - NA_WIN_C // 2, 0, GRID_W - NA_WIN_C)
    inwin = (kc >= cs) & (kc < cs + NA_WIN_C)
    off = jnp.where(inwin, kc - qi + NA_WIN_C - 1, -1)
    n_c = 2 * NA_WIN_C - 1
    n_r = 2 * NA_WIN_R - 1
    blocks = []
    for ro in range(n_r):
        val = jnp.full(shape, MASK_VALUE, F32)
        for d in range(n_c):
            val = jnp.where(off == d, rpb_ref[hd * (n_r * n_c) + ro * n_c + d], val)
        blocks.append(val * LOG2E)
    masked = jnp.full(shape, MASK_VALUE * LOG2E, F32)
    for v, (r0, k0) in enumerate(_na_block_geometry(rows)):
        for rq in range(NA_Q_ROWS):
            r = r0 + rq
            rs = min(max(r - NA_WIN_R // 2, 0), rows - NA_WIN_R)
            for j, kr in enumerate(range(k0, k0 + NA_K_ROWS)):
                blk = blocks[kr - r + NA_WIN_R - 1] if rs <= kr < rs + NA_WIN_R else masked
                o_ref[0, v, GRID_W * rq:GRID_W * (rq + 1), GRID_W * j:GRID_W * (j + 1)] = blk


def _na_bias_tables(rpb, rows):
    n_lh = rpb.shape[0] * rpb.shape[1]
    blk = (3, NA_Q_ROWS * GRID_W, NA_K_ROWS * GRID_W)
    tabs = pl.pallas_call(
        functools.partial(_na_bias_kernel, rows=rows),
        grid=(n_lh,),
        in_specs=[pl.BlockSpec(memory_space=pltpu.SMEM)],
        out_specs=pl.BlockSpec((1,) + blk, lambda h: (h, 0, 0, 0)),
        out_shape=jax.ShapeDtypeStruct((n_lh,) + blk, F32),
        compiler_params=_cparams(1),
        name="na_bias",
    )(rpb.reshape(-1))
    return tabs.reshape(rpb.shape[:2] + blk)


def _lat_na_kernel(q_ref, k_ref, v_ref, kc_ref, vc_ref, bias_ref, o_ref, *, rows):
    i = pl.program_id(1)
    k0 = jnp.clip(i * NA_Q_ROWS - NA_WIN_R // 2, 0, rows - NA_K_ROWS)
    krows = pl.ds(pl.multiple_of(k0 * GRID_W, GRID_W), NA_K_ROWS * GRID_W)
    q = q_ref[...] * (NA_DH ** -0.5 * LOG2E)
    tiles = []
    for t in range(2):
        sl = slice(LANES * t, LANES * (t + 1))
        tiles.append((q[:, sl],
                      [k_ref[krows, sl].astype(BF16), kc_ref[0, :, sl].astype(BF16)],
                      [v_ref[krows, sl].astype(BF16), vc_ref[0, :, sl].astype(BF16)],
                      [bias_ref[2 * t], bias_ref[2 * t + 1]]))
    for t, o in enumerate(_attend_masked_heads(tiles)):
        o_ref[:, LANES * t:LANES * (t + 1)] = o


def _lat_na(q, k, v, kc, vc, bias_tabs, layer, seq):
    T = q.shape[0]
    past = kc.shape[1]
    rows = seq // GRID_W
    nq = rows // NA_Q_ROWS
    qrows = NA_Q_ROWS * GRID_W
    qblk = pl.BlockSpec((qrows, 256), lambda b, i: (b * nq + i, 0))
    kblk = pl.BlockSpec((seq, 256), lambda b, i: (b, 0))
    cblk = pl.BlockSpec((1, past, 256), lambda b, i: (b, 0, 0))
    variant = lambda b, i: (layer, 0, jnp.where(i == 0, 0, jnp.where(i == nq - 1, 2, 1)), 0, 0)
    return pl.pallas_call(
        functools.partial(_lat_na_kernel, rows=rows),
        grid=(T // seq, nq),
        in_specs=[qblk, kblk, kblk, cblk, cblk,
                  pl.BlockSpec((None, NA_HEADS, None, qrows, NA_K_ROWS * GRID_W), variant)],
        out_specs=qblk,
        out_shape=jax.ShapeDtypeStruct((T, 256), F32),
        compiler_params=_cparams(2),
        name="lat_na",
    )(q, k, v, kc, vc, bias_tabs)


def _lat_df_kernel(q_ref, k_ref, v_ref, kc_ref, vc_ref, lam_ref, subln_ref, o_ref, *, lam_init):
    lam = _df_lambda(lam_ref, lam_init)
    outs = _diff_attn(q_ref[...], [k_ref[...], kc_ref[0]], [v_ref[...], vc_ref[0]], lam, DF_SCORE_LEAD_LAT)
    for t, o in enumerate(outs):
        sl = slice(LANES * t, LANES * (t + 1))
        o_ref[:, sl] = _subln(o, subln_ref[:, sl], lam_init)


def _lat_df(q, k, v, kc, vc, df_lam, subln, seq, lam_init):
    T = q.shape[0]
    past = kc.shape[1]
    nq = seq // Q_BLOCK_DF
    qblk = pl.BlockSpec((Q_BLOCK_DF, 256), lambda b, i: (b * nq + i, 0))
    kblk = pl.BlockSpec((seq, 256), lambda b, i: (b, 0))
    cblk = pl.BlockSpec((1, past, 256), lambda b, i: (b, 0, 0))
    return pl.pallas_call(
        functools.partial(_lat_df_kernel, lam_init=lam_init),
        grid=(T // seq, nq),
        in_specs=[qblk, kblk, kblk, cblk, cblk,
                  pl.BlockSpec((4, DF_DQK), lambda b, i: (0, 0)),
                  pl.BlockSpec((1, 256), lambda b, i: (0, 0))],
        out_specs=qblk,
        out_shape=jax.ShapeDtypeStruct((T, 256), F32),
        compiler_params=_cparams(2),
        name="lat_df",
    )(q, k, v, kc, vc, df_lam, subln)


def _mlstm_kernel(*refs, nc, has_init, emit_state, state_first):
    refs = list(refs)
    q_ref, k_ref, v_ref, g_ref = refs[:4]
    pos = 4
    if has_init:
        c0_ref, m0_ref = refs[pos:pos + 2]
        pos += 2
    if emit_state and not state_first:
        pos += 2
    h_ref = refs[pos]
    pos += 1
    if emit_state:
        cn_ref, nn_ref, mn_ref = refs[pos:pos + 3]
        pos += 3
    caug_sc, m_sc = refs[pos:pos + 2]

    if has_init:
        caug_sc[...] = c0_ref[0]
        m_sc[...] = m0_ref[0]
    else:
        caug_sc[...] = jnp.zeros(caug_sc.shape, F32)
        m_sc[...] = jnp.zeros(m_sc.shape, F32)
    static_chunks = nc <= LOOP_UNROLL
    if not static_chunks:
        h_ref[...] = jnp.zeros(h_ref.shape, F32)
    seen = set()

    ti = lax.broadcasted_iota(jnp.int32, (ML_CHUNK, ML_CHUNK), 0)
    si = lax.broadcasted_iota(jnp.int32, (ML_CHUNK, ML_CHUNK), 1)
    masks = (si <= ti, si >= ti)
    ones_blk = jnp.ones((ML_CHUNK, LANES), BF16)
    full = (ML_CHUNK, ML_CHUNK)

    def body(c, carry):
        for d in range(2):
            ci = c if d == 0 else nc - 1 - c
            if static_chunks:
                rows = pl.ds(ci * ML_CHUNK, ML_CHUNK)
                accumulate = ci in seen
                seen.add(ci)
            else:
                rows = pl.ds(pl.multiple_of(ci * ML_CHUNK, ML_CHUNK), ML_CHUNK)
                accumulate = True
            mask = masks[d]
            tri = jnp.where(mask, 1.0, 0.0).astype(BF16)
            gblk = g_ref[rows, :]
            hi = gblk.astype(BF16)
            lo = (gblk - hi.astype(F32)).astype(BF16)
            bc = _dot(tri, hi) + _dot(tri, lo)
            g_t = gblk.T
            bc_t = bc.T
            last = ML_CHUNK - 1 if d == 0 else 0
            def qk_scores(hd, rows=rows):
                return _dot_nt(q_ref[rows, ML_DH * hd:ML_DH * (hd + 1)].astype(BF16),
                               k_ref[rows, ML_DH * hd:ML_DH * (hd + 1)].astype(BF16))

            qk = None if static_chunks else [qk_scores(hd) for hd in range(ML_HEADS)]
            for hd in range(ML_HEADS):
                ch_i = 2 * ML_HEADS * d + hd
                ch_f = ch_i + ML_HEADS
                idx = ML_HEADS * d + hd
                hs = slice(ML_DH * hd, ML_DH * (hd + 1))
                b_col = bc[:, ch_f:ch_f + 1]
                a_col = gblk[:, ch_i:ch_i + 1] - b_col
                a_msk = jnp.where(mask, g_t[ch_i:ch_i + 1, :] - bc_t[ch_f:ch_f + 1, :], -jnp.inf)
                m_col = jnp.max(a_msk, axis=1, keepdims=True)
                m_rep = jnp.broadcast_to(m_col, full)
                b_rep = jnp.broadcast_to(b_col, full)
                a_rep = jnp.broadcast_to(a_col, full)
                m_last = m_col[last:last + 1, :]
                b_last = b_col[last:last + 1, :]
                qh = q_ref[rows, hs].astype(BF16)
                kf = k_ref[rows, hs]
                vaug = jnp.concatenate([v_ref[rows, hs].astype(BF16), ones_blk], axis=1)
                s0 = (qk_scores(hd) if qk is None else qk[hd]) * jnp.exp(a_msk - m_rep)
                sv0 = _dot(s0.astype(BF16), vaug)
                u0 = _dot_tn((jnp.exp(a_rep - m_last) * kf).astype(BF16), vaug)
                m_prev = m_sc[idx][0:1, 0:1]
                g_rep = jnp.maximum(m_prev, m_rep)
                w0 = jnp.exp(m_prev - g_rep)
                cc = jnp.exp(m_rep - g_rep)
                caug = caug_sc[idx]
                qc = _dot(qh, caug.astype(BF16))
                num = w0 * qc[:, :ML_DH] + cc * sv0[:, :ML_DH]
                den = w0 * qc[:, ML_DH:] + cc * sv0[:, ML_DH:]
                h_dir = num / jnp.maximum(jnp.abs(den), jnp.exp(-b_rep - g_rep))
                if accumulate:
                    h_ref[rows, hs] += h_dir
                else:
                    h_ref[rows, hs] = h_dir
                g_last = jnp.maximum(m_prev, m_last)
                caug_sc[idx] = jnp.exp(m_prev - g_last) * caug + jnp.exp(m_last - g_last) * u0
                m_sc[idx] = jnp.broadcast_to(b_last + g_last, m_sc.shape[1:])
        return carry

    if static_chunks:
        for c in range(nc):
            body(c, 0)
    else:
        lax.fori_loop(0, nc, body, 0, unroll=LOOP_UNROLL)
    if emit_state:
        slots = [(0, dl) for dl in range(DEPTH)] if state_first else [(0,)]
        for slot in slots:
            cn_ref[slot] = caug_sc[:, :, :ML_DH]
        for idx in range(2 * ML_HEADS):
            n_row = caug_sc[idx][:, ML_DH:].T[0:1, :]
            for slot in slots:
                nn_ref[slot + (slice(idx, idx + 1), slice(None))] = n_row
        mn_ref[0] = m_sc[...]


def _mlstm(q, k, v, g, seq, init=None, emit_state=False, layer=0, state_prev=None):
    T = q.shape[0]
    n_chain = 2 * ML_HEADS
    blk = pl.BlockSpec((seq, 512), lambda b: (b, 0))
    gblk = pl.BlockSpec((seq, LANES), lambda b: (b, 0))
    cblk = pl.BlockSpec((1, n_chain, ML_DH, 2 * ML_DH), lambda b: (b, 0, 0, 0))
    mblk = pl.BlockSpec((1, n_chain, 8, LANES), lambda b: (b, 0, 0, 0))
    in_specs = [blk, blk, blk, gblk]
    args = [q, k, v, g]
    if init is not None:
        in_specs += [cblk, mblk]
        args += list(init)
    out_specs = [blk]
    out_shape = [jax.ShapeDtypeStruct((T, 512), F32)]
    nb = T // seq
    aliases = {}
    if emit_state:
        if state_prev is None:
            out_specs += [pl.BlockSpec((1, DEPTH, n_chain, ML_DH, ML_DH), lambda b: (b, 0, 0, 0, 0)),
                          pl.BlockSpec((1, DEPTH, n_chain, ML_DH), lambda b: (b, 0, 0, 0)), mblk]
        else:
            out_specs += [pl.BlockSpec((1, None, n_chain, ML_DH, ML_DH), lambda b: (b, layer, 0, 0, 0)),
                          pl.BlockSpec((1, None, n_chain, ML_DH), lambda b: (b, layer, 0, 0)), mblk]
            aliases = {len(args): 1, len(args) + 1: 2}
            in_specs += [pl.BlockSpec(memory_space=pl.ANY)] * 2
            args += list(state_prev)
        out_shape += [jax.ShapeDtypeStruct((nb, DEPTH, n_chain, ML_DH, ML_DH), F32),
                      jax.ShapeDtypeStruct((nb, DEPTH, n_chain, ML_DH), F32),
                      jax.ShapeDtypeStruct((nb, n_chain, 8, LANES), F32)]
    return pl.pallas_call(
        functools.partial(_mlstm_kernel, nc=seq // ML_CHUNK, has_init=init is not None,
                          emit_state=emit_state, state_first=state_prev is None),
        grid=(nb,),
        in_specs=in_specs,
        out_specs=out_specs,
        out_shape=out_shape,
        input_output_aliases=aliases,
        scratch_shapes=[pltpu.VMEM((n_chain, ML_DH, 2 * ML_DH), F32),
                        pltpu.VMEM((n_chain, 8, LANES), F32)],
        compiler_params=_cparams(1),
        name="mlstm_state" if emit_state else "mlstm",
    )(*args)


def _outproj_kernel(x_ref, nao_ref, dfo_ref, mlh_ref, mo_ref, mod_ref, g2_ref, outn_ref, wout_ref,
                    wr_ref, br_ref, x1_ref, h2_ref, gates_ref):
    mlh = mlh_ref[...]
    parts = []
    for hd in range(ML_HEADS):
        seg = mlh[:, ML_DH * hd:ML_DH * (hd + 1)]
        parts.append(seg * lax.rsqrt(jnp.mean(seg * seg, axis=-1, keepdims=True) + EPS))
    ml_o = jnp.concatenate(parts, axis=1) * outn_ref[...] * jax.nn.sigmoid(mo_ref[...])
    mix = (_dot(nao_ref[...].astype(BF16), wout_ref[0:256, :])
           + _dot(dfo_ref[...].astype(BF16), wout_ref[256:512, :])
           + _dot(ml_o.astype(BF16), wout_ref[512:1024, :]))
    x1 = x_ref[...] + mod_ref[0, 2:3, :] * mix
    x1_ref[...] = x1
    y = x1 * lax.rsqrt(jnp.mean(x1 * x1, axis=-1, keepdims=True) + EPS) * g2_ref[...]
    h2 = y * (1.0 + mod_ref[0, 4:5, :]) + mod_ref[0, 3:4, :]
    h2_ref[...] = h2.astype(BF16)

    wr = wr_ref[...]
    h_hi = h2.astype(BF16)
    h_lo = (h2 - h_hi.astype(F32)).astype(BF16)
    w_hi = wr.astype(BF16)
    w_lo = (wr - w_hi.astype(F32)).astype(BF16)
    logits = _dot(h_hi, w_hi) + _dot(h_lo, w_hi) + _dot(h_hi, w_lo)
    tm = logits.shape[0]
    logits_t = jnp.concatenate([logits, jnp.zeros((tm, LANES - N_EXPERTS), F32)], axis=1).T[:N_EXPERTS]
    s = jax.nn.sigmoid(logits_t)
    work = s + br_ref[...]
    row = lax.broadcasted_iota(jnp.int32, work.shape, 0)
    s_sel = jnp.zeros(work.shape, F32)
    for _ in range(TOP_K):
        mx = jnp.max(work, axis=0, keepdims=True)
        first = jnp.min(jnp.where(work == mx, row, N_EXPERTS), axis=0, keepdims=True)
        hit = row == first
        s_sel = jnp.where(hit, s, s_sel)
        work = jnp.where(hit, -jnp.inf, work)
    gates_t = s_sel / jnp.sum(s_sel, axis=0, keepdims=True) * ROUTED_SCALE
    gates_ref[...] = jnp.concatenate([gates_t, jnp.zeros((LANES - N_EXPERTS, tm), F32)], axis=0).T[:, :N_EXPERTS]


def _outproj(x, nao, dfo, mlh, mo, mod_p, tokens_per_mod, layer, g2, outn, w_out_bf, w_router, b_router):
    T = x.shape[0]
    tm = TM_PROJ
    row = lambda i: (i, 0)
    const = lambda i: (0, 0)
    lay = lambda i: (layer, 0, 0)
    return pl.pallas_call(
        _outproj_kernel,
        grid=(T // tm,),
        in_specs=[pl.BlockSpec((tm, D_MODEL), row),
                  pl.BlockSpec((tm, 256), row), pl.BlockSpec((tm, 256), row),
                  pl.BlockSpec((tm, 512), row), pl.BlockSpec((tm, 512), row),
                  pl.BlockSpec((1, 6, D_MODEL), lambda i: ((i * tm) // tokens_per_mod, 0, 0)),
                  pl.BlockSpec((1, D_MODEL), const), pl.BlockSpec((1, 512), const),
                  pl.BlockSpec((None, D_MODEL, D_MODEL), lay),
                  pl.BlockSpec((None, D_MODEL, N_EXPERTS), lay), pl.BlockSpec((N_EXPERTS, 1), const)],
        out_specs=[pl.BlockSpec((tm, D_MODEL), row), pl.BlockSpec((tm, D_MODEL), row),
                   pl.BlockSpec((tm, N_EXPERTS), row)],
        out_shape=[jax.ShapeDtypeStruct((T, D_MODEL), F32), jax.ShapeDtypeStruct((T, D_MODEL), BF16),
                   jax.ShapeDtypeStruct((T, N_EXPERTS), F32)],
        compiler_params=_cparams(1),
        name="outproj_router",
    )(x, nao, dfo, mlh, mo, mod_p, g2, outn, w_out_bf, w_router, b_router)


def _dispatch_kernel(h_ref, gates_ref, xs_ref, cnt_ref, route_ref, *, n_tiles):
    i = pl.program_id(0)

    @pl.when(i == n_tiles)
    def _():
        xs_ref[...] = jnp.zeros(xs_ref.shape, BF16)

    @pl.when(i < n_tiles)
    def _():
        _dispatch_tile(h_ref, gates_ref, xs_ref, cnt_ref, route_ref)


def _dispatch_tile(h_ref, gates_ref, xs_ref, cnt_ref, route_ref):
    tt = h_ref.shape[0]
    n_slots = xs_ref.shape[0]
    gates = gates_ref[...]
    sel = jnp.where(gates > 0.0, 1.0, 0.0)
    selb = sel.astype(BF16)
    ti = lax.broadcasted_iota(jnp.int32, (tt, tt), 0)
    si = lax.broadcasted_iota(jnp.int32, (tt, tt), 1)
    earlier = jnp.where(si < ti, 1.0, 0.0).astype(BF16)
    pos = _dot(earlier, selb)
    cnt = jnp.sum(sel, axis=0, keepdims=True)
    seg = jnp.floor((cnt + (MOE_GRANULE - 1.0)) * (1.0 / MOE_GRANULE)) * MOE_GRANULE
    ei = lax.broadcasted_iota(jnp.int32, (N_EXPERTS, N_EXPERTS), 0)
    ej = lax.broadcasted_iota(jnp.int32, (N_EXPERTS, N_EXPERTS), 1)
    before = jnp.where(ei < ej, 1.0, 0.0).astype(BF16)
    off = _dot(jnp.broadcast_to(seg, (16, N_EXPERTS)).astype(BF16), before)[0:1]
    rank = _dot(selb, before)
    slot = off + pos
    lane = _lane_iota((tt, LANES))
    route = jnp.zeros((tt, LANES), F32)
    for k in range(TOP_K):
        hit = jnp.where(rank == k, sel, 0.0)
        has = jnp.sum(hit, axis=1, keepdims=True)
        slot_k = jnp.sum(hit * slot, axis=1, keepdims=True) + has - 1.0
        w_k = jnp.sum(hit * gates, axis=1, keepdims=True)
        route = jnp.where(lane == k, slot_k, jnp.where(lane == TOP_K + k, w_k, route))
    route_ref[...] = route
    cnt_ref[0] = cnt
    slots_t = route.T
    srow = lax.broadcasted_iota(jnp.int32, (n_slots, tt), 0).astype(F32)
    perm = jnp.zeros((n_slots, tt), F32)
    for k in range(TOP_K):
        perm = jnp.where(srow == slots_t[k:k + 1, :], 1.0, perm)
    xs_ref[...] = _dot(perm.astype(BF16), h_ref[...]).astype(BF16)


def _dispatch(h2, gates):
    T = h2.shape[0]
    nt = T // MOE_TILE
    tile = lambda i: (jnp.minimum(i, nt - 1), 0)
    return pl.pallas_call(
        functools.partial(_dispatch_kernel, n_tiles=nt),
        grid=(nt + 1,),
        in_specs=[pl.BlockSpec((MOE_TILE, D_MODEL), tile),
                  pl.BlockSpec((MOE_TILE, N_EXPERTS), tile)],
        out_specs=[pl.BlockSpec((MOE_SLOTS, D_MODEL), lambda i: (i, 0)),
                   pl.BlockSpec((1, 1, N_EXPERTS), lambda i: (jnp.minimum(i, nt - 1), 0, 0)),
                   pl.BlockSpec((MOE_TILE, LANES), tile)],
        out_shape=[jax.ShapeDtypeStruct(((nt + 1) * MOE_SLOTS, D_MODEL), BF16),
                   jax.ShapeDtypeStruct((nt, 1, N_EXPERTS), F32),
                   jax.ShapeDtypeStruct((T, LANES), F32)],
        compiler_params=_cparams(1),
        name="moe_dispatch",
    )(h2, gates)


def _expert_fused_kernel(te_ref, nact_ref, src_ref, dst_ref, slots_in, w1_ref, w3_ref, w2_ref,
                         slots_hbm, xbuf, ybuf, sem_in, sem_out, *, n_tiles):
    del slots_in
    xs_hbm = ys_hbm = slots_hbm
    i = pl.program_id(0)
    n_act = nact_ref[0]
    gr = MOE_ROWS // MOE_GRANULE
    spare0 = ys_hbm.shape[0] // MOE_GRANULE - MOE_SLOTS // MOE_GRANULE

    def rows(g):
        if isinstance(g, int):
            return pl.ds(g * MOE_GRANULE, MOE_GRANULE)
        return pl.ds(pl.multiple_of(g * MOE_GRANULE, MOE_GRANULE), MOE_GRANULE)

    zero_granule = MOE_SLOTS // MOE_GRANULE - 1

    def gather_start(tile, slot, dummy=False):
        for g in range(gr):
            s = src_ref[tile * gr + g]
            if dummy is not False:
                s = jnp.where(dummy, zero_granule, s)
            pltpu.make_async_copy(xs_hbm.at[rows(s)], xbuf.at[slot, rows(g)], sem_in.at[slot]).start()

    def gather_wait(slot):
        for g in range(gr):
            pltpu.make_async_copy(xs_hbm.at[rows(0)], xbuf.at[slot, rows(g)], sem_in.at[slot]).wait()

    def scatter_start(tile, slot):
        for g in range(gr):
            dd = dst_ref[tile * gr + g]
            dd = jnp.where(dd < 0, spare0 + slot * gr + g, dd)
            pltpu.make_async_copy(ybuf.at[slot, rows(g)], ys_hbm.at[rows(dd)], sem_out.at[slot]).start()

    def scatter_wait(slot):
        for g in range(gr):
            pltpu.make_async_copy(ybuf.at[slot, rows(g)], ys_hbm.at[rows(0)], sem_out.at[slot]).wait()

    slot = lax.rem(i, 2)

    @pl.when((i == 0) & (n_act > 0))
    def _():
        gather_start(0, 0)

    @pl.when(i < n_act)
    def _():
        gather_wait(slot)

        @pl.when(i >= 2)
        def _():
            scatter_wait(slot)

        x = xbuf[slot]
        a = _dot(x, w1_ref[...].astype(BF16))
        b = _dot(x, w3_ref[...].astype(BF16))
        gather_start(jnp.minimum(i + 1, n_act - 1), 1 - slot, dummy=i + 1 >= n_act)
        ybuf[slot] = _dot((_silu(a) * b).astype(BF16), w2_ref[...].astype(BF16)).astype(BF16)
        scatter_start(i, slot)

    @pl.when((i == n_tiles - 1) & (n_act > 0))
    def _():
        last_slot = lax.rem(n_act - 1, 2)
        gather_wait(1 - last_slot)
        scatter_wait(last_slot)

        @pl.when(n_act > 1)
        def _():
            scatter_wait(1 - last_slot)


def _expert_fused(tile_expert, n_active, src, dst, xs_tiles, layer, w1, w3, w2):
    n_tiles = src.shape[0] // (MOE_ROWS // MOE_GRANULE)
    wmap = lambda i, te, na, s, d: (layer, te[i], 0, 0)
    stage = pltpu.VMEM((2, MOE_ROWS, D_MODEL), BF16)
    assert MOE_SLOTS >= 2 * MOE_ROWS
    return pl.pallas_call(
        functools.partial(_expert_fused_kernel, n_tiles=n_tiles),
        grid_spec=pltpu.PrefetchScalarGridSpec(
            num_scalar_prefetch=4, grid=(n_tiles,),
            in_specs=[pl.BlockSpec(memory_space=pl.ANY),
                      pl.BlockSpec((None, None, D_MODEL, D_EXPERT), wmap),
                      pl.BlockSpec((None, None, D_MODEL, D_EXPERT), wmap),
                      pl.BlockSpec((None, None, D_EXPERT, D_MODEL), wmap)],
            out_specs=pl.BlockSpec(memory_space=pl.ANY),
            scratch_shapes=[stage, stage, pltpu.SemaphoreType.DMA((2,)), pltpu.SemaphoreType.DMA((2,))]),
        out_shape=jax.ShapeDtypeStruct(xs_tiles.shape, BF16),
        input_output_aliases={4: 0},
        compiler_params=_cparams(1),
        name="moe_expert_rows",
    )(tile_expert, n_active, src, dst, xs_tiles, w1, w3, w2)


def _combine_kernel(ys_ref, route_ref, h_ref, x1_ref, mod_ref, s1_ref, s3_ref, s2_ref, o_ref):
    route = route_ref[...]
    tt = route.shape[0]
    n_slots = ys_ref.shape[0]
    slot_lane = _lane_iota((tt, n_slots)).astype(F32)
    wp = jnp.zeros((tt, n_slots), F32)
    for k in range(TOP_K):
        wp = jnp.where(slot_lane == route[:, k:k + 1], route[:, TOP_K + k:TOP_K + k + 1], wp)
    routed = _dot(wp.astype(BF16), ys_ref[...])
    h = h_ref[...]
    a = _dot(h, s1_ref[...].astype(BF16))
    b = _dot(h, s3_ref[...].astype(BF16))
    shared = _dot((_silu(a) * b).astype(BF16), s2_ref[...].astype(BF16))
    o_ref[...] = x1_ref[...] + mod_ref[0, 5:6, :] * (routed + shared)


def _combine(ys_tiles, route, h2, x1, mod_p, tokens_per_mod, layer, s1, s3, s2):
    T = h2.shape[0]
    tt = MOE_TILE
    row = lambda i: (i, 0)
    wsh = lambda i: (layer, 0, 0)
    return pl.pallas_call(
        _combine_kernel,
        grid=(T // tt,),
        in_specs=[pl.BlockSpec((MOE_SLOTS, D_MODEL), row), pl.BlockSpec((tt, LANES), row),
                  pl.BlockSpec((tt, D_MODEL), row), pl.BlockSpec((tt, D_MODEL), row),
                  pl.BlockSpec((1, 6, D_MODEL), lambda i: ((i * tt) // tokens_per_mod, 0, 0)),
                  pl.BlockSpec((None, D_MODEL, D_EXPERT), wsh), pl.BlockSpec((None, D_MODEL, D_EXPERT), wsh),
                  pl.BlockSpec((None, D_EXPERT, D_MODEL), wsh)],
        out_specs=pl.BlockSpec((tt, D_MODEL), row),
        out_shape=jax.ShapeDtypeStruct((T, D_MODEL), F32),
        compiler_params=_cparams(1),
        name="moe_combine",
    )(ys_tiles, route, h2, x1, mod_p, s1, s3, s2)


def _excl_cumsum(a, axis):
    return jnp.cumsum(a, axis=axis) - a


def _moe_plan(cnt):
    nt = cnt.shape[0]
    gt = MOE_SLOTS // MOE_GRANULE
    gr = MOE_ROWS // MOE_GRANULE
    ng = _moe_sorted_granules(nt)
    g = (cnt + MOE_GRANULE - 1) // MOE_GRANULE
    loc = _excl_cumsum(g, 1)
    tot = jnp.sum(g, axis=0)
    grp = ((tot + gr - 1) // gr) * gr
    gs = _excl_cumsum(grp, 0)
    start = gs[None, :] + _excl_cumsum(g, 0)
    base = jnp.arange(nt, dtype=jnp.int32)[:, None] * gt + loc
    d = jnp.arange(ng, dtype=jnp.int32)[:, None, None]
    inseg = (start[None] <= d) & (d < (start + g)[None])
    local = jnp.sum(jnp.where(inseg, base[None] + d - start[None], 0), axis=(1, 2))
    real = jnp.any(inseg, axis=(1, 2))
    src = jnp.where(real, local, gt - 1)
    dst = jnp.where(real, local, -1)
    n_active = jnp.sum(grp) // gr
    first = jnp.minimum(jnp.arange(ng // gr, dtype=jnp.int32), jnp.maximum(n_active - 1, 0)) * gr
    tile_expert = jnp.sum((gs[None, :] <= first[:, None]).astype(jnp.int32), axis=1) - 1
    return (src.astype(jnp.int32), dst.astype(jnp.int32), tile_expert.astype(jnp.int32),
            n_active.astype(jnp.int32).reshape(1))


def _moe_sorted_granules(nt):
    gt_used = (TOP_K * MOE_TILE + N_EXPERTS * (MOE_GRANULE - 1)) // MOE_GRANULE
    gr = MOE_ROWS // MOE_GRANULE
    need = nt * gt_used + N_EXPERTS * (gr - 1) + 1
    return -(-need // gr) * gr


def _moe_sparse(h2, gates, x1, mod_p, tokens_per_mod, layer, w1, w3, w2, s1, s3, s2):
    xs_tiles, cnt, route = _dispatch(h2, gates)
    nt = cnt.shape[0]
    src, dst, tile_expert, n_active = _moe_plan(cnt.reshape(nt, N_EXPERTS).astype(jnp.int32))
    ys_tiles = _expert_fused(tile_expert, n_active, src, dst, xs_tiles, layer, w1, w3, w2)
    return _combine(ys_tiles, route, h2, x1, mod_p, tokens_per_mod, layer, s1, s3, s2)


def _group_matrix(group):
    i = np.arange(256)
    return jnp.asarray((i[:, None] // group) == (i[None, :] // group), BF16)


def _rope_tables(seq):
    t = np.arange(seq)
    lane = np.arange(256)
    j = lane % DF_DQK
    nf = DF_DQK // 4
    pos = np.where((j < DF_DQK // 2)[None, :], (t // GRID_W)[:, None], (t % GRID_W)[:, None]).astype(np.float32)
    inv = (1.0 / (ROPE_BASE ** (np.arange(nf, dtype=np.float32) / nf))).astype(np.float32)
    ang = pos * inv[j % nf][None, :]
    sign = np.where((lane % (2 * nf)) < nf, -1.0, 1.0).astype(np.float32)
    return jnp.asarray(np.cos(ang), F32), jnp.asarray(np.sin(ang) * sign[None, :], F32)


def _token_major(a):
    b, h, l, d = a.shape
    return a.transpose(0, 2, 1, 3).reshape(b, l, h * d)


def kernel(x_prompt, x_sample, cache_na_k, cache_na_v, cache_df_k, cache_df_v, state_ml_C, state_ml_n,
           state_ml_m, c, c_ctx, w_ada, b_ada, g_norm1, g_norm2, w_in, w_out, na_qn, na_kn, na_rpb,
           df_qn, df_kn, df_lam, df_subln, ml_gate_b, ml_outn, w_router, b_router, w_exp1, w_exp3, w_exp2,
           w_sh1, w_sh3, w_sh2):
    bc, lc, _ = x_prompt.shape
    bs, ls, _ = x_sample.shape
    past = cache_na_k.shape[3]

    cond = jnp.zeros((8, D_MODEL), F32).at[0].set(c_ctx).at[1:1 + bs].set(c)
    mod = _modulation(cond, w_ada, b_ada)
    g64 = _group_matrix(NA_DH)
    g32 = _group_matrix(DF_DQK)
    rope_tabs = _rope_tables(ls)
    w_in_bf = jnp.pad(w_in.astype(BF16), ((0, 0), (0, 0), (0, IN_COLS_PAD - IN_COLS)))
    w_out_bf = w_out.astype(BF16)
    bias_tabs = _na_bias_tables(na_rpb, ls // GRID_W)
    moe_w = (w_exp1, w_exp3, w_exp2, w_sh1, w_sh3, w_sh2)

    y_p = x_prompt.reshape(bc * lc, D_MODEL)
    y_s = x_sample.reshape(bs * ls, D_MODEL)
    caches = states = None
    m_states = []
    for l in range(DEPTH):
        lam_init = 0.8 - 0.6 * math.exp(-0.3 * l)
        mod_l = mod[l].reshape(8, 6, D_MODEL)
        g1 = g_norm1[l][None, :]
        g2 = g_norm2[l][None, :]
        qn = jnp.tile(na_qn[l], NA_HEADS)[None, :]
        kn = jnp.tile(na_kn[l], NA_HEADS)[None, :]
        dqn = jnp.tile(df_qn[l], 2 * DF_HEADS)[None, :]
        dkn = jnp.tile(df_kn[l], 2 * DF_HEADS)[None, :]
        subln = jnp.tile(df_subln[l], DF_HEADS)[None, :]
        outn = jnp.tile(ml_outn[l], ML_HEADS)[None, :]
        gate_b = jnp.pad(ml_gate_b[l].reshape(-1), (0, LANES - 4 * ML_HEADS))[None, :]
        br = b_router[l][:, None]
        proj_w = (l, g1, w_in_bf, qn, kn, dqn, dkn, gate_b, g64, g32)

        mod_c = mod_l[0:1]
        tpm = bc * lc
        res = _inproj(y_p, mod_c, tpm, *proj_w, None, cache_seq=lc, cache_prev=caches)
        naq, nak, nav, dfq, dfk, dfv, mq, mk, mv, mo, mg = res[:11]
        caches = tuple(res[11:])
        nao, dfo = _ctx_attention(naq, nak, nav, dfq, dfk, dfv, df_lam[l], subln, lc, lam_init)
        mlh, c_n, n_n, m_n = _mlstm(mq, mk, mv, mg, lc, emit_state=True, layer=l, state_prev=states)
        states = (c_n, n_n)
        x1, h2, gates = _outproj(y_p, nao, dfo, mlh, mo, mod_c, tpm, l, g2, outn, w_out_bf, w_router, br)
        y_p = _moe_sparse(h2, gates, x1, mod_c, tpm, l, *moe_w)
        m_states.append(m_n[:, :, 0, 0].reshape(bc, 2, ML_HEADS))

        mod_s = mod_l[1:1 + bs]
        naq, nak, nav, dfq, dfk, dfv, mq, mk, mv, mo, mg = _inproj(y_s, mod_s, ls, *proj_w, rope_tabs)
        kc_na = _token_major(cache_na_k[:, l])
        vc_na = _token_major(cache_na_v[:, l])
        kc_df = cache_df_k[:, l].transpose(0, 3, 1, 2, 4).reshape(bs, past, 256)
        vc_df = _token_major(cache_df_v[:, l])
        nao = _lat_na(naq, nak, nav, kc_na, vc_na, bias_tabs, l, ls)
        dfo = _lat_df(dfq, dfk, dfv, kc_df, vc_df, df_lam[l], subln, ls, lam_init)
        n0 = jnp.broadcast_to(state_ml_n[:, l][..., None], (bs, 2, ML_HEADS, ML_DH, ML_DH))
        c0 = jnp.concatenate([state_ml_C[:, l], n0], axis=-1)
        c0 = c0.reshape(bs, 2 * ML_HEADS, ML_DH, 2 * ML_DH)
        m0 = jnp.broadcast_to(state_ml_m[:, l].reshape(bs, 2 * ML_HEADS, 1, 1), (bs, 2 * ML_HEADS, 8, LANES))
        (mlh,) = _mlstm(mq, mk, mv, mg, ls, init=(c0, m0))
        x1, h2, gates = _outproj(y_s, nao, dfo, mlh, mo, mod_s, ls, l, g2, outn, w_out_bf, w_router, br)
        y_s = _moe_sparse(h2, gates, x1, mod_s, ls, l, *moe_w)

    new_ml_c = states[0].reshape(bc, DEPTH, 2, ML_HEADS, ML_DH, ML_DH)
    new_ml_n = states[1].reshape(bc, DEPTH, 2, ML_HEADS, ML_DH)
    return (y_p.reshape(bc, lc, D_MODEL), y_s.reshape(bs, ls, D_MODEL), *caches,
            new_ml_c, new_ml_n, jnp.stack(m_states, axis=1))
```
